```python
import math
import jax, jax.numpy as jnp
from jax import lax
import numpy as np

D_MODEL = 2048
BATCH = 4
SEQ = 2048
DEPTH = 1

MIX_WIDTH = D_MODEL
GLA_WIDTH = MIX_WIDTH // 2
NSA_WIDTH = MIX_WIDTH - GLA_WIDTH
GLA_HEADS = 4
GLA_DK = (GLA_WIDTH // 2) // GLA_HEADS
GLA_DV = GLA_WIDTH // GLA_HEADS
GLA_GATE_RANK = 16
GLA_GATE_NORM = 16.0
GLA_CHUNK = 64
NSA_HEADS = 8
NSA_HD = NSA_WIDTH // NSA_HEADS
NSA_KV_GROUPS = 2
NSA_HPG = NSA_HEADS // NSA_KV_GROUPS
CMP_BLOCK = 32
CMP_STRIDE = 16
CMP_HIDDEN = 2 * NSA_HD
SLC_BLOCK = 64
SLC_TOPK = 16
SLC_QCHUNK = 64
WINDOW = 512
WIN_QBLOCK = 128
ROPE_THETA = 500000.0
ROPE_DIM = NSA_HD // 4
FFN_HIDDEN = -(-8 * D_MODEL // (3 * 256)) * 256
DEEPNORM_ALPHA = (2.0 * DEPTH) ** 0.25
DEEPNORM_BETA = (8.0 * DEPTH) ** -0.25
LN_EPS = 1e-5
FORCED_SCORE = 1e4
INVALID_SCORE = -1e4

IN_SIZES = (
    GLA_HEADS * GLA_DK,
    GLA_HEADS * GLA_DK,
    GLA_HEADS * GLA_DV,
    GLA_HEADS * GLA_DV,
    GLA_GATE_RANK,
    NSA_HEADS * NSA_HD,
    NSA_KV_GROUPS * NSA_HD,
    NSA_KV_GROUPS * NSA_HD,
    NSA_KV_GROUPS * NSA_HD,
    NSA_KV_GROUPS * NSA_HD,
    NSA_KV_GROUPS * NSA_HD,
    NSA_KV_GROUPS * NSA_HD,
    NSA_HEADS * 3,
)
IN_WIDTH = sum(IN_SIZES)

kernel_name = "hymba_gla_nsa_deepnorm_layer"


def layer_norm(x, g, b):
    xf = x.astype(jnp.float32)
    mu = jnp.mean(xf, axis=-1, keepdims=True)
    var = jnp.mean(jnp.square(xf - mu), axis=-1, keepdims=True)
    y = (xf - mu) * lax.rsqrt(var + LN_EPS) * g.astype(jnp.float32) + b.astype(jnp.float32)
    return y.astype(x.dtype)


def masked_softmax(s, mask):
    s = jnp.where(mask, s.astype(jnp.float32), -jnp.inf)
    m = jnp.max(s, axis=-1, keepdims=True)
    m = jnp.where(jnp.isfinite(m), m, 0.0)
    e = jnp.where(mask, jnp.exp(s - m), 0.0)
    den = jnp.sum(e, axis=-1, keepdims=True)
    return e / jnp.where(den > 0, den, 1.0)


def rope_tables(S):
    pos = jnp.arange(S, dtype=jnp.float32)
    inv = jnp.power(ROPE_THETA, -jnp.arange(0, ROPE_DIM, 2, dtype=jnp.float32) / ROPE_DIM)
    ang = pos[:, None] * inv[None, :]
    return jnp.cos(ang), jnp.sin(ang)


def apply_partial_rope(x, cos, sin):
    half = ROPE_DIM // 2
    c = cos[None, :, None, :].astype(x.dtype)
    s = sin[None, :, None, :].astype(x.dtype)
    x1 = x[..., :half]
    x2 = x[..., half:ROPE_DIM]
    return jnp.concatenate([x1 * c - x2 * s, x2 * c + x1 * s, x[..., ROPE_DIM:]], axis=-1)


def gla_mixer(q, k, v, gk, g_out, norm_w):
    B, S, H, dk = q.shape
    dv = v.shape[-1]
    C = GLA_CHUNK
    nC = S // C

    def chunks(t):
        return t.astype(jnp.float32).reshape(B, nC, C, H, t.shape[-1]).transpose(0, 3, 1, 2, 4)

    qc = chunks(q) * (dk ** -0.5)
    kc = chunks(k)
    vc = chunks(v)
    bc = jnp.cumsum(chunks(gk), axis=3)
    b_last = bc[:, :, :, -1:, :]
    q_dec = qc * jnp.exp(bc)
    k_intra = kc * jnp.exp(-bc)
    k_state = kc * jnp.exp(b_last - bc)
    decay = jnp.exp(b_last[:, :, :, 0, :])

    causal = jnp.tril(jnp.ones((C, C), dtype=bool))
    attn = jnp.where(causal, jnp.einsum('bhnid,bhnjd->bhnij', q_dec, k_intra), 0.0)
    o_intra = jnp.einsum('bhnij,bhnje->bhnie', attn, vc)

    def step(state, inp):
        qd, ks, vv, dec = inp
        o = jnp.einsum('bhid,bhde->bhie', qd, state)
        state = dec[..., None] * state + jnp.einsum('bhjd,bhje->bhde', ks, vv)
        return state, o

    xs = (jnp.moveaxis(q_dec, 2, 0), jnp.moveaxis(k_state, 2, 0),
          jnp.moveaxis(vc, 2, 0), jnp.moveaxis(decay, 2, 0))
    state0 = jnp.zeros((B, H, dk, dv), jnp.float32)
    _, o_inter = lax.scan(step, state0, xs)
    o = o_intra + jnp.moveaxis(o_inter, 0, 2)
    o = o.transpose(0, 2, 3, 1, 4).reshape(B, S, H, dv)
    o = o * lax.rsqrt(jnp.mean(jnp.square(o), axis=-1, keepdims=True) + LN_EPS) * norm_w.astype(jnp.float32)
    o = o.reshape(B, S, H * dv) * jax.nn.silu(g_out.astype(jnp.float32))
    return o.astype(q.dtype)


def compress_blocks(kv, pos_emb, w1, w2):
    B, S, G, hd = kv.shape
    n_cmp = (S - CMP_BLOCK) // CMP_STRIDE + 1
    idx = np.arange(n_cmp)[:, None] * CMP_STRIDE + np.arange(CMP_BLOCK)[None, :]
    blocks = kv[:, idx] + pos_emb[None, None, :, None, :]
    flat = blocks.transpose(0, 1, 3, 2, 4).reshape(B, n_cmp, G, CMP_BLOCK * hd)
    return jax.nn.gelu(flat @ w1) @ w2


def nsa_mixer(q, kc, vc, ks, vs, kw, vw, gate_logits, cmp_k_pos, cmp_k_w1, cmp_k_w2,
              cmp_v_pos, cmp_v_w1, cmp_v_w2, cos, sin):
    B, S, H, hd = q.shape
    G, R = NSA_KV_GROUPS, NSA_HPG
    scale = hd ** -0.5
    t = jnp.arange(S)

    k_cmp = compress_blocks(kc, cmp_k_pos, cmp_k_w1, cmp_k_w2)
    v_cmp = compress_blocks(vc, cmp_v_pos, cmp_v_w1, cmp_v_w2)
    n_cmp = k_cmp.shape[1]
    cmp_end = jnp.arange(n_cmp) * CMP_STRIDE + CMP_BLOCK - 1
    qg = q.reshape(B, S, G, R, hd)
    s_cmp = jnp.einsum('bsgrd,bngd->bsgrn', qg, k_cmp) * scale
    cmp_mask = (cmp_end[None, :] <= t[:, None])[None, :, None, None, :]
    p_cmp = masked_softmax(s_cmp, cmp_mask)
    o_cmp = jnp.einsum('bsgrn,bngd->bsgrd', p_cmp, v_cmp.astype(jnp.float32))

    nb = S // SLC_BLOCK
    k_sel = min(SLC_TOPK, nb)
    c_start = np.arange(n_cmp) * CMP_STRIDE
    b_start = np.arange(nb) * SLC_BLOCK
    overlap = ((c_start[:, None] < b_start[None, :] + SLC_BLOCK) &
               (b_start[None, :] < c_start[:, None] + CMP_BLOCK)).astype(np.float32)
    imp = jnp.einsum('bsgrn,nj->bsgj', p_cmp, jnp.asarray(overlap))
    blk = jnp.arange(nb)
    cur = t // SLC_BLOCK
    valid = blk[None, :] <= cur[:, None]
    forced = (blk[None, :] == 0) | (blk[None, :] == cur[:, None]) | (blk[None, :] == cur[:, None] - 1)
    imp = jnp.where(valid[None, :, None, :], imp, INVALID_SCORE)
    imp = jnp.where(forced[None, :, None, :], FORCED_SCORE, imp)
    _, sel_idx = lax.top_k(imp, k_sel)

    q_r = apply_partial_rope(q, cos, sin).reshape(B, S, G, R, hd)
    k_s = apply_partial_rope(ks, cos, sin)
    k_blocks = k_s.reshape(B, nb, SLC_BLOCK, G, hd).transpose(0, 3, 1, 2, 4)
    v_blocks = vs.reshape(B, nb, SLC_BLOCK, G, hd).transpose(0, 3, 1, 2, 4)
    nQ = S // SLC_QCHUNK
    bi = jnp.arange(B)[:, None, None, None]
    gi = jnp.arange(G)[None, None, :, None]

    def sel_chunk(args):
        qc_, idx, tq = args
        Tq = qc_.shape[1]
        kg = k_blocks[bi, gi, idx]
        vg = v_blocks[bi, gi, idx]
        kpos = idx[..., None] * SLC_BLOCK + jnp.arange(SLC_BLOCK)
        mask = (kpos <= tq[None, :, None, None, None])[:, :, :, None].reshape(B, Tq, G, 1, k_sel * SLC_BLOCK)
        s = jnp.einsum('bqgrd,bqgjcd->bqgrjc', qc_, kg) * scale
        p = masked_softmax(s.reshape(B, Tq, G, R, k_sel * SLC_BLOCK), mask)
        p = p.reshape(B, Tq, G, R, k_sel, SLC_BLOCK)
        return jnp.einsum('bqgrjc,bqgjcd->bqgrd', p, vg.astype(jnp.float32))

    q_chunks = jnp.moveaxis(q_r.reshape(B, nQ, SLC_QCHUNK, G, R, hd), 1, 0)
    i_chunks = jnp.moveaxis(sel_idx.reshape(B, nQ, SLC_QCHUNK, G, k_sel), 1, 0)
    t_chunks = t.reshape(nQ, SLC_QCHUNK)
    o_slc = lax.map(sel_chunk, (q_chunks, i_chunks, t_chunks))
    o_slc = jnp.moveaxis(o_slc, 0, 1).reshape(B, S, G, R, hd)

    k_w = apply_partial_rope(kw, cos, sin)
    nWQ = S // WIN_QBLOCK
    n_pad_blk = WINDOW // WIN_QBLOCK
    nkb = n_pad_blk + 1

    def band(a):
        a = jnp.pad(a, ((0, 0), (WINDOW, 0), (0, 0), (0, 0))).reshape(B, nWQ + n_pad_blk, WIN_QBLOCK, G, hd)
        return jnp.concatenate([a[:, j:j + nWQ] for j in range(nkb)], axis=2)

    kwin = band(k_w)
    vwin = band(vw)
    qw = q_r.reshape(B, nWQ, WIN_QBLOCK, G, R, hd)
    qpos = t.reshape(nWQ, WIN_QBLOCK)
    kpos = jnp.arange(nWQ)[:, None] * WIN_QBLOCK + jnp.arange(nkb * WIN_QBLOCK)[None, :] - WINDOW
    diff = qpos[:, :, None] - kpos[:, None, :]
    win_mask = (kpos[:, None, :] >= 0) & (diff >= 0) & (diff < WINDOW)
    s_win = jnp.einsum('bnqgrd,bnkgd->bnqgrk', qw, kwin) * scale
    p_win = masked_softmax(s_win, win_mask[None, :, :, None, None, :])
    o_win = jnp.einsum('bnqgrk,bnkgd->bnqgrd', p_win, vwin.astype(jnp.float32)).reshape(B, S, G, R, hd)

    g = jax.nn.sigmoid(gate_logits.astype(jnp.float32)).reshape(B, S, G, R, 3)
    o = g[..., 0:1] * o_cmp + g[..., 1:2] * o_slc + g[..., 2:3] * o_win
    return o.reshape(B, S, H * hd).astype(q.dtype)


def setup_inputs(seed: int = 0) -> dict:
    key = jax.random.key(seed)
    ks = jax.random.split(key, 20)
    L = DEPTH

    def nrm(k, shape, scale):
        return jax.random.normal(k, shape, jnp.float32) * scale

    return {
        "x": nrm(ks[0], (BATCH, SEQ, D_MODEL), 1.0),
        "w_in": nrm(ks[1], (L, D_MODEL, IN_WIDTH), D_MODEL ** -0.5),
        "gla_gate_w2": nrm(ks[2], (L, GLA_GATE_RANK, GLA_HEADS * GLA_DK), GLA_GATE_RANK ** -0.5),
        "gla_gate_b2": nrm(ks[3], (L, GLA_HEADS * GLA_DK), 0.1),
        "gla_norm_w": 1.0 + nrm(ks[4], (L, GLA_DV), 0.02),
        "cmp_k_pos": nrm(ks[5], (L, CMP_BLOCK, NSA_HD), 0.1),
        "cmp_k_w1": nrm(ks[6], (L, CMP_BLOCK * NSA_HD, CMP_HIDDEN), (CMP_BLOCK * NSA_HD) ** -0.5),
        "cmp_k_w2": nrm(ks[7], (L, CMP_HIDDEN, NSA_HD), CMP_HIDDEN ** -0.5),
        "cmp_v_pos": nrm(ks[8], (L, CMP_BLOCK, NSA_HD), 0.1),
        "cmp_v_w1": nrm(ks[9], (L, CMP_BLOCK * NSA_HD, CMP_HIDDEN), (CMP_BLOCK * NSA_HD) ** -0.5),
        "cmp_v_w2": nrm(ks[10], (L, CMP_HIDDEN, NSA_HD), CMP_HIDDEN ** -0.5),
        "w_out": nrm(ks[11], (L, MIX_WIDTH, D_MODEL), MIX_WIDTH ** -0.5 * DEEPNORM_BETA),
        "ln1_g": 1.0 + nrm(ks[12], (L, D_MODEL), 0.02),
        "ln1_b": nrm(ks[13], (L, D_MODEL), 0.02),
        "ffn_w1": nrm(ks[14], (L, D_MODEL, FFN_HIDDEN), D_MODEL ** -0.5),
        "ffn_w3": nrm(ks[15], (L, D_MODEL, FFN_HIDDEN), D_MODEL ** -0.5),
        "ffn_w2": nrm(ks[16], (L, FFN_HIDDEN, D_MODEL), FFN_HIDDEN ** -0.5 * DEEPNORM_BETA),
        "ln2_g": 1.0 + nrm(ks[17], (L, D_MODEL), 0.02),
        "ln2_b": nrm(ks[18], (L, D_MODEL), 0.02),
    }


def reference(x, w_in, gla_gate_w2, gla_gate_b2, gla_norm_w, cmp_k_pos, cmp_k_w1, cmp_k_w2,
              cmp_v_pos, cmp_v_w1, cmp_v_w2, w_out, ln1_g, ln1_b, ffn_w1, ffn_w3, ffn_w2,
              ln2_g, ln2_b):
    B, S, _ = x.shape
    cos, sin = rope_tables(S)
    split_points = []
    acc = 0
    for size in IN_SIZES[:-1]:
        acc += size
        split_points.append(acc)

    for l in range(DEPTH):
        u = jnp.einsum('bsd,de->bse', x, w_in[l])
        (g_q, g_k, g_v, g_o, g_lr, n_q, n_kc, n_vc, n_ks, n_vs, n_kw, n_vw, n_gate) = jnp.split(u, split_points, axis=-1)

        gk = jax.nn.log_sigmoid((g_lr @ gla_gate_w2[l] + gla_gate_b2[l]).astype(jnp.float32)) / GLA_GATE_NORM
        o_gla = gla_mixer(g_q.reshape(B, S, GLA_HEADS, GLA_DK), g_k.reshape(B, S, GLA_HEADS, GLA_DK),
                          g_v.reshape(B, S, GLA_HEADS, GLA_DV), gk.reshape(B, S, GLA_HEADS, GLA_DK),
                          g_o, gla_norm_w[l])

        kvs = lambda a: a.reshape(B, S, NSA_KV_GROUPS, NSA_HD)
        o_nsa = nsa_mixer(n_q.reshape(B, S, NSA_HEADS, NSA_HD), kvs(n_kc), kvs(n_vc), kvs(n_ks), kvs(n_vs),
                          kvs(n_kw), kvs(n_vw), n_gate.reshape(B, S, NSA_HEADS, 3),
                          cmp_k_pos[l], cmp_k_w1[l], cmp_k_w2[l], cmp_v_pos[l], cmp_v_w1[l], cmp_v_w2[l],
                          cos, sin)

        mix = jnp.einsum('bse,ed->bsd', jnp.concatenate([o_gla, o_nsa], axis=-1), w_out[l])
        h = layer_norm(DEEPNORM_ALPHA * x + mix, ln1_g[l], ln1_b[l])
        ffn = (jax.nn.silu(h @ ffn_w1[l]) * (h @ ffn_w3[l])) @ ffn_w2[l]
        x = layer_norm(DEEPNORM_ALPHA * h + ffn, ln2_g[l], ln2_b[l])
    return x
```

```python
import functools

import numpy as np
import jax
import jax.numpy as jnp
from jax import lax
from jax.experimental import pallas as pl
from jax.experimental.pallas import tpu as pltpu

F32 = jnp.float32
BF16 = jnp.bfloat16

D_MODEL = 2048
DEPTH = 1
GLA_HEADS = 4
GLA_DK = 128
GLA_DV = 256
GLA_GATE_RANK = 16
GLA_GATE_NORM = 16.0
GLA_CHUNK = 64
NSA_HEADS = 8
NSA_HD = 128
NSA_KV_GROUPS = 2
NSA_HPG = NSA_HEADS // NSA_KV_GROUPS
CMP_BLOCK = 32
CMP_STRIDE = 16
CMP_HIDDEN = 2 * NSA_HD
SLC_BLOCK = 64
SLC_TOPK = 16
WINDOW = 512
ROPE_THETA = 500000.0
ROPE_DIM = NSA_HD // 4
FFN_HIDDEN = 5632
DEEPNORM_ALPHA = (2.0 * DEPTH) ** 0.25
LN_EPS = 1e-5
FORCED_SCORE = 1e4
INVALID_SCORE = -1e4

GLA_WIDTH = GLA_HEADS * GLA_DV
NSA_WIDTH = NSA_HEADS * NSA_HD

LANES = 128
VMEM_LIMIT = 56 * 1024 * 1024

CB_GQ = 0
CB_GK = 4
CB_GV = 8
CB_GO = 16
CB_NQ = 24
CB_KC = 32
CB_KS = 36
CB_VS = 38
CB_KW = 40
CB_VW = 42
CB_MISC = 44
MISC_GATE_LANE = GLA_GATE_RANK
U_WIDTH = 48 * LANES

IN_TM, IN_TN = 1024, 512
OUT_TM = 512
FFN_TM, FFN_TF = 512, 512
NSA_TQ = 128
NSA_TK = 512
WIN_SPAN = WINDOW + NSA_TQ


def _nt_dot(a, b):
    return lax.dot_general(a, b, (((1,), (1,)), ((), ())), preferred_element_type=F32)


def _tn_dot(a, b):
    return lax.dot_general(a, b, (((0,), (0,)), ((), ())), preferred_element_type=F32)


def _dot(a, b):
    return jnp.dot(a, b, preferred_element_type=F32)


def _layer_norm(z, g, b):
    mu = jnp.mean(z, axis=-1, keepdims=True)
    zc = z - mu
    var = jnp.mean(zc * zc, axis=-1, keepdims=True)
    return zc * lax.rsqrt(var + LN_EPS) * g + b


def _masked_softmax(s, mask):
    sm = jnp.where(mask, s, -jnp.inf)
    m = jnp.max(sm, axis=-1, keepdims=True)
    m = jnp.where(m == -jnp.inf, 0.0, m)
    e = jnp.where(mask, jnp.exp(s - m), 0.0)
    den = jnp.sum(e, axis=-1, keepdims=True)
    return e * (1.0 / jnp.where(den > 0, den, 1.0))


def _matmul_kernel(x_ref, w_ref, o_ref):
    o_ref[...] = _dot(x_ref[...], w_ref[...])


def _in_proj(x_bf, w_bf):
    T, D = x_bf.shape
    N = w_bf.shape[1]
    return pl.pallas_call(
        _matmul_kernel,
        grid=(T // IN_TM, N // IN_TN),
        in_specs=[pl.BlockSpec((IN_TM, D), lambda i, j: (i, 0)),
                  pl.BlockSpec((D, IN_TN), lambda i, j: (0, j))],
        out_specs=pl.BlockSpec((IN_TM, IN_TN), lambda i, j: (i, j)),
        out_shape=jax.ShapeDtypeStruct((T, N), F32),
        compiler_params=pltpu.CompilerParams(
            dimension_semantics=("parallel", "arbitrary"), vmem_limit_bytes=VMEM_LIMIT),
        name="in_proj",
    )(x_bf, w_bf)


GLA_ROWBLK = 256


def _split3(a):
    hi = a.astype(BF16)
    r1 = a - hi.astype(F32)
    mid = r1.astype(BF16)
    lo = (r1 - mid.astype(F32)).astype(BF16)
    return hi, mid, lo


def _gla_kernel(q_ref, k_ref, v_ref, go_ref, misc_ref, w2_ref, b2_ref, nw_ref, o_ref,
                qd_s, ki_s, ks_s, dec_s):
    S = q_ref.shape[0]
    C = GLA_CHUNK
    scale = GLA_DK ** -0.5

    r = lax.broadcasted_iota(jnp.int32, (GLA_ROWBLK, GLA_ROWBLK), 0)
    c = lax.broadcasted_iota(jnp.int32, (GLA_ROWBLK, GLA_ROWBLK), 1)
    same = (r // C) == (c // C)
    cum_m = jnp.where(same & (c <= r), 1.0, 0.0).astype(BF16)
    tot_m = jnp.where(same, 1.0, 0.0).astype(BF16)
    lt = jnp.concatenate([cum_m, tot_m], axis=0)

    w2 = w2_ref[...].astype(BF16)
    b2 = b2_ref[...]

    def prep(rb, _):
        rows = pl.ds(pl.multiple_of(rb * GLA_ROWBLK, GLA_ROWBLK), GLA_ROWBLK)
        glr = misc_ref[rows, 0:GLA_GATE_RANK].astype(BF16)
        z = _dot(glr, w2) + b2
        gk = jax.nn.log_sigmoid(z) * (1.0 / GLA_GATE_NORM)
        hi, mid, lo = _split3(gk)
        cs = _dot(lt, hi) + _dot(lt, mid) + _dot(lt, lo)
        bc = cs[:GLA_ROWBLK]
        bl = cs[GLA_ROWBLK:]
        q = q_ref[rows, :] * scale
        k = k_ref[rows, :]
        qd_s[rows, :] = (q * jnp.exp(bc)).astype(BF16)
        ki_s[rows, :] = (k * jnp.exp(-bc)).astype(BF16)
        ks_s[rows, :] = (k * jnp.exp(bl - bc)).astype(BF16)
        dec_s[rows, :] = jnp.exp(bl)
        return 0

    lax.fori_loop(0, S // GLA_ROWBLK, prep, 0)

    ri = lax.broadcasted_iota(jnp.int32, (C, C), 0)
    ci = lax.broadcasted_iota(jnp.int32, (C, C), 1)
    causal = ci <= ri
    nw = nw_ref[...]

    def chunk(cidx, state_t):
        rows = pl.ds(pl.multiple_of(cidx * C, C), C)
        qd = qd_s[rows, :]
        ki = ki_s[rows, :]
        ks = ks_s[rows, :]
        v = v_ref[rows, :].astype(BF16)
        attn = jnp.where(causal, _nt_dot(qd, ki), 0.0).astype(BF16)
        o = _dot(attn, v) + _nt_dot(qd, state_t.astype(BF16))
        dec = dec_s[pl.ds(pl.multiple_of(cidx * C, C), 1), :]
        state_t = state_t * dec + _tn_dot(v, ks)
        ms = jnp.mean(o * o, axis=-1, keepdims=True)
        o = o * lax.rsqrt(ms + LN_EPS) * nw
        go = go_ref[rows, :]
        o_ref[rows, :] = (o * (go * jax.nn.sigmoid(go))).astype(o_ref.dtype)
        return state_t

    lax.fori_loop(0, S // C, chunk, jnp.zeros((GLA_DV, GLA_DK), F32))


def _gla(u3, gate_w2, gate_b2, norm_w):
    B, S, _ = u3.shape
    dvb = GLA_DV // LANES
    return pl.pallas_call(
        _gla_kernel,
        grid=(B, GLA_HEADS),
        in_specs=[
            pl.BlockSpec((None, S, GLA_DK), lambda b, h: (b, 0, CB_GQ + h)),
            pl.BlockSpec((None, S, GLA_DK), lambda b, h: (b, 0, CB_GK + h)),
            pl.BlockSpec((None, S, GLA_DV), lambda b, h: (b, 0, CB_GV // dvb + h)),
            pl.BlockSpec((None, S, GLA_DV), lambda b, h: (b, 0, CB_GO // dvb + h)),
            pl.BlockSpec((None, S, LANES), lambda b, h: (b, 0, CB_MISC)),
            pl.BlockSpec((GLA_GATE_RANK, GLA_DK), lambda b, h: (0, h)),
            pl.BlockSpec((1, GLA_DK), lambda b, h: (0, h)),
            pl.BlockSpec((1, GLA_DV), lambda b, h: (0, 0)),
        ],
        out_specs=pl.BlockSpec((None, S, GLA_DV), lambda b, h: (b, 0, h)),
        out_shape=jax.ShapeDtypeStruct((B, S, GLA_WIDTH), BF16),
        scratch_shapes=[pltpu.VMEM((S, GLA_DK), BF16), pltpu.VMEM((S, GLA_DK), BF16),
                        pltpu.VMEM((S, GLA_DK), BF16), pltpu.VMEM((S, GLA_DK), F32)],
        compiler_params=pltpu.CompilerParams(
            dimension_semantics=("parallel", "parallel"), vmem_limit_bytes=VMEM_LIMIT),
        name="gla",
    )(u3, u3, u3, u3, u3, gate_w2, gate_b2, norm_w)


N_CMP_PAD = 128


def _gelu_tanh(x):
    return x * (0.5 * (1.0 + jnp.tanh(0.7978845608028654 * (x + 0.044715 * (x * x * x)))))


def _compress_kernel(kv_ref, pos_ref, w1_ref, w2_ref, o_ref):
    half = CMP_BLOCK // 2
    p0 = jnp.zeros((N_CMP_PAD, CMP_HIDDEN), F32)
    p1 = jnp.zeros((N_CMP_PAD, CMP_HIDDEN), F32)
    for l in range(half):
        x = kv_ref[pl.ds(l, N_CMP_PAD, stride=CMP_STRIDE), :]
        a0 = (x + pos_ref[l:l + 1, :]).astype(BF16)
        a1 = (x + pos_ref[half + l:half + l + 1, :]).astype(BF16)
        p0 = p0 + _dot(a0, w1_ref[l * NSA_HD:(l + 1) * NSA_HD, :])
        p1 = p1 + _dot(a1, w1_ref[(half + l) * NSA_HD:(half + l + 1) * NSA_HD, :])
    pre = p0 + pltpu.roll(p1, N_CMP_PAD - 1, 0)
    h = _gelu_tanh(pre).astype(BF16)
    out = _dot(h, w2_ref[...])
    row = lax.broadcasted_iota(jnp.int32, out.shape, 0)
    o_ref[...] = jnp.where(row < N_CMP_PAD - 1, out, 0.0).astype(o_ref.dtype)


def _compress(u3, pos, w1, w2):
    B, S, _ = u3.shape
    G = NSA_KV_GROUPS
    return pl.pallas_call(
        _compress_kernel,
        grid=(2, B, G),
        in_specs=[
            pl.BlockSpec((None, S, NSA_HD), lambda kv, b, g: (b, 0, CB_KC + 2 * kv + g)),
            pl.BlockSpec((None, CMP_BLOCK, NSA_HD), lambda kv, b, g: (kv, 0, 0)),
            pl.BlockSpec((None, CMP_BLOCK * NSA_HD, CMP_HIDDEN), lambda kv, b, g: (kv, 0, 0)),
            pl.BlockSpec((None, CMP_HIDDEN, NSA_HD), lambda kv, b, g: (kv, 0, 0)),
        ],
        out_specs=pl.BlockSpec((None, None, None, N_CMP_PAD, NSA_HD), lambda kv, b, g: (kv, b, g, 0, 0)),
        out_shape=jax.ShapeDtypeStruct((2, B, G, N_CMP_PAD, NSA_HD), BF16),
        compiler_params=pltpu.CompilerParams(
            dimension_semantics=("parallel", "parallel", "parallel"), vmem_limit_bytes=VMEM_LIMIT),
        name="compress",
    )(u3, pos, w1, w2)


def _rope(x, cos, sin_lo, sin_hi):
    half = ROPE_DIM // 2
    return x * cos + pltpu.roll(x, LANES - half, 1) * sin_lo + pltpu.roll(x, half, 1) * sin_hi


def _nsa_kernel(q_ref, misc_ref, kc_ref, vc_ref, ks_ref, vs_ref, kw_ref, vw_ref,
                cos_ref, sinlo_ref, sinhi_ref, ovt_ref, exp_ref, o_ref,
                ksr_s, vsb_s, kwr_s, vwb_s):
    S = ks_ref.shape[0]
    R = NSA_HPG
    TQ = NSA_TQ
    g = pl.program_id(1)
    qi = pl.program_id(2)
    scale = NSA_HD ** -0.5

    @pl.when(qi == 0)
    def _():
        def prep(rb, _):
            rows = pl.ds(pl.multiple_of(rb * 256, 256), 256)
            cos, slo, shi = cos_ref[rows, :], sinlo_ref[rows, :], sinhi_ref[rows, :]
            ksr_s[rows, :] = _rope(ks_ref[rows, :], cos, slo, shi).astype(BF16)
            kwr_s[rows, :] = _rope(kw_ref[rows, :], cos, slo, shi).astype(BF16)
            vsb_s[rows, :] = vs_ref[rows, :].astype(BF16)
            vwb_s[rows, :] = vw_ref[rows, :].astype(BF16)
            return 0
        lax.fori_loop(0, S // 256, prep, 0)

    t0 = qi * TQ
    trows = pl.ds(pl.multiple_of(t0, TQ), TQ)
    cos_t, slo_t, shi_t = cos_ref[trows, :], sinlo_ref[trows, :], sinhi_ref[trows, :]
    q = q_ref[...] * scale
    heads = [q[:, r * NSA_HD:(r + 1) * NSA_HD] for r in range(R)]
    qc = jnp.concatenate([h.astype(BF16) for h in heads], axis=0)
    qr = jnp.concatenate([_rope(h, cos_t, slo_t, shi_t).astype(BF16) for h in heads], axis=0)
    t_tok = t0 + lax.broadcasted_iota(jnp.int32, (TQ, 1), 0)
    t_col = jnp.concatenate([t_tok] * R, axis=0)

    s_c = _nt_dot(qc, kc_ref[...])
    n_idx = lax.broadcasted_iota(jnp.int32, (1, N_CMP_PAD), 1)
    p_c = _masked_softmax(s_c, (n_idx * CMP_STRIDE + (CMP_BLOCK - 1)) <= t_col)
    p_cb = p_c.astype(BF16)
    o_cmp = _dot(p_cb, vc_ref[...])

    ovt = ovt_ref[...]
    imp_t = _nt_dot(ovt, p_cb[0:TQ])
    for r in range(1, R):
        imp_t = imp_t + _nt_dot(ovt, p_cb[r * TQ:(r + 1) * TQ])
    nb = S // SLC_BLOCK
    imp = imp_t[0:nb, :]
    j_idx = lax.broadcasted_iota(jnp.int32, (nb, TQ), 0)
    cur = (t0 + lax.broadcasted_iota(jnp.int32, (nb, TQ), 1)) // SLC_BLOCK
    imp = jnp.where(j_idx <= cur, imp, INVALID_SCORE)
    imp = jnp.where((j_idx == 0) | (j_idx == cur) | (j_idx == cur - 1), FORCED_SCORE, imp)
    rank = jnp.zeros((nb, TQ), jnp.int32)
    for i in range(nb):
        ri = imp[i:i + 1, :]
        ahead = (ri > imp) | ((ri == imp) & (i < j_idx))
        rank = rank + jnp.where(ahead, 1, 0)
    sel_t = jnp.where(rank < min(SLC_TOPK, nb), 1.0, 0.0)
    sel_t = jnp.concatenate([sel_t, jnp.zeros((LANES - nb, TQ), F32)], axis=0)
    sel = jnp.transpose(sel_t).astype(BF16)

    TK = NSA_TK

    def sel_step(kt, carry):
        m, l, acc = carry
        k0 = pl.multiple_of(kt * TK, TK)
        s = _nt_dot(qr, ksr_s[pl.ds(k0, TK), :])
        chosen = _dot(sel, exp_ref[kt])
        chosen = jnp.concatenate([chosen] * R, axis=0)
        kpos = k0 + lax.broadcasted_iota(jnp.int32, (1, TK), 1)
        mask = (chosen > 0.5) & (kpos <= t_col)
        m_new = jnp.maximum(m, jnp.max(jnp.where(mask, s, -jnp.inf), axis=-1, keepdims=True))
        m_use = jnp.where(m_new == -jnp.inf, 0.0, m_new)
        alpha = jnp.exp(jnp.where(m == -jnp.inf, m_use, m) - m_use)
        alpha = jnp.where(m == -jnp.inf, 0.0, alpha)
        e = jnp.where(mask, jnp.exp(s - m_use), 0.0)
        l = alpha * l + jnp.sum(e, axis=-1, keepdims=True)
        acc = alpha * acc + _dot(e.astype(BF16), vsb_s[pl.ds(k0, TK), :])
        return m_new, l, acc

    n_kt = (t0 + TQ + TK - 1) // TK
    init = (jnp.full((R * TQ, 1), -jnp.inf, F32), jnp.zeros((R * TQ, 1), F32),
            jnp.zeros((R * TQ, NSA_HD), F32))
    _, l_s, acc_s = lax.fori_loop(0, n_kt, sel_step, init)
    o_slc = acc_s * (1.0 / jnp.where(l_s > 0, l_s, 1.0))

    start = pl.multiple_of(jnp.maximum(qi - WINDOW // TQ, 0) * TQ, TQ)
    s_w = _nt_dot(qr, kwr_s[pl.ds(start, WIN_SPAN), :])
    diff = t_col - (start + lax.broadcasted_iota(jnp.int32, (1, WIN_SPAN), 1))
    p_w = _masked_softmax(s_w, (diff >= 0) & (diff < WINDOW))
    o_win = _dot(p_w.astype(BF16), vwb_s[pl.ds(start, WIN_SPAN), :])

    misc = misc_ref[...]
    ng = 3 * R
    logits = jnp.where(g == 0, misc[:, MISC_GATE_LANE:MISC_GATE_LANE + ng],
                       misc[:, MISC_GATE_LANE + ng:MISC_GATE_LANE + 2 * ng])
    gate = jax.nn.sigmoid(logits)
    for r in range(R):
        rs = slice(r * TQ, (r + 1) * TQ)
        o = (gate[:, 3 * r:3 * r + 1] * o_cmp[rs] + gate[:, 3 * r + 1:3 * r + 2] * o_slc[rs]
             + gate[:, 3 * r + 2:3 * r + 3] * o_win[rs])
        o_ref[:, r * NSA_HD:(r + 1) * NSA_HD] = o.astype(o_ref.dtype)


def _nsa(u3, cmp_kv, cos_f, sin_lo, sin_hi, ovt, expand):
    B, S, _ = u3.shape
    G, R = NSA_KV_GROUPS, NSA_HPG
    qb = R * NSA_HD // LANES
    full = lambda cb: pl.BlockSpec((None, S, NSA_HD), lambda b, g, i: (b, 0, cb + g))
    table = pl.BlockSpec((S, LANES), lambda b, g, i: (0, 0))
    return pl.pallas_call(
        _nsa_kernel,
        grid=(B, G, S // NSA_TQ),
        in_specs=[
            pl.BlockSpec((None, NSA_TQ, R * NSA_HD), lambda b, g, i: (b, i, CB_NQ // qb + g)),
            pl.BlockSpec((None, NSA_TQ, LANES), lambda b, g, i: (b, i, CB_MISC)),
            pl.BlockSpec((None, None, None, N_CMP_PAD, NSA_HD), lambda b, g, i: (0, b, g, 0, 0)),
            pl.BlockSpec((None, None, None, N_CMP_PAD, NSA_HD), lambda b, g, i: (1, b, g, 0, 0)),
            full(CB_KS), full(CB_VS), full(CB_KW), full(CB_VW),
            table, table, table,
            pl.BlockSpec((LANES, N_CMP_PAD), lambda b, g, i: (0, 0)),
            pl.BlockSpec((S // NSA_TK, LANES, NSA_TK), lambda b, g, i: (0, 0, 0)),
        ],
        out_specs=pl.BlockSpec((None, NSA_TQ, R * NSA_HD), lambda b, g, i: (b, i, g)),
        out_shape=jax.ShapeDtypeStruct((B, S, NSA_WIDTH), BF16),
        scratch_shapes=[pltpu.VMEM((S, NSA_HD), BF16)] * 4,
        compiler_params=pltpu.CompilerParams(
            dimension_semantics=("parallel", "parallel", "arbitrary"), vmem_limit_bytes=VMEM_LIMIT),
        name="nsa",
    )(u3, u3, cmp_kv, cmp_kv, u3, u3, u3, u3, cos_f, sin_lo, sin_hi, ovt, expand)


def _out_proj_kernel(og_ref, on_ref, x_ref, w_ref, g_ref, b_ref, h_ref, hb_ref):
    mix = _dot(og_ref[...], w_ref[0:GLA_WIDTH, :]) + _dot(on_ref[...], w_ref[GLA_WIDTH:, :])
    h = _layer_norm(DEEPNORM_ALPHA * x_ref[...] + mix, g_ref[...], b_ref[...])
    h_ref[...] = h
    hb_ref[...] = h.astype(BF16)


def _out_proj(o_gla, o_nsa, x2, w_bf, g, b):
    T, D = x2.shape
    tile = lambda w: pl.BlockSpec((OUT_TM, w), lambda i: (i, 0))
    const = lambda s: pl.BlockSpec(s, lambda i: (0, 0))
    return pl.pallas_call(
        _out_proj_kernel,
        grid=(T // OUT_TM,),
        in_specs=[tile(GLA_WIDTH), tile(NSA_WIDTH), tile(D), const(w_bf.shape), const((1, D)), const((1, D))],
        out_specs=[tile(D), tile(D)],
        out_shape=[jax.ShapeDtypeStruct((T, D), F32), jax.ShapeDtypeStruct((T, D), BF16)],
        compiler_params=pltpu.CompilerParams(
            dimension_semantics=("parallel",), vmem_limit_bytes=VMEM_LIMIT),
        name="out_proj",
    )(o_gla, o_nsa, x2, w_bf, g, b)


def _ffn_kernel(hb_ref, h_ref, w1_ref, w3_ref, w2_ref, g_ref, b_ref, o_ref, acc_s):
    f = pl.program_id(1)
    hb = hb_ref[...]
    a = _dot(hb, w1_ref[...])
    c = _dot(hb, w3_ref[...])
    part = _dot((a * jax.nn.sigmoid(a) * c).astype(BF16), w2_ref[...])

    @pl.when(f == 0)
    def _():
        acc_s[...] = part

    @pl.when(f > 0)
    def _():
        acc_s[...] += part

    @pl.when(f == pl.num_programs(1) - 1)
    def _():
        o_ref[...] = _layer_norm(DEEPNORM_ALPHA * h_ref[...] + acc_s[...], g_ref[...], b_ref[...])


def _ffn(h_bf, h, w1, w3, w2, g, b):
    T, D = h.shape
    F = w1.shape[1]
    return pl.pallas_call(
        _ffn_kernel,
        grid=(T // FFN_TM, F // FFN_TF),
        in_specs=[
            pl.BlockSpec((FFN_TM, D), lambda i, f: (i, 0)),
            pl.BlockSpec((FFN_TM, D), lambda i, f: (i, 0)),
            pl.BlockSpec((D, FFN_TF), lambda i, f: (0, f)),
            pl.BlockSpec((D, FFN_TF), lambda i, f: (0, f)),
            pl.BlockSpec((FFN_TF, D), lambda i, f: (f, 0)),
            pl.BlockSpec((1, D), lambda i, f: (0, 0)),
            pl.BlockSpec((1, D), lambda i, f: (0, 0)),
        ],
        out_specs=pl.BlockSpec((FFN_TM, D), lambda i, f: (i, 0)),
        out_shape=jax.ShapeDtypeStruct((T, D), F32),
        scratch_shapes=[pltpu.VMEM((FFN_TM, D), F32)],
        compiler_params=pltpu.CompilerParams(
            dimension_semantics=("parallel", "arbitrary"), vmem_limit_bytes=VMEM_LIMIT),
        name="ffn",
    )(h_bf, h, w1, w3, w2, g, b)


def _rope_tables(S):
    half = ROPE_DIM // 2
    pos = jnp.arange(S, dtype=F32)
    inv = jnp.power(ROPE_THETA, -jnp.arange(0, ROPE_DIM, 2, dtype=F32) / ROPE_DIM)
    ang = pos[:, None] * inv[None, :]
    cos, sin = jnp.cos(ang), jnp.sin(ang)
    cos_f = jnp.concatenate([cos, cos, jnp.ones((S, LANES - ROPE_DIM), F32)], axis=1)
    sin_lo = jnp.concatenate([-sin, jnp.zeros((S, LANES - half), F32)], axis=1)
    sin_hi = jnp.concatenate([jnp.zeros((S, half), F32), sin, jnp.zeros((S, LANES - ROPE_DIM), F32)], axis=1)
    return cos_f, sin_lo, sin_hi


def _selection_constants(S):
    n_cmp = (S - CMP_BLOCK) // CMP_STRIDE + 1
    nb = S // SLC_BLOCK
    c_start = np.arange(n_cmp) * CMP_STRIDE
    b_start = np.arange(nb) * SLC_BLOCK
    overlap = ((c_start[:, None] < b_start[None, :] + SLC_BLOCK) &
               (b_start[None, :] < c_start[:, None] + CMP_BLOCK)).astype(np.float32)
    ovt = np.zeros((LANES, N_CMP_PAD), np.float32)
    ovt[:nb, :n_cmp] = overlap.T
    key_blk = np.arange(S) // SLC_BLOCK
    expand = (np.arange(LANES)[:, None] == key_blk[None, :]).astype(np.float32)
    expand = expand.reshape(LANES, S // NSA_TK, NSA_TK).transpose(1, 0, 2)
    return jnp.asarray(ovt, BF16), jnp.asarray(expand, BF16)


def _regroup_w_in(w):
    o_glr = 2 * GLA_HEADS * GLA_DK + 2 * GLA_HEADS * GLA_DV
    o_nq = o_glr + GLA_GATE_RANK
    o_gate = o_nq + NSA_HEADS * NSA_HD + 6 * NSA_KV_GROUPS * NSA_HD
    n_gate = NSA_HEADS * 3
    used = o_gate + n_gate
    assert w.shape[1] == used
    pad = jnp.zeros((w.shape[0], U_WIDTH - used), w.dtype)
    return jnp.concatenate([w[:, :o_glr], w[:, o_nq:o_gate], w[:, o_glr:o_nq], w[:, o_gate:], pad], axis=1)


def kernel(x, w_in, gla_gate_w2, gla_gate_b2, gla_norm_w, cmp_k_pos, cmp_k_w1, cmp_k_w2,
           cmp_v_pos, cmp_v_w1, cmp_v_w2, w_out, ln1_g, ln1_b, ffn_w1, ffn_w3, ffn_w2, ln2_g, ln2_b):
    B, S, D = x.shape
    T = B * S
    cos_f, sin_lo, sin_hi = _rope_tables(S)
    ovt, expand = _selection_constants(S)

    x2 = x.reshape(T, D)
    for l in range(DEPTH):
        u = _in_proj(x2.astype(BF16), _regroup_w_in(w_in[l]).astype(BF16))
        u3 = u.reshape(B, S, U_WIDTH)
        o_gla = _gla(u3, gla_gate_w2[l], gla_gate_b2[l][None, :], gla_norm_w[l][None, :])
        cmp_kv = _compress(u3, jnp.stack([cmp_k_pos[l], cmp_v_pos[l]]),
                           jnp.stack([cmp_k_w1[l], cmp_v_w1[l]]).astype(BF16),
                           jnp.stack([cmp_k_w2[l], cmp_v_w2[l]]).astype(BF16))
        o_nsa = _nsa(u3, cmp_kv, cos_f, sin_lo, sin_hi, ovt, expand)
        h, h_bf = _out_proj(o_gla.reshape(T, GLA_WIDTH), o_nsa.reshape(T, NSA_WIDTH), x2,
                            w_out[l].astype(BF16), ln1_g[l][None, :], ln1_b[l][None, :])
        x2 = _ffn(h_bf, h, ffn_w1[l].astype(BF16), ffn_w3[l].astype(BF16), ffn_w2[l].astype(BF16),
                  ln2_g[l][None, :], ln2_b[l][None, :])
    return x2.reshape(B, S, D)
```

```python
import numpy as np
import jax
import jax.numpy as jnp
from jax import lax
from jax.experimental import pallas as pl
from jax.experimental.pallas import tpu as pltpu

F32 = jnp.float32
BF16 = jnp.bfloat16

D_MODEL = 2048
DEPTH = 1
GLA_HEADS = 4
GLA_DK = 128
GLA_DV = 256
GLA_GATE_RANK = 16
GLA_GATE_NORM = 16.0
GLA_CHUNK = 64
NSA_HEADS = 8
NSA_HD = 128
NSA_KV_GROUPS = 2
NSA_HPG = NSA_HEADS // NSA_KV_GROUPS
CMP_BLOCK = 32
CMP_STRIDE = 16
CMP_HIDDEN = 2 * NSA_HD
SLC_BLOCK = 64
SLC_TOPK = 16
WINDOW = 512
ROPE_THETA = 500000.0
ROPE_DIM = NSA_HD // 4
FFN_HIDDEN = 5632
DEEPNORM_ALPHA = (2.0 * DEPTH) ** 0.25
LN_EPS = 1e-5
FORCED_SCORE = 1e4
INVALID_SCORE = -1e4

GLA_WIDTH = GLA_HEADS * GLA_DV
NSA_WIDTH = NSA_HEADS * NSA_HD

LANES = 128
VMEM_LIMIT = 56 * 1024 * 1024

CB_GQ = 0
CB_GK = 4
CB_GV = 8
CB_GO = 16
CB_NQ = 24
CB_KC = 32
CB_KS = 36
CB_VS = 38
CB_KW = 40
CB_VW = 42
CB_MISC = 44
MISC_GATE_LANE = GLA_GATE_RANK
U_WIDTH = 48 * LANES

IN_TM, IN_TN = 1024, 512
OUT_TM = 512
FFN_TM, FFN_TF = 512, 512
NSA_TQ = 128
NSA_SUB = 2
NSA_TK = 512
WIN_SPAN = WINDOW + NSA_TQ
MASKED = -1e30
LOG2E = 1.4426950408889634


def _nt_dot(a, b):
    return lax.dot_general(a, b, (((1,), (1,)), ((), ())), preferred_element_type=F32)


def _tn_dot(a, b):
    return lax.dot_general(a, b, (((0,), (0,)), ((), ())), preferred_element_type=F32)


def _dot(a, b):
    return jnp.dot(a, b, preferred_element_type=F32)


def _layer_norm(z, g, b):
    mu = jnp.mean(z, axis=-1, keepdims=True)
    zc = z - mu
    var = jnp.mean(zc * zc, axis=-1, keepdims=True)
    return zc * lax.rsqrt(var + LN_EPS) * g + b


def _masked_softmax(s, mask):
    sm = jnp.where(mask, s, -jnp.inf)
    m = jnp.max(sm, axis=-1, keepdims=True)
    m = jnp.where(m == -jnp.inf, 0.0, m)
    e = jnp.where(mask, jnp.exp(s - m), 0.0)
    den = jnp.sum(e, axis=-1, keepdims=True)
    return e * (1.0 / jnp.where(den > 0, den, 1.0))


def _matmul_kernel(x_ref, w_ref, o_ref):
    o_ref[...] = _dot(x_ref[...], w_ref[...])


def _in_proj(x_bf, w_bf):
    T, D = x_bf.shape
    N = w_bf.shape[1]
    return pl.pallas_call(
        _matmul_kernel,
        grid=(T // IN_TM, N // IN_TN),
        in_specs=[pl.BlockSpec((IN_TM, D), lambda i, j: (i, 0)),
                  pl.BlockSpec((D, IN_TN), lambda i, j: (0, j))],
        out_specs=pl.BlockSpec((IN_TM, IN_TN), lambda i, j: (i, j)),
        out_shape=jax.ShapeDtypeStruct((T, N), F32),
        compiler_params=pltpu.CompilerParams(
            dimension_semantics=("parallel", "arbitrary"), vmem_limit_bytes=VMEM_LIMIT),
        name="in_proj",
    )(x_bf, w_bf)


GLA_ROWBLK = 256
GLA_TS = 512


def _split3(a):
    hi = a.astype(BF16)
    r1 = a - hi.astype(F32)
    mid = r1.astype(BF16)
    lo = (r1 - mid.astype(F32)).astype(BF16)
    return hi, mid, lo


def _log_sigmoid(z):
    return jnp.minimum(z, 0.0) - jnp.log1p(jnp.exp(-jnp.abs(z)))


def _gla_kernel(q_ref, k_ref, v_ref, go_ref, misc_ref, w2_ref, b2_ref, nw_ref, o_ref,
                qd_s, ki_s, ks_s, dec_s, state_s):
    TS = q_ref.shape[0]
    H, C, DK, DV = GLA_HEADS, GLA_CHUNK, GLA_DK, GLA_DV
    scale = DK ** -0.5

    @pl.when(pl.program_id(1) == 0)
    def _():
        state_s[...] = jnp.zeros(state_s.shape, F32)

    r = lax.broadcasted_iota(jnp.int32, (GLA_ROWBLK, GLA_ROWBLK), 0)
    c = lax.broadcasted_iota(jnp.int32, (GLA_ROWBLK, GLA_ROWBLK), 1)
    same = (r // C) == (c // C)
    cum_m = jnp.where(same & (c <= r), 1.0, 0.0).astype(BF16)

    w2 = w2_ref[...].astype(BF16)
    b2 = b2_ref[...]
    for rb in range(TS // GLA_ROWBLK):
        rows = slice(rb * GLA_ROWBLK, (rb + 1) * GLA_ROWBLK)
        glr = misc_ref[rows, 0:GLA_GATE_RANK].astype(BF16)
        gk = _log_sigmoid(_dot(glr, w2) + b2) * (1.0 / GLA_GATE_NORM)
        hi, mid, lo = _split3(gk)
        bc_all = _dot(cum_m, hi) + _dot(cum_m, mid) + _dot(cum_m, lo)
        bl_all = jnp.concatenate(
            [jnp.broadcast_to(bc_all[j * C + C - 1:j * C + C, :], (C, H * DK)) for j in range(GLA_ROWBLK // C)],
            axis=0)
        for h in range(H):
            hk = slice(h * DK, (h + 1) * DK)
            bc = bc_all[:, hk]
            bl = bl_all[:, hk]
            q = q_ref[rows, hk] * scale
            k = k_ref[rows, hk]
            qd_s[h, rows, :] = (q * jnp.exp(bc)).astype(BF16)
            ki_s[h, rows, :] = (k * jnp.exp(-bc)).astype(BF16)
            ks_s[h, rows, :] = (k * jnp.exp(bl - bc)).astype(BF16)
            dec_s[h, rows, :] = jnp.exp(bl)

    ri = lax.broadcasted_iota(jnp.int32, (C, C), 0)
    ci = lax.broadcasted_iota(jnp.int32, (C, C), 1)
    causal = ci <= ri
    nw = nw_ref[...]

    def chunk(cidx, _):
        r0 = pl.multiple_of(cidx * C, C)
        rows = pl.ds(r0, C)
        for h in range(H):
            hv = slice(h * DV, (h + 1) * DV)
            qd = qd_s[h, rows, :]
            v = v_ref[rows, hv].astype(BF16)
            state_t = state_s[h]
            attn = jnp.where(causal, _nt_dot(qd, ki_s[h, rows, :]), 0.0).astype(BF16)
            o = _dot(attn, v) + _nt_dot(qd, state_t.astype(BF16))
            state_s[h] = state_t * dec_s[h, pl.ds(r0, 1), :] + _tn_dot(v, ks_s[h, rows, :])
            ms = jnp.mean(o * o, axis=-1, keepdims=True)
            o = o * lax.rsqrt(ms + LN_EPS) * nw
            go = go_ref[rows, hv]
            o_ref[rows, hv] = (o * (go * jax.nn.sigmoid(go))).astype(o_ref.dtype)
        return 0

    lax.fori_loop(0, TS // C, chunk, 0)


def _gla(u3, gate_w2, gate_b2, norm_w):
    B, S, _ = u3.shape
    H, DK, DV = GLA_HEADS, GLA_DK, GLA_DV
    qk_w, v_w = H * DK, H * DV
    tok = lambda w, off: pl.BlockSpec((None, GLA_TS, w), lambda b, s: (b, s, off * LANES // w))
    const = lambda shape: pl.BlockSpec(shape, lambda b, s: (0, 0))
    return pl.pallas_call(
        _gla_kernel,
        grid=(B, S // GLA_TS),
        in_specs=[tok(qk_w, CB_GQ), tok(qk_w, CB_GK), tok(v_w, CB_GV), tok(v_w, CB_GO), tok(LANES, CB_MISC),
                  const((GLA_GATE_RANK, qk_w)), const((1, qk_w)), const((1, DV))],
        out_specs=pl.BlockSpec((None, GLA_TS, v_w), lambda b, s: (b, s, 0)),
        out_shape=jax.ShapeDtypeStruct((B, S, GLA_WIDTH), BF16),
        scratch_shapes=[pltpu.VMEM((H, GLA_TS, DK), BF16), pltpu.VMEM((H, GLA_TS, DK), BF16),
                        pltpu.VMEM((H, GLA_TS, DK), BF16), pltpu.VMEM((H, GLA_TS, DK), F32),
                        pltpu.VMEM((H, DV, DK), F32)],
        compiler_params=pltpu.CompilerParams(
            dimension_semantics=("parallel", "arbitrary"), vmem_limit_bytes=VMEM_LIMIT),
        name="gla",
    )(u3, u3, u3, u3, u3, gate_w2, gate_b2, norm_w)


N_CMP_PAD = 128


def _gelu_tanh(x):
    return x * (0.5 * (1.0 + jnp.tanh(0.7978845608028654 * (x + 0.044715 * (x * x * x)))))


def _compress_kernel(kv_ref, pos_ref, w1_ref, w2_ref, o_ref):
    half = CMP_BLOCK // 2
    p0 = jnp.zeros((N_CMP_PAD, CMP_HIDDEN), F32)
    p1 = jnp.zeros((N_CMP_PAD, CMP_HIDDEN), F32)
    for l in range(half):
        x = kv_ref[pl.ds(l, N_CMP_PAD, stride=CMP_STRIDE), :]
        a0 = (x + pos_ref[l:l + 1, :]).astype(BF16)
        a1 = (x + pos_ref[half + l:half + l + 1, :]).astype(BF16)
        p0 = p0 + _dot(a0, w1_ref[l * NSA_HD:(l + 1) * NSA_HD, :])
        p1 = p1 + _dot(a1, w1_ref[(half + l) * NSA_HD:(half + l + 1) * NSA_HD, :])
    pre = p0 + pltpu.roll(p1, N_CMP_PAD - 1, 0)
    h = _gelu_tanh(pre).astype(BF16)
    out = _dot(h, w2_ref[...])
    row = lax.broadcasted_iota(jnp.int32, out.shape, 0)
    o_ref[...] = jnp.where(row < N_CMP_PAD - 1, out, 0.0).astype(o_ref.dtype)


def _compress(u3, pos, w1, w2):
    B, S, _ = u3.shape
    G = NSA_KV_GROUPS
    return pl.pallas_call(
        _compress_kernel,
        grid=(2, B, G),
        in_specs=[
            pl.BlockSpec((None, S, NSA_HD), lambda kv, b, g: (b, 0, CB_KC + 2 * kv + g)),
            pl.BlockSpec((None, CMP_BLOCK, NSA_HD), lambda kv, b, g: (kv, 0, 0)),
            pl.BlockSpec((None, CMP_BLOCK * NSA_HD, CMP_HIDDEN), lambda kv, b, g: (kv, 0, 0)),
            pl.BlockSpec((None, CMP_HIDDEN, NSA_HD), lambda kv, b, g: (kv, 0, 0)),
        ],
        out_specs=pl.BlockSpec((None, None, None, N_CMP_PAD, NSA_HD), lambda kv, b, g: (kv, b, g, 0, 0)),
        out_shape=jax.ShapeDtypeStruct((2, B, G, N_CMP_PAD, NSA_HD), BF16),
        compiler_params=pltpu.CompilerParams(
            dimension_semantics=("parallel", "parallel", "parallel"), vmem_limit_bytes=VMEM_LIMIT),
        name="compress",
    )(u3, pos, w1, w2)


def _rope(x, cos, sin_lo, sin_hi):
    half = ROPE_DIM // 2
    return x * cos + pltpu.roll(x, LANES - half, 1) * sin_lo + pltpu.roll(x, half, 1) * sin_hi


def _nsa_kernel(q_ref, misc_ref, kc_ref, vc_ref, ks_ref, vs_ref, kw_ref, vw_ref,
                cos_ref, sinlo_ref, sinhi_ref, ovt_ref, blk_ref, o_ref,
                ksa_s, vsa_s, kwr_s, vwa_s, ss_s, es_s, sw_s, ew_s):
    S = ks_ref.shape[0]
    R, TQ, HD = NSA_HPG, NSA_TQ, NSA_HD
    g = pl.program_id(1)
    qi = pl.program_id(2)
    scale = HD ** -0.5

    @pl.when(qi == 0)
    def _():
        def prep(rb, _):
            rows = pl.ds(pl.multiple_of(rb * 256, 256), 256)
            cos, slo, shi = cos_ref[rows, :], sinlo_ref[rows, :], sinhi_ref[rows, :]
            ones = jnp.ones((256, LANES), BF16)
            ksa_s[rows, 0:HD] = _rope(ks_ref[rows, :], cos, slo, shi).astype(BF16)
            ksa_s[rows, HD:] = blk_ref[rows, :]
            kwr_s[rows, :] = _rope(kw_ref[rows, :], cos, slo, shi).astype(BF16)
            vsa_s[rows, 0:HD] = vs_ref[rows, :].astype(BF16)
            vsa_s[rows, HD:] = ones
            vwa_s[rows, 0:HD] = vw_ref[rows, :].astype(BF16)
            vwa_s[rows, HD:] = ones
            return 0
        lax.fori_loop(0, S // 256, prep, 0)

    nb = S // SLC_BLOCK
    TK = NSA_TK
    ovt = ovt_ref[...]
    kc, vc = kc_ref[...], vc_ref[...]

    def front(u):
        tq0 = (qi * NSA_SUB + u) * TQ
        urows = slice(u * TQ, (u + 1) * TQ)
        trows = pl.ds(pl.multiple_of(tq0, TQ), TQ)
        cos_t, slo_t, shi_t = cos_ref[trows, :], sinlo_ref[trows, :], sinhi_ref[trows, :]
        heads = [q_ref[urows, r * HD:(r + 1) * HD] for r in range(R)]
        qc = jnp.concatenate([(h * scale).astype(BF16) for h in heads], axis=0)
        qr_heads = [_rope(h * (scale * LOG2E), cos_t, slo_t, shi_t).astype(BF16) for h in heads]
        t_tok = tq0 + lax.broadcasted_iota(jnp.int32, (TQ, 1), 0)
        t_col = jnp.concatenate([t_tok] * R, axis=0)

        s_c = _nt_dot(qc, kc)
        n_idx = lax.broadcasted_iota(jnp.int32, (1, N_CMP_PAD), 1)
        p_c = _masked_softmax(s_c, (n_idx * CMP_STRIDE + (CMP_BLOCK - 1)) <= t_col)
        p_cb = p_c.astype(BF16)
        o_cmp = _dot(p_cb, vc)

        imp_t = _nt_dot(ovt, p_cb[0:TQ])
        for r in range(1, R):
            imp_t = imp_t + _nt_dot(ovt, p_cb[r * TQ:(r + 1) * TQ])
        imp = imp_t[0:nb, :]
        j_idx = lax.broadcasted_iota(jnp.int32, (nb, TQ), 0)
        cur = (tq0 + lax.broadcasted_iota(jnp.int32, (nb, TQ), 1)) // SLC_BLOCK
        imp = jnp.where(j_idx <= cur, imp, INVALID_SCORE)
        imp = jnp.where((j_idx == 0) | (j_idx == cur) | (j_idx == cur - 1), FORCED_SCORE, imp)
        rank = jnp.zeros((nb, TQ), jnp.int32)
        for i in range(nb):
            ri = imp[i:i + 1, :]
            ahead = (ri > imp) | ((ri == imp) & (i < j_idx))
            rank = rank + jnp.where(ahead, 1, 0)
        blk_bias_t = jnp.where(rank < min(SLC_TOPK, nb), 0.0, MASKED)
        blk_bias_t = jnp.concatenate([blk_bias_t, jnp.zeros((LANES - nb, TQ), F32)], axis=0)
        blk_bias = jnp.transpose(blk_bias_t).astype(BF16)
        q_aug = jnp.concatenate([jnp.concatenate([h, blk_bias], axis=1) for h in qr_heads], axis=0)
        qr = jnp.concatenate(qr_heads, axis=0)
        return dict(tq0=tq0, t_tok=t_tok, t_col=t_col, o_cmp=o_cmp, q_aug=q_aug, qr=qr)

    subs = [front(u) for u in range(NSA_SUB)]

    def sel_tile(u, kt, carry, diagonal):
        m, acc = carry
        k0 = pl.multiple_of(kt * TK, TK)
        s = _nt_dot(subs[u]["q_aug"], ksa_s[pl.ds(k0, TK), :])
        if diagonal:
            kpos = k0 + lax.broadcasted_iota(jnp.int32, (1, TK), 1)
            s = jnp.where(kpos <= subs[u]["t_col"], s, MASKED)
        ss_s[u] = s
        m_new = jnp.maximum(m, jnp.max(ss_s[u], axis=-1, keepdims=True))
        alpha = jnp.exp2(m - m_new)
        es_s[u] = jnp.exp2(ss_s[u] - m_new).astype(BF16)
        acc = alpha * acc + _dot(es_s[u], vsa_s[pl.ds(k0, TK), :])
        return m_new, acc

    def sel_step(kt, carries, diagonal):
        return tuple(sel_tile(u, kt, carries[u], diagonal) for u in range(NSA_SUB))

    last = (qi * NSA_SUB * TQ) // TK
    init = tuple((jnp.full((R * TQ, 1), -jnp.inf, F32), jnp.zeros((R * TQ, 2 * HD), F32))
                 for _ in range(NSA_SUB))
    carries = lax.fori_loop(0, last, lambda kt, c: sel_step(kt, c, False), init)
    carries = sel_step(last, carries, True)

    misc = misc_ref[...]
    ng = 3 * R
    logits = jnp.where(g == 0, misc[:, MISC_GATE_LANE:MISC_GATE_LANE + ng],
                       misc[:, MISC_GATE_LANE + ng:MISC_GATE_LANE + 2 * ng])
    gate = jax.nn.sigmoid(logits)

    for u in range(NSA_SUB):
        sub = subs[u]
        acc_s = carries[u][1]
        o_slc = acc_s[:, 0:HD] * (1.0 / acc_s[:, HD:])

        start = pl.multiple_of(jnp.maximum(sub["tq0"] - WINDOW, 0), TQ)
        s_w = _nt_dot(sub["qr"], kwr_s[pl.ds(start, WIN_SPAN), :])
        diff = sub["t_tok"] - (start + lax.broadcasted_iota(jnp.int32, (1, WIN_SPAN), 1))
        band = jnp.where((diff >= 0) & (diff < WINDOW), 0.0, MASKED)
        for r in range(R):
            sw_s[u, r * TQ:(r + 1) * TQ, :] = s_w[r * TQ:(r + 1) * TQ] + band
        m_w = jnp.max(sw_s[u], axis=-1, keepdims=True)
        ew_s[u] = jnp.exp2(sw_s[u] - m_w).astype(BF16)
        acc_w = _dot(ew_s[u], vwa_s[pl.ds(start, WIN_SPAN), :])
        o_win = acc_w[:, 0:HD] * (1.0 / acc_w[:, HD:])

        urows = slice(u * TQ, (u + 1) * TQ)
        for r in range(R):
            rs = slice(r * TQ, (r + 1) * TQ)
            o = (gate[urows, 3 * r:3 * r + 1] * sub["o_cmp"][rs] + gate[urows, 3 * r + 1:3 * r + 2] * o_slc[rs]
                 + gate[urows, 3 * r + 2:3 * r + 3] * o_win[rs])
            o_ref[urows, r * HD:(r + 1) * HD] = o.astype(o_ref.dtype)


def _nsa(u3, cmp_kv, cos_f, sin_lo, sin_hi, ovt, blk_onehot):
    B, S, _ = u3.shape
    G, R, HD = NSA_KV_GROUPS, NSA_HPG, NSA_HD
    qb = R * HD // LANES
    rows = R * NSA_TQ
    step = NSA_SUB * NSA_TQ
    assert NSA_TK % step == 0
    full = lambda cb: pl.BlockSpec((None, S, HD), lambda b, g, i: (b, 0, cb + g))
    table = pl.BlockSpec((S, LANES), lambda b, g, i: (0, 0))
    return pl.pallas_call(
        _nsa_kernel,
        grid=(B, G, S // step),
        in_specs=[
            pl.BlockSpec((None, step, R * HD), lambda b, g, i: (b, i, CB_NQ // qb + g)),
            pl.BlockSpec((None, step, LANES), lambda b, g, i: (b, i, CB_MISC)),
            pl.BlockSpec((None, None, None, N_CMP_PAD, HD), lambda b, g, i: (0, b, g, 0, 0)),
            pl.BlockSpec((None, None, None, N_CMP_PAD, HD), lambda b, g, i: (1, b, g, 0, 0)),
            full(CB_KS), full(CB_VS), full(CB_KW), full(CB_VW),
            table, table, table,
            pl.BlockSpec((LANES, N_CMP_PAD), lambda b, g, i: (0, 0)),
            table,
        ],
        out_specs=pl.BlockSpec((None, step, R * HD), lambda b, g, i: (b, i, g)),
        out_shape=jax.ShapeDtypeStruct((B, S, NSA_WIDTH), BF16),
        scratch_shapes=[pltpu.VMEM((S, 2 * HD), BF16), pltpu.VMEM((S, 2 * HD), BF16),
                        pltpu.VMEM((S, HD), BF16), pltpu.VMEM((S, 2 * HD), BF16),
                        pltpu.VMEM((NSA_SUB, rows, NSA_TK), F32), pltpu.VMEM((NSA_SUB, rows, NSA_TK), BF16),
                        pltpu.VMEM((NSA_SUB, rows, WIN_SPAN), F32), pltpu.VMEM((NSA_SUB, rows, WIN_SPAN), BF16)],
        compiler_params=pltpu.CompilerParams(
            dimension_semantics=("parallel", "parallel", "arbitrary"), vmem_limit_bytes=VMEM_LIMIT),
        name="nsa",
    )(u3, u3, cmp_kv, cmp_kv, u3, u3, u3, u3, cos_f, sin_lo, sin_hi, ovt, blk_onehot)


def _out_proj_kernel(og_ref, on_ref, x_ref, w_ref, g_ref, b_ref, h_ref, hb_ref):
    mix = _dot(og_ref[...], w_ref[0:GLA_WIDTH, :]) + _dot(on_ref[...], w_ref[GLA_WIDTH:, :])
    h = _layer_norm(DEEPNORM_ALPHA * x_ref[...] + mix, g_ref[...], b_ref[...])
    h_ref[...] = h
    hb_ref[...] = h.astype(BF16)


def _out_proj(o_gla, o_nsa, x2, w_bf, g, b):
    T, D = x2.shape
    tile = lambda w: pl.BlockSpec((OUT_TM, w), lambda i: (i, 0))
    const = lambda s: pl.BlockSpec(s, lambda i: (0, 0))
    return pl.pallas_call(
        _out_proj_kernel,
        grid=(T // OUT_TM,),
        in_specs=[tile(GLA_WIDTH), tile(NSA_WIDTH), tile(D), const(w_bf.shape), const((1, D)), const((1, D))],
        out_specs=[tile(D), tile(D)],
        out_shape=[jax.ShapeDtypeStruct((T, D), F32), jax.ShapeDtypeStruct((T, D), BF16)],
        compiler_params=pltpu.CompilerParams(
            dimension_semantics=("parallel",), vmem_limit_bytes=VMEM_LIMIT),
        name="out_proj",
    )(o_gla, o_nsa, x2, w_bf, g, b)


def _ffn_kernel(hb_ref, h_ref, w1_ref, w3_ref, w2_ref, g_ref, b_ref, o_ref, acc_s):
    f = pl.program_id(1)
    hb = hb_ref[...]
    a = _dot(hb, w1_ref[...])
    c = _dot(hb, w3_ref[...])
    part = _dot((a * jax.nn.sigmoid(a) * c).astype(BF16), w2_ref[...])

    @pl.when(f == 0)
    def _():
        acc_s[...] = part

    @pl.when(f > 0)
    def _():
        acc_s[...] += part

    @pl.when(f == pl.num_programs(1) - 1)
    def _():
        o_ref[...] = _layer_norm(DEEPNORM_ALPHA * h_ref[...] + acc_s[...], g_ref[...], b_ref[...])


def _ffn(h_bf, h, w1, w3, w2, g, b):
    T, D = h.shape
    F = w1.shape[1]
    return pl.pallas_call(
        _ffn_kernel,
        grid=(T // FFN_TM, F // FFN_TF),
        in_specs=[
            pl.BlockSpec((FFN_TM, D), lambda i, f: (i, 0)),
            pl.BlockSpec((FFN_TM, D), lambda i, f: (i, 0)),
            pl.BlockSpec((D, FFN_TF), lambda i, f: (0, f)),
            pl.BlockSpec((D, FFN_TF), lambda i, f: (0, f)),
            pl.BlockSpec((FFN_TF, D), lambda i, f: (f, 0)),
            pl.BlockSpec((1, D), lambda i, f: (0, 0)),
            pl.BlockSpec((1, D), lambda i, f: (0, 0)),
        ],
        out_specs=pl.BlockSpec((FFN_TM, D), lambda i, f: (i, 0)),
        out_shape=jax.ShapeDtypeStruct((T, D), F32),
        scratch_shapes=[pltpu.VMEM((FFN_TM, D), F32)],
        compiler_params=pltpu.CompilerParams(
            dimension_semantics=("parallel", "arbitrary"), vmem_limit_bytes=VMEM_LIMIT),
        name="ffn",
    )(h_bf, h, w1, w3, w2, g, b)


def _rope_tables(S):
    half = ROPE_DIM // 2
    pos = jnp.arange(S, dtype=F32)
    inv = jnp.power(ROPE_THETA, -jnp.arange(0, ROPE_DIM, 2, dtype=F32) / ROPE_DIM)
    ang = pos[:, None] * inv[None, :]
    cos, sin = jnp.cos(ang), jnp.sin(ang)
    cos_f = jnp.concatenate([cos, cos, jnp.ones((S, LANES - ROPE_DIM), F32)], axis=1)
    sin_lo = jnp.concatenate([-sin, jnp.zeros((S, LANES - half), F32)], axis=1)
    sin_hi = jnp.concatenate([jnp.zeros((S, half), F32), sin, jnp.zeros((S, LANES - ROPE_DIM), F32)], axis=1)
    return cos_f, sin_lo, sin_hi


def _selection_constants(S):
    n_cmp = (S - CMP_BLOCK) // CMP_STRIDE + 1
    nb = S // SLC_BLOCK
    c_start = np.arange(n_cmp) * CMP_STRIDE
    b_start = np.arange(nb) * SLC_BLOCK
    overlap = ((c_start[:, None] < b_start[None, :] + SLC_BLOCK) &
               (b_start[None, :] < c_start[:, None] + CMP_BLOCK)).astype(np.float32)
    ovt = np.zeros((LANES, N_CMP_PAD), np.float32)
    ovt[:nb, :n_cmp] = overlap.T
    blk_onehot = ((np.arange(S) // SLC_BLOCK)[:, None] == np.arange(LANES)[None, :]).astype(np.float32)
    return jnp.asarray(ovt, BF16), jnp.asarray(blk_onehot, BF16)


def _regroup_w_in(w):
    o_glr = 2 * GLA_HEADS * GLA_DK + 2 * GLA_HEADS * GLA_DV
    o_nq = o_glr + GLA_GATE_RANK
    o_gate = o_nq + NSA_HEADS * NSA_HD + 6 * NSA_KV_GROUPS * NSA_HD
    n_gate = NSA_HEADS * 3
    used = o_gate + n_gate
    assert w.shape[1] == used
    pad = jnp.zeros((w.shape[0], U_WIDTH - used), w.dtype)
    return jnp.concatenate([w[:, :o_glr], w[:, o_nq:o_gate], w[:, o_glr:o_nq], w[:, o_gate:], pad], axis=1)


def kernel(x, w_in, gla_gate_w2, gla_gate_b2, gla_norm_w, cmp_k_pos, cmp_k_w1, cmp_k_w2,
           cmp_v_pos, cmp_v_w1, cmp_v_w2, w_out, ln1_g, ln1_b, ffn_w1, ffn_w3, ffn_w2, ln2_g, ln2_b):
    B, S, D = x.shape
    T = B * S
    cos_f, sin_lo, sin_hi = _rope_tables(S)
    ovt, blk_onehot = _selection_constants(S)

    x2 = x.reshape(T, D)
    for l in range(DEPTH):
        u = _in_proj(x2.astype(BF16), _regroup_w_in(w_in[l]).astype(BF16))
        u3 = u.reshape(B, S, U_WIDTH)
        o_gla = _gla(u3, gla_gate_w2[l], gla_gate_b2[l][None, :], gla_norm_w[l][None, :])
        cmp_kv = _compress(u3, jnp.stack([cmp_k_pos[l], cmp_v_pos[l]]),
                           jnp.stack([cmp_k_w1[l], cmp_v_w1[l]]).astype(BF16),
                           jnp.stack([cmp_k_w2[l], cmp_v_w2[l]]).astype(BF16))
        o_nsa = _nsa(u3, cmp_kv, cos_f, sin_lo, sin_hi, ovt, blk_onehot)
        h, h_bf = _out_proj(o_gla.reshape(T, GLA_WIDTH), o_nsa.reshape(T, NSA_WIDTH), x2,
                            w_out[l].astype(BF16), ln1_g[l][None, :], ln1_b[l][None, :])
        x2 = _ffn(h_bf, h, ffn_w1[l].astype(BF16), ffn_w3[l].astype(BF16), ffn_w2[l].astype(BF16),
                  ln2_g[l][None, :], ln2_b[l][None, :])
    return x2.reshape(B, S, D)
```

```python
import numpy as np
import jax
import jax.numpy as jnp
from jax import lax
from jax.experimental import pallas as pl
from jax.experimental.pallas import tpu as pltpu

F32 = jnp.float32
BF16 = jnp.bfloat16

D_MODEL = 2048
DEPTH = 1
GLA_HEADS = 4
GLA_DK = 128
GLA_DV = 256
GLA_GATE_RANK = 16
GLA_GATE_NORM = 16.0
GLA_CHUNK = 64
NSA_HEADS = 8
NSA_HD = 128
NSA_KV_GROUPS = 2
NSA_HPG = NSA_HEADS // NSA_KV_GROUPS
CMP_BLOCK = 32
CMP_STRIDE = 16
CMP_HIDDEN = 2 * NSA_HD
SLC_BLOCK = 64
SLC_TOPK = 16
WINDOW = 512
ROPE_THETA = 500000.0
ROPE_DIM = NSA_HD // 4
FFN_HIDDEN = 5632
DEEPNORM_ALPHA = (2.0 * DEPTH) ** 0.25
LN_EPS = 1e-5
FORCED_SCORE = 1e4
INVALID_SCORE = -1e4

GLA_WIDTH = GLA_HEADS * GLA_DV
NSA_WIDTH = NSA_HEADS * NSA_HD

LANES = 128
VMEM_LIMIT = 56 * 1024 * 1024

CB_GQ = 0
CB_GK = 4
CB_GV = 8
CB_GO = 16
CB_NQ = 24
CB_KC = 32
CB_KS = 36
CB_VS = 38
CB_KW = 40
CB_VW = 42
CB_MISC = 44
MISC_GATE_LANE = GLA_GATE_RANK
U_WIDTH = 45 * LANES

REGROUP_TR = 256
IN_TM, IN_TN = 512, U_WIDTH // 3
OUT_TM = 512
FFN_TM, FFN_TF = 512, 512
NSA_TQ = 128
NSA_SUB = 2
NSA_TK = 512
WIN_SPAN = WINDOW + NSA_TQ
MASKED = -1e30
LOG2E = 1.4426950408889634


def _nt_dot(a, b):
    return lax.dot_general(a, b, (((1,), (1,)), ((), ())), preferred_element_type=F32)


def _tn_dot(a, b):
    return lax.dot_general(a, b, (((0,), (0,)), ((), ())), preferred_element_type=F32)


def _dot(a, b):
    return jnp.dot(a, b, preferred_element_type=F32)


def _layer_norm(z, g, b):
    mu = jnp.mean(z, axis=-1, keepdims=True)
    zc = z - mu
    var = jnp.mean(zc * zc, axis=-1, keepdims=True)
    return zc * lax.rsqrt(var + LN_EPS) * g + b


def _masked_softmax(s, mask):
    sm = jnp.where(mask, s, -jnp.inf)
    m = jnp.max(sm, axis=-1, keepdims=True)
    m = jnp.where(m == -jnp.inf, 0.0, m)
    e = jnp.where(mask, jnp.exp(s - m), 0.0)
    den = jnp.sum(e, axis=-1, keepdims=True)
    return e * (1.0 / jnp.where(den > 0, den, 1.0))


def _regroup_kernel(w_ref, o_ref):
    rows = w_ref.shape[0]
    o_glr = 2 * GLA_HEADS * GLA_DK + 2 * GLA_HEADS * GLA_DV
    o_nq = o_glr + GLA_GATE_RANK
    o_gate = o_nq + NSA_HEADS * NSA_HD + 6 * NSA_KV_GROUPS * NSA_HD
    n_gate = NSA_HEADS * 3
    o_ref[:, 0:o_glr] = w_ref[:, 0:o_glr].astype(BF16)
    o_ref[:, o_glr:CB_MISC * LANES] = w_ref[:, o_nq:o_gate].astype(BF16)
    misc = jnp.concatenate([w_ref[:, o_glr:o_nq], w_ref[:, o_gate:o_gate + n_gate],
                            jnp.zeros((rows, LANES - GLA_GATE_RANK - n_gate), F32)], axis=1)
    o_ref[:, CB_MISC * LANES:] = misc.astype(BF16)


def _regroup_w_in(w):
    D, n_in = w.shape
    return pl.pallas_call(
        _regroup_kernel,
        grid=(D // REGROUP_TR,),
        in_specs=[pl.BlockSpec((REGROUP_TR, n_in), lambda i: (i, 0))],
        out_specs=pl.BlockSpec((REGROUP_TR, U_WIDTH), lambda i: (i, 0)),
        out_shape=jax.ShapeDtypeStruct((D, U_WIDTH), BF16),
        compiler_params=pltpu.CompilerParams(
            dimension_semantics=("parallel",), vmem_limit_bytes=VMEM_LIMIT),
        name="regroup_w_in",
    )(w)


def _in_proj_kernel(x_ref, w_ref, o_ref):
    o_ref[...] = _dot(x_ref[...].astype(BF16), w_ref[...])


def _in_proj(x2, w_bf):
    T, D = x2.shape
    N = w_bf.shape[1]
    return pl.pallas_call(
        _in_proj_kernel,
        grid=(N // IN_TN, T // IN_TM),
        in_specs=[pl.BlockSpec((IN_TM, D), lambda j, i: (i, 0)),
                  pl.BlockSpec((D, IN_TN), lambda j, i: (0, j))],
        out_specs=pl.BlockSpec((IN_TM, IN_TN), lambda j, i: (i, j)),
        out_shape=jax.ShapeDtypeStruct((T, N), F32),
        compiler_params=pltpu.CompilerParams(
            dimension_semantics=("parallel", "parallel"), vmem_limit_bytes=VMEM_LIMIT),
        name="in_proj",
    )(x2, w_bf)


GLA_ROWBLK = 256
GLA_TS = 512


def _split3(a):
    hi = a.astype(BF16)
    r1 = a - hi.astype(F32)
    mid = r1.astype(BF16)
    lo = (r1 - mid.astype(F32)).astype(BF16)
    return hi, mid, lo


def _log_sigmoid(z):
    return jnp.minimum(z, 0.0) - jnp.log1p(jnp.exp(-jnp.abs(z)))


def _gla_kernel(q_ref, k_ref, v_ref, go_ref, misc_ref, w2_ref, b2_ref, nw_ref, o_ref,
                qd_s, ki_s, ks_s, dec_s, state_s):
    TS = q_ref.shape[0]
    H, C, DK, DV = GLA_HEADS, GLA_CHUNK, GLA_DK, GLA_DV
    scale = DK ** -0.5

    @pl.when(pl.program_id(1) == 0)
    def _():
        state_s[...] = jnp.zeros(state_s.shape, F32)

    r = lax.broadcasted_iota(jnp.int32, (GLA_ROWBLK, GLA_ROWBLK), 0)
    c = lax.broadcasted_iota(jnp.int32, (GLA_ROWBLK, GLA_ROWBLK), 1)
    same = (r // C) == (c // C)
    cum_m = jnp.where(same & (c <= r), 1.0, 0.0).astype(BF16)

    w2 = w2_ref[...].astype(BF16)
    b2 = b2_ref[...]
    for rb in range(TS // GLA_ROWBLK):
        rows = slice(rb * GLA_ROWBLK, (rb + 1) * GLA_ROWBLK)
        glr = misc_ref[rows, 0:GLA_GATE_RANK].astype(BF16)
        gk = _log_sigmoid(_dot(glr, w2) + b2) * (1.0 / GLA_GATE_NORM)
        hi, mid, lo = _split3(gk)
        bc_all = _dot(cum_m, hi) + _dot(cum_m, mid) + _dot(cum_m, lo)
        bl_all = jnp.concatenate(
            [jnp.broadcast_to(bc_all[j * C + C - 1:j * C + C, :], (C, H * DK)) for j in range(GLA_ROWBLK // C)],
            axis=0)
        for h in range(H):
            hk = slice(h * DK, (h + 1) * DK)
            bc = bc_all[:, hk]
            bl = bl_all[:, hk]
            q = q_ref[rows, hk] * scale
            k = k_ref[rows, hk]
            qd_s[h, rows, :] = (q * jnp.exp(bc)).astype(BF16)
            ki_s[h, rows, :] = (k * jnp.exp(-bc)).astype(BF16)
            ks_s[h, rows, :] = (k * jnp.exp(bl - bc)).astype(BF16)
            dec_s[h, rows, :] = jnp.exp(bl)

    ri = lax.broadcasted_iota(jnp.int32, (C, C), 0)
    ci = lax.broadcasted_iota(jnp.int32, (C, C), 1)
    causal = ci <= ri
    nw = nw_ref[...]

    def chunk(cidx, _):
        r0 = pl.multiple_of(cidx * C, C)
        rows = pl.ds(r0, C)
        for h in range(H):
            hv = slice(h * DV, (h + 1) * DV)
            qd = qd_s[h, rows, :]
            v = v_ref[rows, hv].astype(BF16)
            state_t = state_s[h]
            attn = jnp.where(causal, _nt_dot(qd, ki_s[h, rows, :]), 0.0).astype(BF16)
            o = _dot(attn, v) + _nt_dot(qd, state_t.astype(BF16))
            state_s[h] = state_t * dec_s[h, pl.ds(r0, 1), :] + _tn_dot(v, ks_s[h, rows, :])
            ms = jnp.mean(o * o, axis=-1, keepdims=True)
            o = o * lax.rsqrt(ms + LN_EPS) * nw
            go = go_ref[rows, hv]
            o_ref[rows, hv] = (o * (go * jax.nn.sigmoid(go))).astype(o_ref.dtype)
        return 0

    lax.fori_loop(0, TS // C, chunk, 0)


def _gla(u3, gate_w2, gate_b2, norm_w):
    B, S, _ = u3.shape
    H, DK, DV = GLA_HEADS, GLA_DK, GLA_DV
    qk_w, v_w = H * DK, H * DV
    tok = lambda w, off: pl.BlockSpec((None, GLA_TS, w), lambda b, s: (b, s, off * LANES // w))
    const = lambda shape: pl.BlockSpec(shape, lambda b, s: (0, 0))
    return pl.pallas_call(
        _gla_kernel,
        grid=(B, S // GLA_TS),
        in_specs=[tok(qk_w, CB_GQ), tok(qk_w, CB_GK), tok(v_w, CB_GV), tok(v_w, CB_GO), tok(LANES, CB_MISC),
                  const((GLA_GATE_RANK, qk_w)), const((1, qk_w)), const((1, DV))],
        out_specs=pl.BlockSpec((None, GLA_TS, v_w), lambda b, s: (b, s, 0)),
        out_shape=jax.ShapeDtypeStruct((B, S, GLA_WIDTH), BF16),
        scratch_shapes=[pltpu.VMEM((H, GLA_TS, DK), BF16), pltpu.VMEM((H, GLA_TS, DK), BF16),
                        pltpu.VMEM((H, GLA_TS, DK), BF16), pltpu.VMEM((H, GLA_TS, DK), F32),
                        pltpu.VMEM((H, DV, DK), F32)],
        compiler_params=pltpu.CompilerParams(
            dimension_semantics=("parallel", "arbitrary"), vmem_limit_bytes=VMEM_LIMIT),
        name="gla",
    )(u3, u3, u3, u3, u3, gate_w2, gate_b2, norm_w)


N_CMP_PAD = 128


def _gelu_tanh(x):
    return x * (0.5 * (1.0 + jnp.tanh(0.7978845608028654 * (x + 0.044715 * (x * x * x)))))


def _compress_kernel(kv_ref, pos_ref, w1_ref, w2_ref, o_ref):
    half = CMP_BLOCK // 2
    p0 = jnp.zeros((N_CMP_PAD, CMP_HIDDEN), F32)
    p1 = jnp.zeros((N_CMP_PAD, CMP_HIDDEN), F32)
    for l in range(half):
        x = kv_ref[pl.ds(l, N_CMP_PAD, stride=CMP_STRIDE), :]
        a0 = (x + pos_ref[l:l + 1, :]).astype(BF16)
        a1 = (x + pos_ref[half + l:half + l + 1, :]).astype(BF16)
        p0 = p0 + _dot(a0, w1_ref[l * NSA_HD:(l + 1) * NSA_HD, :])
        p1 = p1 + _dot(a1, w1_ref[(half + l) * NSA_HD:(half + l + 1) * NSA_HD, :])
    pre = p0 + pltpu.roll(p1, N_CMP_PAD - 1, 0)
    h = _gelu_tanh(pre).astype(BF16)
    out = _dot(h, w2_ref[...])
    row = lax.broadcasted_iota(jnp.int32, out.shape, 0)
    o_ref[...] = jnp.where(row < N_CMP_PAD - 1, out, 0.0).astype(o_ref.dtype)


def _compress(u3, pos, w1, w2):
    B, S, _ = u3.shape
    G = NSA_KV_GROUPS
    return pl.pallas_call(
        _compress_kernel,
        grid=(2, B, G),
        in_specs=[
            pl.BlockSpec((None, S, NSA_HD), lambda kv, b, g: (b, 0, CB_KC + 2 * kv + g)),
            pl.BlockSpec((None, CMP_BLOCK, NSA_HD), lambda kv, b, g: (kv, 0, 0)),
            pl.BlockSpec((None, CMP_BLOCK * NSA_HD, CMP_HIDDEN), lambda kv, b, g: (kv, 0, 0)),
            pl.BlockSpec((None, CMP_HIDDEN, NSA_HD), lambda kv, b, g: (kv, 0, 0)),
        ],
        out_specs=pl.BlockSpec((None, None, None, N_CMP_PAD, NSA_HD), lambda kv, b, g: (kv, b, g, 0, 0)),
        out_shape=jax.ShapeDtypeStruct((2, B, G, N_CMP_PAD, NSA_HD), BF16),
        compiler_params=pltpu.CompilerParams(
            dimension_semantics=("parallel", "parallel", "parallel"), vmem_limit_bytes=VMEM_LIMIT),
        name="compress",
    )(u3, pos, w1, w2)


def _rope(x, cos, sin_lo, sin_hi):
    half = ROPE_DIM // 2
    return x * cos + pltpu.roll(x, LANES - half, 1) * sin_lo + pltpu.roll(x, half, 1) * sin_hi


def _nsa_kernel(q_ref, misc_ref, kc_ref, vc_ref, ks_ref, vs_ref, kw_ref, vw_ref,
                cos_ref, sinlo_ref, sinhi_ref, ovt_ref, blk_ref, o_ref,
                ksa_s, vsa_s, kwr_s, vwa_s, ss_s, es_s, sw_s, ew_s):
    S = ks_ref.shape[0]
    R, TQ, HD = NSA_HPG, NSA_TQ, NSA_HD
    g = pl.program_id(1)
    qi = pl.program_id(2)
    scale = HD ** -0.5

    @pl.when(qi == 0)
    def _():
        def prep(rb, _):
            rows = pl.ds(pl.multiple_of(rb * 256, 256), 256)
            cos, slo, shi = cos_ref[rows, :], sinlo_ref[rows, :], sinhi_ref[rows, :]
            ones = jnp.ones((256, LANES), BF16)
            ksa_s[rows, 0:HD] = _rope(ks_ref[rows, :], cos, slo, shi).astype(BF16)
            ksa_s[rows, HD:] = blk_ref[rows, :]
            kwr_s[rows, :] = _rope(kw_ref[rows, :], cos, slo, shi).astype(BF16)
            vsa_s[rows, 0:HD] = vs_ref[rows, :].astype(BF16)
            vsa_s[rows, HD:] = ones
            vwa_s[rows, 0:HD] = vw_ref[rows, :].astype(BF16)
            vwa_s[rows, HD:] = ones
            return 0
        lax.fori_loop(0, S // 256, prep, 0)

    nb = S // SLC_BLOCK
    TK = NSA_TK
    ovt = ovt_ref[...]
    kc, vc = kc_ref[...], vc_ref[...]

    def front(u):
        tq0 = (qi * NSA_SUB + u) * TQ
        urows = slice(u * TQ, (u + 1) * TQ)
        trows = pl.ds(pl.multiple_of(tq0, TQ), TQ)
        cos_t, slo_t, shi_t = cos_ref[trows, :], sinlo_ref[trows, :], sinhi_ref[trows, :]
        heads = [q_ref[urows, r * HD:(r + 1) * HD] for r in range(R)]
        qc = jnp.concatenate([(h * scale).astype(BF16) for h in heads], axis=0)
        qr_heads = [_rope(h * (scale * LOG2E), cos_t, slo_t, shi_t).astype(BF16) for h in heads]
        t_tok = tq0 + lax.broadcasted_iota(jnp.int32, (TQ, 1), 0)
        t_col = jnp.concatenate([t_tok] * R, axis=0)

        s_c = _nt_dot(qc, kc)
        n_idx = lax.broadcasted_iota(jnp.int32, (1, N_CMP_PAD), 1)
        p_c = _masked_softmax(s_c, (n_idx * CMP_STRIDE + (CMP_BLOCK - 1)) <= t_col)
        p_cb = p_c.astype(BF16)
        o_cmp = _dot(p_cb, vc)

        imp_t = _nt_dot(ovt, p_cb[0:TQ])
        for r in range(1, R):
            imp_t = imp_t + _nt_dot(ovt, p_cb[r * TQ:(r + 1) * TQ])
        imp = imp_t[0:nb, :]
        j_idx = lax.broadcasted_iota(jnp.int32, (nb, TQ), 0)
        cur = (tq0 + lax.broadcasted_iota(jnp.int32, (nb, TQ), 1)) // SLC_BLOCK
        imp = jnp.where(j_idx <= cur, imp, INVALID_SCORE)
        imp = jnp.where((j_idx == 0) | (j_idx == cur) | (j_idx == cur - 1), FORCED_SCORE, imp)
        rank = jnp.zeros((nb, TQ), jnp.int32)
        for i in range(nb):
            ri = imp[i:i + 1, :]
            ahead = (ri > imp) | ((ri == imp) & (i < j_idx))
            rank = rank + jnp.where(ahead, 1, 0)
        blk_bias_t = jnp.where(rank < min(SLC_TOPK, nb), 0.0, MASKED)
        blk_bias_t = jnp.concatenate([blk_bias_t, jnp.zeros((LANES - nb, TQ), F32)], axis=0)
        blk_bias = jnp.transpose(blk_bias_t).astype(BF16)
        q_aug = jnp.concatenate([jnp.concatenate([h, blk_bias], axis=1) for h in qr_heads], axis=0)
        qr = jnp.concatenate(qr_heads, axis=0)
        return dict(tq0=tq0, t_tok=t_tok, t_col=t_col, o_cmp=o_cmp, q_aug=q_aug, qr=qr)

    subs = [front(u) for u in range(NSA_SUB)]

    def sel_tile(u, kt, carry, diagonal):
        m, acc = carry
        k0 = pl.multiple_of(kt * TK, TK)
        s = _nt_dot(subs[u]["q_aug"], ksa_s[pl.ds(k0, TK), :])
        if diagonal:
            kpos = k0 + lax.broadcasted_iota(jnp.int32, (1, TK), 1)
            s = jnp.where(kpos <= subs[u]["t_col"], s, MASKED)
        ss_s[u] = s
        m_new = jnp.maximum(m, jnp.max(ss_s[u], axis=-1, keepdims=True))
        alpha = jnp.exp2(m - m_new)
        es_s[u] = jnp.exp2(ss_s[u] - m_new).astype(BF16)
        acc = alpha * acc + _dot(es_s[u], vsa_s[pl.ds(k0, TK), :])
        return m_new, acc

    def sel_step(kt, carries, diagonal):
        return tuple(sel_tile(u, kt, carries[u], diagonal) for u in range(NSA_SUB))

    last = (qi * NSA_SUB * TQ) // TK
    init = tuple((jnp.full((R * TQ, 1), -jnp.inf, F32), jnp.zeros((R * TQ, 2 * HD), F32))
                 for _ in range(NSA_SUB))
    carries = lax.fori_loop(0, last, lambda kt, c: sel_step(kt, c, False), init)
    carries = sel_step(last, carries, True)

    misc = misc_ref[...]
    ng = 3 * R
    logits = jnp.where(g == 0, misc[:, MISC_GATE_LANE:MISC_GATE_LANE + ng],
                       misc[:, MISC_GATE_LANE + ng:MISC_GATE_LANE + 2 * ng])
    gate = jax.nn.sigmoid(logits)

    for u in range(NSA_SUB):
        sub = subs[u]
        acc_s = carries[u][1]
        o_slc = acc_s[:, 0:HD] * (1.0 / acc_s[:, HD:])

        start = pl.multiple_of(jnp.maximum(sub["tq0"] - WINDOW, 0), TQ)
        s_w = _nt_dot(sub["qr"], kwr_s[pl.ds(start, WIN_SPAN), :])
        diff = sub["t_tok"] - (start + lax.broadcasted_iota(jnp.int32, (1, WIN_SPAN), 1))
        band = jnp.where((diff >= 0) & (diff < WINDOW), 0.0, MASKED)
        for r in range(R):
            sw_s[u, r * TQ:(r + 1) * TQ, :] = s_w[r * TQ:(r + 1) * TQ] + band
        m_w = jnp.max(sw_s[u], axis=-1, keepdims=True)
        ew_s[u] = jnp.exp2(sw_s[u] - m_w).astype(BF16)
        acc_w = _dot(ew_s[u], vwa_s[pl.ds(start, WIN_SPAN), :])
        o_win = acc_w[:, 0:HD] * (1.0 / acc_w[:, HD:])

        urows = slice(u * TQ, (u + 1) * TQ)
        for r in range(R):
            rs = slice(r * TQ, (r + 1) * TQ)
            o = (gate[urows, 3 * r:3 * r + 1] * sub["o_cmp"][rs] + gate[urows, 3 * r + 1:3 * r + 2] * o_slc[rs]
                 + gate[urows, 3 * r + 2:3 * r + 3] * o_win[rs])
            o_ref[urows, r * HD:(r + 1) * HD] = o.astype(o_ref.dtype)


def _nsa(u3, cmp_kv, cos_f, sin_lo, sin_hi, ovt, blk_onehot):
    B, S, _ = u3.shape
    G, R, HD = NSA_KV_GROUPS, NSA_HPG, NSA_HD
    qb = R * HD // LANES
    rows = R * NSA_TQ
    step = NSA_SUB * NSA_TQ
    assert NSA_TK % step == 0
    full = lambda cb: pl.BlockSpec((None, S, HD), lambda b, g, i: (b, 0, cb + g))
    table = pl.BlockSpec((S, LANES), lambda b, g, i: (0, 0))
    return pl.pallas_call(
        _nsa_kernel,
        grid=(B, G, S // step),
        in_specs=[
            pl.BlockSpec((None, step, R * HD), lambda b, g, i: (b, i, CB_NQ // qb + g)),
            pl.BlockSpec((None, step, LANES), lambda b, g, i: (b, i, CB_MISC)),
            pl.BlockSpec((None, None, None, N_CMP_PAD, HD), lambda b, g, i: (0, b, g, 0, 0)),
            pl.BlockSpec((None, None, None, N_CMP_PAD, HD), lambda b, g, i: (1, b, g, 0, 0)),
            full(CB_KS), full(CB_VS), full(CB_KW), full(CB_VW),
            table, table, table,
            pl.BlockSpec((LANES, N_CMP_PAD), lambda b, g, i: (0, 0)),
            table,
        ],
        out_specs=pl.BlockSpec((None, step, R * HD), lambda b, g, i: (b, i, g)),
        out_shape=jax.ShapeDtypeStruct((B, S, NSA_WIDTH), BF16),
        scratch_shapes=[pltpu.VMEM((S, 2 * HD), BF16), pltpu.VMEM((S, 2 * HD), BF16),
                        pltpu.VMEM((S, HD), BF16), pltpu.VMEM((S, 2 * HD), BF16),
                        pltpu.VMEM((NSA_SUB, rows, NSA_TK), F32), pltpu.VMEM((NSA_SUB, rows, NSA_TK), BF16),
                        pltpu.VMEM((NSA_SUB, rows, WIN_SPAN), F32), pltpu.VMEM((NSA_SUB, rows, WIN_SPAN), BF16)],
        compiler_params=pltpu.CompilerParams(
            dimension_semantics=("parallel", "parallel", "arbitrary"), vmem_limit_bytes=VMEM_LIMIT),
        name="nsa",
    )(u3, u3, cmp_kv, cmp_kv, u3, u3, u3, u3, cos_f, sin_lo, sin_hi, ovt, blk_onehot)


def _out_proj_kernel(og_ref, on_ref, x_ref, w_ref, g_ref, b_ref, h_ref, hb_ref):
    mix = _dot(og_ref[...], w_ref[0:GLA_WIDTH, :]) + _dot(on_ref[...], w_ref[GLA_WIDTH:, :])
    h = _layer_norm(DEEPNORM_ALPHA * x_ref[...] + mix, g_ref[...], b_ref[...])
    h_ref[...] = h
    hb_ref[...] = h.astype(BF16)


def _out_proj(o_gla, o_nsa, x2, w_bf, g, b):
    T, D = x2.shape
    tile = lambda w: pl.BlockSpec((OUT_TM, w), lambda i: (i, 0))
    const = lambda s: pl.BlockSpec(s, lambda i: (0, 0))
    return pl.pallas_call(
        _out_proj_kernel,
        grid=(T // OUT_TM,),
        in_specs=[tile(GLA_WIDTH), tile(NSA_WIDTH), tile(D), const(w_bf.shape), const((1, D)), const((1, D))],
        out_specs=[tile(D), tile(D)],
        out_shape=[jax.ShapeDtypeStruct((T, D), F32), jax.ShapeDtypeStruct((T, D), BF16)],
        compiler_params=pltpu.CompilerParams(
            dimension_semantics=("parallel",), vmem_limit_bytes=VMEM_LIMIT),
        name="out_proj",
    )(o_gla, o_nsa, x2, w_bf, g, b)


def _ffn_kernel(hb_ref, h_ref, w1_ref, w3_ref, w2_ref, g_ref, b_ref, o_ref, acc_s):
    f = pl.program_id(1)

    @pl.when(f == 0)
    def _():
        acc_s[...] = jnp.zeros(acc_s.shape, F32)

    hb = hb_ref[...]
    a = _dot(hb, w1_ref[...])
    c = _dot(hb, w3_ref[...])
    acc_s[...] += _dot((a * jax.nn.sigmoid(a) * c).astype(BF16), w2_ref[...])

    @pl.when(f == pl.num_programs(1) - 1)
    def _():
        o_ref[...] = _layer_norm(DEEPNORM_ALPHA * h_ref[...] + acc_s[...], g_ref[...], b_ref[...])


def _ffn(h_bf, h, w1, w3, w2, g, b):
    T, D = h.shape
    F = w1.shape[1]
    return pl.pallas_call(
        _ffn_kernel,
        grid=(T // FFN_TM, F // FFN_TF),
        in_specs=[
            pl.BlockSpec((FFN_TM, D), lambda i, f: (i, 0)),
            pl.BlockSpec((FFN_TM, D), lambda i, f: (i, 0)),
            pl.BlockSpec((D, FFN_TF), lambda i, f: (0, f)),
            pl.BlockSpec((D, FFN_TF), lambda i, f: (0, f)),
            pl.BlockSpec((FFN_TF, D), lambda i, f: (f, 0)),
            pl.BlockSpec((1, D), lambda i, f: (0, 0)),
            pl.BlockSpec((1, D), lambda i, f: (0, 0)),
        ],
        out_specs=pl.BlockSpec((FFN_TM, D), lambda i, f: (i, 0)),
        out_shape=jax.ShapeDtypeStruct((T, D), F32),
        scratch_shapes=[pltpu.VMEM((FFN_TM, D), F32)],
        compiler_params=pltpu.CompilerParams(
            dimension_semantics=("parallel", "arbitrary"), vmem_limit_bytes=VMEM_LIMIT),
        name="ffn",
    )(h_bf, h, w1, w3, w2, g, b)


def _rope_tables(S):
    half = ROPE_DIM // 2
    pos = jnp.arange(S, dtype=F32)
    inv = jnp.power(ROPE_THETA, -jnp.arange(0, ROPE_DIM, 2, dtype=F32) / ROPE_DIM)
    ang = pos[:, None] * inv[None, :]
    cos, sin = jnp.cos(ang), jnp.sin(ang)
    cos_f = jnp.concatenate([cos, cos, jnp.ones((S, LANES - ROPE_DIM), F32)], axis=1)
    sin_lo = jnp.concatenate([-sin, jnp.zeros((S, LANES - half), F32)], axis=1)
    sin_hi = jnp.concatenate([jnp.zeros((S, half), F32), sin, jnp.zeros((S, LANES - ROPE_DIM), F32)], axis=1)
    return cos_f, sin_lo, sin_hi


def _selection_constants(S):
    n_cmp = (S - CMP_BLOCK) // CMP_STRIDE + 1
    nb = S // SLC_BLOCK
    c_start = np.arange(n_cmp) * CMP_STRIDE
    b_start = np.arange(nb) * SLC_BLOCK
    overlap = ((c_start[:, None] < b_start[None, :] + SLC_BLOCK) &
               (b_start[None, :] < c_start[:, None] + CMP_BLOCK)).astype(np.float32)
    ovt = np.zeros((LANES, N_CMP_PAD), np.float32)
    ovt[:nb, :n_cmp] = overlap.T
    blk_onehot = ((np.arange(S) // SLC_BLOCK)[:, None] == np.arange(LANES)[None, :]).astype(np.float32)
    return jnp.asarray(ovt, BF16), jnp.asarray(blk_onehot, BF16)


def kernel(x, w_in, gla_gate_w2, gla_gate_b2, gla_norm_w, cmp_k_pos, cmp_k_w1, cmp_k_w2,
           cmp_v_pos, cmp_v_w1, cmp_v_w2, w_out, ln1_g, ln1_b, ffn_w1, ffn_w3, ffn_w2, ln2_g, ln2_b):
    B, S, D = x.shape
    T = B * S
    cos_f, sin_lo, sin_hi = _rope_tables(S)
    ovt, blk_onehot = _selection_constants(S)

    x2 = x.reshape(T, D)
    for l in range(DEPTH):
        u = _in_proj(x2, _regroup_w_in(w_in[l]))
        u3 = u.reshape(B, S, U_WIDTH)
        o_gla = _gla(u3, gla_gate_w2[l], gla_gate_b2[l][None, :], gla_norm_w[l][None, :])
        cmp_kv = _compress(u3, jnp.stack([cmp_k_pos[l], cmp_v_pos[l]]),
                           jnp.stack([cmp_k_w1[l], cmp_v_w1[l]]).astype(BF16),
                           jnp.stack([cmp_k_w2[l], cmp_v_w2[l]]).astype(BF16))
        o_nsa = _nsa(u3, cmp_kv, cos_f, sin_lo, sin_hi, ovt, blk_onehot)
        h, h_bf = _out_proj(o_gla.reshape(T, GLA_WIDTH), o_nsa.reshape(T, NSA_WIDTH), x2,
                            w_out[l].astype(BF16), ln1_g[l][None, :], ln1_b[l][None, :])
        x2 = _ffn(h_bf, h, ffn_w1[l].astype(BF16), ffn_w3[l].astype(BF16), ffn_w2[l].astype(BF16),
                  ln2_g[l][None, :], ln2_b[l][None, :])
    return x2.reshape(B, S, D)
```

```python
import numpy as np
import jax
import jax.numpy as jnp
from jax import lax
from jax.experimental import pallas as pl
from jax.experimental.pallas import tpu as pltpu

F32 = jnp.float32
BF16 = jnp.bfloat16

D_MODEL = 2048
DEPTH = 1
GLA_HEADS = 4
GLA_DK = 128
GLA_DV = 256
GLA_GATE_RANK = 16
GLA_GATE_NORM = 16.0
GLA_CHUNK = 64
NSA_HEADS = 8
NSA_HD = 128
NSA_KV_GROUPS = 2
NSA_HPG = NSA_HEADS // NSA_KV_GROUPS
CMP_BLOCK = 32
CMP_STRIDE = 16
CMP_HIDDEN = 2 * NSA_HD
SLC_BLOCK = 64
SLC_TOPK = 16
WINDOW = 512
ROPE_THETA = 500000.0
ROPE_DIM = NSA_HD // 4
FFN_HIDDEN = 5632
DEEPNORM_ALPHA = (2.0 * DEPTH) ** 0.25
LN_EPS = 1e-5
FORCED_SCORE = 1e4
INVALID_SCORE = -1e4

GLA_WIDTH = GLA_HEADS * GLA_DV
NSA_WIDTH = NSA_HEADS * NSA_HD

LANES = 128
VMEM_LIMIT = 56 * 1024 * 1024

CB_GQ = 0
CB_GK = 4
CB_GV = 8
CB_GO = 16
CB_NQ = 24
CB_KC = 32
CB_KS = 36
CB_VS = 38
CB_KW = 40
CB_VW = 42
CB_MISC = 44
MISC_GATE_LANE = GLA_GATE_RANK
U_WIDTH = 45 * LANES

REGROUP_TC = 256
IN_TM, IN_TN = 512, U_WIDTH // 3
OUT_TM = 512
FFN_TM, FFN_TF = 512, 512
NSA_TQ = 128
NSA_SUB = 2
NSA_TK = 512
WIN_SPAN = WINDOW + NSA_TQ
MASKED = -1e30
LOG2E = 1.4426950408889634


def _nt_dot(a, b):
    return lax.dot_general(a, b, (((1,), (1,)), ((), ())), preferred_element_type=F32)


def _tn_dot(a, b):
    return lax.dot_general(a, b, (((0,), (0,)), ((), ())), preferred_element_type=F32)


def _dot(a, b):
    return jnp.dot(a, b, preferred_element_type=F32)


def _layer_norm(z, g, b):
    mu = jnp.mean(z, axis=-1, keepdims=True)
    zc = z - mu
    var = jnp.mean(zc * zc, axis=-1, keepdims=True)
    return zc * lax.rsqrt(var + LN_EPS) * g + b


def _masked_softmax(s, mask):
    sm = jnp.where(mask, s, -jnp.inf)
    m = jnp.max(sm, axis=-1, keepdims=True)
    m = jnp.where(m == -jnp.inf, 0.0, m)
    e = jnp.where(mask, jnp.exp(s - m), 0.0)
    den = jnp.sum(e, axis=-1, keepdims=True)
    return e * (1.0 / jnp.where(den > 0, den, 1.0))


def _regroup_kernel(w_ref, o_ref):
    cols = w_ref.shape[1]
    o_glr = 2 * GLA_HEADS * GLA_DK + 2 * GLA_HEADS * GLA_DV
    o_nq = o_glr + GLA_GATE_RANK
    o_gate = o_nq + NSA_HEADS * NSA_HD + 6 * NSA_KV_GROUPS * NSA_HD
    n_gate = NSA_HEADS * 3
    o_ref[0:o_glr, :] = w_ref[0:o_glr, :].astype(BF16)
    o_ref[o_glr:CB_MISC * LANES, :] = w_ref[o_nq:o_gate, :].astype(BF16)
    misc = jnp.concatenate([w_ref[o_glr:o_nq, :], w_ref[o_gate:o_gate + n_gate, :],
                            jnp.zeros((LANES - GLA_GATE_RANK - n_gate, cols), F32)], axis=0)
    o_ref[CB_MISC * LANES:, :] = misc.astype(BF16)


def _regroup_w_in(wt):
    n_in, D = wt.shape
    return pl.pallas_call(
        _regroup_kernel,
        grid=(D // REGROUP_TC,),
        in_specs=[pl.BlockSpec((n_in, REGROUP_TC), lambda i: (0, i))],
        out_specs=pl.BlockSpec((U_WIDTH, REGROUP_TC), lambda i: (0, i)),
        out_shape=jax.ShapeDtypeStruct((U_WIDTH, D), BF16),
        compiler_params=pltpu.CompilerParams(
            dimension_semantics=("parallel",), vmem_limit_bytes=VMEM_LIMIT),
        name="regroup_w_in",
    )(wt)


def _in_proj_kernel(x_ref, w_ref, o_ref):
    o_ref[...] = _nt_dot(x_ref[...].astype(BF16), w_ref[...])


def _in_proj(x2, wt_bf):
    T, D = x2.shape
    N = wt_bf.shape[0]
    return pl.pallas_call(
        _in_proj_kernel,
        grid=(N // IN_TN, T // IN_TM),
        in_specs=[pl.BlockSpec((IN_TM, D), lambda j, i: (i, 0)),
                  pl.BlockSpec((IN_TN, D), lambda j, i: (j, 0))],
        out_specs=pl.BlockSpec((IN_TM, IN_TN), lambda j, i: (i, j)),
        out_shape=jax.ShapeDtypeStruct((T, N), F32),
        compiler_params=pltpu.CompilerParams(
            dimension_semantics=("parallel", "parallel"), vmem_limit_bytes=VMEM_LIMIT),
        name="in_proj",
    )(x2, wt_bf)


GLA_ROWBLK = 256
GLA_TS = 512
GLA_UNROLL = 4


def _split3(a):
    hi = a.astype(BF16)
    r1 = a - hi.astype(F32)
    mid = r1.astype(BF16)
    lo = (r1 - mid.astype(F32)).astype(BF16)
    return hi, mid, lo


def _log_sigmoid(z):
    return jnp.minimum(z, 0.0) - jnp.log1p(jnp.exp(-jnp.abs(z)))


def _gla_kernel(q_ref, k_ref, v_ref, go_ref, misc_ref, w2_ref, b2_ref, nw_ref, o_ref,
                qd_s, ki_s, ks_s, dec_s, state_s):
    TS = q_ref.shape[0]
    H, C, DK, DV = GLA_HEADS, GLA_CHUNK, GLA_DK, GLA_DV
    scale = DK ** -0.5

    @pl.when(pl.program_id(1) == 0)
    def _():
        state_s[...] = jnp.zeros(state_s.shape, F32)

    r = lax.broadcasted_iota(jnp.int32, (GLA_ROWBLK, GLA_ROWBLK), 0)
    c = lax.broadcasted_iota(jnp.int32, (GLA_ROWBLK, GLA_ROWBLK), 1)
    same = (r // C) == (c // C)
    cum_m = jnp.where(same & (c <= r), 1.0, 0.0).astype(BF16)

    w2 = w2_ref[...].astype(BF16)
    b2 = b2_ref[...]
    for rb in range(TS // GLA_ROWBLK):
        rows = slice(rb * GLA_ROWBLK, (rb + 1) * GLA_ROWBLK)
        glr = misc_ref[rows, 0:GLA_GATE_RANK].astype(BF16)
        gk = _log_sigmoid(_dot(glr, w2) + b2) * (1.0 / GLA_GATE_NORM)
        hi, mid, lo = _split3(gk)
        bc_all = _dot(cum_m, hi) + _dot(cum_m, mid) + _dot(cum_m, lo)
        bl_all = jnp.concatenate(
            [jnp.broadcast_to(bc_all[j * C + C - 1:j * C + C, :], (C, H * DK)) for j in range(GLA_ROWBLK // C)],
            axis=0)
        for h in range(H):
            hk = slice(h * DK, (h + 1) * DK)
            bc = bc_all[:, hk]
            bl = bl_all[:, hk]
            q = q_ref[rows, hk] * scale
            k = k_ref[rows, hk]
            qd_s[h, rows, :] = (q * jnp.exp(bc)).astype(BF16)
            ki_s[h, rows, :] = (k * jnp.exp(-bc)).astype(BF16)
            ks_s[h, rows, :] = (k * jnp.exp(bl - bc)).astype(BF16)
            dec_s[h, rows, :] = jnp.exp(bl)

    ri = lax.broadcasted_iota(jnp.int32, (C, C), 0)
    ci = lax.broadcasted_iota(jnp.int32, (C, C), 1)
    causal = ci <= ri
    nw = nw_ref[...]

    def chunk(cidx, _):
        r0 = pl.multiple_of(cidx * C, C)
        rows = pl.ds(r0, C)
        for h in range(H):
            hv = slice(h * DV, (h + 1) * DV)
            qd = qd_s[h, rows, :]
            v = v_ref[rows, hv].astype(BF16)
            state_t = state_s[h]
            attn = jnp.where(causal, _nt_dot(qd, ki_s[h, rows, :]), 0.0).astype(BF16)
            o = _dot(attn, v) + _nt_dot(qd, state_t.astype(BF16))
            state_s[h] = state_t * dec_s[h, pl.ds(r0, 1), :] + _tn_dot(v, ks_s[h, rows, :])
            ms = jnp.mean(o * o, axis=-1, keepdims=True)
            o = o * lax.rsqrt(ms + LN_EPS) * nw
            go = go_ref[rows, hv]
            o_ref[rows, hv] = (o * (go * jax.nn.sigmoid(go))).astype(o_ref.dtype)
        return 0

    lax.fori_loop(0, TS // C, chunk, 0, unroll=GLA_UNROLL)


def _gla(u3, gate_w2, gate_b2, norm_w):
    B, S, _ = u3.shape
    H, DK, DV = GLA_HEADS, GLA_DK, GLA_DV
    qk_w, v_w = H * DK, H * DV
    tok = lambda w, off: pl.BlockSpec((None, GLA_TS, w), lambda b, s: (b, s, off * LANES // w))
    const = lambda shape: pl.BlockSpec(shape, lambda b, s: (0, 0))
    return pl.pallas_call(
        _gla_kernel,
        grid=(B, S // GLA_TS),
        in_specs=[tok(qk_w, CB_GQ), tok(qk_w, CB_GK), tok(v_w, CB_GV), tok(v_w, CB_GO), tok(LANES, CB_MISC),
                  const((GLA_GATE_RANK, qk_w)), const((1, qk_w)), const((1, DV))],
        out_specs=pl.BlockSpec((None, GLA_TS, v_w), lambda b, s: (b, s, 0)),
        out_shape=jax.ShapeDtypeStruct((B, S, GLA_WIDTH), BF16),
        scratch_shapes=[pltpu.VMEM((H, GLA_TS, DK), BF16), pltpu.VMEM((H, GLA_TS, DK), BF16),
                        pltpu.VMEM((H, GLA_TS, DK), BF16), pltpu.VMEM((H, GLA_TS, DK), F32),
                        pltpu.VMEM((H, DV, DK), F32)],
        compiler_params=pltpu.CompilerParams(
            dimension_semantics=("parallel", "arbitrary"), vmem_limit_bytes=VMEM_LIMIT),
        name="gla",
    )(u3, u3, u3, u3, u3, gate_w2, gate_b2, norm_w)


N_CMP_PAD = 128


def _gelu_tanh(x):
    return x * (0.5 * (1.0 + jnp.tanh(0.7978845608028654 * (x + 0.044715 * (x * x * x)))))


def _compress_kernel(kv_ref, pos_ref, w1_ref, w2_ref, o_ref):
    half = CMP_BLOCK // 2
    p0 = jnp.zeros((N_CMP_PAD, CMP_HIDDEN), F32)
    p1 = jnp.zeros((N_CMP_PAD, CMP_HIDDEN), F32)
    for l in range(half):
        x = kv_ref[pl.ds(l, N_CMP_PAD, stride=CMP_STRIDE), :]
        a0 = (x + pos_ref[l:l + 1, :]).astype(BF16)
        a1 = (x + pos_ref[half + l:half + l + 1, :]).astype(BF16)
        p0 = p0 + _dot(a0, w1_ref[l * NSA_HD:(l + 1) * NSA_HD, :])
        p1 = p1 + _dot(a1, w1_ref[(half + l) * NSA_HD:(half + l + 1) * NSA_HD, :])
    pre = p0 + pltpu.roll(p1, N_CMP_PAD - 1, 0)
    h = _gelu_tanh(pre).astype(BF16)
    out = _dot(h, w2_ref[...])
    row = lax.broadcasted_iota(jnp.int32, out.shape, 0)
    o_ref[...] = jnp.where(row < N_CMP_PAD - 1, out, 0.0).astype(o_ref.dtype)


def _compress(u3, pos, w1, w2):
    B, S, _ = u3.shape
    G = NSA_KV_GROUPS
    return pl.pallas_call(
        _compress_kernel,
        grid=(2, B, G),
        in_specs=[
            pl.BlockSpec((None, S, NSA_HD), lambda kv, b, g: (b, 0, CB_KC + 2 * kv + g)),
            pl.BlockSpec((None, CMP_BLOCK, NSA_HD), lambda kv, b, g: (kv, 0, 0)),
            pl.BlockSpec((None, CMP_BLOCK * NSA_HD, CMP_HIDDEN), lambda kv, b, g: (kv, 0, 0)),
            pl.BlockSpec((None, CMP_HIDDEN, NSA_HD), lambda kv, b, g: (kv, 0, 0)),
        ],
        out_specs=pl.BlockSpec((None, None, None, N_CMP_PAD, NSA_HD), lambda kv, b, g: (kv, b, g, 0, 0)),
        out_shape=jax.ShapeDtypeStruct((2, B, G, N_CMP_PAD, NSA_HD), BF16),
        compiler_params=pltpu.CompilerParams(
            dimension_semantics=("parallel", "parallel", "parallel"), vmem_limit_bytes=VMEM_LIMIT),
        name="compress",
    )(u3, pos, w1, w2)


def _rope(x, cos, sin_lo, sin_hi):
    half = ROPE_DIM // 2
    return x * cos + pltpu.roll(x, LANES - half, 1) * sin_lo + pltpu.roll(x, half, 1) * sin_hi


def _nsa_kernel(q_ref, misc_ref, kc_ref, vc_ref, ks_ref, vs_ref, kw_ref, vw_ref,
                cos_ref, sinlo_ref, sinhi_ref, ovt_ref, blk_ref, o_ref,
                ksa_s, vsa_s, kwr_s, vwa_s, ss_s, es_s, sw_s, ew_s):
    S = ks_ref.shape[0]
    R, TQ, HD = NSA_HPG, NSA_TQ, NSA_HD
    g = pl.program_id(1)
    qi = pl.program_id(2)
    scale = HD ** -0.5

    @pl.when(qi == 0)
    def _():
        def prep(rb, _):
            rows = pl.ds(pl.multiple_of(rb * 256, 256), 256)
            cos, slo, shi = cos_ref[rows, :], sinlo_ref[rows, :], sinhi_ref[rows, :]
            ones = jnp.ones((256, LANES), BF16)
            ksa_s[rows, 0:HD] = _rope(ks_ref[rows, :], cos, slo, shi).astype(BF16)
            ksa_s[rows, HD:] = blk_ref[rows, :]
            kwr_s[rows, :] = _rope(kw_ref[rows, :], cos, slo, shi).astype(BF16)
            vsa_s[rows, 0:HD] = vs_ref[rows, :].astype(BF16)
            vsa_s[rows, HD:] = ones
            vwa_s[rows, 0:HD] = vw_ref[rows, :].astype(BF16)
            vwa_s[rows, HD:] = ones
            return 0
        lax.fori_loop(0, S // 256, prep, 0)

    nb = S // SLC_BLOCK
    TK = NSA_TK
    ovt = ovt_ref[...]
    kc, vc = kc_ref[...], vc_ref[...]

    def front(u):
        tq0 = (qi * NSA_SUB + u) * TQ
        urows = slice(u * TQ, (u + 1) * TQ)
        trows = pl.ds(pl.multiple_of(tq0, TQ), TQ)
        cos_t, slo_t, shi_t = cos_ref[trows, :], sinlo_ref[trows, :], sinhi_ref[trows, :]
        heads = [q_ref[urows, r * HD:(r + 1) * HD] for r in range(R)]
        qc = jnp.concatenate([(h * scale).astype(BF16) for h in heads], axis=0)
        qr_heads = [_rope(h * (scale * LOG2E), cos_t, slo_t, shi_t).astype(BF16) for h in heads]
        t_tok = tq0 + lax.broadcasted_iota(jnp.int32, (TQ, 1), 0)
        t_col = jnp.concatenate([t_tok] * R, axis=0)

        s_c = _nt_dot(qc, kc)
        n_idx = lax.broadcasted_iota(jnp.int32, (1, N_CMP_PAD), 1)
        p_c = _masked_softmax(s_c, (n_idx * CMP_STRIDE + (CMP_BLOCK - 1)) <= t_col)
        p_cb = p_c.astype(BF16)
        o_cmp = _dot(p_cb, vc)

        imp_t = _nt_dot(ovt, p_cb[0:TQ])
        for r in range(1, R):
            imp_t = imp_t + _nt_dot(ovt, p_cb[r * TQ:(r + 1) * TQ])
        imp = imp_t[0:nb, :]
        j_idx = lax.broadcasted_iota(jnp.int32, (nb, TQ), 0)
        cur = (tq0 + lax.broadcasted_iota(jnp.int32, (nb, TQ), 1)) // SLC_BLOCK
        imp = jnp.where(j_idx <= cur, imp, INVALID_SCORE)
        imp = jnp.where((j_idx == 0) | (j_idx == cur) | (j_idx == cur - 1), FORCED_SCORE, imp)
        rank = jnp.zeros((nb, TQ), jnp.int32)
        for i in range(nb):
            ri = imp[i:i + 1, :]
            ahead = (ri > imp) | ((ri == imp) & (i < j_idx))
            rank = rank + jnp.where(ahead, 1, 0)
        blk_bias_t = jnp.where(rank < min(SLC_TOPK, nb), 0.0, MASKED)
        blk_bias_t = jnp.concatenate([blk_bias_t, jnp.zeros((LANES - nb, TQ), F32)], axis=0)
        blk_bias = jnp.transpose(blk_bias_t).astype(BF16)
        q_aug = jnp.concatenate([jnp.concatenate([h, blk_bias], axis=1) for h in qr_heads], axis=0)
        qr = jnp.concatenate(qr_heads, axis=0)
        return dict(tq0=tq0, t_tok=t_tok, t_col=t_col, o_cmp=o_cmp, q_aug=q_aug, qr=qr)

    subs = [front(u) for u in range(NSA_SUB)]

    def sel_tile(u, kt, carry, diagonal):
        m, acc = carry
        k0 = pl.multiple_of(kt * TK, TK)
        s = _nt_dot(subs[u]["q_aug"], ksa_s[pl.ds(k0, TK), :])
        if diagonal:
            kpos = k0 + lax.broadcasted_iota(jnp.int32, (1, TK), 1)
            s = jnp.where(kpos <= subs[u]["t_col"], s, MASKED)
        ss_s[u] = s
        m_new = jnp.maximum(m, jnp.max(ss_s[u], axis=-1, keepdims=True))
        alpha = jnp.exp2(m - m_new)
        es_s[u] = jnp.exp2(ss_s[u] - m_new).astype(BF16)
        acc = alpha * acc + _dot(es_s[u], vsa_s[pl.ds(k0, TK), :])
        return m_new, acc

    def sel_step(kt, carries, diagonal):
        return tuple(sel_tile(u, kt, carries[u], diagonal) for u in range(NSA_SUB))

    last = (qi * NSA_SUB * TQ) // TK
    init = tuple((jnp.full((R * TQ, 1), -jnp.inf, F32), jnp.zeros((R * TQ, 2 * HD), F32))
                 for _ in range(NSA_SUB))
    carries = lax.fori_loop(0, last, lambda kt, c: sel_step(kt, c, False), init)
    carries = sel_step(last, carries, True)

    misc = misc_ref[...]
    ng = 3 * R
    logits = jnp.where(g == 0, misc[:, MISC_GATE_LANE:MISC_GATE_LANE + ng],
                       misc[:, MISC_GATE_LANE + ng:MISC_GATE_LANE + 2 * ng])
    gate = jax.nn.sigmoid(logits)

    for u in range(NSA_SUB):
        sub = subs[u]
        acc_s = carries[u][1]
        o_slc = acc_s[:, 0:HD] * (1.0 / acc_s[:, HD:])

        start = pl.multiple_of(jnp.maximum(sub["tq0"] - WINDOW, 0), TQ)
        s_w = _nt_dot(sub["qr"], kwr_s[pl.ds(start, WIN_SPAN), :])
        diff = sub["t_tok"] - (start + lax.broadcasted_iota(jnp.int32, (1, WIN_SPAN), 1))
        band = jnp.where((diff >= 0) & (diff < WINDOW), 0.0, MASKED)
        for r in range(R):
            sw_s[u, r * TQ:(r + 1) * TQ, :] = s_w[r * TQ:(r + 1) * TQ] + band
        m_w = jnp.max(sw_s[u], axis=-1, keepdims=True)
        ew_s[u] = jnp.exp2(sw_s[u] - m_w).astype(BF16)
        acc_w = _dot(ew_s[u], vwa_s[pl.ds(start, WIN_SPAN), :])
        o_win = acc_w[:, 0:HD] * (1.0 / acc_w[:, HD:])

        urows = slice(u * TQ, (u + 1) * TQ)
        for r in range(R):
            rs = slice(r * TQ, (r + 1) * TQ)
            o = (gate[urows, 3 * r:3 * r + 1] * sub["o_cmp"][rs] + gate[urows, 3 * r + 1:3 * r + 2] * o_slc[rs]
                 + gate[urows, 3 * r + 2:3 * r + 3] * o_win[rs])
            o_ref[urows, r * HD:(r + 1) * HD] = o.astype(o_ref.dtype)


def _nsa(u3, cmp_kv, cos_f, sin_lo, sin_hi, ovt, blk_onehot):
    B, S, _ = u3.shape
    G, R, HD = NSA_KV_GROUPS, NSA_HPG, NSA_HD
    qb = R * HD // LANES
    rows = R * NSA_TQ
    step = NSA_SUB * NSA_TQ
    assert NSA_TK % step == 0
    full = lambda cb: pl.BlockSpec((None, S, HD), lambda b, g, i: (b, 0, cb + g))
    table = pl.BlockSpec((S, LANES), lambda b, g, i: (0, 0))
    return pl.pallas_call(
        _nsa_kernel,
        grid=(B, G, S // step),
        in_specs=[
            pl.BlockSpec((None, step, R * HD), lambda b, g, i: (b, i, CB_NQ // qb + g)),
            pl.BlockSpec((None, step, LANES), lambda b, g, i: (b, i, CB_MISC)),
            pl.BlockSpec((None, None, None, N_CMP_PAD, HD), lambda b, g, i: (0, b, g, 0, 0)),
            pl.BlockSpec((None, None, None, N_CMP_PAD, HD), lambda b, g, i: (1, b, g, 0, 0)),
            full(CB_KS), full(CB_VS), full(CB_KW), full(CB_VW),
            table, table, table,
            pl.BlockSpec((LANES, N_CMP_PAD), lambda b, g, i: (0, 0)),
            table,
        ],
        out_specs=pl.BlockSpec((None, step, R * HD), lambda b, g, i: (b, i, g)),
        out_shape=jax.ShapeDtypeStruct((B, S, NSA_WIDTH), BF16),
        scratch_shapes=[pltpu.VMEM((S, 2 * HD), BF16), pltpu.VMEM((S, 2 * HD), BF16),
                        pltpu.VMEM((S, HD), BF16), pltpu.VMEM((S, 2 * HD), BF16),
                        pltpu.VMEM((NSA_SUB, rows, NSA_TK), F32), pltpu.VMEM((NSA_SUB, rows, NSA_TK), BF16),
                        pltpu.VMEM((NSA_SUB, rows, WIN_SPAN), F32), pltpu.VMEM((NSA_SUB, rows, WIN_SPAN), BF16)],
        compiler_params=pltpu.CompilerParams(
            dimension_semantics=("parallel", "parallel", "arbitrary"), vmem_limit_bytes=VMEM_LIMIT),
        name="nsa",
    )(u3, u3, cmp_kv, cmp_kv, u3, u3, u3, u3, cos_f, sin_lo, sin_hi, ovt, blk_onehot)


def _out_proj_kernel(og_ref, on_ref, x_ref, w_ref, g_ref, b_ref, h_ref, hb_ref):
    mix = _dot(og_ref[...], w_ref[0:GLA_WIDTH, :]) + _dot(on_ref[...], w_ref[GLA_WIDTH:, :])
    h = _layer_norm(DEEPNORM_ALPHA * x_ref[...] + mix, g_ref[...], b_ref[...])
    h_ref[...] = h
    hb_ref[...] = h.astype(BF16)


def _out_proj(o_gla, o_nsa, x2, w_bf, g, b):
    T, D = x2.shape
    tile = lambda w: pl.BlockSpec((OUT_TM, w), lambda i: (i, 0))
    const = lambda s: pl.BlockSpec(s, lambda i: (0, 0))
    return pl.pallas_call(
        _out_proj_kernel,
        grid=(T // OUT_TM,),
        in_specs=[tile(GLA_WIDTH), tile(NSA_WIDTH), tile(D), const(w_bf.shape), const((1, D)), const((1, D))],
        out_specs=[tile(D), tile(D)],
        out_shape=[jax.ShapeDtypeStruct((T, D), F32), jax.ShapeDtypeStruct((T, D), BF16)],
        compiler_params=pltpu.CompilerParams(
            dimension_semantics=("parallel",), vmem_limit_bytes=VMEM_LIMIT),
        name="out_proj",
    )(o_gla, o_nsa, x2, w_bf, g, b)


def _ffn_kernel(hb_ref, h_ref, w1_ref, w3_ref, w2_ref, g_ref, b_ref, o_ref, acc_s):
    f = pl.program_id(1)

    @pl.when(f == 0)
    def _():
        acc_s[...] = jnp.zeros(acc_s.shape, F32)

    hb = hb_ref[...]
    a = _dot(hb, w1_ref[...])
    c = _dot(hb, w3_ref[...])
    acc_s[...] += _dot((a * jax.nn.sigmoid(a) * c).astype(BF16), w2_ref[...])

    @pl.when(f == pl.num_programs(1) - 1)
    def _():
        o_ref[...] = _layer_norm(DEEPNORM_ALPHA * h_ref[...] + acc_s[...], g_ref[...], b_ref[...])


def _ffn(h_bf, h, w1, w3, w2, g, b):
    T, D = h.shape
    F = w1.shape[1]
    return pl.pallas_call(
        _ffn_kernel,
        grid=(T // FFN_TM, F // FFN_TF),
        in_specs=[
            pl.BlockSpec((FFN_TM, D), lambda i, f: (i, 0)),
            pl.BlockSpec((FFN_TM, D), lambda i, f: (i, 0)),
            pl.BlockSpec((D, FFN_TF), lambda i, f: (0, f)),
            pl.BlockSpec((D, FFN_TF), lambda i, f: (0, f)),
            pl.BlockSpec((FFN_TF, D), lambda i, f: (f, 0)),
            pl.BlockSpec((1, D), lambda i, f: (0, 0)),
            pl.BlockSpec((1, D), lambda i, f: (0, 0)),
        ],
        out_specs=pl.BlockSpec((FFN_TM, D), lambda i, f: (i, 0)),
        out_shape=jax.ShapeDtypeStruct((T, D), F32),
        scratch_shapes=[pltpu.VMEM((FFN_TM, D), F32)],
        compiler_params=pltpu.CompilerParams(
            dimension_semantics=("parallel", "arbitrary"), vmem_limit_bytes=VMEM_LIMIT),
        name="ffn",
    )(h_bf, h, w1, w3, w2, g, b)


def _rope_tables(S):
    half = ROPE_DIM // 2
    pos = jnp.arange(S, dtype=F32)
    inv = jnp.power(ROPE_THETA, -jnp.arange(0, ROPE_DIM, 2, dtype=F32) / ROPE_DIM)
    ang = pos[:, None] * inv[None, :]
    cos, sin = jnp.cos(ang), jnp.sin(ang)
    cos_f = jnp.concatenate([cos, cos, jnp.ones((S, LANES - ROPE_DIM), F32)], axis=1)
    sin_lo = jnp.concatenate([-sin, jnp.zeros((S, LANES - half), F32)], axis=1)
    sin_hi = jnp.concatenate([jnp.zeros((S, half), F32), sin, jnp.zeros((S, LANES - ROPE_DIM), F32)], axis=1)
    return cos_f, sin_lo, sin_hi


def _selection_constants(S):
    n_cmp = (S - CMP_BLOCK) // CMP_STRIDE + 1
    nb = S // SLC_BLOCK
    c_start = np.arange(n_cmp) * CMP_STRIDE
    b_start = np.arange(nb) * SLC_BLOCK
    overlap = ((c_start[:, None] < b_start[None, :] + SLC_BLOCK) &
               (b_start[None, :] < c_start[:, None] + CMP_BLOCK)).astype(np.float32)
    ovt = np.zeros((LANES, N_CMP_PAD), np.float32)
    ovt[:nb, :n_cmp] = overlap.T
    blk_onehot = ((np.arange(S) // SLC_BLOCK)[:, None] == np.arange(LANES)[None, :]).astype(np.float32)
    return jnp.asarray(ovt, BF16), jnp.asarray(blk_onehot, BF16)


def kernel(x, w_in, gla_gate_w2, gla_gate_b2, gla_norm_w, cmp_k_pos, cmp_k_w1, cmp_k_w2,
           cmp_v_pos, cmp_v_w1, cmp_v_w2, w_out, ln1_g, ln1_b, ffn_w1, ffn_w3, ffn_w2, ln2_g, ln2_b):
    B, S, D = x.shape
    T = B * S
    cos_f, sin_lo, sin_hi = _rope_tables(S)
    ovt, blk_onehot = _selection_constants(S)

    x2 = x.reshape(T, D)
    for l in range(DEPTH):
        u = _in_proj(x2, _regroup_w_in(w_in[l].T))
        u3 = u.reshape(B, S, U_WIDTH)
        o_gla = _gla(u3, gla_gate_w2[l], gla_gate_b2[l][None, :], gla_norm_w[l][None, :])
        cmp_kv = _compress(u3, jnp.stack([cmp_k_pos[l], cmp_v_pos[l]]),
                           jnp.stack([cmp_k_w1[l], cmp_v_w1[l]]).astype(BF16),
                           jnp.stack([cmp_k_w2[l], cmp_v_w2[l]]).astype(BF16))
        o_nsa = _nsa(u3, cmp_kv, cos_f, sin_lo, sin_hi, ovt, blk_onehot)
        h, h_bf = _out_proj(o_gla.reshape(T, GLA_WIDTH), o_nsa.reshape(T, NSA_WIDTH), x2,
                            w_out[l].astype(BF16), ln1_g[l][None, :], ln1_b[l][None, :])
        x2 = _ffn(h_bf, h, ffn_w1[l].astype(BF16), ffn_w3[l].astype(BF16), ffn_w2[l].astype(BF16),
                  ln2_g[l][None, :], ln2_b[l][None, :])
    return x2.reshape(B, S, D)
```

```python
import numpy as np
import jax
import jax.numpy as jnp
from jax import lax
from jax.experimental import pallas as pl
from jax.experimental.pallas import tpu as pltpu

F32 = jnp.float32
BF16 = jnp.bfloat16

D_MODEL = 2048
DEPTH = 1
GLA_HEADS = 4
GLA_DK = 128
GLA_DV = 256
GLA_GATE_RANK = 16
GLA_GATE_NORM = 16.0
GLA_CHUNK = 64
NSA_HEADS = 8
NSA_HD = 128
NSA_KV_GROUPS = 2
NSA_HPG = NSA_HEADS // NSA_KV_GROUPS
CMP_BLOCK = 32
CMP_STRIDE = 16
CMP_HIDDEN = 2 * NSA_HD
SLC_BLOCK = 64
SLC_TOPK = 16
WINDOW = 512
ROPE_THETA = 500000.0
ROPE_DIM = NSA_HD // 4
FFN_HIDDEN = 5632
DEEPNORM_ALPHA = (2.0 * DEPTH) ** 0.25
LN_EPS = 1e-5
FORCED_SCORE = 1e4
INVALID_SCORE = -1e4

GLA_WIDTH = GLA_HEADS * GLA_DV
NSA_WIDTH = NSA_HEADS * NSA_HD

LANES = 128
VMEM_LIMIT = 56 * 1024 * 1024

CB_GQ = 0
CB_GK = 4
CB_GV = 8
CB_GO = 16
CB_NQ = 24
CB_KC = 32
CB_KS = 36
CB_VS = 38
CB_KW = 40
CB_VW = 42
CB_MISC = 44
MISC_GATE_LANE = GLA_GATE_RANK
U_WIDTH = 45 * LANES

REGROUP_TC = 256
IN_TM, IN_TN = 512, U_WIDTH // 3
OUT_TM = 512
FFN_TM, FFN_TF = 512, 512
NSA_TQ = 128
NSA_SUB = 2
NSA_TK = 512
WIN_SPAN = WINDOW + NSA_TQ
MASKED = -1e30
LOG2E = 1.4426950408889634


def _nt_dot(a, b):
    return lax.dot_general(a, b, (((1,), (1,)), ((), ())), preferred_element_type=F32)


def _tn_dot(a, b):
    return lax.dot_general(a, b, (((0,), (0,)), ((), ())), preferred_element_type=F32)


def _dot(a, b):
    return jnp.dot(a, b, preferred_element_type=F32)


def _layer_norm(z, g, b):
    mu = jnp.mean(z, axis=-1, keepdims=True)
    zc = z - mu
    var = jnp.mean(zc * zc, axis=-1, keepdims=True)
    return zc * lax.rsqrt(var + LN_EPS) * g + b


def _masked_softmax(s, mask):
    sm = jnp.where(mask, s, -jnp.inf)
    m = jnp.max(sm, axis=-1, keepdims=True)
    m = jnp.where(m == -jnp.inf, 0.0, m)
    e = jnp.where(mask, jnp.exp(s - m), 0.0)
    den = jnp.sum(e, axis=-1, keepdims=True)
    return e * (1.0 / jnp.where(den > 0, den, 1.0))


def _regroup_kernel(w_ref, o_ref):
    cols = w_ref.shape[1]
    o_glr = 2 * GLA_HEADS * GLA_DK + 2 * GLA_HEADS * GLA_DV
    o_nq = o_glr + GLA_GATE_RANK
    o_gate = o_nq + NSA_HEADS * NSA_HD + 6 * NSA_KV_GROUPS * NSA_HD
    n_gate = NSA_HEADS * 3
    o_ref[0:o_glr, :] = w_ref[0:o_glr, :].astype(BF16)
    o_ref[o_glr:CB_MISC * LANES, :] = w_ref[o_nq:o_gate, :].astype(BF16)
    misc = jnp.concatenate([w_ref[o_glr:o_nq, :], w_ref[o_gate:o_gate + n_gate, :],
                            jnp.zeros((LANES - GLA_GATE_RANK - n_gate, cols), F32)], axis=0)
    o_ref[CB_MISC * LANES:, :] = misc.astype(BF16)


def _regroup_w_in(wt):
    n_in, D = wt.shape
    return pl.pallas_call(
        _regroup_kernel,
        grid=(D // REGROUP_TC,),
        in_specs=[pl.BlockSpec((n_in, REGROUP_TC), lambda i: (0, i))],
        out_specs=pl.BlockSpec((U_WIDTH, REGROUP_TC), lambda i: (0, i)),
        out_shape=jax.ShapeDtypeStruct((U_WIDTH, D), BF16),
        compiler_params=pltpu.CompilerParams(
            dimension_semantics=("parallel",), vmem_limit_bytes=VMEM_LIMIT),
        name="regroup_w_in",
    )(wt)


def _in_proj_kernel(x_ref, w_ref, o_ref):
    o_ref[...] = _nt_dot(x_ref[...].astype(BF16), w_ref[...])


def _in_proj(x2, wt_bf):
    T, D = x2.shape
    N = wt_bf.shape[0]
    return pl.pallas_call(
        _in_proj_kernel,
        grid=(N // IN_TN, T // IN_TM),
        in_specs=[pl.BlockSpec((IN_TM, D), lambda j, i: (i, 0)),
                  pl.BlockSpec((IN_TN, D), lambda j, i: (j, 0))],
        out_specs=pl.BlockSpec((IN_TM, IN_TN), lambda j, i: (i, j)),
        out_shape=jax.ShapeDtypeStruct((T, N), F32),
        compiler_params=pltpu.CompilerParams(
            dimension_semantics=("parallel", "parallel"), vmem_limit_bytes=VMEM_LIMIT),
        name="in_proj",
    )(x2, wt_bf)


GLA_ROWBLK = 256
GLA_TS = 512
GLA_UNROLL = 4


def _split3(a):
    hi = a.astype(BF16)
    r1 = a - hi.astype(F32)
    mid = r1.astype(BF16)
    lo = (r1 - mid.astype(F32)).astype(BF16)
    return hi, mid, lo


def _log_sigmoid(z):
    return jnp.minimum(z, 0.0) - jnp.log1p(jnp.exp(-jnp.abs(z)))


def _gla_kernel(q_ref, k_ref, v_ref, go_ref, misc_ref, w2_ref, b2_ref, nw_ref, o_ref,
                qd_s, ki_s, ks_s, dec_s, state_s):
    TS = q_ref.shape[0]
    H, C, DK, DV = GLA_HEADS, GLA_CHUNK, GLA_DK, GLA_DV
    scale = DK ** -0.5

    @pl.when(pl.program_id(1) == 0)
    def _():
        state_s[...] = jnp.zeros(state_s.shape, F32)

    r = lax.broadcasted_iota(jnp.int32, (GLA_ROWBLK, GLA_ROWBLK), 0)
    c = lax.broadcasted_iota(jnp.int32, (GLA_ROWBLK, GLA_ROWBLK), 1)
    same = (r // C) == (c // C)
    cum_m = jnp.where(same & (c <= r), 1.0, 0.0).astype(BF16)

    w2 = w2_ref[...].astype(BF16)
    b2 = b2_ref[...]
    for rb in range(TS // GLA_ROWBLK):
        rows = slice(rb * GLA_ROWBLK, (rb + 1) * GLA_ROWBLK)
        glr = misc_ref[rows, 0:GLA_GATE_RANK].astype(BF16)
        gk = _log_sigmoid(_dot(glr, w2) + b2) * (1.0 / GLA_GATE_NORM)
        hi, mid, lo = _split3(gk)
        bc_all = _dot(cum_m, hi) + _dot(cum_m, mid) + _dot(cum_m, lo)
        bl_all = jnp.concatenate(
            [jnp.broadcast_to(bc_all[j * C + C - 1:j * C + C, :], (C, H * DK)) for j in range(GLA_ROWBLK // C)],
            axis=0)
        for h in range(H):
            hk = slice(h * DK, (h + 1) * DK)
            bc = bc_all[:, hk]
            bl = bl_all[:, hk]
            q = q_ref[rows, hk] * scale
            k = k_ref[rows, hk]
            qd_s[h, rows, :] = (q * jnp.exp(bc)).astype(BF16)
            ki_s[h, rows, :] = (k * jnp.exp(-bc)).astype(BF16)
            ks_s[h, rows, :] = (k * jnp.exp(bl - bc)).astype(BF16)
            dec_s[h, rows, :] = jnp.exp(bl)

    ri = lax.broadcasted_iota(jnp.int32, (C, C), 0)
    ci = lax.broadcasted_iota(jnp.int32, (C, C), 1)
    causal = ci <= ri
    nw = nw_ref[...]

    def chunk(cidx, _):
        r0 = pl.multiple_of(cidx * C, C)
        rows = pl.ds(r0, C)
        for h in range(H):
            hv = slice(h * DV, (h + 1) * DV)
            qd = qd_s[h, rows, :]
            v = v_ref[rows, hv].astype(BF16)
            state_t = state_s[h]
            attn = jnp.where(causal, _nt_dot(qd, ki_s[h, rows, :]), 0.0).astype(BF16)
            o = _dot(attn, v) + _nt_dot(qd, state_t.astype(BF16))
            state_s[h] = state_t * dec_s[h, pl.ds(r0, 1), :] + _tn_dot(v, ks_s[h, rows, :])
            ms = jnp.mean(o * o, axis=-1, keepdims=True)
            o = o * lax.rsqrt(ms + LN_EPS) * nw
            go = go_ref[rows, hv]
            o_ref[rows, hv] = (o * (go * jax.nn.sigmoid(go))).astype(o_ref.dtype)
        return 0

    lax.fori_loop(0, TS // C, chunk, 0, unroll=GLA_UNROLL)


def _gla(u3, gate_w2, gate_b2, norm_w):
    B, S, _ = u3.shape
    H, DK, DV = GLA_HEADS, GLA_DK, GLA_DV
    qk_w, v_w = H * DK, H * DV
    tok = lambda w, off: pl.BlockSpec((None, GLA_TS, w), lambda b, s: (b, s, off * LANES // w))
    const = lambda shape: pl.BlockSpec(shape, lambda b, s: (0, 0))
    return pl.pallas_call(
        _gla_kernel,
        grid=(B, S // GLA_TS),
        in_specs=[tok(qk_w, CB_GQ), tok(qk_w, CB_GK), tok(v_w, CB_GV), tok(v_w, CB_GO), tok(LANES, CB_MISC),
                  const((GLA_GATE_RANK, qk_w)), const((1, qk_w)), const((1, DV))],
        out_specs=pl.BlockSpec((None, GLA_TS, v_w), lambda b, s: (b, s, 0)),
        out_shape=jax.ShapeDtypeStruct((B, S, GLA_WIDTH), BF16),
        scratch_shapes=[pltpu.VMEM((H, GLA_TS, DK), BF16), pltpu.VMEM((H, GLA_TS, DK), BF16),
                        pltpu.VMEM((H, GLA_TS, DK), BF16), pltpu.VMEM((H, GLA_TS, DK), F32),
                        pltpu.VMEM((H, DV, DK), F32)],
        compiler_params=pltpu.CompilerParams(
            dimension_semantics=("parallel", "arbitrary"), vmem_limit_bytes=VMEM_LIMIT),
        name="gla",
    )(u3, u3, u3, u3, u3, gate_w2, gate_b2, norm_w)


N_CMP_PAD = 128


def _gelu_tanh(x):
    return x * (0.5 * (1.0 + jnp.tanh(0.7978845608028654 * (x + 0.044715 * (x * x * x)))))


def _compress_kernel(kv_ref, pos_ref, w1_ref, w2_ref, o_ref):
    half = CMP_BLOCK // 2
    p0 = jnp.zeros((N_CMP_PAD, CMP_HIDDEN), F32)
    p1 = jnp.zeros((N_CMP_PAD, CMP_HIDDEN), F32)
    for l in range(half):
        x = kv_ref[pl.ds(l, N_CMP_PAD, stride=CMP_STRIDE), :]
        a0 = (x + pos_ref[l:l + 1, :]).astype(BF16)
        a1 = (x + pos_ref[half + l:half + l + 1, :]).astype(BF16)
        p0 = p0 + _dot(a0, w1_ref[l * NSA_HD:(l + 1) * NSA_HD, :])
        p1 = p1 + _dot(a1, w1_ref[(half + l) * NSA_HD:(half + l + 1) * NSA_HD, :])
    pre = p0 + pltpu.roll(p1, N_CMP_PAD - 1, 0)
    h = _gelu_tanh(pre).astype(BF16)
    out = _dot(h, w2_ref[...])
    row = lax.broadcasted_iota(jnp.int32, out.shape, 0)
    o_ref[...] = jnp.where(row < N_CMP_PAD - 1, out, 0.0).astype(o_ref.dtype)


def _compress(u3, pos, w1, w2):
    B, S, _ = u3.shape
    G = NSA_KV_GROUPS
    return pl.pallas_call(
        _compress_kernel,
        grid=(2, B, G),
        in_specs=[
            pl.BlockSpec((None, S, NSA_HD), lambda kv, b, g: (b, 0, CB_KC + 2 * kv + g)),
            pl.BlockSpec((None, CMP_BLOCK, NSA_HD), lambda kv, b, g: (kv, 0, 0)),
            pl.BlockSpec((None, CMP_BLOCK * NSA_HD, CMP_HIDDEN), lambda kv, b, g: (kv, 0, 0)),
            pl.BlockSpec((None, CMP_HIDDEN, NSA_HD), lambda kv, b, g: (kv, 0, 0)),
        ],
        out_specs=pl.BlockSpec((None, None, None, N_CMP_PAD, NSA_HD), lambda kv, b, g: (kv, b, g, 0, 0)),
        out_shape=jax.ShapeDtypeStruct((2, B, G, N_CMP_PAD, NSA_HD), BF16),
        compiler_params=pltpu.CompilerParams(
            dimension_semantics=("parallel", "parallel", "parallel"), vmem_limit_bytes=VMEM_LIMIT),
        name="compress",
    )(u3, pos, w1, w2)


def _rope(x, cos, sin_lo, sin_hi):
    half = ROPE_DIM // 2
    return x * cos + pltpu.roll(x, LANES - half, 1) * sin_lo + pltpu.roll(x, half, 1) * sin_hi


def _nsa_kernel(q_ref, misc_ref, kc_ref, vc_ref, ks_ref, vs_ref, kw_ref, vw_ref,
                cos_ref, sinlo_ref, sinhi_ref, ovt_ref, blk_ref, o_ref,
                ksa_s, vsa_s, kwr_s, vwa_s, osl_s, *chain_scratch):
    S = ks_ref.shape[0]
    R, TQ, HD = NSA_HPG, NSA_TQ, NSA_HD
    ss_s, es_s = chain_scratch[0:NSA_SUB], chain_scratch[NSA_SUB:2 * NSA_SUB]
    sw_s, ew_s = chain_scratch[2 * NSA_SUB:2 * NSA_SUB + R // 2], chain_scratch[2 * NSA_SUB + R // 2:]
    g = pl.program_id(1)
    qi = pl.program_id(2)
    scale = HD ** -0.5

    @pl.when(qi == 0)
    def _():
        def prep(rb, _):
            rows = pl.ds(pl.multiple_of(rb * 256, 256), 256)
            cos, slo, shi = cos_ref[rows, :], sinlo_ref[rows, :], sinhi_ref[rows, :]
            ones = jnp.ones((256, LANES), BF16)
            ksa_s[rows, 0:HD] = _rope(ks_ref[rows, :], cos, slo, shi).astype(BF16)
            ksa_s[rows, HD:] = blk_ref[rows, :]
            kwr_s[rows, :] = _rope(kw_ref[rows, :], cos, slo, shi).astype(BF16)
            vsa_s[rows, 0:HD] = vs_ref[rows, :].astype(BF16)
            vsa_s[rows, HD:] = ones
            vwa_s[rows, 0:HD] = vw_ref[rows, :].astype(BF16)
            vwa_s[rows, HD:] = ones
            return 0
        lax.fori_loop(0, S // 256, prep, 0)

    nb = S // SLC_BLOCK
    TK = NSA_TK
    ovt = ovt_ref[...]
    kc, vc = kc_ref[...], vc_ref[...]

    def front(u):
        tq0 = (qi * NSA_SUB + u) * TQ
        urows = slice(u * TQ, (u + 1) * TQ)
        trows = pl.ds(pl.multiple_of(tq0, TQ), TQ)
        cos_t, slo_t, shi_t = cos_ref[trows, :], sinlo_ref[trows, :], sinhi_ref[trows, :]
        heads = [q_ref[urows, r * HD:(r + 1) * HD] for r in range(R)]
        qc = jnp.concatenate([(h * scale).astype(BF16) for h in heads], axis=0)
        qr_heads = [_rope(h * (scale * LOG2E), cos_t, slo_t, shi_t).astype(BF16) for h in heads]
        t_tok = tq0 + lax.broadcasted_iota(jnp.int32, (TQ, 1), 0)
        t_col = jnp.concatenate([t_tok] * R, axis=0)

        s_c = _nt_dot(qc, kc)
        n_idx = lax.broadcasted_iota(jnp.int32, (1, N_CMP_PAD), 1)
        p_c = _masked_softmax(s_c, (n_idx * CMP_STRIDE + (CMP_BLOCK - 1)) <= t_col)
        p_cb = p_c.astype(BF16)
        o_cmp = _dot(p_cb, vc)

        imp_t = _nt_dot(ovt, p_cb[0:TQ])
        for r in range(1, R):
            imp_t = imp_t + _nt_dot(ovt, p_cb[r * TQ:(r + 1) * TQ])
        imp = imp_t[0:nb, :]
        j_idx = lax.broadcasted_iota(jnp.int32, (nb, TQ), 0)
        cur = (tq0 + lax.broadcasted_iota(jnp.int32, (nb, TQ), 1)) // SLC_BLOCK
        imp = jnp.where(j_idx <= cur, imp, INVALID_SCORE)
        imp = jnp.where((j_idx == 0) | (j_idx == cur) | (j_idx == cur - 1), FORCED_SCORE, imp)
        rank = jnp.zeros((nb, TQ), jnp.int32)
        for i in range(nb):
            ri = imp[i:i + 1, :]
            ahead = (ri > imp) | ((ri == imp) & (i < j_idx))
            rank = rank + jnp.where(ahead, 1, 0)
        blk_bias_t = jnp.where(rank < min(SLC_TOPK, nb), 0.0, MASKED)
        blk_bias_t = jnp.concatenate([blk_bias_t, jnp.zeros((LANES - nb, TQ), F32)], axis=0)
        blk_bias = jnp.transpose(blk_bias_t).astype(BF16)
        q_aug = jnp.concatenate([jnp.concatenate([h, blk_bias], axis=1) for h in qr_heads], axis=0)
        return dict(t_col=t_col, o_cmp=o_cmp, q_aug=q_aug, qr_heads=qr_heads)

    subs = [front(u) for u in range(NSA_SUB)]

    def sel_variant(n_keys):
        for u in range(NSA_SUB):
            q_aug, t_col = subs[u]["q_aug"], subs[u]["t_col"]
            for k0 in range(0, n_keys, TK):
                w = min(TK, n_keys - k0)
                s = _nt_dot(q_aug, ksa_s[k0:k0 + w, :])
                if k0 + w == n_keys:
                    kpos = k0 + lax.broadcasted_iota(jnp.int32, (1, w), 1)
                    s = jnp.where(kpos <= t_col, s, MASKED)
                ss_s[u][:, k0:k0 + w] = s
        for u in range(NSA_SUB):
            m = jnp.max(ss_s[u][:, 0:n_keys], axis=-1, keepdims=True)
            es_s[u][:, 0:n_keys] = jnp.exp2(ss_s[u][:, 0:n_keys] - m).astype(BF16)
        for u in range(NSA_SUB):
            acc = _dot(es_s[u][:, 0:n_keys], vsa_s[0:n_keys, :])
            osl_s[u] = acc[:, 0:HD] * (1.0 / acc[:, HD:])

    step = NSA_SUB * TQ
    for v in range(S // step):
        pl.when(qi == v)(lambda v=v: sel_variant((v + 1) * step))

    tq_step = qi * step
    start = pl.multiple_of(jnp.maximum(tq_step - WINDOW, 0), step)
    span = WINDOW + step
    diff = (tq_step + lax.broadcasted_iota(jnp.int32, (step, 1), 0)) - (
        start + lax.broadcasted_iota(jnp.int32, (1, span), 1))
    band = jnp.where((diff >= 0) & (diff < WINDOW), 0.0, MASKED)
    n_pair = R // 2
    for c in range(n_pair):
        q_pair = jnp.concatenate([subs[u]["qr_heads"][2 * c + rr] for rr in range(2) for u in range(NSA_SUB)], axis=0)
        s_w = _nt_dot(q_pair, kwr_s[pl.ds(start, span), :])
        for rr in range(2):
            sw_s[c][rr * step:(rr + 1) * step, :] = s_w[rr * step:(rr + 1) * step] + band
    for c in range(n_pair):
        m_w = jnp.max(sw_s[c][...], axis=-1, keepdims=True)
        ew_s[c][...] = jnp.exp2(sw_s[c][...] - m_w).astype(BF16)
    o_win = []
    for c in range(n_pair):
        acc_w = _dot(ew_s[c][...], vwa_s[pl.ds(start, span), :])
        o_win.append(acc_w[:, 0:HD] * (1.0 / acc_w[:, HD:]))

    misc = misc_ref[...]
    ng = 3 * R
    logits = jnp.where(g == 0, misc[:, MISC_GATE_LANE:MISC_GATE_LANE + ng],
                       misc[:, MISC_GATE_LANE + ng:MISC_GATE_LANE + 2 * ng])
    gate = jax.nn.sigmoid(logits)
    for u in range(NSA_SUB):
        urows = slice(u * TQ, (u + 1) * TQ)
        o_slc = osl_s[u]
        for r in range(R):
            rs = slice(r * TQ, (r + 1) * TQ)
            ws = slice((r % 2) * step + u * TQ, (r % 2) * step + (u + 1) * TQ)
            o = (gate[urows, 3 * r:3 * r + 1] * subs[u]["o_cmp"][rs] + gate[urows, 3 * r + 1:3 * r + 2] * o_slc[rs]
                 + gate[urows, 3 * r + 2:3 * r + 3] * o_win[r // 2][ws])
            o_ref[urows, r * HD:(r + 1) * HD] = o.astype(o_ref.dtype)


def _nsa(u3, cmp_kv, cos_f, sin_lo, sin_hi, ovt, blk_onehot):
    B, S, _ = u3.shape
    G, R, HD = NSA_KV_GROUPS, NSA_HPG, NSA_HD
    qb = R * HD // LANES
    rows = R * NSA_TQ
    step = NSA_SUB * NSA_TQ
    assert NSA_TK % step == 0
    full = lambda cb: pl.BlockSpec((None, S, HD), lambda b, g, i: (b, 0, cb + g))
    table = pl.BlockSpec((S, LANES), lambda b, g, i: (0, 0))
    return pl.pallas_call(
        _nsa_kernel,
        grid=(B, G, S // step),
        in_specs=[
            pl.BlockSpec((None, step, R * HD), lambda b, g, i: (b, i, CB_NQ // qb + g)),
            pl.BlockSpec((None, step, LANES), lambda b, g, i: (b, i, CB_MISC)),
            pl.BlockSpec((None, None, None, N_CMP_PAD, HD), lambda b, g, i: (0, b, g, 0, 0)),
            pl.BlockSpec((None, None, None, N_CMP_PAD, HD), lambda b, g, i: (1, b, g, 0, 0)),
            full(CB_KS), full(CB_VS), full(CB_KW), full(CB_VW),
            table, table, table,
            pl.BlockSpec((LANES, N_CMP_PAD), lambda b, g, i: (0, 0)),
            table,
        ],
        out_specs=pl.BlockSpec((None, step, R * HD), lambda b, g, i: (b, i, g)),
        out_shape=jax.ShapeDtypeStruct((B, S, NSA_WIDTH), BF16),
        scratch_shapes=[pltpu.VMEM((S, 2 * HD), BF16), pltpu.VMEM((S, 2 * HD), BF16),
                        pltpu.VMEM((S, HD), BF16), pltpu.VMEM((S, 2 * HD), BF16),
                        pltpu.VMEM((NSA_SUB, rows, HD), F32)]
        + [pltpu.VMEM((rows, S), F32)] * NSA_SUB + [pltpu.VMEM((rows, S), BF16)] * NSA_SUB
        + [pltpu.VMEM((2 * step, WINDOW + step), F32)] * (R // 2)
        + [pltpu.VMEM((2 * step, WINDOW + step), BF16)] * (R // 2),
        compiler_params=pltpu.CompilerParams(
            dimension_semantics=("parallel", "parallel", "arbitrary"), vmem_limit_bytes=VMEM_LIMIT),
        name="nsa",
    )(u3, u3, cmp_kv, cmp_kv, u3, u3, u3, u3, cos_f, sin_lo, sin_hi, ovt, blk_onehot)


def _out_proj_kernel(og_ref, on_ref, x_ref, w_ref, g_ref, b_ref, h_ref, hb_ref):
    mix = _dot(og_ref[...], w_ref[0:GLA_WIDTH, :]) + _dot(on_ref[...], w_ref[GLA_WIDTH:, :])
    h = _layer_norm(DEEPNORM_ALPHA * x_ref[...] + mix, g_ref[...], b_ref[...])
    h_ref[...] = h
    hb_ref[...] = h.astype(BF16)


def _out_proj(o_gla, o_nsa, x2, w_bf, g, b):
    T, D = x2.shape
    tile = lambda w: pl.BlockSpec((OUT_TM, w), lambda i: (i, 0))
    const = lambda s: pl.BlockSpec(s, lambda i: (0, 0))
    return pl.pallas_call(
        _out_proj_kernel,
        grid=(T // OUT_TM,),
        in_specs=[tile(GLA_WIDTH), tile(NSA_WIDTH), tile(D), const(w_bf.shape), const((1, D)), const((1, D))],
        out_specs=[tile(D), tile(D)],
        out_shape=[jax.ShapeDtypeStruct((T, D), F32), jax.ShapeDtypeStruct((T, D), BF16)],
        compiler_params=pltpu.CompilerParams(
            dimension_semantics=("parallel",), vmem_limit_bytes=VMEM_LIMIT),
        name="out_proj",
    )(o_gla, o_nsa, x2, w_bf, g, b)


def _ffn_kernel(hb_ref, h_ref, w1_ref, w3_ref, w2_ref, g_ref, b_ref, o_ref, acc_s):
    f = pl.program_id(1)

    @pl.when(f == 0)
    def _():
        acc_s[...] = jnp.zeros(acc_s.shape, F32)

    hb = hb_ref[...]
    a = _dot(hb, w1_ref[...])
    c = _dot(hb, w3_ref[...])
    acc_s[...] += _dot((a * jax.nn.sigmoid(a) * c).astype(BF16), w2_ref[...])

    @pl.when(f == pl.num_programs(1) - 1)
    def _():
        o_ref[...] = _layer_norm(DEEPNORM_ALPHA * h_ref[...] + acc_s[...], g_ref[...], b_ref[...])


def _ffn(h_bf, h, w1, w3, w2, g, b):
    T, D = h.shape
    F = w1.shape[1]
    return pl.pallas_call(
        _ffn_kernel,
        grid=(T // FFN_TM, F // FFN_TF),
        in_specs=[
            pl.BlockSpec((FFN_TM, D), lambda i, f: (i, 0)),
            pl.BlockSpec((FFN_TM, D), lambda i, f: (i, 0)),
            pl.BlockSpec((D, FFN_TF), lambda i, f: (0, f)),
            pl.BlockSpec((D, FFN_TF), lambda i, f: (0, f)),
            pl.BlockSpec((FFN_TF, D), lambda i, f: (f, 0)),
            pl.BlockSpec((1, D), lambda i, f: (0, 0)),
            pl.BlockSpec((1, D), lambda i, f: (0, 0)),
        ],
        out_specs=pl.BlockSpec((FFN_TM, D), lambda i, f: (i, 0)),
        out_shape=jax.ShapeDtypeStruct((T, D), F32),
        scratch_shapes=[pltpu.VMEM((FFN_TM, D), F32)],
        compiler_params=pltpu.CompilerParams(
            dimension_semantics=("parallel", "arbitrary"), vmem_limit_bytes=VMEM_LIMIT),
        name="ffn",
    )(h_bf, h, w1, w3, w2, g, b)


def _rope_tables(S):
    half = ROPE_DIM // 2
    pos = jnp.arange(S, dtype=F32)
    inv = jnp.power(ROPE_THETA, -jnp.arange(0, ROPE_DIM, 2, dtype=F32) / ROPE_DIM)
    ang = pos[:, None] * inv[None, :]
    cos, sin = jnp.cos(ang), jnp.sin(ang)
    cos_f = jnp.concatenate([cos, cos, jnp.ones((S, LANES - ROPE_DIM), F32)], axis=1)
    sin_lo = jnp.concatenate([-sin, jnp.zeros((S, LANES - half), F32)], axis=1)
    sin_hi = jnp.concatenate([jnp.zeros((S, half), F32), sin, jnp.zeros((S, LANES - ROPE_DIM), F32)], axis=1)
    return cos_f, sin_lo, sin_hi


def _selection_constants(S):
    n_cmp = (S - CMP_BLOCK) // CMP_STRIDE + 1
    nb = S // SLC_BLOCK
    c_start = np.arange(n_cmp) * CMP_STRIDE
    b_start = np.arange(nb) * SLC_BLOCK
    overlap = ((c_start[:, None] < b_start[None, :] + SLC_BLOCK) &
               (b_start[None, :] < c_start[:, None] + CMP_BLOCK)).astype(np.float32)
    ovt = np.zeros((LANES, N_CMP_PAD), np.float32)
    ovt[:nb, :n_cmp] = overlap.T
    blk_onehot = ((np.arange(S) // SLC_BLOCK)[:, None] == np.arange(LANES)[None, :]).astype(np.float32)
    return jnp.asarray(ovt, BF16), jnp.asarray(blk_onehot, BF16)


def kernel(x, w_in, gla_gate_w2, gla_gate_b2, gla_norm_w, cmp_k_pos, cmp_k_w1, cmp_k_w2,
           cmp_v_pos, cmp_v_w1, cmp_v_w2, w_out, ln1_g, ln1_b, ffn_w1, ffn_w3, ffn_w2, ln2_g, ln2_b):
    B, S, D = x.shape
    T = B * S
    cos_f, sin_lo, sin_hi = _rope_tables(S)
    ovt, blk_onehot = _selection_constants(S)

    x2 = x.reshape(T, D)
    for l in range(DEPTH):
        u = _in_proj(x2, _regroup_w_in(w_in[l].T))
        u3 = u.reshape(B, S, U_WIDTH)
        o_gla = _gla(u3, gla_gate_w2[l], gla_gate_b2[l][None, :], gla_norm_w[l][None, :])
        cmp_kv = _compress(u3, jnp.stack([cmp_k_pos[l], cmp_v_pos[l]]),
                           jnp.stack([cmp_k_w1[l], cmp_v_w1[l]]).astype(BF16),
                           jnp.stack([cmp_k_w2[l], cmp_v_w2[l]]).astype(BF16))
        o_nsa = _nsa(u3, cmp_kv, cos_f, sin_lo, sin_hi, ovt, blk_onehot)
        h, h_bf = _out_proj(o_gla.reshape(T, GLA_WIDTH), o_nsa.reshape(T, NSA_WIDTH), x2,
                            w_out[l].astype(BF16), ln1_g[l][None, :], ln1_b[l][None, :])
        x2 = _ffn(h_bf, h, ffn_w1[l].astype(BF16), ffn_w3[l].astype(BF16), ffn_w2[l].astype(BF16),
                  ln2_g[l][None, :], ln2_b[l][None, :])
    return x2.reshape(B, S, D)
```

```python
import functools

import numpy as np
import jax
import jax.numpy as jnp
from jax import lax
from jax.experimental import pallas as pl
from jax.experimental.pallas import tpu as pltpu

F32 = jnp.float32
BF16 = jnp.bfloat16

D_MODEL = 2048
DEPTH = 1
GLA_HEADS = 4
GLA_DK = 128
GLA_DV = 256
GLA_GATE_RANK = 16
GLA_GATE_NORM = 16.0
GLA_CHUNK = 64
NSA_HEADS = 8
NSA_HD = 128
NSA_KV_GROUPS = 2
NSA_HPG = NSA_HEADS // NSA_KV_GROUPS
CMP_BLOCK = 32
CMP_STRIDE = 16
CMP_HIDDEN = 2 * NSA_HD
SLC_BLOCK = 64
SLC_TOPK = 16
WINDOW = 512
ROPE_THETA = 500000.0
ROPE_DIM = NSA_HD // 4
FFN_HIDDEN = 5632
DEEPNORM_ALPHA = (2.0 * DEPTH) ** 0.25
LN_EPS = 1e-5
FORCED_SCORE = 1e4
INVALID_SCORE = -1e4

GLA_WIDTH = GLA_HEADS * GLA_DV
NSA_WIDTH = NSA_HEADS * NSA_HD

LANES = 128
VMEM_LIMIT = 56 * 1024 * 1024

CB_GQ = 0
CB_GK = 4
CB_GV = 8
CB_GO = 16
CB_NQ = 24
CB_KC = 32
CB_KS = 36
CB_VS = 38
CB_KW = 40
CB_VW = 42
CB_MISC = 44
MISC_GATE_LANE = GLA_GATE_RANK
U_WIDTH = 45 * LANES

REGROUP_TC = 256
IN_TM, IN_TN = 512, U_WIDTH // 3
OUT_TM = 512
FFN_TM, FFN_TF = 512, 512
NSA_TQ = 128
NSA_SUB = 2
NSA_TK = 512
WIN_SPAN = WINDOW + NSA_TQ
MASKED = -1e30
LOG2E = 1.4426950408889634


def _nt_dot(a, b):
    return lax.dot_general(a, b, (((1,), (1,)), ((), ())), preferred_element_type=F32)


def _tn_dot(a, b):
    return lax.dot_general(a, b, (((0,), (0,)), ((), ())), preferred_element_type=F32)


def _dot(a, b):
    return jnp.dot(a, b, preferred_element_type=F32)


def _layer_norm(z, g, b):
    mu = jnp.mean(z, axis=-1, keepdims=True)
    zc = z - mu
    var = jnp.mean(zc * zc, axis=-1, keepdims=True)
    return zc * lax.rsqrt(var + LN_EPS) * g + b


def _masked_softmax(s, mask):
    sm = jnp.where(mask, s, -jnp.inf)
    m = jnp.max(sm, axis=-1, keepdims=True)
    m = jnp.where(m == -jnp.inf, 0.0, m)
    e = jnp.where(mask, jnp.exp(s - m), 0.0)
    den = jnp.sum(e, axis=-1, keepdims=True)
    return e * (1.0 / jnp.where(den > 0, den, 1.0))


def _regroup_kernel(w_ref, o_ref):
    cols = w_ref.shape[1]
    o_glr = 2 * GLA_HEADS * GLA_DK + 2 * GLA_HEADS * GLA_DV
    o_nq = o_glr + GLA_GATE_RANK
    o_gate = o_nq + NSA_HEADS * NSA_HD + 6 * NSA_KV_GROUPS * NSA_HD
    n_gate = NSA_HEADS * 3
    o_ref[0:o_glr, :] = w_ref[0:o_glr, :].astype(BF16)
    o_ref[o_glr:CB_MISC * LANES, :] = w_ref[o_nq:o_gate, :].astype(BF16)
    misc = jnp.concatenate([w_ref[o_glr:o_nq, :], w_ref[o_gate:o_gate + n_gate, :],
                            jnp.zeros((LANES - GLA_GATE_RANK - n_gate, cols), F32)], axis=0)
    o_ref[CB_MISC * LANES:, :] = misc.astype(BF16)


def _regroup_w_in(wt):
    n_in, D = wt.shape
    return pl.pallas_call(
        _regroup_kernel,
        grid=(D // REGROUP_TC,),
        in_specs=[pl.BlockSpec((n_in, REGROUP_TC), lambda i: (0, i))],
        out_specs=pl.BlockSpec((U_WIDTH, REGROUP_TC), lambda i: (0, i)),
        out_shape=jax.ShapeDtypeStruct((U_WIDTH, D), BF16),
        compiler_params=pltpu.CompilerParams(
            dimension_semantics=("parallel",), vmem_limit_bytes=VMEM_LIMIT),
        name="regroup_w_in",
    )(wt)


def _in_proj_kernel(x_ref, w_ref, o_ref):
    o_ref[...] = _nt_dot(x_ref[...].astype(BF16), w_ref[...])


def _in_proj(x2, wt_bf):
    T, D = x2.shape
    N = wt_bf.shape[0]
    return pl.pallas_call(
        _in_proj_kernel,
        grid=(N // IN_TN, T // IN_TM),
        in_specs=[pl.BlockSpec((IN_TM, D), lambda j, i: (i, 0)),
                  pl.BlockSpec((IN_TN, D), lambda j, i: (j, 0))],
        out_specs=pl.BlockSpec((IN_TM, IN_TN), lambda j, i: (i, j)),
        out_shape=jax.ShapeDtypeStruct((T, N), F32),
        compiler_params=pltpu.CompilerParams(
            dimension_semantics=("parallel", "parallel"), vmem_limit_bytes=VMEM_LIMIT),
        name="in_proj",
    )(x2, wt_bf)


GLA_ROWBLK = 256
GLA_TS = 512
GLA_UNROLL = 4


def _split3(a):
    hi = a.astype(BF16)
    r1 = a - hi.astype(F32)
    mid = r1.astype(BF16)
    lo = (r1 - mid.astype(F32)).astype(BF16)
    return hi, mid, lo


def _log_sigmoid(z):
    return jnp.minimum(z, 0.0) - jnp.log1p(jnp.exp(-jnp.abs(z)))


def _gla_kernel(q_ref, k_ref, v_ref, go_ref, misc_ref, w2_ref, b2_ref, nw_ref, o_ref,
                qd_s, ki_s, ks_s, dec_s, state_s):
    TS = q_ref.shape[0]
    H, C, DK, DV = GLA_HEADS, GLA_CHUNK, GLA_DK, GLA_DV
    scale = DK ** -0.5

    @pl.when(pl.program_id(1) == 0)
    def _():
        state_s[...] = jnp.zeros(state_s.shape, F32)

    r = lax.broadcasted_iota(jnp.int32, (GLA_ROWBLK, GLA_ROWBLK), 0)
    c = lax.broadcasted_iota(jnp.int32, (GLA_ROWBLK, GLA_ROWBLK), 1)
    same = (r // C) == (c // C)
    cum_m = jnp.where(same & (c <= r), 1.0, 0.0).astype(BF16)

    w2 = w2_ref[...].astype(BF16)
    b2 = b2_ref[...]
    for rb in range(TS // GLA_ROWBLK):
        rows = slice(rb * GLA_ROWBLK, (rb + 1) * GLA_ROWBLK)
        glr = misc_ref[rows, 0:GLA_GATE_RANK].astype(BF16)
        gk = _log_sigmoid(_dot(glr, w2) + b2) * (1.0 / GLA_GATE_NORM)
        hi, mid, lo = _split3(gk)
        bc_all = _dot(cum_m, hi) + _dot(cum_m, mid) + _dot(cum_m, lo)
        bl_all = jnp.concatenate(
            [jnp.broadcast_to(bc_all[j * C + C - 1:j * C + C, :], (C, H * DK)) for j in range(GLA_ROWBLK // C)],
            axis=0)
        for h in range(H):
            hk = slice(h * DK, (h + 1) * DK)
            bc = bc_all[:, hk]
            bl = bl_all[:, hk]
            q = q_ref[rows, hk] * scale
            k = k_ref[rows, hk]
            qd_s[h, rows, :] = (q * jnp.exp(bc)).astype(BF16)
            ki_s[h, rows, :] = (k * jnp.exp(-bc)).astype(BF16)
            ks_s[h, rows, :] = (k * jnp.exp(bl - bc)).astype(BF16)
            dec_s[h, rows, :] = jnp.exp(bl)

    ri = lax.broadcasted_iota(jnp.int32, (C, C), 0)
    ci = lax.broadcasted_iota(jnp.int32, (C, C), 1)
    causal = ci <= ri
    nw = nw_ref[...]

    def chunk(cidx, _):
        r0 = pl.multiple_of(cidx * C, C)
        rows = pl.ds(r0, C)
        for h in range(H):
            hv = slice(h * DV, (h + 1) * DV)
            qd = qd_s[h, rows, :]
            v = v_ref[rows, hv].astype(BF16)
            state_t = state_s[h]
            attn = jnp.where(causal, _nt_dot(qd, ki_s[h, rows, :]), 0.0).astype(BF16)
            o = _dot(attn, v) + _nt_dot(qd, state_t.astype(BF16))
            state_s[h] = state_t * dec_s[h, pl.ds(r0, 1), :] + _tn_dot(v, ks_s[h, rows, :])
            ms = jnp.mean(o * o, axis=-1, keepdims=True)
            o = o * lax.rsqrt(ms + LN_EPS) * nw
            go = go_ref[rows, hv]
            o_ref[rows, hv] = (o * (go * jax.nn.sigmoid(go))).astype(o_ref.dtype)
        return 0

    lax.fori_loop(0, TS // C, chunk, 0, unroll=GLA_UNROLL)


def _gla(u3, gate_w2, gate_b2, norm_w):
    B, S, _ = u3.shape
    H, DK, DV = GLA_HEADS, GLA_DK, GLA_DV
    qk_w, v_w = H * DK, H * DV
    tok = lambda w, off: pl.BlockSpec((None, GLA_TS, w), lambda b, s: (b, s, off * LANES // w))
    const = lambda shape: pl.BlockSpec(shape, lambda b, s: (0, 0))
    return pl.pallas_call(
        _gla_kernel,
        grid=(B, S // GLA_TS),
        in_specs=[tok(qk_w, CB_GQ), tok(qk_w, CB_GK), tok(v_w, CB_GV), tok(v_w, CB_GO), tok(LANES, CB_MISC),
                  const((GLA_GATE_RANK, qk_w)), const((1, qk_w)), const((1, DV))],
        out_specs=pl.BlockSpec((None, GLA_TS, v_w), lambda b, s: (b, s, 0)),
        out_shape=jax.ShapeDtypeStruct((B, S, GLA_WIDTH), BF16),
        scratch_shapes=[pltpu.VMEM((H, GLA_TS, DK), BF16), pltpu.VMEM((H, GLA_TS, DK), BF16),
                        pltpu.VMEM((H, GLA_TS, DK), BF16), pltpu.VMEM((H, GLA_TS, DK), F32),
                        pltpu.VMEM((H, DV, DK), F32)],
        compiler_params=pltpu.CompilerParams(
            dimension_semantics=("parallel", "arbitrary"), vmem_limit_bytes=VMEM_LIMIT),
        name="gla",
    )(u3, u3, u3, u3, u3, gate_w2, gate_b2, norm_w)


N_CMP_PAD = 128


def _gelu_tanh(x):
    return x * (0.5 * (1.0 + jnp.tanh(0.7978845608028654 * (x + 0.044715 * (x * x * x)))))


def _compress_kernel(kv_ref, pos_ref, w1_ref, w2_ref, o_ref):
    half = CMP_BLOCK // 2
    p0 = jnp.zeros((N_CMP_PAD, CMP_HIDDEN), F32)
    p1 = jnp.zeros((N_CMP_PAD, CMP_HIDDEN), F32)
    for l in range(half):
        x = kv_ref[pl.ds(l, N_CMP_PAD, stride=CMP_STRIDE), :]
        a0 = (x + pos_ref[l:l + 1, :]).astype(BF16)
        a1 = (x + pos_ref[half + l:half + l + 1, :]).astype(BF16)
        p0 = p0 + _dot(a0, w1_ref[l * NSA_HD:(l + 1) * NSA_HD, :])
        p1 = p1 + _dot(a1, w1_ref[(half + l) * NSA_HD:(half + l + 1) * NSA_HD, :])
    pre = p0 + pltpu.roll(p1, N_CMP_PAD - 1, 0)
    h = _gelu_tanh(pre).astype(BF16)
    out = _dot(h, w2_ref[...])
    row = lax.broadcasted_iota(jnp.int32, out.shape, 0)
    o_ref[...] = jnp.where(row < N_CMP_PAD - 1, out, 0.0).astype(o_ref.dtype)


def _compress(u3, pos, w1, w2):
    B, S, _ = u3.shape
    G = NSA_KV_GROUPS
    return pl.pallas_call(
        _compress_kernel,
        grid=(2, B, G),
        in_specs=[
            pl.BlockSpec((None, S, NSA_HD), lambda kv, b, g: (b, 0, CB_KC + 2 * kv + g)),
            pl.BlockSpec((None, CMP_BLOCK, NSA_HD), lambda kv, b, g: (kv, 0, 0)),
            pl.BlockSpec((None, CMP_BLOCK * NSA_HD, CMP_HIDDEN), lambda kv, b, g: (kv, 0, 0)),
            pl.BlockSpec((None, CMP_HIDDEN, NSA_HD), lambda kv, b, g: (kv, 0, 0)),
        ],
        out_specs=pl.BlockSpec((None, None, None, N_CMP_PAD, NSA_HD), lambda kv, b, g: (kv, b, g, 0, 0)),
        out_shape=jax.ShapeDtypeStruct((2, B, G, N_CMP_PAD, NSA_HD), BF16),
        compiler_params=pltpu.CompilerParams(
            dimension_semantics=("parallel", "parallel", "parallel"), vmem_limit_bytes=VMEM_LIMIT),
        name="compress",
    )(u3, pos, w1, w2)


def _rope(x, cos, sin_lo, sin_hi):
    half = ROPE_DIM // 2
    return x * cos + pltpu.roll(x, LANES - half, 1) * sin_lo + pltpu.roll(x, half, 1) * sin_hi


def _nsa_kernel(n_side, side_slabs, *refs):
    (q_ref, misc_ref, kc_ref, vc_ref, ks_ref, vs_ref, kw_ref, vw_ref,
     cos_ref, sinlo_ref, sinhi_ref, ovt_ref, blk_ref) = refs[0:13]
    side_in = refs[13:13 + n_side]
    o_ref = refs[13 + n_side]
    side_out = refs[14 + n_side:14 + 2 * n_side]
    ksa_s, vsa_s, kwr_s, vwa_s, osl_s = refs[14 + 2 * n_side:19 + 2 * n_side]
    chain_scratch = refs[19 + 2 * n_side:]
    S = ks_ref.shape[0]
    R, TQ, HD = NSA_HPG, NSA_TQ, NSA_HD
    ss_s, es_s = chain_scratch[0:NSA_SUB], chain_scratch[NSA_SUB:2 * NSA_SUB]
    sw_s, ew_s = chain_scratch[2 * NSA_SUB:2 * NSA_SUB + R // 2], chain_scratch[2 * NSA_SUB + R // 2:]
    g = pl.program_id(1)
    qi = pl.program_id(2)
    scale = HD ** -0.5

    step_idx = (pl.program_id(0) * pl.num_programs(1) + g) * pl.num_programs(2) + qi
    for w_in_ref, w_out_ref, n_slabs in zip(side_in, side_out, side_slabs):
        def cast(w_in_ref=w_in_ref, w_out_ref=w_out_ref):
            w_out_ref[...] = w_in_ref[...].astype(BF16)
        if n_slabs is None:
            cast()
        else:
            pl.when(step_idx < n_slabs)(cast)

    @pl.when(qi == 0)
    def _():
        def prep(rb, _):
            rows = pl.ds(pl.multiple_of(rb * 256, 256), 256)
            cos, slo, shi = cos_ref[rows, :], sinlo_ref[rows, :], sinhi_ref[rows, :]
            ones = jnp.ones((256, LANES), BF16)
            ksa_s[rows, 0:HD] = _rope(ks_ref[rows, :], cos, slo, shi).astype(BF16)
            ksa_s[rows, HD:] = blk_ref[rows, :]
            kwr_s[rows, :] = _rope(kw_ref[rows, :], cos, slo, shi).astype(BF16)
            vsa_s[rows, 0:HD] = vs_ref[rows, :].astype(BF16)
            vsa_s[rows, HD:] = ones
            vwa_s[rows, 0:HD] = vw_ref[rows, :].astype(BF16)
            vwa_s[rows, HD:] = ones
            return 0
        lax.fori_loop(0, S // 256, prep, 0)

    nb = S // SLC_BLOCK
    TK = NSA_TK
    ovt = ovt_ref[...]
    kc, vc = kc_ref[...], vc_ref[...]

    def front(u):
        tq0 = (qi * NSA_SUB + u) * TQ
        urows = slice(u * TQ, (u + 1) * TQ)
        trows = pl.ds(pl.multiple_of(tq0, TQ), TQ)
        cos_t, slo_t, shi_t = cos_ref[trows, :], sinlo_ref[trows, :], sinhi_ref[trows, :]
        heads = [q_ref[urows, r * HD:(r + 1) * HD] for r in range(R)]
        qc = jnp.concatenate([(h * scale).astype(BF16) for h in heads], axis=0)
        qr_heads = [_rope(h * (scale * LOG2E), cos_t, slo_t, shi_t).astype(BF16) for h in heads]
        t_tok = tq0 + lax.broadcasted_iota(jnp.int32, (TQ, 1), 0)
        t_col = jnp.concatenate([t_tok] * R, axis=0)

        s_c = _nt_dot(qc, kc)
        n_idx = lax.broadcasted_iota(jnp.int32, (1, N_CMP_PAD), 1)
        p_c = _masked_softmax(s_c, (n_idx * CMP_STRIDE + (CMP_BLOCK - 1)) <= t_col)
        p_cb = p_c.astype(BF16)
        o_cmp = _dot(p_cb, vc)

        imp_t = _nt_dot(ovt, p_cb[0:TQ])
        for r in range(1, R):
            imp_t = imp_t + _nt_dot(ovt, p_cb[r * TQ:(r + 1) * TQ])
        imp = imp_t[0:nb, :]
        j_idx = lax.broadcasted_iota(jnp.int32, (nb, TQ), 0)
        cur = (tq0 + lax.broadcasted_iota(jnp.int32, (nb, TQ), 1)) // SLC_BLOCK
        imp = jnp.where(j_idx <= cur, imp, INVALID_SCORE)
        imp = jnp.where((j_idx == 0) | (j_idx == cur) | (j_idx == cur - 1), FORCED_SCORE, imp)
        rank = jnp.zeros((nb, TQ), jnp.int32)
        for i in range(nb):
            ri = imp[i:i + 1, :]
            ahead = (ri > imp) | ((ri == imp) & (i < j_idx))
            rank = rank + jnp.where(ahead, 1, 0)
        blk_bias_t = jnp.where(rank < min(SLC_TOPK, nb), 0.0, MASKED)
        blk_bias_t = jnp.concatenate([blk_bias_t, jnp.zeros((LANES - nb, TQ), F32)], axis=0)
        blk_bias = jnp.transpose(blk_bias_t).astype(BF16)
        q_aug = jnp.concatenate([jnp.concatenate([h, blk_bias], axis=1) for h in qr_heads], axis=0)
        return dict(t_col=t_col, o_cmp=o_cmp, q_aug=q_aug, qr_heads=qr_heads)

    subs = [front(u) for u in range(NSA_SUB)]

    def sel_variant(n_keys):
        for u in range(NSA_SUB):
            q_aug, t_col = subs[u]["q_aug"], subs[u]["t_col"]
            for k0 in range(0, n_keys, TK):
                w = min(TK, n_keys - k0)
                s = _nt_dot(q_aug, ksa_s[k0:k0 + w, :])
                if k0 + w == n_keys:
                    kpos = k0 + lax.broadcasted_iota(jnp.int32, (1, w), 1)
                    s = jnp.where(kpos <= t_col, s, MASKED)
                ss_s[u][:, k0:k0 + w] = s
        for u in range(NSA_SUB):
            m = jnp.max(ss_s[u][:, 0:n_keys], axis=-1, keepdims=True)
            es_s[u][:, 0:n_keys] = jnp.exp2(ss_s[u][:, 0:n_keys] - m).astype(BF16)
        for u in range(NSA_SUB):
            acc = _dot(es_s[u][:, 0:n_keys], vsa_s[0:n_keys, :])
            osl_s[u] = acc[:, 0:HD] * (1.0 / acc[:, HD:])

    step = NSA_SUB * TQ
    for v in range(S // step):
        pl.when(qi == v)(lambda v=v: sel_variant((v + 1) * step))

    tq_step = qi * step
    start = pl.multiple_of(jnp.maximum(tq_step - WINDOW, 0), step)
    span = WINDOW + step
    diff = (tq_step + lax.broadcasted_iota(jnp.int32, (step, 1), 0)) - (
        start + lax.broadcasted_iota(jnp.int32, (1, span), 1))
    band = jnp.where((diff >= 0) & (diff < WINDOW), 0.0, MASKED)
    n_pair = R // 2
    for c in range(n_pair):
        q_pair = jnp.concatenate([subs[u]["qr_heads"][2 * c + rr] for rr in range(2) for u in range(NSA_SUB)], axis=0)
        s_w = _nt_dot(q_pair, kwr_s[pl.ds(start, span), :])
        for rr in range(2):
            sw_s[c][rr * step:(rr + 1) * step, :] = s_w[rr * step:(rr + 1) * step] + band
    for c in range(n_pair):
        m_w = jnp.max(sw_s[c][...], axis=-1, keepdims=True)
        ew_s[c][...] = jnp.exp2(sw_s[c][...] - m_w).astype(BF16)
    o_win = []
    for c in range(n_pair):
        acc_w = _dot(ew_s[c][...], vwa_s[pl.ds(start, span), :])
        o_win.append(acc_w[:, 0:HD] * (1.0 / acc_w[:, HD:]))

    misc = misc_ref[...]
    ng = 3 * R
    logits = jnp.where(g == 0, misc[:, MISC_GATE_LANE:MISC_GATE_LANE + ng],
                       misc[:, MISC_GATE_LANE + ng:MISC_GATE_LANE + 2 * ng])
    gate = jax.nn.sigmoid(logits)
    for u in range(NSA_SUB):
        urows = slice(u * TQ, (u + 1) * TQ)
        o_slc = osl_s[u]
        for r in range(R):
            rs = slice(r * TQ, (r + 1) * TQ)
            ws = slice((r % 2) * step + u * TQ, (r % 2) * step + (u + 1) * TQ)
            o = (gate[urows, 3 * r:3 * r + 1] * subs[u]["o_cmp"][rs] + gate[urows, 3 * r + 1:3 * r + 2] * o_slc[rs]
                 + gate[urows, 3 * r + 2:3 * r + 3] * o_win[r // 2][ws])
            o_ref[urows, r * HD:(r + 1) * HD] = o.astype(o_ref.dtype)


BF16_ROWS = 16


def _side_cast_plan(weights, n_steps):
    plan = []
    for w in weights:
        n_rows = w.shape[0]
        if n_rows % (n_steps * BF16_ROWS) == 0:
            plan.append((n_rows // n_steps, None))
        else:
            assert n_rows % LANES == 0 and n_rows // LANES <= n_steps
            plan.append((LANES, n_rows // LANES))
    return plan


def _nsa(u3, cmp_kv, cos_f, sin_lo, sin_hi, ovt, blk_onehot, side_weights):
    B, S, _ = u3.shape
    G, R, HD = NSA_KV_GROUPS, NSA_HPG, NSA_HD
    qb = R * HD // LANES
    rows = R * NSA_TQ
    step = NSA_SUB * NSA_TQ
    assert NSA_TK % step == 0
    n_q = S // step
    plan = _side_cast_plan(side_weights, B * G * n_q)

    def side_spec(w, slab_rows, n_slabs):
        def index(b, g, i):
            s = (b * G + g) * n_q + i
            return (s if n_slabs is None else jnp.minimum(s, n_slabs - 1), 0)
        return pl.BlockSpec((slab_rows, w.shape[1]), index)

    side_specs = [side_spec(w, *p) for w, p in zip(side_weights, plan)]
    full = lambda cb: pl.BlockSpec((None, S, HD), lambda b, g, i: (b, 0, cb + g))
    table = pl.BlockSpec((S, LANES), lambda b, g, i: (0, 0))
    kern = functools.partial(_nsa_kernel, len(side_weights), tuple(p[1] for p in plan))
    return pl.pallas_call(
        kern,
        grid=(B, G, n_q),
        in_specs=[
            pl.BlockSpec((None, step, R * HD), lambda b, g, i: (b, i, CB_NQ // qb + g)),
            pl.BlockSpec((None, step, LANES), lambda b, g, i: (b, i, CB_MISC)),
            pl.BlockSpec((None, None, None, N_CMP_PAD, HD), lambda b, g, i: (0, b, g, 0, 0)),
            pl.BlockSpec((None, None, None, N_CMP_PAD, HD), lambda b, g, i: (1, b, g, 0, 0)),
            full(CB_KS), full(CB_VS), full(CB_KW), full(CB_VW),
            table, table, table,
            pl.BlockSpec((LANES, N_CMP_PAD), lambda b, g, i: (0, 0)),
            table,
        ] + side_specs,
        out_specs=[pl.BlockSpec((None, step, R * HD), lambda b, g, i: (b, i, g))] + side_specs,
        out_shape=[jax.ShapeDtypeStruct((B, S, NSA_WIDTH), BF16)]
        + [jax.ShapeDtypeStruct(w.shape, BF16) for w in side_weights],
        scratch_shapes=[pltpu.VMEM((S, 2 * HD), BF16), pltpu.VMEM((S, 2 * HD), BF16),
                        pltpu.VMEM((S, HD), BF16), pltpu.VMEM((S, 2 * HD), BF16),
                        pltpu.VMEM((NSA_SUB, rows, HD), F32)]
        + [pltpu.VMEM((rows, S), F32)] * NSA_SUB + [pltpu.VMEM((rows, S), BF16)] * NSA_SUB
        + [pltpu.VMEM((2 * step, WINDOW + step), F32)] * (R // 2)
        + [pltpu.VMEM((2 * step, WINDOW + step), BF16)] * (R // 2),
        compiler_params=pltpu.CompilerParams(
            dimension_semantics=("arbitrary", "arbitrary", "arbitrary"), vmem_limit_bytes=VMEM_LIMIT),
        name="nsa",
    )(u3, u3, cmp_kv, cmp_kv, u3, u3, u3, u3, cos_f, sin_lo, sin_hi, ovt, blk_onehot, *side_weights)


def _out_proj_kernel(og_ref, on_ref, x_ref, w_ref, g_ref, b_ref, h_ref, hb_ref):
    mix = _dot(og_ref[...], w_ref[0:GLA_WIDTH, :]) + _dot(on_ref[...], w_ref[GLA_WIDTH:, :])
    h = _layer_norm(DEEPNORM_ALPHA * x_ref[...] + mix, g_ref[...], b_ref[...])
    h_ref[...] = h
    hb_ref[...] = h.astype(BF16)


def _out_proj(o_gla, o_nsa, x2, w_bf, g, b):
    T, D = x2.shape
    tile = lambda w: pl.BlockSpec((OUT_TM, w), lambda i: (i, 0))
    const = lambda s: pl.BlockSpec(s, lambda i: (0, 0))
    return pl.pallas_call(
        _out_proj_kernel,
        grid=(T // OUT_TM,),
        in_specs=[tile(GLA_WIDTH), tile(NSA_WIDTH), tile(D), const(w_bf.shape), const((1, D)), const((1, D))],
        out_specs=[tile(D), tile(D)],
        out_shape=[jax.ShapeDtypeStruct((T, D), F32), jax.ShapeDtypeStruct((T, D), BF16)],
        compiler_params=pltpu.CompilerParams(
            dimension_semantics=("parallel",), vmem_limit_bytes=VMEM_LIMIT),
        name="out_proj",
    )(o_gla, o_nsa, x2, w_bf, g, b)


def _ffn_kernel(hb_ref, h_ref, w1_ref, w3_ref, w2_ref, g_ref, b_ref, o_ref, acc_s):
    f = pl.program_id(1)

    @pl.when(f == 0)
    def _():
        acc_s[...] = jnp.zeros(acc_s.shape, F32)

    hb = hb_ref[...]
    a = _dot(hb, w1_ref[...])
    c = _dot(hb, w3_ref[...])
    acc_s[...] += _dot((a * jax.nn.sigmoid(a) * c).astype(BF16), w2_ref[...])

    @pl.when(f == pl.num_programs(1) - 1)
    def _():
        o_ref[...] = _layer_norm(DEEPNORM_ALPHA * h_ref[...] + acc_s[...], g_ref[...], b_ref[...])


def _ffn(h_bf, h, w1, w3, w2, g, b):
    T, D = h.shape
    F = w1.shape[1]
    return pl.pallas_call(
        _ffn_kernel,
        grid=(T // FFN_TM, F // FFN_TF),
        in_specs=[
            pl.BlockSpec((FFN_TM, D), lambda i, f: (i, 0)),
            pl.BlockSpec((FFN_TM, D), lambda i, f: (i, 0)),
            pl.BlockSpec((D, FFN_TF), lambda i, f: (0, f)),
            pl.BlockSpec((D, FFN_TF), lambda i, f: (0, f)),
            pl.BlockSpec((FFN_TF, D), lambda i, f: (f, 0)),
            pl.BlockSpec((1, D), lambda i, f: (0, 0)),
            pl.BlockSpec((1, D), lambda i, f: (0, 0)),
        ],
        out_specs=pl.BlockSpec((FFN_TM, D), lambda i, f: (i, 0)),
        out_shape=jax.ShapeDtypeStruct((T, D), F32),
        scratch_shapes=[pltpu.VMEM((FFN_TM, D), F32)],
        compiler_params=pltpu.CompilerParams(
            dimension_semantics=("parallel", "arbitrary"), vmem_limit_bytes=VMEM_LIMIT),
        name="ffn",
    )(h_bf, h, w1, w3, w2, g, b)


def _rope_tables(S):
    half = ROPE_DIM // 2
    pos = jnp.arange(S, dtype=F32)
    inv = jnp.power(ROPE_THETA, -jnp.arange(0, ROPE_DIM, 2, dtype=F32) / ROPE_DIM)
    ang = pos[:, None] * inv[None, :]
    cos, sin = jnp.cos(ang), jnp.sin(ang)
    cos_f = jnp.concatenate([cos, cos, jnp.ones((S, LANES - ROPE_DIM), F32)], axis=1)
    sin_lo = jnp.concatenate([-sin, jnp.zeros((S, LANES - half), F32)], axis=1)
    sin_hi = jnp.concatenate([jnp.zeros((S, half), F32), sin, jnp.zeros((S, LANES - ROPE_DIM), F32)], axis=1)
    return cos_f, sin_lo, sin_hi


def _selection_constants(S):
    n_cmp = (S - CMP_BLOCK) // CMP_STRIDE + 1
    nb = S // SLC_BLOCK
    c_start = np.arange(n_cmp) * CMP_STRIDE
    b_start = np.arange(nb) * SLC_BLOCK
    overlap = ((c_start[:, None] < b_start[None, :] + SLC_BLOCK) &
               (b_start[None, :] < c_start[:, None] + CMP_BLOCK)).astype(np.float32)
    ovt = np.zeros((LANES, N_CMP_PAD), np.float32)
    ovt[:nb, :n_cmp] = overlap.T
    blk_onehot = ((np.arange(S) // SLC_BLOCK)[:, None] == np.arange(LANES)[None, :]).astype(np.float32)
    return jnp.asarray(ovt, BF16), jnp.asarray(blk_onehot, BF16)


def kernel(x, w_in, gla_gate_w2, gla_gate_b2, gla_norm_w, cmp_k_pos, cmp_k_w1, cmp_k_w2,
           cmp_v_pos, cmp_v_w1, cmp_v_w2, w_out, ln1_g, ln1_b, ffn_w1, ffn_w3, ffn_w2, ln2_g, ln2_b):
    B, S, D = x.shape
    T = B * S
    cos_f, sin_lo, sin_hi = _rope_tables(S)
    ovt, blk_onehot = _selection_constants(S)

    x2 = x.reshape(T, D)
    for l in range(DEPTH):
        u = _in_proj(x2, _regroup_w_in(w_in[l].T))
        u3 = u.reshape(B, S, U_WIDTH)
        o_gla = _gla(u3, gla_gate_w2[l], gla_gate_b2[l][None, :], gla_norm_w[l][None, :])
        cmp_kv = _compress(u3, jnp.stack([cmp_k_pos[l], cmp_v_pos[l]]),
                           jnp.stack([cmp_k_w1[l], cmp_v_w1[l]]).astype(BF16),
                           jnp.stack([cmp_k_w2[l], cmp_v_w2[l]]).astype(BF16))
        o_nsa, w_out_bf, w1_bf, w3_bf, w2_bf = _nsa(u3, cmp_kv, cos_f, sin_lo, sin_hi, ovt, blk_onehot,
                                                    [w_out[l], ffn_w1[l], ffn_w3[l], ffn_w2[l]])
        h, h_bf = _out_proj(o_gla.reshape(T, GLA_WIDTH), o_nsa.reshape(T, NSA_WIDTH), x2,
                            w_out_bf, ln1_g[l][None, :], ln1_b[l][None, :])
        x2 = _ffn(h_bf, h, w1_bf, w3_bf, w2_bf, ln2_g[l][None, :], ln2_b[l][None, :])
    return x2.reshape(B, S, D)
```

```python
import functools

import numpy as np
import jax
import jax.numpy as jnp
from jax import lax
from jax.experimental import pallas as pl
from jax.experimental.pallas import tpu as pltpu

F32 = jnp.float32
BF16 = jnp.bfloat16

D_MODEL = 2048
DEPTH = 1
GLA_HEADS = 4
GLA_DK = 128
GLA_DV = 256
GLA_GATE_RANK = 16
GLA_GATE_NORM = 16.0
GLA_CHUNK = 64
NSA_HEADS = 8
NSA_HD = 128
NSA_KV_GROUPS = 2
NSA_HPG = NSA_HEADS // NSA_KV_GROUPS
CMP_BLOCK = 32
CMP_STRIDE = 16
CMP_HIDDEN = 2 * NSA_HD
SLC_BLOCK = 64
SLC_TOPK = 16
WINDOW = 512
ROPE_THETA = 500000.0
ROPE_DIM = NSA_HD // 4
FFN_HIDDEN = 5632
DEEPNORM_ALPHA = (2.0 * DEPTH) ** 0.25
LN_EPS = 1e-5
FORCED_SCORE = 1e4
INVALID_SCORE = -1e4

GLA_WIDTH = GLA_HEADS * GLA_DV
NSA_WIDTH = NSA_HEADS * NSA_HD

LANES = 128
VMEM_LIMIT = 56 * 1024 * 1024

CB_GQ = 0
CB_GK = 4
CB_GV = 8
CB_GO = 16
CB_NQ = 24
CB_KC = 32
CB_KS = 36
CB_VS = 38
CB_KW = 40
CB_VW = 42
CB_MISC = 44
MISC_GATE_LANE = GLA_GATE_RANK
U_WIDTH = 45 * LANES

REGROUP_TC = 256
IN_TM, IN_TN = 512, U_WIDTH // 3
OUT_TM = 512
FFN_TM, FFN_TF = 512, 512
NSA_TQ = 128
NSA_SUB = 2
NSA_TK = 512
WIN_SPAN = WINDOW + NSA_TQ
MASKED = -1e30
LOG2E = 1.4426950408889634


def _nt_dot(a, b):
    return lax.dot_general(a, b, (((1,), (1,)), ((), ())), preferred_element_type=F32)


def _tn_dot(a, b):
    return lax.dot_general(a, b, (((0,), (0,)), ((), ())), preferred_element_type=F32)


def _dot(a, b):
    return jnp.dot(a, b, preferred_element_type=F32)


def _layer_norm(z, g, b):
    mu = jnp.mean(z, axis=-1, keepdims=True)
    zc = z - mu
    var = jnp.mean(zc * zc, axis=-1, keepdims=True)
    return zc * lax.rsqrt(var + LN_EPS) * g + b


def _masked_softmax(s, mask):
    sm = jnp.where(mask, s, -jnp.inf)
    m = jnp.max(sm, axis=-1, keepdims=True)
    m = jnp.where(m == -jnp.inf, 0.0, m)
    e = jnp.where(mask, jnp.exp(s - m), 0.0)
    den = jnp.sum(e, axis=-1, keepdims=True)
    return e * (1.0 / jnp.where(den > 0, den, 1.0))


def _regroup_kernel(w_ref, o_ref):
    cols = w_ref.shape[1]
    o_glr = 2 * GLA_HEADS * GLA_DK + 2 * GLA_HEADS * GLA_DV
    o_nq = o_glr + GLA_GATE_RANK
    o_gate = o_nq + NSA_HEADS * NSA_HD + 6 * NSA_KV_GROUPS * NSA_HD
    n_gate = NSA_HEADS * 3
    o_ref[0:o_glr, :] = w_ref[0:o_glr, :].astype(BF16)
    o_ref[o_glr:CB_MISC * LANES, :] = w_ref[o_nq:o_gate, :].astype(BF16)
    misc = jnp.concatenate([w_ref[o_glr:o_nq, :], w_ref[o_gate:o_gate + n_gate, :],
                            jnp.zeros((LANES - GLA_GATE_RANK - n_gate, cols), F32)], axis=0)
    o_ref[CB_MISC * LANES:, :] = misc.astype(BF16)


def _regroup_w_in(wt):
    n_in, D = wt.shape
    return pl.pallas_call(
        _regroup_kernel,
        grid=(D // REGROUP_TC,),
        in_specs=[pl.BlockSpec((n_in, REGROUP_TC), lambda i: (0, i))],
        out_specs=pl.BlockSpec((U_WIDTH, REGROUP_TC), lambda i: (0, i)),
        out_shape=jax.ShapeDtypeStruct((U_WIDTH, D), BF16),
        compiler_params=pltpu.CompilerParams(
            dimension_semantics=("parallel",), vmem_limit_bytes=VMEM_LIMIT),
        name="regroup_w_in",
    )(wt)


def _in_proj_kernel(x_ref, w_ref, o_ref):
    o_ref[...] = _nt_dot(x_ref[...].astype(BF16), w_ref[...])


def _in_proj(x2, wt_bf):
    T, D = x2.shape
    N = wt_bf.shape[0]
    return pl.pallas_call(
        _in_proj_kernel,
        grid=(N // IN_TN, T // IN_TM),
        in_specs=[pl.BlockSpec((IN_TM, D), lambda j, i: (i, 0)),
                  pl.BlockSpec((IN_TN, D), lambda j, i: (j, 0))],
        out_specs=pl.BlockSpec((IN_TM, IN_TN), lambda j, i: (i, j)),
        out_shape=jax.ShapeDtypeStruct((T, N), F32),
        compiler_params=pltpu.CompilerParams(
            dimension_semantics=("parallel", "parallel"), vmem_limit_bytes=VMEM_LIMIT),
        name="in_proj",
    )(x2, wt_bf)


GLA_ROWBLK = 256
GLA_TS = 512
GLA_UNROLL = 4


def _split3(a):
    hi = a.astype(BF16)
    r1 = a - hi.astype(F32)
    mid = r1.astype(BF16)
    lo = (r1 - mid.astype(F32)).astype(BF16)
    return hi, mid, lo


def _log_sigmoid(z):
    return jnp.minimum(z, 0.0) - jnp.log1p(jnp.exp(-jnp.abs(z)))


def _gla_kernel(q_ref, k_ref, v_ref, go_ref, misc_ref, w2_ref, b2_ref, nw_ref, o_ref,
                qd_s, ki_s, ks_s, dec_s, state_s):
    TS = q_ref.shape[0]
    H, C, DK, DV = GLA_HEADS, GLA_CHUNK, GLA_DK, GLA_DV
    scale = DK ** -0.5

    @pl.when(pl.program_id(1) == 0)
    def _():
        state_s[...] = jnp.zeros(state_s.shape, F32)

    r = lax.broadcasted_iota(jnp.int32, (GLA_ROWBLK, GLA_ROWBLK), 0)
    c = lax.broadcasted_iota(jnp.int32, (GLA_ROWBLK, GLA_ROWBLK), 1)
    same = (r // C) == (c // C)
    cum_m = jnp.where(same & (c <= r), 1.0, 0.0).astype(BF16)

    w2 = w2_ref[...].astype(BF16)
    b2 = b2_ref[...]
    for rb in range(TS // GLA_ROWBLK):
        rows = slice(rb * GLA_ROWBLK, (rb + 1) * GLA_ROWBLK)
        glr = misc_ref[rows, 0:GLA_GATE_RANK].astype(BF16)
        gk = _log_sigmoid(_dot(glr, w2) + b2) * (1.0 / GLA_GATE_NORM)
        hi, mid, lo = _split3(gk)
        bc_all = _dot(cum_m, hi) + _dot(cum_m, mid) + _dot(cum_m, lo)
        bl_all = jnp.concatenate(
            [jnp.broadcast_to(bc_all[j * C + C - 1:j * C + C, :], (C, H * DK)) for j in range(GLA_ROWBLK // C)],
            axis=0)
        for h in range(H):
            hk = slice(h * DK, (h + 1) * DK)
            bc = bc_all[:, hk]
            bl = bl_all[:, hk]
            q = q_ref[rows, hk] * scale
            k = k_ref[rows, hk]
            qd_s[h, rows, :] = (q * jnp.exp(bc)).astype(BF16)
            ki_s[h, rows, :] = (k * jnp.exp(-bc)).astype(BF16)
            ks_s[h, rows, :] = (k * jnp.exp(bl - bc)).astype(BF16)
            dec_s[h, rows, :] = jnp.exp(bl)

    ri = lax.broadcasted_iota(jnp.int32, (C, C), 0)
    ci = lax.broadcasted_iota(jnp.int32, (C, C), 1)
    causal = ci <= ri
    nw = nw_ref[...]

    def chunk(cidx, _):
        r0 = pl.multiple_of(cidx * C, C)
        rows = pl.ds(r0, C)
        for h in range(H):
            hv = slice(h * DV, (h + 1) * DV)
            qd = qd_s[h, rows, :]
            v = v_ref[rows, hv].astype(BF16)
            state_t = state_s[h]
            attn = jnp.where(causal, _nt_dot(qd, ki_s[h, rows, :]), 0.0).astype(BF16)
            o = _dot(attn, v) + _nt_dot(qd, state_t.astype(BF16))
            state_s[h] = state_t * dec_s[h, pl.ds(r0, 1), :] + _tn_dot(v, ks_s[h, rows, :])
            ms = jnp.mean(o * o, axis=-1, keepdims=True)
            o = o * lax.rsqrt(ms + LN_EPS) * nw
            go = go_ref[rows, hv]
            o_ref[rows, hv] = (o * (go * jax.nn.sigmoid(go))).astype(o_ref.dtype)
        return 0

    lax.fori_loop(0, TS // C, chunk, 0, unroll=GLA_UNROLL)


def _gla(u3, gate_w2, gate_b2, norm_w):
    B, S, _ = u3.shape
    H, DK, DV = GLA_HEADS, GLA_DK, GLA_DV
    qk_w, v_w = H * DK, H * DV
    tok = lambda w, off: pl.BlockSpec((None, GLA_TS, w), lambda b, s: (b, s, off * LANES // w))
    const = lambda shape: pl.BlockSpec(shape, lambda b, s: (0, 0))
    return pl.pallas_call(
        _gla_kernel,
        grid=(B, S // GLA_TS),
        in_specs=[tok(qk_w, CB_GQ), tok(qk_w, CB_GK), tok(v_w, CB_GV), tok(v_w, CB_GO), tok(LANES, CB_MISC),
                  const((GLA_GATE_RANK, qk_w)), const((1, qk_w)), const((1, DV))],
        out_specs=pl.BlockSpec((None, GLA_TS, v_w), lambda b, s: (b, s, 0)),
        out_shape=jax.ShapeDtypeStruct((B, S, GLA_WIDTH), BF16),
        scratch_shapes=[pltpu.VMEM((H, GLA_TS, DK), BF16), pltpu.VMEM((H, GLA_TS, DK), BF16),
                        pltpu.VMEM((H, GLA_TS, DK), BF16), pltpu.VMEM((H, GLA_TS, DK), F32),
                        pltpu.VMEM((H, DV, DK), F32)],
        compiler_params=pltpu.CompilerParams(
            dimension_semantics=("parallel", "arbitrary"), vmem_limit_bytes=VMEM_LIMIT),
        name="gla",
    )(u3, u3, u3, u3, u3, gate_w2, gate_b2, norm_w)


N_CMP_PAD = 128


def _gelu_tanh(x):
    return x * (0.5 * (1.0 + jnp.tanh(0.7978845608028654 * (x + 0.044715 * (x * x * x)))))


def _compress_kernel(kv_ref, pos_ref, w1_ref, w2_ref, o_ref):
    half = CMP_BLOCK // 2
    p0 = jnp.zeros((N_CMP_PAD, CMP_HIDDEN), F32)
    p1 = jnp.zeros((N_CMP_PAD, CMP_HIDDEN), F32)
    for l in range(half):
        x = kv_ref[pl.ds(l, N_CMP_PAD, stride=CMP_STRIDE), :]
        a0 = (x + pos_ref[l:l + 1, :]).astype(BF16)
        a1 = (x + pos_ref[half + l:half + l + 1, :]).astype(BF16)
        p0 = p0 + _dot(a0, w1_ref[l * NSA_HD:(l + 1) * NSA_HD, :])
        p1 = p1 + _dot(a1, w1_ref[(half + l) * NSA_HD:(half + l + 1) * NSA_HD, :])
    pre = p0 + pltpu.roll(p1, N_CMP_PAD - 1, 0)
    h = _gelu_tanh(pre).astype(BF16)
    out = _dot(h, w2_ref[...])
    row = lax.broadcasted_iota(jnp.int32, out.shape, 0)
    o_ref[...] = jnp.where(row < N_CMP_PAD - 1, out, 0.0).astype(o_ref.dtype)


def _compress(u3, pos, w1, w2):
    B, S, _ = u3.shape
    G = NSA_KV_GROUPS
    return pl.pallas_call(
        _compress_kernel,
        grid=(2, B, G),
        in_specs=[
            pl.BlockSpec((None, S, NSA_HD), lambda kv, b, g: (b, 0, CB_KC + 2 * kv + g)),
            pl.BlockSpec((None, CMP_BLOCK, NSA_HD), lambda kv, b, g: (kv, 0, 0)),
            pl.BlockSpec((None, CMP_BLOCK * NSA_HD, CMP_HIDDEN), lambda kv, b, g: (kv, 0, 0)),
            pl.BlockSpec((None, CMP_HIDDEN, NSA_HD), lambda kv, b, g: (kv, 0, 0)),
        ],
        out_specs=pl.BlockSpec((None, None, None, N_CMP_PAD, NSA_HD), lambda kv, b, g: (kv, b, g, 0, 0)),
        out_shape=jax.ShapeDtypeStruct((2, B, G, N_CMP_PAD, NSA_HD), BF16),
        compiler_params=pltpu.CompilerParams(
            dimension_semantics=("parallel", "parallel", "parallel"), vmem_limit_bytes=VMEM_LIMIT),
        name="compress",
    )(u3, pos, w1, w2)


def _rope(x, cos, sin_lo, sin_hi):
    half = ROPE_DIM // 2
    return x * cos + pltpu.roll(x, LANES - half, 1) * sin_lo + pltpu.roll(x, half, 1) * sin_hi


def _nsa_kernel(n_side, side_slabs, *refs):
    (q_ref, misc_ref, kc_ref, vc_ref, ks_ref, vs_ref, kw_ref, vw_ref,
     cos_ref, sinlo_ref, sinhi_ref, ovt_ref, blk_ref) = refs[0:13]
    side_in = refs[13:13 + n_side]
    o_ref = refs[13 + n_side]
    side_out = refs[14 + n_side:14 + 2 * n_side]
    ksa_s, vsa_s, kwr_s, vwa_s = refs[14 + 2 * n_side:18 + 2 * n_side]
    chain_scratch = refs[18 + 2 * n_side:]
    S = ks_ref.shape[0]
    R, TQ, HD = NSA_HPG, NSA_TQ, NSA_HD
    ss_s, es_s = chain_scratch[0:NSA_SUB], chain_scratch[NSA_SUB:2 * NSA_SUB]
    sw_s, ew_s = chain_scratch[2 * NSA_SUB:2 * NSA_SUB + R // 2], chain_scratch[2 * NSA_SUB + R // 2:]
    g = pl.program_id(1)
    qi = pl.program_id(2)
    scale = HD ** -0.5

    step_idx = (pl.program_id(0) * pl.num_programs(1) + g) * pl.num_programs(2) + qi
    for w_in_ref, w_out_ref, n_slabs in zip(side_in, side_out, side_slabs):
        def cast(w_in_ref=w_in_ref, w_out_ref=w_out_ref):
            w_out_ref[...] = w_in_ref[...].astype(BF16)
        if n_slabs is None:
            cast()
        else:
            pl.when(step_idx < n_slabs)(cast)

    @pl.when(qi == 0)
    def _():
        def prep(rb, _):
            rows = pl.ds(pl.multiple_of(rb * 256, 256), 256)
            cos, slo, shi = cos_ref[rows, :], sinlo_ref[rows, :], sinhi_ref[rows, :]
            ones = jnp.ones((256, LANES), BF16)
            ksa_s[rows, 0:HD] = _rope(ks_ref[rows, :], cos, slo, shi).astype(BF16)
            ksa_s[rows, HD:] = blk_ref[rows, :]
            kwr_s[rows, :] = _rope(kw_ref[rows, :], cos, slo, shi).astype(BF16)
            vsa_s[rows, 0:HD] = vs_ref[rows, :].astype(BF16)
            vsa_s[rows, HD:] = ones
            vwa_s[rows, 0:HD] = vw_ref[rows, :].astype(BF16)
            vwa_s[rows, HD:] = ones
            return 0
        lax.fori_loop(0, S // 256, prep, 0)

    nb = S // SLC_BLOCK
    TK = NSA_TK
    ovt = ovt_ref[...]
    kc, vc = kc_ref[...], vc_ref[...]

    def front(v, u):
        tq0 = (v * NSA_SUB + u) * TQ
        urows = slice(u * TQ, (u + 1) * TQ)
        trows = slice(tq0, tq0 + TQ)
        cos_t, slo_t, shi_t = cos_ref[trows, :], sinlo_ref[trows, :], sinhi_ref[trows, :]
        heads = [q_ref[urows, r * HD:(r + 1) * HD] for r in range(R)]
        qc = jnp.concatenate([(h * scale).astype(BF16) for h in heads], axis=0)
        qr_heads = [_rope(h * (scale * LOG2E), cos_t, slo_t, shi_t).astype(BF16) for h in heads]
        t_tok = tq0 + lax.broadcasted_iota(jnp.int32, (TQ, 1), 0)
        t_col = jnp.concatenate([t_tok] * R, axis=0)

        s_c = _nt_dot(qc, kc)
        n_idx = lax.broadcasted_iota(jnp.int32, (1, N_CMP_PAD), 1)
        p_c = _masked_softmax(s_c, (n_idx * CMP_STRIDE + (CMP_BLOCK - 1)) <= t_col)
        p_cb = p_c.astype(BF16)
        o_cmp = _dot(p_cb, vc)

        imp_t = _nt_dot(ovt, p_cb[0:TQ])
        for r in range(1, R):
            imp_t = imp_t + _nt_dot(ovt, p_cb[r * TQ:(r + 1) * TQ])
        imp = imp_t[0:nb, :]
        j_idx = lax.broadcasted_iota(jnp.int32, (nb, TQ), 0)
        cur = (tq0 + lax.broadcasted_iota(jnp.int32, (nb, TQ), 1)) // SLC_BLOCK
        imp = jnp.where(j_idx <= cur, imp, INVALID_SCORE)
        imp = jnp.where((j_idx == 0) | (j_idx == cur) | (j_idx == cur - 1), FORCED_SCORE, imp)
        rank = jnp.zeros((nb, TQ), jnp.int32)
        for i in range(nb):
            ri = imp[i:i + 1, :]
            ahead = (ri > imp) | ((ri == imp) & (i < j_idx))
            rank = rank + jnp.where(ahead, 1, 0)
        blk_bias_t = jnp.where(rank < min(SLC_TOPK, nb), 0.0, MASKED)
        blk_bias_t = jnp.concatenate([blk_bias_t, jnp.zeros((LANES - nb, TQ), F32)], axis=0)
        blk_bias = jnp.transpose(blk_bias_t).astype(BF16)
        q_aug = jnp.concatenate([jnp.concatenate([h, blk_bias], axis=1) for h in qr_heads], axis=0)
        return dict(t_col=t_col, o_cmp=o_cmp, q_aug=q_aug, qr_heads=qr_heads)

    step = NSA_SUB * TQ
    span = WINDOW + step
    n_pair = R // 2

    def step_body(v):
        subs = [front(v, u) for u in range(NSA_SUB)]
        n_keys = (v + 1) * step
        start = max(v * step - WINDOW, 0)
        diff = (v * step + lax.broadcasted_iota(jnp.int32, (step, 1), 0)) - (
            start + lax.broadcasted_iota(jnp.int32, (1, span), 1))
        band = jnp.where((diff >= 0) & (diff < WINDOW), 0.0, MASKED)

        def win_scores(c):
            q_pair = jnp.concatenate(
                [subs[u]["qr_heads"][2 * c + rr] for rr in range(2) for u in range(NSA_SUB)], axis=0)
            s_w = _nt_dot(q_pair, kwr_s[start:start + span, :])
            for rr in range(2):
                sw_s[c][rr * step:(rr + 1) * step, :] = s_w[rr * step:(rr + 1) * step] + band

        def win_softmax(c):
            m_w = jnp.max(sw_s[c][...], axis=-1, keepdims=True)
            ew_s[c][...] = jnp.exp2(sw_s[c][...] - m_w).astype(BF16)

        def win_values(c):
            acc_w = _dot(ew_s[c][...], vwa_s[start:start + span, :])
            return acc_w[:, 0:HD] * (1.0 / acc_w[:, HD:])

        def sel_scores(u):
            q_aug, t_col = subs[u]["q_aug"], subs[u]["t_col"]
            for k0 in range(0, n_keys, TK):
                w = min(TK, n_keys - k0)
                s = _nt_dot(q_aug, ksa_s[k0:k0 + w, :])
                if k0 + w == n_keys:
                    kpos = k0 + lax.broadcasted_iota(jnp.int32, (1, w), 1)
                    s = jnp.where(kpos <= t_col, s, MASKED)
                ss_s[u][:, k0:k0 + w] = s

        def sel_softmax(u):
            m = jnp.max(ss_s[u][:, 0:n_keys], axis=-1, keepdims=True)
            es_s[u][:, 0:n_keys] = jnp.exp2(ss_s[u][:, 0:n_keys] - m).astype(BF16)

        def sel_values(u):
            acc = _dot(es_s[u][:, 0:n_keys], vsa_s[0:n_keys, :])
            return acc[:, 0:HD] * (1.0 / acc[:, HD:])

        assert NSA_SUB == 2 and n_pair == 2
        win_scores(0)
        win_scores(1)
        sel_scores(0)
        win_softmax(0)
        sel_scores(1)
        win_softmax(1)
        o_win = [win_values(0)]
        sel_softmax(0)
        o_win.append(win_values(1))
        sel_softmax(1)
        o_slc = [sel_values(0), sel_values(1)]

        misc = misc_ref[...]
        ng = 3 * R
        logits = jnp.where(g == 0, misc[:, MISC_GATE_LANE:MISC_GATE_LANE + ng],
                           misc[:, MISC_GATE_LANE + ng:MISC_GATE_LANE + 2 * ng])
        gate = jax.nn.sigmoid(logits)
        for u in range(NSA_SUB):
            urows = slice(u * TQ, (u + 1) * TQ)
            for r in range(R):
                rs = slice(r * TQ, (r + 1) * TQ)
                ws = slice((r % 2) * step + u * TQ, (r % 2) * step + (u + 1) * TQ)
                o = (gate[urows, 3 * r:3 * r + 1] * subs[u]["o_cmp"][rs]
                     + gate[urows, 3 * r + 1:3 * r + 2] * o_slc[u][rs]
                     + gate[urows, 3 * r + 2:3 * r + 3] * o_win[r // 2][ws])
                o_ref[urows, r * HD:(r + 1) * HD] = o.astype(o_ref.dtype)

    for v in range(S // step):
        pl.when(qi == v)(functools.partial(step_body, v))


BF16_ROWS = 16


def _side_cast_plan(weights, n_steps):
    plan = []
    for w in weights:
        n_rows = w.shape[0]
        if n_rows % (n_steps * BF16_ROWS) == 0:
            plan.append((n_rows // n_steps, None))
        else:
            assert n_rows % LANES == 0 and n_rows // LANES <= n_steps
            plan.append((LANES, n_rows // LANES))
    return plan


def _nsa(u3, cmp_kv, cos_f, sin_lo, sin_hi, ovt, blk_onehot, side_weights):
    B, S, _ = u3.shape
    G, R, HD = NSA_KV_GROUPS, NSA_HPG, NSA_HD
    qb = R * HD // LANES
    rows = R * NSA_TQ
    step = NSA_SUB * NSA_TQ
    assert NSA_TK % step == 0
    n_q = S // step
    plan = _side_cast_plan(side_weights, B * G * n_q)

    def side_spec(w, slab_rows, n_slabs):
        def index(b, g, i):
            s = (b * G + g) * n_q + i
            return (s if n_slabs is None else jnp.minimum(s, n_slabs - 1), 0)
        return pl.BlockSpec((slab_rows, w.shape[1]), index)

    side_specs = [side_spec(w, *p) for w, p in zip(side_weights, plan)]
    full = lambda cb: pl.BlockSpec((None, S, HD), lambda b, g, i: (b, 0, cb + g))
    table = pl.BlockSpec((S, LANES), lambda b, g, i: (0, 0))
    kern = functools.partial(_nsa_kernel, len(side_weights), tuple(p[1] for p in plan))
    return pl.pallas_call(
        kern,
        grid=(B, G, n_q),
        in_specs=[
            pl.BlockSpec((None, step, R * HD), lambda b, g, i: (b, i, CB_NQ // qb + g)),
            pl.BlockSpec((None, step, LANES), lambda b, g, i: (b, i, CB_MISC)),
            pl.BlockSpec((None, None, None, N_CMP_PAD, HD), lambda b, g, i: (0, b, g, 0, 0)),
            pl.BlockSpec((None, None, None, N_CMP_PAD, HD), lambda b, g, i: (1, b, g, 0, 0)),
            full(CB_KS), full(CB_VS), full(CB_KW), full(CB_VW),
            table, table, table,
            pl.BlockSpec((LANES, N_CMP_PAD), lambda b, g, i: (0, 0)),
            table,
        ] + side_specs,
        out_specs=[pl.BlockSpec((None, step, R * HD), lambda b, g, i: (b, i, g))] + side_specs,
        out_shape=[jax.ShapeDtypeStruct((B, S, NSA_WIDTH), BF16)]
        + [jax.ShapeDtypeStruct(w.shape, BF16) for w in side_weights],
        scratch_shapes=[pltpu.VMEM((S, 2 * HD), BF16), pltpu.VMEM((S, 2 * HD), BF16),
                        pltpu.VMEM((S, HD), BF16), pltpu.VMEM((S, 2 * HD), BF16)]
        + [pltpu.VMEM((rows, S), F32)] * NSA_SUB + [pltpu.VMEM((rows, S), BF16)] * NSA_SUB
        + [pltpu.VMEM((2 * step, WINDOW + step), F32)] * (R // 2)
        + [pltpu.VMEM((2 * step, WINDOW + step), BF16)] * (R // 2),
        compiler_params=pltpu.CompilerParams(
            dimension_semantics=("arbitrary", "arbitrary", "arbitrary"), vmem_limit_bytes=VMEM_LIMIT),
        name="nsa",
    )(u3, u3, cmp_kv, cmp_kv, u3, u3, u3, u3, cos_f, sin_lo, sin_hi, ovt, blk_onehot, *side_weights)


def _out_proj_kernel(og_ref, on_ref, x_ref, w_ref, g_ref, b_ref, h_ref, hb_ref):
    mix = _dot(og_ref[...], w_ref[0:GLA_WIDTH, :]) + _dot(on_ref[...], w_ref[GLA_WIDTH:, :])
    h = _layer_norm(DEEPNORM_ALPHA * x_ref[...] + mix, g_ref[...], b_ref[...])
    h_ref[...] = h
    hb_ref[...] = h.astype(BF16)


def _out_proj(o_gla, o_nsa, x2, w_bf, g, b):
    T, D = x2.shape
    tile = lambda w: pl.BlockSpec((OUT_TM, w), lambda i: (i, 0))
    const = lambda s: pl.BlockSpec(s, lambda i: (0, 0))
    return pl.pallas_call(
        _out_proj_kernel,
        grid=(T // OUT_TM,),
        in_specs=[tile(GLA_WIDTH), tile(NSA_WIDTH), tile(D), const(w_bf.shape), const((1, D)), const((1, D))],
        out_specs=[tile(D), tile(D)],
        out_shape=[jax.ShapeDtypeStruct((T, D), F32), jax.ShapeDtypeStruct((T, D), BF16)],
        compiler_params=pltpu.CompilerParams(
            dimension_semantics=("parallel",), vmem_limit_bytes=VMEM_LIMIT),
        name="out_proj",
    )(o_gla, o_nsa, x2, w_bf, g, b)


def _ffn_kernel(hb_ref, h_ref, w1_ref, w3_ref, w2_ref, g_ref, b_ref, o_ref, acc_s):
    f = pl.program_id(1)

    @pl.when(f == 0)
    def _():
        acc_s[...] = jnp.zeros(acc_s.shape, F32)

    hb = hb_ref[...]
    a = _dot(hb, w1_ref[...])
    c = _dot(hb, w3_ref[...])
    acc_s[...] += _dot((a * jax.nn.sigmoid(a) * c).astype(BF16), w2_ref[...])

    @pl.when(f == pl.num_programs(1) - 1)
    def _():
        o_ref[...] = _layer_norm(DEEPNORM_ALPHA * h_ref[...] + acc_s[...], g_ref[...], b_ref[...])


def _ffn(h_bf, h, w1, w3, w2, g, b):
    T, D = h.shape
    F = w1.shape[1]
    return pl.pallas_call(
        _ffn_kernel,
        grid=(T // FFN_TM, F // FFN_TF),
        in_specs=[
            pl.BlockSpec((FFN_TM, D), lambda i, f: (i, 0)),
            pl.BlockSpec((FFN_TM, D), lambda i, f: (i, 0)),
            pl.BlockSpec((D, FFN_TF), lambda i, f: (0, f)),
            pl.BlockSpec((D, FFN_TF), lambda i, f: (0, f)),
            pl.BlockSpec((FFN_TF, D), lambda i, f: (f, 0)),
            pl.BlockSpec((1, D), lambda i, f: (0, 0)),
            pl.BlockSpec((1, D), lambda i, f: (0, 0)),
        ],
        out_specs=pl.BlockSpec((FFN_TM, D), lambda i, f: (i, 0)),
        out_shape=jax.ShapeDtypeStruct((T, D), F32),
        scratch_shapes=[pltpu.VMEM((FFN_TM, D), F32)],
        compiler_params=pltpu.CompilerParams(
            dimension_semantics=("parallel", "arbitrary"), vmem_limit_bytes=VMEM_LIMIT),
        name="ffn",
    )(h_bf, h, w1, w3, w2, g, b)


def _rope_tables(S):
    half = ROPE_DIM // 2
    pos = jnp.arange(S, dtype=F32)
    inv = jnp.power(ROPE_THETA, -jnp.arange(0, ROPE_DIM, 2, dtype=F32) / ROPE_DIM)
    ang = pos[:, None] * inv[None, :]
    cos, sin = jnp.cos(ang), jnp.sin(ang)
    cos_f = jnp.concatenate([cos, cos, jnp.ones((S, LANES - ROPE_DIM), F32)], axis=1)
    sin_lo = jnp.concatenate([-sin, jnp.zeros((S, LANES - half), F32)], axis=1)
    sin_hi = jnp.concatenate([jnp.zeros((S, half), F32), sin, jnp.zeros((S, LANES - ROPE_DIM), F32)], axis=1)
    return cos_f, sin_lo, sin_hi


def _selection_constants(S):
    n_cmp = (S - CMP_BLOCK) // CMP_STRIDE + 1
    nb = S // SLC_BLOCK
    c_start = np.arange(n_cmp) * CMP_STRIDE
    b_start = np.arange(nb) * SLC_BLOCK
    overlap = ((c_start[:, None] < b_start[None, :] + SLC_BLOCK) &
               (b_start[None, :] < c_start[:, None] + CMP_BLOCK)).astype(np.float32)
    ovt = np.zeros((LANES, N_CMP_PAD), np.float32)
    ovt[:nb, :n_cmp] = overlap.T
    blk_onehot = ((np.arange(S) // SLC_BLOCK)[:, None] == np.arange(LANES)[None, :]).astype(np.float32)
    return jnp.asarray(ovt, BF16), jnp.asarray(blk_onehot, BF16)


def kernel(x, w_in, gla_gate_w2, gla_gate_b2, gla_norm_w, cmp_k_pos, cmp_k_w1, cmp_k_w2,
           cmp_v_pos, cmp_v_w1, cmp_v_w2, w_out, ln1_g, ln1_b, ffn_w1, ffn_w3, ffn_w2, ln2_g, ln2_b):
    B, S, D = x.shape
    T = B * S
    cos_f, sin_lo, sin_hi = _rope_tables(S)
    ovt, blk_onehot = _selection_constants(S)

    x2 = x.reshape(T, D)
    for l in range(DEPTH):
        u = _in_proj(x2, _regroup_w_in(w_in[l].T))
        u3 = u.reshape(B, S, U_WIDTH)
        o_gla = _gla(u3, gla_gate_w2[l], gla_gate_b2[l][None, :], gla_norm_w[l][None, :])
        cmp_kv = _compress(u3, jnp.stack([cmp_k_pos[l], cmp_v_pos[l]]),
                           jnp.stack([cmp_k_w1[l], cmp_v_w1[l]]).astype(BF16),
                           jnp.stack([cmp_k_w2[l], cmp_v_w2[l]]).astype(BF16))
        o_nsa, w_out_bf, w1_bf, w3_bf, w2_bf = _nsa(u3, cmp_kv, cos_f, sin_lo, sin_hi, ovt, blk_onehot,
                                                    [w_out[l], ffn_w1[l], ffn_w3[l], ffn_w2[l]])
        h, h_bf = _out_proj(o_gla.reshape(T, GLA_WIDTH), o_nsa.reshape(T, NSA_WIDTH), x2,
                            w_out_bf, ln1_g[l][None, :], ln1_b[l][None, :])
        x2 = _ffn(h_bf, h, w1_bf, w3_bf, w2_bf, ln2_g[l][None, :], ln2_b[l][None, :])
    return x2.reshape(B, S, D)
```

```python
import functools

import numpy as np
import jax
import jax.numpy as jnp
from jax import lax
from jax.experimental import pallas as pl
from jax.experimental.pallas import tpu as pltpu

F32 = jnp.float32
BF16 = jnp.bfloat16

D_MODEL = 2048
DEPTH = 1
GLA_HEADS = 4
GLA_DK = 128
GLA_DV = 256
GLA_GATE_RANK = 16
GLA_GATE_NORM = 16.0
GLA_CHUNK = 64
NSA_HEADS = 8
NSA_HD = 128
NSA_KV_GROUPS = 2
NSA_HPG = NSA_HEADS // NSA_KV_GROUPS
CMP_BLOCK = 32
CMP_STRIDE = 16
CMP_HIDDEN = 2 * NSA_HD
SLC_BLOCK = 64
SLC_TOPK = 16
WINDOW = 512
ROPE_THETA = 500000.0
ROPE_DIM = NSA_HD // 4
FFN_HIDDEN = 5632
DEEPNORM_ALPHA = (2.0 * DEPTH) ** 0.25
LN_EPS = 1e-5
FORCED_SCORE = 1e4
INVALID_SCORE = -1e4

GLA_WIDTH = GLA_HEADS * GLA_DV
NSA_WIDTH = NSA_HEADS * NSA_HD

LANES = 128
VMEM_LIMIT = 56 * 1024 * 1024

CB_GQ = 0
CB_GK = 4
CB_GV = 8
CB_GO = 16
CB_NQ = 24
CB_KC = 32
CB_KS = 36
CB_VS = 38
CB_KW = 40
CB_VW = 42
CB_MISC = 44
MISC_GATE_LANE = GLA_GATE_RANK
U_WIDTH = 45 * LANES

REGROUP_TC = 256
IN_TM, IN_TN = 512, U_WIDTH // 3
OUT_TM = 512
FFN_TM, FFN_TF = 512, 512
NSA_TQ = 128
NSA_SUB = 2
NSA_TK = 512
WIN_SPAN = WINDOW + NSA_TQ
MASKED = -1e30
LOG2E = 1.4426950408889634


def _nt_dot(a, b):
    return lax.dot_general(a, b, (((1,), (1,)), ((), ())), preferred_element_type=F32)


def _tn_dot(a, b):
    return lax.dot_general(a, b, (((0,), (0,)), ((), ())), preferred_element_type=F32)


def _dot(a, b):
    return jnp.dot(a, b, preferred_element_type=F32)


def _layer_norm(z, g, b):
    mu = jnp.mean(z, axis=-1, keepdims=True)
    zc = z - mu
    var = jnp.mean(zc * zc, axis=-1, keepdims=True)
    return zc * lax.rsqrt(var + LN_EPS) * g + b


def _masked_softmax(s, mask):
    sm = jnp.where(mask, s, -jnp.inf)
    m = jnp.max(sm, axis=-1, keepdims=True)
    m = jnp.where(m == -jnp.inf, 0.0, m)
    e = jnp.where(mask, jnp.exp(s - m), 0.0)
    den = jnp.sum(e, axis=-1, keepdims=True)
    return e * (1.0 / jnp.where(den > 0, den, 1.0))


def _regroup_kernel(w_ref, o_ref):
    cols = w_ref.shape[1]
    o_glr = 2 * GLA_HEADS * GLA_DK + 2 * GLA_HEADS * GLA_DV
    o_nq = o_glr + GLA_GATE_RANK
    o_gate = o_nq + NSA_HEADS * NSA_HD + 6 * NSA_KV_GROUPS * NSA_HD
    n_gate = NSA_HEADS * 3
    o_ref[0:o_glr, :] = w_ref[0:o_glr, :].astype(BF16)
    o_ref[o_glr:CB_MISC * LANES, :] = w_ref[o_nq:o_gate, :].astype(BF16)
    misc = jnp.concatenate([w_ref[o_glr:o_nq, :], w_ref[o_gate:o_gate + n_gate, :],
                            jnp.zeros((LANES - GLA_GATE_RANK - n_gate, cols), F32)], axis=0)
    o_ref[CB_MISC * LANES:, :] = misc.astype(BF16)


def _regroup_w_in(wt):
    n_in, D = wt.shape
    return pl.pallas_call(
        _regroup_kernel,
        grid=(D // REGROUP_TC,),
        in_specs=[pl.BlockSpec((n_in, REGROUP_TC), lambda i: (0, i))],
        out_specs=pl.BlockSpec((U_WIDTH, REGROUP_TC), lambda i: (0, i)),
        out_shape=jax.ShapeDtypeStruct((U_WIDTH, D), BF16),
        compiler_params=pltpu.CompilerParams(
            dimension_semantics=("parallel",), vmem_limit_bytes=VMEM_LIMIT),
        name="regroup_w_in",
    )(wt)


def _in_proj_kernel(x_ref, w_ref, o_ref):
    o_ref[...] = _nt_dot(x_ref[...].astype(BF16), w_ref[...])


def _in_proj(x2, wt_bf):
    T, D = x2.shape
    N = wt_bf.shape[0]
    return pl.pallas_call(
        _in_proj_kernel,
        grid=(N // IN_TN, T // IN_TM),
        in_specs=[pl.BlockSpec((IN_TM, D), lambda j, i: (i, 0)),
                  pl.BlockSpec((IN_TN, D), lambda j, i: (j, 0))],
        out_specs=pl.BlockSpec((IN_TM, IN_TN), lambda j, i: (i, j)),
        out_shape=jax.ShapeDtypeStruct((T, N), F32),
        compiler_params=pltpu.CompilerParams(
            dimension_semantics=("parallel", "parallel"), vmem_limit_bytes=VMEM_LIMIT),
        name="in_proj",
    )(x2, wt_bf)


GLA_ROWBLK = 256
GLA_TS = 512


def _split3(a):
    hi = a.astype(BF16)
    r1 = a - hi.astype(F32)
    mid = r1.astype(BF16)
    lo = (r1 - mid.astype(F32)).astype(BF16)
    return hi, mid, lo


def _log_sigmoid(z):
    return jnp.minimum(z, 0.0) - jnp.log1p(jnp.exp(-jnp.abs(z)))


def _gla_kernel(q_ref, k_ref, v_ref, go_ref, misc_ref, w2_ref, b2_ref, nw_ref, o_ref,
                qd_s, ki_s, ks_s, dec_s, state_s, sbf_s):
    TS = q_ref.shape[0]
    H, C, DK, DV = GLA_HEADS, GLA_CHUNK, GLA_DK, GLA_DV
    scale = DK ** -0.5

    @pl.when(pl.program_id(1) == 0)
    def _():
        state_s[...] = jnp.zeros(state_s.shape, F32)

    r = lax.broadcasted_iota(jnp.int32, (GLA_ROWBLK, GLA_ROWBLK), 0)
    c = lax.broadcasted_iota(jnp.int32, (GLA_ROWBLK, GLA_ROWBLK), 1)
    chunk_causal = ((r // C) == (c // C)) & (c <= r)
    cum_m = jnp.where(chunk_causal, 1.0, 0.0).astype(BF16)

    w2 = w2_ref[...].astype(BF16)
    b2 = b2_ref[...]
    for rb in range(TS // GLA_ROWBLK):
        rows = slice(rb * GLA_ROWBLK, (rb + 1) * GLA_ROWBLK)
        glr = misc_ref[rows, 0:GLA_GATE_RANK].astype(BF16)
        gk = _log_sigmoid(_dot(glr, w2) + b2) * (1.0 / GLA_GATE_NORM)
        hi, mid, lo = _split3(gk)
        bc_all = _dot(cum_m, hi) + _dot(cum_m, mid) + _dot(cum_m, lo)
        bl_all = jnp.concatenate(
            [jnp.broadcast_to(bc_all[j * C + C - 1:j * C + C, :], (C, H * DK)) for j in range(GLA_ROWBLK // C)],
            axis=0)
        for h in range(H):
            hk = slice(h * DK, (h + 1) * DK)
            bc = bc_all[:, hk]
            bl = bl_all[:, hk]
            q = q_ref[rows, hk] * scale
            k = k_ref[rows, hk]
            qd_s[h, rows, :] = (q * jnp.exp(bc)).astype(BF16)
            ki_s[h, rows, :] = (k * jnp.exp(-bc)).astype(BF16)
            ks_s[h, rows, :] = (k * jnp.exp(bl - bc)).astype(BF16)
            dec_s[h, rows, :] = jnp.exp(bl)

    n_chunk = TS // C
    nw = nw_ref[...]
    for h in range(H):
        hv = slice(h * DV, (h + 1) * DV)
        kv = [_tn_dot(ks_s[h, c * C:(c + 1) * C, :], v_ref[c * C:(c + 1) * C, hv].astype(BF16))
              for c in range(n_chunk)]
        dec_rows = jnp.concatenate([dec_s[h, c * C:c * C + 1, :] for c in range(n_chunk)]
                                   + [jnp.zeros((LANES - n_chunk, DK), F32)], axis=0)
        dec_cols = jnp.transpose(dec_rows)
        state = state_s[h]
        for c in range(n_chunk):
            sbf_s[h, c] = state.astype(BF16)
            state = state * dec_cols[:, c:c + 1] + kv[c]
        state_s[h] = state

    for h in range(H):
        hv = slice(h * DV, (h + 1) * DV)
        for rb in range(TS // GLA_ROWBLK):
            rows = slice(rb * GLA_ROWBLK, (rb + 1) * GLA_ROWBLK)
            qd = qd_s[h, rows, :]
            attn = jnp.where(chunk_causal, _nt_dot(qd, ki_s[h, rows, :]), 0.0).astype(BF16)
            o = _dot(attn, v_ref[rows, hv].astype(BF16))
            inter = [_dot(qd_s[h, c * C:(c + 1) * C, :], sbf_s[h, c])
                     for c in range(rb * GLA_ROWBLK // C, (rb + 1) * GLA_ROWBLK // C)]
            o = o + jnp.concatenate(inter, axis=0)
            ms = jnp.mean(o * o, axis=-1, keepdims=True)
            o = o * lax.rsqrt(ms + LN_EPS) * nw
            go = go_ref[rows, hv]
            o_ref[rows, hv] = (o * (go * jax.nn.sigmoid(go))).astype(o_ref.dtype)


def _gla(u3, gate_w2, gate_b2, norm_w):
    B, S, _ = u3.shape
    H, DK, DV = GLA_HEADS, GLA_DK, GLA_DV
    qk_w, v_w = H * DK, H * DV
    tok = lambda w, off: pl.BlockSpec((None, GLA_TS, w), lambda b, s: (b, s, off * LANES // w))
    const = lambda shape: pl.BlockSpec(shape, lambda b, s: (0, 0))
    return pl.pallas_call(
        _gla_kernel,
        grid=(B, S // GLA_TS),
        in_specs=[tok(qk_w, CB_GQ), tok(qk_w, CB_GK), tok(v_w, CB_GV), tok(v_w, CB_GO), tok(LANES, CB_MISC),
                  const((GLA_GATE_RANK, qk_w)), const((1, qk_w)), const((1, DV))],
        out_specs=pl.BlockSpec((None, GLA_TS, v_w), lambda b, s: (b, s, 0)),
        out_shape=jax.ShapeDtypeStruct((B, S, GLA_WIDTH), BF16),
        scratch_shapes=[pltpu.VMEM((H, GLA_TS, DK), BF16), pltpu.VMEM((H, GLA_TS, DK), BF16),
                        pltpu.VMEM((H, GLA_TS, DK), BF16), pltpu.VMEM((H, GLA_TS, DK), F32),
                        pltpu.VMEM((H, DK, DV), F32), pltpu.VMEM((H, GLA_TS // GLA_CHUNK, DK, DV), BF16)],
        compiler_params=pltpu.CompilerParams(
            dimension_semantics=("parallel", "arbitrary"), vmem_limit_bytes=VMEM_LIMIT),
        name="gla",
    )(u3, u3, u3, u3, u3, gate_w2, gate_b2, norm_w)


N_CMP_PAD = 128


def _gelu_tanh(x):
    return x * (0.5 * (1.0 + jnp.tanh(0.7978845608028654 * (x + 0.044715 * (x * x * x)))))


def _compress_kernel(kv_ref, pos_ref, w1_ref, w2_ref, o_ref):
    half = CMP_BLOCK // 2
    p0 = jnp.zeros((N_CMP_PAD, CMP_HIDDEN), F32)
    p1 = jnp.zeros((N_CMP_PAD, CMP_HIDDEN), F32)
    for l in range(half):
        x = kv_ref[pl.ds(l, N_CMP_PAD, stride=CMP_STRIDE), :]
        a0 = (x + pos_ref[l:l + 1, :]).astype(BF16)
        a1 = (x + pos_ref[half + l:half + l + 1, :]).astype(BF16)
        p0 = p0 + _dot(a0, w1_ref[l * NSA_HD:(l + 1) * NSA_HD, :])
        p1 = p1 + _dot(a1, w1_ref[(half + l) * NSA_HD:(half + l + 1) * NSA_HD, :])
    pre = p0 + pltpu.roll(p1, N_CMP_PAD - 1, 0)
    h = _gelu_tanh(pre).astype(BF16)
    out = _dot(h, w2_ref[...])
    row = lax.broadcasted_iota(jnp.int32, out.shape, 0)
    o_ref[...] = jnp.where(row < N_CMP_PAD - 1, out, 0.0).astype(o_ref.dtype)


def _compress(u3, pos, w1, w2):
    B, S, _ = u3.shape
    G = NSA_KV_GROUPS
    return pl.pallas_call(
        _compress_kernel,
        grid=(2, B, G),
        in_specs=[
            pl.BlockSpec((None, S, NSA_HD), lambda kv, b, g: (b, 0, CB_KC + 2 * kv + g)),
            pl.BlockSpec((None, CMP_BLOCK, NSA_HD), lambda kv, b, g: (kv, 0, 0)),
            pl.BlockSpec((None, CMP_BLOCK * NSA_HD, CMP_HIDDEN), lambda kv, b, g: (kv, 0, 0)),
            pl.BlockSpec((None, CMP_HIDDEN, NSA_HD), lambda kv, b, g: (kv, 0, 0)),
        ],
        out_specs=pl.BlockSpec((None, None, None, N_CMP_PAD, NSA_HD), lambda kv, b, g: (kv, b, g, 0, 0)),
        out_shape=jax.ShapeDtypeStruct((2, B, G, N_CMP_PAD, NSA_HD), BF16),
        compiler_params=pltpu.CompilerParams(
            dimension_semantics=("parallel", "parallel", "parallel"), vmem_limit_bytes=VMEM_LIMIT),
        name="compress",
    )(u3, pos, w1, w2)


def _rope(x, cos, sin_lo, sin_hi):
    half = ROPE_DIM // 2
    return x * cos + pltpu.roll(x, LANES - half, 1) * sin_lo + pltpu.roll(x, half, 1) * sin_hi


def _nsa_kernel(n_side, side_slabs, *refs):
    (q_ref, misc_ref, kc_ref, vc_ref, ks_ref, vs_ref, kw_ref, vw_ref,
     cos_ref, sinlo_ref, sinhi_ref, ovt_ref, blk_ref) = refs[0:13]
    side_in = refs[13:13 + n_side]
    o_ref = refs[13 + n_side]
    side_out = refs[14 + n_side:14 + 2 * n_side]
    ksa_s, vsa_s, kwr_s, vwa_s = refs[14 + 2 * n_side:18 + 2 * n_side]
    chain_scratch = refs[18 + 2 * n_side:]
    S = ks_ref.shape[0]
    R, TQ, HD = NSA_HPG, NSA_TQ, NSA_HD
    ss_s, es_s = chain_scratch[0:NSA_SUB], chain_scratch[NSA_SUB:2 * NSA_SUB]
    sw_s, ew_s = chain_scratch[2 * NSA_SUB:2 * NSA_SUB + R // 2], chain_scratch[2 * NSA_SUB + R // 2:]
    g = pl.program_id(1)
    qi = pl.program_id(2)
    scale = HD ** -0.5

    step_idx = (pl.program_id(0) * pl.num_programs(1) + g) * pl.num_programs(2) + qi
    for w_in_ref, w_out_ref, n_slabs in zip(side_in, side_out, side_slabs):
        def cast(w_in_ref=w_in_ref, w_out_ref=w_out_ref):
            w_out_ref[...] = w_in_ref[...].astype(BF16)
        if n_slabs is None:
            cast()
        else:
            pl.when(step_idx < n_slabs)(cast)

    @pl.when(qi == 0)
    def _():
        def prep(rb, _):
            rows = pl.ds(pl.multiple_of(rb * 256, 256), 256)
            cos, slo, shi = cos_ref[rows, :], sinlo_ref[rows, :], sinhi_ref[rows, :]
            ones = jnp.ones((256, LANES), BF16)
            ksa_s[rows, 0:HD] = _rope(ks_ref[rows, :], cos, slo, shi).astype(BF16)
            ksa_s[rows, HD:] = blk_ref[rows, :]
            kwr_s[rows, :] = _rope(kw_ref[rows, :], cos, slo, shi).astype(BF16)
            vsa_s[rows, 0:HD] = vs_ref[rows, :].astype(BF16)
            vsa_s[rows, HD:] = ones
            vwa_s[rows, 0:HD] = vw_ref[rows, :].astype(BF16)
            vwa_s[rows, HD:] = ones
            return 0
        lax.fori_loop(0, S // 256, prep, 0)

    nb = S // SLC_BLOCK
    TK = NSA_TK
    ovt = ovt_ref[...]
    kc, vc = kc_ref[...], vc_ref[...]

    def front(v, u):
        tq0 = (v * NSA_SUB + u) * TQ
        urows = slice(u * TQ, (u + 1) * TQ)
        trows = slice(tq0, tq0 + TQ)
        cos_t, slo_t, shi_t = cos_ref[trows, :], sinlo_ref[trows, :], sinhi_ref[trows, :]
        heads = [q_ref[urows, r * HD:(r + 1) * HD] for r in range(R)]
        qc = jnp.concatenate([(h * scale).astype(BF16) for h in heads], axis=0)
        qr_heads = [_rope(h * (scale * LOG2E), cos_t, slo_t, shi_t).astype(BF16) for h in heads]
        t_tok = tq0 + lax.broadcasted_iota(jnp.int32, (TQ, 1), 0)
        t_col = jnp.concatenate([t_tok] * R, axis=0)

        s_c = _nt_dot(qc, kc)
        n_idx = lax.broadcasted_iota(jnp.int32, (1, N_CMP_PAD), 1)
        p_c = _masked_softmax(s_c, (n_idx * CMP_STRIDE + (CMP_BLOCK - 1)) <= t_col)
        p_cb = p_c.astype(BF16)
        o_cmp = _dot(p_cb, vc)

        imp = _nt_dot(ovt, p_cb[0:TQ])
        for r in range(1, R):
            imp = imp + _nt_dot(ovt, p_cb[r * TQ:(r + 1) * TQ])
        j_idx = lax.broadcasted_iota(jnp.int32, (nb, TQ), 0)
        cur = (tq0 + lax.broadcasted_iota(jnp.int32, (nb, TQ), 1)) // SLC_BLOCK
        imp = jnp.where(j_idx <= cur, imp, INVALID_SCORE)
        imp = jnp.where((j_idx == 0) | (j_idx == cur) | (j_idx == cur - 1), FORCED_SCORE, imp)
        rank = jnp.zeros((nb, TQ), jnp.int32)
        for i in range(nb):
            ri = imp[i:i + 1, :]
            ahead = (ri > imp) | ((ri == imp) & (i < j_idx))
            rank = rank + jnp.where(ahead, 1, 0)
        blk_bias_t = jnp.where(rank < min(SLC_TOPK, nb), 0.0, MASKED)
        blk_bias_t = jnp.concatenate([blk_bias_t, jnp.zeros((LANES - nb, TQ), F32)], axis=0)
        blk_bias = jnp.transpose(blk_bias_t).astype(BF16)
        q_aug = jnp.concatenate([jnp.concatenate([h, blk_bias], axis=1) for h in qr_heads], axis=0)
        return dict(t_col=t_col, o_cmp=o_cmp, q_aug=q_aug, qr_heads=qr_heads)

    step = NSA_SUB * TQ
    span = WINDOW + step
    n_pair = R // 2

    def step_body(v):
        subs = [front(v, u) for u in range(NSA_SUB)]
        n_keys = (v + 1) * step
        start = max(v * step - WINDOW, 0)
        diff = (v * step + lax.broadcasted_iota(jnp.int32, (step, 1), 0)) - (
            start + lax.broadcasted_iota(jnp.int32, (1, span), 1))
        band = jnp.where((diff >= 0) & (diff < WINDOW), 0.0, MASKED)

        def win_scores(c):
            q_pair = jnp.concatenate(
                [subs[u]["qr_heads"][2 * c + rr] for rr in range(2) for u in range(NSA_SUB)], axis=0)
            s_w = _nt_dot(q_pair, kwr_s[start:start + span, :])
            for rr in range(2):
                sw_s[c][rr * step:(rr + 1) * step, :] = s_w[rr * step:(rr + 1) * step] + band

        def win_softmax(c):
            m_w = jnp.max(sw_s[c][...], axis=-1, keepdims=True)
            ew_s[c][...] = jnp.exp2(sw_s[c][...] - m_w).astype(BF16)

        def win_values(c):
            acc_w = _dot(ew_s[c][...], vwa_s[start:start + span, :])
            return acc_w[:, 0:HD] * (1.0 / acc_w[:, HD:])

        def sel_scores(u):
            q_aug, t_col = subs[u]["q_aug"], subs[u]["t_col"]
            for k0 in range(0, n_keys, TK):
                w = min(TK, n_keys - k0)
                s = _nt_dot(q_aug, ksa_s[k0:k0 + w, :])
                if k0 + w == n_keys:
                    kpos = k0 + lax.broadcasted_iota(jnp.int32, (1, w), 1)
                    s = jnp.where(kpos <= t_col, s, MASKED)
                ss_s[u][:, k0:k0 + w] = s

        def sel_softmax(u):
            m = jnp.max(ss_s[u][:, 0:n_keys], axis=-1, keepdims=True)
            es_s[u][:, 0:n_keys] = jnp.exp2(ss_s[u][:, 0:n_keys] - m).astype(BF16)

        def sel_values(u):
            acc = _dot(es_s[u][:, 0:n_keys], vsa_s[0:n_keys, :])
            return acc[:, 0:HD] * (1.0 / acc[:, HD:])

        assert NSA_SUB == 2 and n_pair == 2
        win_scores(0)
        win_scores(1)
        sel_scores(0)
        win_softmax(0)
        sel_scores(1)
        win_softmax(1)
        o_win = [win_values(0)]
        sel_softmax(0)
        o_win.append(win_values(1))
        sel_softmax(1)
        o_slc = [sel_values(0), sel_values(1)]

        misc = misc_ref[...]
        ng = 3 * R
        logits = jnp.where(g == 0, misc[:, MISC_GATE_LANE:MISC_GATE_LANE + ng],
                           misc[:, MISC_GATE_LANE + ng:MISC_GATE_LANE + 2 * ng])
        gate = jax.nn.sigmoid(logits)
        for u in range(NSA_SUB):
            urows = slice(u * TQ, (u + 1) * TQ)
            for r in range(R):
                rs = slice(r * TQ, (r + 1) * TQ)
                ws = slice((r % 2) * step + u * TQ, (r % 2) * step + (u + 1) * TQ)
                o = (gate[urows, 3 * r:3 * r + 1] * subs[u]["o_cmp"][rs]
                     + gate[urows, 3 * r + 1:3 * r + 2] * o_slc[u][rs]
                     + gate[urows, 3 * r + 2:3 * r + 3] * o_win[r // 2][ws])
                o_ref[urows, r * HD:(r + 1) * HD] = o.astype(o_ref.dtype)

    for v in range(S // step):
        pl.when(qi == v)(functools.partial(step_body, v))


BF16_ROWS = 16


def _side_cast_plan(weights, n_steps):
    plan = []
    for w in weights:
        n_rows = w.shape[0]
        if n_rows % (n_steps * BF16_ROWS) == 0:
            plan.append((n_rows // n_steps, None))
        else:
            assert n_rows % LANES == 0 and n_rows // LANES <= n_steps
            plan.append((LANES, n_rows // LANES))
    return plan


def _nsa(u3, cmp_kv, cos_f, sin_lo, sin_hi, ovt, blk_onehot, side_weights):
    B, S, _ = u3.shape
    G, R, HD = NSA_KV_GROUPS, NSA_HPG, NSA_HD
    qb = R * HD // LANES
    rows = R * NSA_TQ
    step = NSA_SUB * NSA_TQ
    assert NSA_TK % step == 0
    n_q = S // step
    plan = _side_cast_plan(side_weights, B * G * n_q)

    def side_spec(w, slab_rows, n_slabs):
        def index(b, g, i):
            s = (b * G + g) * n_q + i
            return (s if n_slabs is None else jnp.minimum(s, n_slabs - 1), 0)
        return pl.BlockSpec((slab_rows, w.shape[1]), index)

    side_specs = [side_spec(w, *p) for w, p in zip(side_weights, plan)]
    full = lambda cb: pl.BlockSpec((None, S, HD), lambda b, g, i: (b, 0, cb + g))
    table = pl.BlockSpec((S, LANES), lambda b, g, i: (0, 0))
    kern = functools.partial(_nsa_kernel, len(side_weights), tuple(p[1] for p in plan))
    return pl.pallas_call(
        kern,
        grid=(B, G, n_q),
        in_specs=[
            pl.BlockSpec((None, step, R * HD), lambda b, g, i: (b, i, CB_NQ // qb + g)),
            pl.BlockSpec((None, step, LANES), lambda b, g, i: (b, i, CB_MISC)),
            pl.BlockSpec((None, None, None, N_CMP_PAD, HD), lambda b, g, i: (0, b, g, 0, 0)),
            pl.BlockSpec((None, None, None, N_CMP_PAD, HD), lambda b, g, i: (1, b, g, 0, 0)),
            full(CB_KS), full(CB_VS), full(CB_KW), full(CB_VW),
            table, table, table,
            pl.BlockSpec((S // SLC_BLOCK, N_CMP_PAD), lambda b, g, i: (0, 0)),
            table,
        ] + side_specs,
        out_specs=[pl.BlockSpec((None, step, R * HD), lambda b, g, i: (b, i, g))] + side_specs,
        out_shape=[jax.ShapeDtypeStruct((B, S, NSA_WIDTH), BF16)]
        + [jax.ShapeDtypeStruct(w.shape, BF16) for w in side_weights],
        scratch_shapes=[pltpu.VMEM((S, 2 * HD), BF16), pltpu.VMEM((S, 2 * HD), BF16),
                        pltpu.VMEM((S, HD), BF16), pltpu.VMEM((S, 2 * HD), BF16)]
        + [pltpu.VMEM((rows, S), F32)] * NSA_SUB + [pltpu.VMEM((rows, S), BF16)] * NSA_SUB
        + [pltpu.VMEM((2 * step, WINDOW + step), F32)] * (R // 2)
        + [pltpu.VMEM((2 * step, WINDOW + step), BF16)] * (R // 2),
        compiler_params=pltpu.CompilerParams(
            dimension_semantics=("arbitrary", "arbitrary", "arbitrary"), vmem_limit_bytes=VMEM_LIMIT),
        name="nsa",
    )(u3, u3, cmp_kv, cmp_kv, u3, u3, u3, u3, cos_f, sin_lo, sin_hi, ovt, blk_onehot, *side_weights)


def _out_proj_kernel(og_ref, on_ref, x_ref, w_ref, g_ref, b_ref, h_ref, hb_ref):
    half = og_ref.shape[0] // 2
    halves = [slice(0, half), slice(half, 2 * half)]
    mixes = [_dot(og_ref[rows, :], w_ref[0:GLA_WIDTH, :]) + _dot(on_ref[rows, :], w_ref[GLA_WIDTH:, :])
             for rows in halves]
    for rows, mix in zip(halves, mixes):
        h = _layer_norm(DEEPNORM_ALPHA * x_ref[rows, :] + mix, g_ref[...], b_ref[...])
        h_ref[rows, :] = h
        hb_ref[rows, :] = h.astype(BF16)


def _out_proj(o_gla, o_nsa, x2, w_bf, g, b):
    T, D = x2.shape
    tile = lambda w: pl.BlockSpec((OUT_TM, w), lambda i: (i, 0))
    const = lambda s: pl.BlockSpec(s, lambda i: (0, 0))
    return pl.pallas_call(
        _out_proj_kernel,
        grid=(T // OUT_TM,),
        in_specs=[tile(GLA_WIDTH), tile(NSA_WIDTH), tile(D), const(w_bf.shape), const((1, D)), const((1, D))],
        out_specs=[tile(D), tile(D)],
        out_shape=[jax.ShapeDtypeStruct((T, D), F32), jax.ShapeDtypeStruct((T, D), BF16)],
        compiler_params=pltpu.CompilerParams(
            dimension_semantics=("parallel",), vmem_limit_bytes=VMEM_LIMIT),
        name="out_proj",
    )(o_gla, o_nsa, x2, w_bf, g, b)


def _ffn_kernel(hb_ref, h_ref, w1_ref, w3_ref, w2_ref, g_ref, b_ref, o_ref, acc_s):
    f = pl.program_id(1)

    @pl.when(f == 0)
    def _():
        acc_s[...] = jnp.zeros(acc_s.shape, F32)

    hb = hb_ref[...]
    a = _dot(hb, w1_ref[...])
    c = _dot(hb, w3_ref[...])
    acc_s[...] += _dot((a * jax.nn.sigmoid(a) * c).astype(BF16), w2_ref[...])

    @pl.when(f == pl.num_programs(1) - 1)
    def _():
        o_ref[...] = _layer_norm(DEEPNORM_ALPHA * h_ref[...] + acc_s[...], g_ref[...], b_ref[...])


def _ffn(h_bf, h, w1, w3, w2, g, b):
    T, D = h.shape
    F = w1.shape[1]
    return pl.pallas_call(
        _ffn_kernel,
        grid=(T // FFN_TM, F // FFN_TF),
        in_specs=[
            pl.BlockSpec((FFN_TM, D), lambda i, f: (i, 0)),
            pl.BlockSpec((FFN_TM, D), lambda i, f: (i, 0)),
            pl.BlockSpec((D, FFN_TF), lambda i, f: (0, f)),
            pl.BlockSpec((D, FFN_TF), lambda i, f: (0, f)),
            pl.BlockSpec((FFN_TF, D), lambda i, f: (f, 0)),
            pl.BlockSpec((1, D), lambda i, f: (0, 0)),
            pl.BlockSpec((1, D), lambda i, f: (0, 0)),
        ],
        out_specs=pl.BlockSpec((FFN_TM, D), lambda i, f: (i, 0)),
        out_shape=jax.ShapeDtypeStruct((T, D), F32),
        scratch_shapes=[pltpu.VMEM((FFN_TM, D), F32)],
        compiler_params=pltpu.CompilerParams(
            dimension_semantics=("parallel", "arbitrary"), vmem_limit_bytes=VMEM_LIMIT),
        name="ffn",
    )(h_bf, h, w1, w3, w2, g, b)


def _rope_tables(S):
    half = ROPE_DIM // 2
    pos = np.arange(S, dtype=np.float64)
    inv = np.power(ROPE_THETA, -np.arange(0, ROPE_DIM, 2, dtype=np.float64) / ROPE_DIM)
    ang = pos[:, None] * inv[None, :]
    cos, sin = np.cos(ang).astype(np.float32), np.sin(ang).astype(np.float32)
    cos_f = np.concatenate([cos, cos, np.ones((S, LANES - ROPE_DIM), np.float32)], axis=1)
    sin_lo = np.concatenate([-sin, np.zeros((S, LANES - half), np.float32)], axis=1)
    sin_hi = np.concatenate([np.zeros((S, half), np.float32), sin, np.zeros((S, LANES - ROPE_DIM), np.float32)], axis=1)
    return jnp.asarray(cos_f), jnp.asarray(sin_lo), jnp.asarray(sin_hi)


def _selection_constants(S):
    n_cmp = (S - CMP_BLOCK) // CMP_STRIDE + 1
    nb = S // SLC_BLOCK
    c_start = np.arange(n_cmp) * CMP_STRIDE
    b_start = np.arange(nb) * SLC_BLOCK
    overlap = ((c_start[:, None] < b_start[None, :] + SLC_BLOCK) &
               (b_start[None, :] < c_start[:, None] + CMP_BLOCK)).astype(np.float32)
    ovt = np.zeros((nb, N_CMP_PAD), np.float32)
    ovt[:, :n_cmp] = overlap.T
    blk_onehot = ((np.arange(S) // SLC_BLOCK)[:, None] == np.arange(LANES)[None, :]).astype(np.float32)
    return jnp.asarray(ovt, BF16), jnp.asarray(blk_onehot, BF16)


def kernel(x, w_in, gla_gate_w2, gla_gate_b2, gla_norm_w, cmp_k_pos, cmp_k_w1, cmp_k_w2,
           cmp_v_pos, cmp_v_w1, cmp_v_w2, w_out, ln1_g, ln1_b, ffn_w1, ffn_w3, ffn_w2, ln2_g, ln2_b):
    B, S, D = x.shape
    T = B * S
    cos_f, sin_lo, sin_hi = _rope_tables(S)
    ovt, blk_onehot = _selection_constants(S)

    x2 = x.reshape(T, D)
    for l in range(DEPTH):
        u = _in_proj(x2, _regroup_w_in(w_in[l].T))
        u3 = u.reshape(B, S, U_WIDTH)
        o_gla = _gla(u3, gla_gate_w2[l], gla_gate_b2[l][None, :], gla_norm_w[l][None, :])
        cmp_kv = _compress(u3, jnp.stack([cmp_k_pos[l], cmp_v_pos[l]]),
                           jnp.stack([cmp_k_w1[l], cmp_v_w1[l]]).astype(BF16),
                           jnp.stack([cmp_k_w2[l], cmp_v_w2[l]]).astype(BF16))
        o_nsa, w_out_bf, w1_bf, w3_bf, w2_bf = _nsa(u3, cmp_kv, cos_f, sin_lo, sin_hi, ovt, blk_onehot,
                                                    [w_out[l], ffn_w1[l], ffn_w3[l], ffn_w2[l]])
        h, h_bf = _out_proj(o_gla.reshape(T, GLA_WIDTH), o_nsa.reshape(T, NSA_WIDTH), x2,
                            w_out_bf, ln1_g[l][None, :], ln1_b[l][None, :])
        x2 = _ffn(h_bf, h, w1_bf, w3_bf, w2_bf, ln2_g[l][None, :], ln2_b[l][None, :])
    return x2.reshape(B, S, D)
```

```python
import functools

import numpy as np
import jax
import jax.numpy as jnp
from jax import lax
from jax.experimental import pallas as pl
from jax.experimental.pallas import tpu as pltpu

F32 = jnp.float32
BF16 = jnp.bfloat16

D_MODEL = 2048
DEPTH = 1
GLA_HEADS = 4
GLA_DK = 128
GLA_DV = 256
GLA_GATE_RANK = 16
GLA_GATE_NORM = 16.0
GLA_CHUNK = 64
NSA_HEADS = 8
NSA_HD = 128
NSA_KV_GROUPS = 2
NSA_HPG = NSA_HEADS // NSA_KV_GROUPS
CMP_BLOCK = 32
CMP_STRIDE = 16
CMP_HIDDEN = 2 * NSA_HD
SLC_BLOCK = 64
SLC_TOPK = 16
WINDOW = 512
ROPE_THETA = 500000.0
ROPE_DIM = NSA_HD // 4
FFN_HIDDEN = 5632
DEEPNORM_ALPHA = (2.0 * DEPTH) ** 0.25
LN_EPS = 1e-5
FORCED_SCORE = 1e4
INVALID_SCORE = -1e4

GLA_WIDTH = GLA_HEADS * GLA_DV
NSA_WIDTH = NSA_HEADS * NSA_HD

LANES = 128
VMEM_LIMIT = 56 * 1024 * 1024

CB_GQ = 0
CB_GK = 4
CB_GV = 8
CB_GO = 16
CB_NQ = 24
CB_KC = 32
CB_KS = 36
CB_VS = 38
CB_KW = 40
CB_VW = 42
CB_MISC = 44
MISC_GATE_LANE = GLA_GATE_RANK
U_WIDTH = 45 * LANES

REGROUP_TC = 256
IN_TM, IN_TN = 512, U_WIDTH // 3
OUT_TM = 512
FFN_TM, FFN_TF = 512, 512
NSA_TQ = 128
NSA_SUB = 2
NSA_TK = 512
WIN_SPAN = WINDOW + NSA_TQ
MASKED = -1e30
LOG2E = 1.4426950408889634


def _nt_dot(a, b):
    return lax.dot_general(a, b, (((1,), (1,)), ((), ())), preferred_element_type=F32)


def _tn_dot(a, b):
    return lax.dot_general(a, b, (((0,), (0,)), ((), ())), preferred_element_type=F32)


def _dot(a, b):
    return jnp.dot(a, b, preferred_element_type=F32)


def _layer_norm(z, g, b):
    mu = jnp.mean(z, axis=-1, keepdims=True)
    zc = z - mu
    var = jnp.mean(zc * zc, axis=-1, keepdims=True)
    return zc * lax.rsqrt(var + LN_EPS) * g + b


def _masked_softmax(s, mask):
    sm = jnp.where(mask, s, -jnp.inf)
    m = jnp.max(sm, axis=-1, keepdims=True)
    m = jnp.where(m == -jnp.inf, 0.0, m)
    e = jnp.where(mask, jnp.exp(s - m), 0.0)
    den = jnp.sum(e, axis=-1, keepdims=True)
    return e * (1.0 / jnp.where(den > 0, den, 1.0))


def _regroup_kernel(w_ref, o_ref):
    cols = w_ref.shape[1]
    o_glr = 2 * GLA_HEADS * GLA_DK + 2 * GLA_HEADS * GLA_DV
    o_nq = o_glr + GLA_GATE_RANK
    o_gate = o_nq + NSA_HEADS * NSA_HD + 6 * NSA_KV_GROUPS * NSA_HD
    n_gate = NSA_HEADS * 3
    o_ref[0:o_glr, :] = w_ref[0:o_glr, :].astype(BF16)
    roped = (list(range(CB_NQ, CB_NQ + NSA_HEADS)) + list(range(CB_KS, CB_KS + NSA_KV_GROUPS))
             + list(range(CB_KW, CB_KW + NSA_KV_GROUPS)))
    for blk in range(CB_NQ, CB_MISC):
        dst, src = blk * LANES, blk * LANES + GLA_GATE_RANK
        pieces = _head_dim_order() if blk in roped else [(0, NSA_HD)]
        at = 0
        for lo, hi in pieces:
            o_ref[dst + at:dst + at + hi - lo, :] = w_ref[src + lo:src + hi, :].astype(BF16)
            at += hi - lo
    misc = jnp.concatenate([w_ref[o_glr:o_nq, :], w_ref[o_gate:o_gate + n_gate, :],
                            jnp.zeros((LANES - GLA_GATE_RANK - n_gate, cols), F32)], axis=0)
    o_ref[CB_MISC * LANES:, :] = misc.astype(BF16)


def _regroup_w_in(wt):
    n_in, D = wt.shape
    return pl.pallas_call(
        _regroup_kernel,
        grid=(D // REGROUP_TC,),
        in_specs=[pl.BlockSpec((n_in, REGROUP_TC), lambda i: (0, i))],
        out_specs=pl.BlockSpec((U_WIDTH, REGROUP_TC), lambda i: (0, i)),
        out_shape=jax.ShapeDtypeStruct((U_WIDTH, D), BF16),
        compiler_params=pltpu.CompilerParams(
            dimension_semantics=("parallel",), vmem_limit_bytes=VMEM_LIMIT),
        name="regroup_w_in",
    )(wt)


def _in_proj_kernel(x_ref, w_ref, o_ref):
    o_ref[...] = _nt_dot(x_ref[...].astype(BF16), w_ref[...])


def _in_proj(x2, wt_bf):
    T, D = x2.shape
    N = wt_bf.shape[0]
    return pl.pallas_call(
        _in_proj_kernel,
        grid=(N // IN_TN, T // IN_TM),
        in_specs=[pl.BlockSpec((IN_TM, D), lambda j, i: (i, 0)),
                  pl.BlockSpec((IN_TN, D), lambda j, i: (j, 0))],
        out_specs=pl.BlockSpec((IN_TM, IN_TN), lambda j, i: (i, j)),
        out_shape=jax.ShapeDtypeStruct((T, N), F32),
        compiler_params=pltpu.CompilerParams(
            dimension_semantics=("parallel", "parallel"), vmem_limit_bytes=VMEM_LIMIT),
        name="in_proj",
    )(x2, wt_bf)


GLA_ROWBLK = 256
GLA_TS = 512


def _split3(a):
    hi = a.astype(BF16)
    r1 = a - hi.astype(F32)
    mid = r1.astype(BF16)
    lo = (r1 - mid.astype(F32)).astype(BF16)
    return hi, mid, lo


def _log_sigmoid(z):
    return jnp.minimum(z, 0.0) - jnp.log1p(jnp.exp(-jnp.abs(z)))


def _gla_kernel(q_ref, k_ref, v_ref, go_ref, misc_ref, w2_ref, b2_ref, nw_ref, o_ref,
                qd_s, ki_s, ks_s, dec_s, state_s, sbf_s):
    TS = q_ref.shape[0]
    H, C, DK, DV = GLA_HEADS, GLA_CHUNK, GLA_DK, GLA_DV
    scale = DK ** -0.5

    @pl.when(pl.program_id(1) == 0)
    def _():
        state_s[...] = jnp.zeros(state_s.shape, F32)

    r = lax.broadcasted_iota(jnp.int32, (GLA_ROWBLK, GLA_ROWBLK), 0)
    c = lax.broadcasted_iota(jnp.int32, (GLA_ROWBLK, GLA_ROWBLK), 1)
    chunk_causal = ((r // C) == (c // C)) & (c <= r)
    cum_m = jnp.where(chunk_causal, 1.0, 0.0).astype(BF16)

    w2 = w2_ref[...].astype(BF16)
    b2 = b2_ref[...]
    for rb in range(TS // GLA_ROWBLK):
        rows = slice(rb * GLA_ROWBLK, (rb + 1) * GLA_ROWBLK)
        glr = misc_ref[rows, 0:GLA_GATE_RANK].astype(BF16)
        gk = _log_sigmoid(_dot(glr, w2) + b2) * (1.0 / GLA_GATE_NORM)
        hi, mid, lo = _split3(gk)
        bc_all = _dot(cum_m, hi) + _dot(cum_m, mid) + _dot(cum_m, lo)
        bl_all = jnp.concatenate(
            [jnp.broadcast_to(bc_all[j * C + C - 1:j * C + C, :], (C, H * DK)) for j in range(GLA_ROWBLK // C)],
            axis=0)
        for h in range(H):
            hk = slice(h * DK, (h + 1) * DK)
            bc = bc_all[:, hk]
            bl = bl_all[:, hk]
            q = q_ref[rows, hk] * scale
            k = k_ref[rows, hk]
            qd_s[h, rows, :] = (q * jnp.exp(bc)).astype(BF16)
            ki_s[h, rows, :] = (k * jnp.exp(-bc)).astype(BF16)
            ks_s[h, rows, :] = (k * jnp.exp(bl - bc)).astype(BF16)
            dec_s[h, rows, :] = jnp.exp(bl)

    n_chunk = TS // C
    nw = nw_ref[...]
    for h in range(H):
        hv = slice(h * DV, (h + 1) * DV)
        kv = [_tn_dot(ks_s[h, c * C:(c + 1) * C, :], v_ref[c * C:(c + 1) * C, hv].astype(BF16))
              for c in range(n_chunk)]
        dec_rows = jnp.concatenate([dec_s[h, c * C:c * C + 1, :] for c in range(n_chunk)]
                                   + [jnp.zeros((LANES - n_chunk, DK), F32)], axis=0)
        dec_cols = jnp.transpose(dec_rows)
        state = state_s[h]
        for c in range(n_chunk):
            sbf_s[h, c] = state.astype(BF16)
            state = state * dec_cols[:, c:c + 1] + kv[c]
        state_s[h] = state

    for h in range(H):
        hv = slice(h * DV, (h + 1) * DV)
        for rb in range(TS // GLA_ROWBLK):
            rows = slice(rb * GLA_ROWBLK, (rb + 1) * GLA_ROWBLK)
            qd = qd_s[h, rows, :]
            attn = jnp.where(chunk_causal, _nt_dot(qd, ki_s[h, rows, :]), 0.0).astype(BF16)
            o = _dot(attn, v_ref[rows, hv].astype(BF16))
            inter = [_dot(qd_s[h, c * C:(c + 1) * C, :], sbf_s[h, c])
                     for c in range(rb * GLA_ROWBLK // C, (rb + 1) * GLA_ROWBLK // C)]
            o = o + jnp.concatenate(inter, axis=0)
            ms = jnp.mean(o * o, axis=-1, keepdims=True)
            o = o * lax.rsqrt(ms + LN_EPS) * nw
            go = go_ref[rows, hv]
            o_ref[rows, hv] = (o * (go * jax.nn.sigmoid(go))).astype(o_ref.dtype)


def _gla(u3, gate_w2, gate_b2, norm_w):
    B, S, _ = u3.shape
    H, DK, DV = GLA_HEADS, GLA_DK, GLA_DV
    qk_w, v_w = H * DK, H * DV
    tok = lambda w, off: pl.BlockSpec((None, GLA_TS, w), lambda b, s: (b, s, off * LANES // w))
    const = lambda shape: pl.BlockSpec(shape, lambda b, s: (0, 0))
    return pl.pallas_call(
        _gla_kernel,
        grid=(B, S // GLA_TS),
        in_specs=[tok(qk_w, CB_GQ), tok(qk_w, CB_GK), tok(v_w, CB_GV), tok(v_w, CB_GO), tok(LANES, CB_MISC),
                  const((GLA_GATE_RANK, qk_w)), const((1, qk_w)), const((1, DV))],
        out_specs=pl.BlockSpec((None, GLA_TS, v_w), lambda b, s: (b, s, 0)),
        out_shape=jax.ShapeDtypeStruct((B, S, GLA_WIDTH), BF16),
        scratch_shapes=[pltpu.VMEM((H, GLA_TS, DK), BF16), pltpu.VMEM((H, GLA_TS, DK), BF16),
                        pltpu.VMEM((H, GLA_TS, DK), BF16), pltpu.VMEM((H, GLA_TS, DK), F32),
                        pltpu.VMEM((H, DK, DV), F32), pltpu.VMEM((H, GLA_TS // GLA_CHUNK, DK, DV), BF16)],
        compiler_params=pltpu.CompilerParams(
            dimension_semantics=("parallel", "arbitrary"), vmem_limit_bytes=VMEM_LIMIT),
        name="gla",
    )(u3, u3, u3, u3, u3, gate_w2, gate_b2, norm_w)


N_CMP_PAD = 128


def _gelu_tanh(x):
    return x * (0.5 * (1.0 + jnp.tanh(0.7978845608028654 * (x + 0.044715 * (x * x * x)))))


def _compress_kernel(kv_ref, pos_ref, w1_ref, w2_ref, o_ref):
    half = CMP_BLOCK // 2
    p0 = jnp.zeros((N_CMP_PAD, CMP_HIDDEN), F32)
    p1 = jnp.zeros((N_CMP_PAD, CMP_HIDDEN), F32)
    for l in range(half):
        x = kv_ref[pl.ds(l, N_CMP_PAD, stride=CMP_STRIDE), :]
        a0 = (x + pos_ref[l:l + 1, :]).astype(BF16)
        a1 = (x + pos_ref[half + l:half + l + 1, :]).astype(BF16)
        p0 = p0 + _dot(a0, w1_ref[l * NSA_HD:(l + 1) * NSA_HD, :])
        p1 = p1 + _dot(a1, w1_ref[(half + l) * NSA_HD:(half + l + 1) * NSA_HD, :])
    pre = p0 + pltpu.roll(p1, N_CMP_PAD - 1, 0)
    h = _gelu_tanh(pre).astype(BF16)
    out = _dot(h, w2_ref[...])
    row = lax.broadcasted_iota(jnp.int32, out.shape, 0)
    o_ref[...] = jnp.where(row < N_CMP_PAD - 1, out, 0.0).astype(o_ref.dtype)


def _compress(u3, pos, w1, w2):
    B, S, _ = u3.shape
    G = NSA_KV_GROUPS
    return pl.pallas_call(
        _compress_kernel,
        grid=(2, B, G),
        in_specs=[
            pl.BlockSpec((None, S, NSA_HD), lambda kv, b, g: (b, 0, CB_KC + 2 * kv + g)),
            pl.BlockSpec((None, CMP_BLOCK, NSA_HD), lambda kv, b, g: (kv, 0, 0)),
            pl.BlockSpec((None, CMP_BLOCK * NSA_HD, CMP_HIDDEN), lambda kv, b, g: (kv, 0, 0)),
            pl.BlockSpec((None, CMP_HIDDEN, NSA_HD), lambda kv, b, g: (kv, 0, 0)),
        ],
        out_specs=pl.BlockSpec((None, None, None, N_CMP_PAD, NSA_HD), lambda kv, b, g: (kv, b, g, 0, 0)),
        out_shape=jax.ShapeDtypeStruct((2, B, G, N_CMP_PAD, NSA_HD), BF16),
        compiler_params=pltpu.CompilerParams(
            dimension_semantics=("parallel", "parallel", "parallel"), vmem_limit_bytes=VMEM_LIMIT),
        name="compress",
    )(u3, pos, w1, w2)


ROPE_PAIR_LANE = LANES // 2


def _head_dim_order():
    half = ROPE_DIM // 2
    return [(0, half), (ROPE_DIM, ROPE_PAIR_LANE + half), (half, ROPE_DIM), (ROPE_PAIR_LANE + half, NSA_HD)]


def _rope(x, cos, sin_signed):
    return x * cos + pltpu.roll(x, ROPE_PAIR_LANE, 1) * sin_signed


def _nsa_kernel(n_side, side_slabs, *refs):
    n_in = 12
    (q_ref, misc_ref, kc_ref, vc_ref, ks_ref, vs_ref, kw_ref, vw_ref,
     cos_ref, sin_ref, ovt_ref, blk_ref) = refs[0:n_in]
    side_in = refs[n_in:n_in + n_side]
    o_ref = refs[n_in + n_side]
    side_out = refs[n_in + 1 + n_side:n_in + 1 + 2 * n_side]
    ksa_s, vsa_s, kwr_s, vwa_s = refs[n_in + 1 + 2 * n_side:n_in + 5 + 2 * n_side]
    chain_scratch = refs[n_in + 5 + 2 * n_side:]
    S = ks_ref.shape[0]
    R, TQ, HD = NSA_HPG, NSA_TQ, NSA_HD
    ss_s, es_s = chain_scratch[0:NSA_SUB], chain_scratch[NSA_SUB:2 * NSA_SUB]
    sw_s, ew_s = chain_scratch[2 * NSA_SUB:2 * NSA_SUB + R // 2], chain_scratch[2 * NSA_SUB + R // 2:]
    g = pl.program_id(1)
    qi = pl.program_id(2)
    scale = HD ** -0.5

    step_idx = (pl.program_id(0) * pl.num_programs(1) + g) * pl.num_programs(2) + qi
    for w_in_ref, w_out_ref, n_slabs in zip(side_in, side_out, side_slabs):
        def cast(w_in_ref=w_in_ref, w_out_ref=w_out_ref):
            w_out_ref[...] = w_in_ref[...].astype(BF16)
        if n_slabs is None:
            cast()
        else:
            pl.when(step_idx < n_slabs)(cast)

    @pl.when(qi == 0)
    def _():
        def prep(rb, _):
            rows = pl.ds(pl.multiple_of(rb * 256, 256), 256)
            cos, sin = cos_ref[rows, :], sin_ref[rows, :]
            ones = jnp.ones((256, LANES), BF16)
            ksa_s[rows, 0:HD] = _rope(ks_ref[rows, :], cos, sin).astype(BF16)
            ksa_s[rows, HD:] = blk_ref[rows, :]
            kwr_s[rows, :] = _rope(kw_ref[rows, :], cos, sin).astype(BF16)
            vsa_s[rows, 0:HD] = vs_ref[rows, :].astype(BF16)
            vsa_s[rows, HD:] = ones
            vwa_s[rows, 0:HD] = vw_ref[rows, :].astype(BF16)
            vwa_s[rows, HD:] = ones
            return 0
        lax.fori_loop(0, S // 256, prep, 0)

    nb = S // SLC_BLOCK
    TK = NSA_TK
    ovt = ovt_ref[...]
    kc, vc = kc_ref[...], vc_ref[...]

    def front(v, u):
        tq0 = (v * NSA_SUB + u) * TQ
        urows = slice(u * TQ, (u + 1) * TQ)
        trows = slice(tq0, tq0 + TQ)
        cos_t, sin_t = cos_ref[trows, :], sin_ref[trows, :]
        heads = [q_ref[urows, r * HD:(r + 1) * HD] for r in range(R)]
        qc = jnp.concatenate([(h * scale).astype(BF16) for h in heads], axis=0)
        qr_heads = [_rope(h * (scale * LOG2E), cos_t, sin_t).astype(BF16) for h in heads]
        t_tok = tq0 + lax.broadcasted_iota(jnp.int32, (TQ, 1), 0)
        t_col = jnp.concatenate([t_tok] * R, axis=0)

        s_c = _nt_dot(qc, kc)
        n_idx = lax.broadcasted_iota(jnp.int32, (1, N_CMP_PAD), 1)
        p_c = _masked_softmax(s_c, (n_idx * CMP_STRIDE + (CMP_BLOCK - 1)) <= t_col)
        p_cb = p_c.astype(BF16)
        o_cmp = _dot(p_cb, vc)

        imp = _nt_dot(ovt, p_cb[0:TQ])
        for r in range(1, R):
            imp = imp + _nt_dot(ovt, p_cb[r * TQ:(r + 1) * TQ])
        j_idx = lax.broadcasted_iota(jnp.int32, (nb, TQ), 0)
        cur = (tq0 + lax.broadcasted_iota(jnp.int32, (nb, TQ), 1)) // SLC_BLOCK
        imp = jnp.where(j_idx <= cur, imp, INVALID_SCORE)
        imp = jnp.where((j_idx == 0) | (j_idx == cur) | (j_idx == cur - 1), FORCED_SCORE, imp)
        rank = jnp.zeros((nb, TQ), jnp.int32)
        for i in range(nb):
            ri = imp[i:i + 1, :]
            ahead = (ri > imp) | ((ri == imp) & (i < j_idx))
            rank = rank + jnp.where(ahead, 1, 0)
        blk_bias_t = jnp.where(rank < min(SLC_TOPK, nb), 0.0, MASKED)
        blk_bias_t = jnp.concatenate([blk_bias_t, jnp.zeros((LANES - nb, TQ), F32)], axis=0)
        blk_bias = jnp.transpose(blk_bias_t).astype(BF16)
        q_aug = jnp.concatenate([jnp.concatenate([h, blk_bias], axis=1) for h in qr_heads], axis=0)
        return dict(t_col=t_col, o_cmp=o_cmp, q_aug=q_aug, qr_heads=qr_heads)

    step = NSA_SUB * TQ
    span = WINDOW + step
    n_pair = R // 2

    def step_body(v):
        subs = [front(v, u) for u in range(NSA_SUB)]
        n_keys = (v + 1) * step
        start = max(v * step - WINDOW, 0)
        diff = (v * step + lax.broadcasted_iota(jnp.int32, (step, 1), 0)) - (
            start + lax.broadcasted_iota(jnp.int32, (1, span), 1))
        band = jnp.where((diff >= 0) & (diff < WINDOW), 0.0, MASKED)

        def win_scores(c):
            q_pair = jnp.concatenate(
                [subs[u]["qr_heads"][2 * c + rr] for rr in range(2) for u in range(NSA_SUB)], axis=0)
            s_w = _nt_dot(q_pair, kwr_s[start:start + span, :])
            for rr in range(2):
                sw_s[c][rr * step:(rr + 1) * step, :] = s_w[rr * step:(rr + 1) * step] + band

        def win_softmax(c):
            m_w = jnp.max(sw_s[c][...], axis=-1, keepdims=True)
            ew_s[c][...] = jnp.exp2(sw_s[c][...] - m_w).astype(BF16)

        def win_values(c):
            acc_w = _dot(ew_s[c][...], vwa_s[start:start + span, :])
            return acc_w[:, 0:HD] * (1.0 / acc_w[:, HD:])

        def sel_scores(u):
            q_aug, t_col = subs[u]["q_aug"], subs[u]["t_col"]
            for k0 in range(0, n_keys, TK):
                w = min(TK, n_keys - k0)
                s = _nt_dot(q_aug, ksa_s[k0:k0 + w, :])
                if k0 + w == n_keys:
                    kpos = k0 + lax.broadcasted_iota(jnp.int32, (1, w), 1)
                    s = jnp.where(kpos <= t_col, s, MASKED)
                ss_s[u][:, k0:k0 + w] = s

        def sel_softmax(u):
            m = jnp.max(ss_s[u][:, 0:n_keys], axis=-1, keepdims=True)
            es_s[u][:, 0:n_keys] = jnp.exp2(ss_s[u][:, 0:n_keys] - m).astype(BF16)

        def sel_values(u):
            acc = _dot(es_s[u][:, 0:n_keys], vsa_s[0:n_keys, :])
            return acc[:, 0:HD] * (1.0 / acc[:, HD:])

        assert NSA_SUB == 2 and n_pair == 2
        win_scores(0)
        win_scores(1)
        sel_scores(0)
        win_softmax(0)
        sel_scores(1)
        win_softmax(1)
        o_win = [win_values(0)]
        sel_softmax(0)
        o_win.append(win_values(1))
        sel_softmax(1)
        o_slc = [sel_values(0), sel_values(1)]

        misc = misc_ref[...]
        ng = 3 * R
        logits = jnp.where(g == 0, misc[:, MISC_GATE_LANE:MISC_GATE_LANE + ng],
                           misc[:, MISC_GATE_LANE + ng:MISC_GATE_LANE + 2 * ng])
        gate = jax.nn.sigmoid(logits)
        for u in range(NSA_SUB):
            urows = slice(u * TQ, (u + 1) * TQ)
            for r in range(R):
                rs = slice(r * TQ, (r + 1) * TQ)
                ws = slice((r % 2) * step + u * TQ, (r % 2) * step + (u + 1) * TQ)
                o = (gate[urows, 3 * r:3 * r + 1] * subs[u]["o_cmp"][rs]
                     + gate[urows, 3 * r + 1:3 * r + 2] * o_slc[u][rs]
                     + gate[urows, 3 * r + 2:3 * r + 3] * o_win[r // 2][ws])
                o_ref[urows, r * HD:(r + 1) * HD] = o.astype(o_ref.dtype)

    for v in range(S // step):
        pl.when(qi == v)(functools.partial(step_body, v))


BF16_ROWS = 16


def _side_cast_plan(weights, n_steps):
    plan = []
    for w in weights:
        n_rows = w.shape[0]
        if n_rows % (n_steps * BF16_ROWS) == 0:
            plan.append((n_rows // n_steps, None))
        else:
            assert n_rows % LANES == 0 and n_rows // LANES <= n_steps
            plan.append((LANES, n_rows // LANES))
    return plan


def _nsa(u3, cmp_kv, cos_f, sin_s, ovt, blk_onehot, side_weights):
    B, S, _ = u3.shape
    G, R, HD = NSA_KV_GROUPS, NSA_HPG, NSA_HD
    qb = R * HD // LANES
    rows = R * NSA_TQ
    step = NSA_SUB * NSA_TQ
    assert NSA_TK % step == 0
    n_q = S // step
    plan = _side_cast_plan(side_weights, B * G * n_q)

    def side_spec(w, slab_rows, n_slabs):
        def index(b, g, i):
            s = (b * G + g) * n_q + i
            return (s if n_slabs is None else jnp.minimum(s, n_slabs - 1), 0)
        return pl.BlockSpec((slab_rows, w.shape[1]), index)

    side_specs = [side_spec(w, *p) for w, p in zip(side_weights, plan)]
    full = lambda cb: pl.BlockSpec((None, S, HD), lambda b, g, i: (b, 0, cb + g))
    table = pl.BlockSpec((S, LANES), lambda b, g, i: (0, 0))
    kern = functools.partial(_nsa_kernel, len(side_weights), tuple(p[1] for p in plan))
    return pl.pallas_call(
        kern,
        grid=(B, G, n_q),
        in_specs=[
            pl.BlockSpec((None, step, R * HD), lambda b, g, i: (b, i, CB_NQ // qb + g)),
            pl.BlockSpec((None, step, LANES), lambda b, g, i: (b, i, CB_MISC)),
            pl.BlockSpec((None, None, None, N_CMP_PAD, HD), lambda b, g, i: (0, b, g, 0, 0)),
            pl.BlockSpec((None, None, None, N_CMP_PAD, HD), lambda b, g, i: (1, b, g, 0, 0)),
            full(CB_KS), full(CB_VS), full(CB_KW), full(CB_VW),
            table, table,
            pl.BlockSpec((S // SLC_BLOCK, N_CMP_PAD), lambda b, g, i: (0, 0)),
            table,
        ] + side_specs,
        out_specs=[pl.BlockSpec((None, step, R * HD), lambda b, g, i: (b, i, g))] + side_specs,
        out_shape=[jax.ShapeDtypeStruct((B, S, NSA_WIDTH), BF16)]
        + [jax.ShapeDtypeStruct(w.shape, BF16) for w in side_weights],
        scratch_shapes=[pltpu.VMEM((S, 2 * HD), BF16), pltpu.VMEM((S, 2 * HD), BF16),
                        pltpu.VMEM((S, HD), BF16), pltpu.VMEM((S, 2 * HD), BF16)]
        + [pltpu.VMEM((rows, S), F32)] * NSA_SUB + [pltpu.VMEM((rows, S), BF16)] * NSA_SUB
        + [pltpu.VMEM((2 * step, WINDOW + step), F32)] * (R // 2)
        + [pltpu.VMEM((2 * step, WINDOW + step), BF16)] * (R // 2),
        compiler_params=pltpu.CompilerParams(
            dimension_semantics=("arbitrary", "arbitrary", "arbitrary"), vmem_limit_bytes=VMEM_LIMIT),
        name="nsa",
    )(u3, u3, cmp_kv, cmp_kv, u3, u3, u3, u3, cos_f, sin_s, ovt, blk_onehot, *side_weights)


def _out_proj_kernel(og_ref, on_ref, x_ref, w_ref, g_ref, b_ref, h_ref, hb_ref):
    half = og_ref.shape[0] // 2
    halves = [slice(0, half), slice(half, 2 * half)]
    mixes = [_dot(og_ref[rows, :], w_ref[0:GLA_WIDTH, :]) + _dot(on_ref[rows, :], w_ref[GLA_WIDTH:, :])
             for rows in halves]
    for rows, mix in zip(halves, mixes):
        h = _layer_norm(DEEPNORM_ALPHA * x_ref[rows, :] + mix, g_ref[...], b_ref[...])
        h_ref[rows, :] = h
        hb_ref[rows, :] = h.astype(BF16)


def _out_proj(o_gla, o_nsa, x2, w_bf, g, b):
    T, D = x2.shape
    tile = lambda w: pl.BlockSpec((OUT_TM, w), lambda i: (i, 0))
    const = lambda s: pl.BlockSpec(s, lambda i: (0, 0))
    return pl.pallas_call(
        _out_proj_kernel,
        grid=(T // OUT_TM,),
        in_specs=[tile(GLA_WIDTH), tile(NSA_WIDTH), tile(D), const(w_bf.shape), const((1, D)), const((1, D))],
        out_specs=[tile(D), tile(D)],
        out_shape=[jax.ShapeDtypeStruct((T, D), F32), jax.ShapeDtypeStruct((T, D), BF16)],
        compiler_params=pltpu.CompilerParams(
            dimension_semantics=("parallel",), vmem_limit_bytes=VMEM_LIMIT),
        name="out_proj",
    )(o_gla, o_nsa, x2, w_bf, g, b)


def _ffn_kernel(hb_ref, h_ref, w1_ref, w3_ref, w2_ref, g_ref, b_ref, o_ref, acc_s):
    f = pl.program_id(1)

    @pl.when(f == 0)
    def _():
        acc_s[...] = jnp.zeros(acc_s.shape, F32)

    hb = hb_ref[...]
    a = _dot(hb, w1_ref[...])
    c = _dot(hb, w3_ref[...])
    acc_s[...] += _dot((a * jax.nn.sigmoid(a) * c).astype(BF16), w2_ref[...])

    @pl.when(f == pl.num_programs(1) - 1)
    def _():
        o_ref[...] = _layer_norm(DEEPNORM_ALPHA * h_ref[...] + acc_s[...], g_ref[...], b_ref[...])


def _ffn(h_bf, h, w1, w3, w2, g, b):
    T, D = h.shape
    F = w1.shape[1]
    return pl.pallas_call(
        _ffn_kernel,
        grid=(T // FFN_TM, F // FFN_TF),
        in_specs=[
            pl.BlockSpec((FFN_TM, D), lambda i, f: (i, 0)),
            pl.BlockSpec((FFN_TM, D), lambda i, f: (i, 0)),
            pl.BlockSpec((D, FFN_TF), lambda i, f: (0, f)),
            pl.BlockSpec((D, FFN_TF), lambda i, f: (0, f)),
            pl.BlockSpec((FFN_TF, D), lambda i, f: (f, 0)),
            pl.BlockSpec((1, D), lambda i, f: (0, 0)),
            pl.BlockSpec((1, D), lambda i, f: (0, 0)),
        ],
        out_specs=pl.BlockSpec((FFN_TM, D), lambda i, f: (i, 0)),
        out_shape=jax.ShapeDtypeStruct((T, D), F32),
        scratch_shapes=[pltpu.VMEM((FFN_TM, D), F32)],
        compiler_params=pltpu.CompilerParams(
            dimension_semantics=("parallel", "arbitrary"), vmem_limit_bytes=VMEM_LIMIT),
        name="ffn",
    )(h_bf, h, w1, w3, w2, g, b)


def _rope_tables(S):
    half = ROPE_DIM // 2
    pos = np.arange(S, dtype=np.float64)
    inv = np.power(ROPE_THETA, -np.arange(0, ROPE_DIM, 2, dtype=np.float64) / ROPE_DIM)
    ang = pos[:, None] * inv[None, :]
    cos_f = np.ones((S, LANES), np.float32)
    sin_s = np.zeros((S, LANES), np.float32)
    for lane0, sign in ((0, -1.0), (ROPE_PAIR_LANE, 1.0)):
        cos_f[:, lane0:lane0 + half] = np.cos(ang)
        sin_s[:, lane0:lane0 + half] = sign * np.sin(ang)
    return jnp.asarray(cos_f), jnp.asarray(sin_s)


def _selection_constants(S):
    n_cmp = (S - CMP_BLOCK) // CMP_STRIDE + 1
    nb = S // SLC_BLOCK
    c_start = np.arange(n_cmp) * CMP_STRIDE
    b_start = np.arange(nb) * SLC_BLOCK
    overlap = ((c_start[:, None] < b_start[None, :] + SLC_BLOCK) &
               (b_start[None, :] < c_start[:, None] + CMP_BLOCK)).astype(np.float32)
    ovt = np.zeros((nb, N_CMP_PAD), np.float32)
    ovt[:, :n_cmp] = overlap.T
    blk_onehot = ((np.arange(S) // SLC_BLOCK)[:, None] == np.arange(LANES)[None, :]).astype(np.float32)
    return jnp.asarray(ovt, BF16), jnp.asarray(blk_onehot, BF16)


def kernel(x, w_in, gla_gate_w2, gla_gate_b2, gla_norm_w, cmp_k_pos, cmp_k_w1, cmp_k_w2,
           cmp_v_pos, cmp_v_w1, cmp_v_w2, w_out, ln1_g, ln1_b, ffn_w1, ffn_w3, ffn_w2, ln2_g, ln2_b):
    B, S, D = x.shape
    T = B * S
    cos_f, sin_s = _rope_tables(S)
    ovt, blk_onehot = _selection_constants(S)
    reorder_dims = lambda w: jnp.concatenate([w[:, lo:hi] for lo, hi in _head_dim_order()], axis=1)

    x2 = x.reshape(T, D)
    for l in range(DEPTH):
        u = _in_proj(x2, _regroup_w_in(w_in[l].T))
        u3 = u.reshape(B, S, U_WIDTH)
        o_gla = _gla(u3, gla_gate_w2[l], gla_gate_b2[l][None, :], gla_norm_w[l][None, :])
        cmp_kv = _compress(u3, jnp.stack([cmp_k_pos[l], cmp_v_pos[l]]),
                           jnp.stack([cmp_k_w1[l], cmp_v_w1[l]]).astype(BF16),
                           jnp.stack([reorder_dims(cmp_k_w2[l]), cmp_v_w2[l]]).astype(BF16))
        o_nsa, w_out_bf, w1_bf, w3_bf, w2_bf = _nsa(u3, cmp_kv, cos_f, sin_s, ovt, blk_onehot,
                                                    [w_out[l], ffn_w1[l], ffn_w3[l], ffn_w2[l]])
        h, h_bf = _out_proj(o_gla.reshape(T, GLA_WIDTH), o_nsa.reshape(T, NSA_WIDTH), x2,
                            w_out_bf, ln1_g[l][None, :], ln1_b[l][None, :])
        x2 = _ffn(h_bf, h, w1_bf, w3_bf, w2_bf, ln2_g[l][None, :], ln2_b[l][None, :])
    return x2.reshape(B, S, D)
```

```python
import functools

import numpy as np
import jax
import jax.numpy as jnp
from jax import lax
from jax.experimental import pallas as pl
from jax.experimental.pallas import tpu as pltpu

F32 = jnp.float32
BF16 = jnp.bfloat16

D_MODEL = 2048
DEPTH = 1
GLA_HEADS = 4
GLA_DK = 128
GLA_DV = 256
GLA_GATE_RANK = 16
GLA_GATE_NORM = 16.0
GLA_CHUNK = 64
NSA_HEADS = 8
NSA_HD = 128
NSA_KV_GROUPS = 2
NSA_HPG = NSA_HEADS // NSA_KV_GROUPS
CMP_BLOCK = 32
CMP_STRIDE = 16
CMP_HIDDEN = 2 * NSA_HD
SLC_BLOCK = 64
SLC_TOPK = 16
WINDOW = 512
ROPE_THETA = 500000.0
ROPE_DIM = NSA_HD // 4
FFN_HIDDEN = 5632
DEEPNORM_ALPHA = (2.0 * DEPTH) ** 0.25
LN_EPS = 1e-5
FORCED_SCORE = 1e4
INVALID_SCORE = -1e4

GLA_WIDTH = GLA_HEADS * GLA_DV
NSA_WIDTH = NSA_HEADS * NSA_HD

LANES = 128
VMEM_LIMIT = 56 * 1024 * 1024

CB_GQ = 0
CB_GK = 4
CB_GV = 8
CB_GO = 16
CB_NQ = 24
CB_KC = 32
CB_KS = 36
CB_VS = 38
CB_KW = 40
CB_VW = 42
CB_MISC = 44
MISC_GATE_LANE = GLA_GATE_RANK
U_WIDTH = 45 * LANES

REGROUP_TC = 256
IN_TM = 512
OUT_TM = 512
FFN_TM, FFN_TF = 512, 512
NSA_TQ = 128
NSA_SUB = 2
NSA_TK = 512
WIN_SPAN = WINDOW + NSA_TQ
MASKED = -1e30
LOG2E = 1.4426950408889634


def _nt_dot(a, b):
    return lax.dot_general(a, b, (((1,), (1,)), ((), ())), preferred_element_type=F32)


def _tn_dot(a, b):
    return lax.dot_general(a, b, (((0,), (0,)), ((), ())), preferred_element_type=F32)


def _dot(a, b):
    return jnp.dot(a, b, preferred_element_type=F32)


def _layer_norm(z, g, b):
    mu = jnp.mean(z, axis=-1, keepdims=True)
    zc = z - mu
    var = jnp.mean(zc * zc, axis=-1, keepdims=True)
    return zc * lax.rsqrt(var + LN_EPS) * g + b


def _masked_softmax(s, mask):
    sm = jnp.where(mask, s, -jnp.inf)
    m = jnp.max(sm, axis=-1, keepdims=True)
    m = jnp.where(m == -jnp.inf, 0.0, m)
    e = jnp.where(mask, jnp.exp(s - m), 0.0)
    den = jnp.sum(e, axis=-1, keepdims=True)
    return e * (1.0 / jnp.where(den > 0, den, 1.0))


def _regroup_kernel(w_ref, oa_ref, ob_ref):
    cols = w_ref.shape[1]
    o_glr = CB_NQ * LANES
    o_nq = o_glr + GLA_GATE_RANK
    o_gate = o_nq + (CB_MISC - CB_NQ) * LANES
    n_gate = NSA_HEADS * 3
    oa_ref[...] = w_ref[0:o_glr, :].astype(BF16)
    roped = (list(range(CB_NQ, CB_NQ + NSA_HEADS)) + list(range(CB_KS, CB_KS + NSA_KV_GROUPS))
             + list(range(CB_KW, CB_KW + NSA_KV_GROUPS)))
    for blk in range(CB_NQ, CB_MISC):
        dst, src = (blk - CB_NQ) * LANES, blk * LANES + GLA_GATE_RANK
        pieces = _head_dim_order() if blk in roped else [(0, NSA_HD)]
        at = 0
        for lo, hi in pieces:
            ob_ref[dst + at:dst + at + hi - lo, :] = w_ref[src + lo:src + hi, :].astype(BF16)
            at += hi - lo
    misc = jnp.concatenate([w_ref[o_glr:o_nq, :], w_ref[o_gate:o_gate + n_gate, :],
                            jnp.zeros((LANES - GLA_GATE_RANK - n_gate, cols), F32)], axis=0)
    ob_ref[(CB_MISC - CB_NQ) * LANES:, :] = misc.astype(BF16)


def _regroup_w_in(wt):
    n_in, D = wt.shape
    na, nb = CB_NQ * LANES, U_WIDTH - CB_NQ * LANES
    return pl.pallas_call(
        _regroup_kernel,
        grid=(D // REGROUP_TC,),
        in_specs=[pl.BlockSpec((n_in, REGROUP_TC), lambda i: (0, i))],
        out_specs=[pl.BlockSpec((na, REGROUP_TC), lambda i: (0, i)), pl.BlockSpec((nb, REGROUP_TC), lambda i: (0, i))],
        out_shape=[jax.ShapeDtypeStruct((na, D), BF16), jax.ShapeDtypeStruct((nb, D), BF16)],
        compiler_params=pltpu.CompilerParams(
            dimension_semantics=("parallel",), vmem_limit_bytes=VMEM_LIMIT),
        name="regroup_w_in",
    )(wt)


def _in_proj_kernel(x_ref, w_ref, o_ref):
    o_ref[...] = _nt_dot(x_ref[...].astype(BF16), w_ref[...])


def _in_proj(x2, wt_bf):
    T, D = x2.shape
    N = wt_bf.shape[0]
    return pl.pallas_call(
        _in_proj_kernel,
        grid=(T // IN_TM,),
        in_specs=[pl.BlockSpec((IN_TM, D), lambda i: (i, 0)),
                  pl.BlockSpec((N, D), lambda i: (0, 0))],
        out_specs=pl.BlockSpec((IN_TM, N), lambda i: (i, 0)),
        out_shape=jax.ShapeDtypeStruct((T, N), F32),
        compiler_params=pltpu.CompilerParams(
            dimension_semantics=("parallel",), vmem_limit_bytes=VMEM_LIMIT),
        name="in_proj",
    )(x2, wt_bf)


GLA_ROWBLK = 256
GLA_TS = 512


def _split3(a):
    hi = a.astype(BF16)
    r1 = a - hi.astype(F32)
    mid = r1.astype(BF16)
    lo = (r1 - mid.astype(F32)).astype(BF16)
    return hi, mid, lo


def _log_sigmoid(z):
    return jnp.minimum(z, 0.0) - jnp.log(1.0 + jnp.exp(-jnp.abs(z)))


def _gla_kernel(q_ref, k_ref, v_ref, go_ref, misc_ref, w2_ref, b2_ref, nw_ref, o_ref,
                qd_s, ki_s, ks_s, dec_s, state_s, sbf_s):
    TS = q_ref.shape[0]
    H, C, DK, DV = GLA_HEADS, GLA_CHUNK, GLA_DK, GLA_DV
    scale = DK ** -0.5

    @pl.when(pl.program_id(1) == 0)
    def _():
        state_s[...] = jnp.zeros(state_s.shape, F32)

    r = lax.broadcasted_iota(jnp.int32, (GLA_ROWBLK, GLA_ROWBLK), 0)
    c = lax.broadcasted_iota(jnp.int32, (GLA_ROWBLK, GLA_ROWBLK), 1)
    chunk_causal = ((r // C) == (c // C)) & (c <= r)
    cum_m = jnp.where(chunk_causal, 1.0, 0.0).astype(BF16)

    w2 = w2_ref[...].astype(BF16)
    b2 = b2_ref[...]
    for rb in range(TS // GLA_ROWBLK):
        rows = slice(rb * GLA_ROWBLK, (rb + 1) * GLA_ROWBLK)
        glr = misc_ref[rows, 0:GLA_GATE_RANK].astype(BF16)
        gk = _log_sigmoid(_dot(glr, w2) + b2) * (1.0 / GLA_GATE_NORM)
        hi, mid, lo = _split3(gk)
        bc_all = _dot(cum_m, hi) + _dot(cum_m, mid) + _dot(cum_m, lo)
        bl_all = jnp.concatenate(
            [jnp.broadcast_to(bc_all[j * C + C - 1:j * C + C, :], (C, H * DK)) for j in range(GLA_ROWBLK // C)],
            axis=0)
        for h in range(H):
            hk = slice(h * DK, (h + 1) * DK)
            bc = bc_all[:, hk]
            bl = bl_all[:, hk]
            q = q_ref[rows, hk] * scale
            k = k_ref[rows, hk]
            qd_s[h, rows, :] = (q * jnp.exp(bc)).astype(BF16)
            ki_s[h, rows, :] = (k * jnp.exp(-bc)).astype(BF16)
            ks_s[h, rows, :] = (k * jnp.exp(bl - bc)).astype(BF16)
            dec_s[h, rows, :] = jnp.exp(bl)

    n_chunk = TS // C
    nw = nw_ref[...]
    for h in range(H):
        hv = slice(h * DV, (h + 1) * DV)
        kv = [_tn_dot(ks_s[h, c * C:(c + 1) * C, :], v_ref[c * C:(c + 1) * C, hv].astype(BF16))
              for c in range(n_chunk)]
        dec_rows = jnp.concatenate([dec_s[h, c * C:c * C + 1, :] for c in range(n_chunk)]
                                   + [jnp.zeros((LANES - n_chunk, DK), F32)], axis=0)
        dec_cols = jnp.transpose(dec_rows)
        state = state_s[h]
        for c in range(n_chunk):
            sbf_s[h, c] = state.astype(BF16)
            state = state * dec_cols[:, c:c + 1] + kv[c]
        state_s[h] = state

    for h in range(H):
        hv = slice(h * DV, (h + 1) * DV)
        for rb in range(TS // GLA_ROWBLK):
            rows = slice(rb * GLA_ROWBLK, (rb + 1) * GLA_ROWBLK)
            qd = qd_s[h, rows, :]
            attn = jnp.where(chunk_causal, _nt_dot(qd, ki_s[h, rows, :]), 0.0).astype(BF16)
            o = _dot(attn, v_ref[rows, hv].astype(BF16))
            inter = [_dot(qd_s[h, c * C:(c + 1) * C, :], sbf_s[h, c])
                     for c in range(rb * GLA_ROWBLK // C, (rb + 1) * GLA_ROWBLK // C)]
            o = o + jnp.concatenate(inter, axis=0)
            ms = jnp.mean(o * o, axis=-1, keepdims=True)
            o = o * lax.rsqrt(ms + LN_EPS) * nw
            go = go_ref[rows, hv]
            o_ref[rows, hv] = (o * (go * jax.nn.sigmoid(go))).astype(o_ref.dtype)


def _gla(ua3, ub3, gate_w2, gate_b2, norm_w):
    B, S, _ = ua3.shape
    H, DK, DV = GLA_HEADS, GLA_DK, GLA_DV
    qk_w, v_w = H * DK, H * DV
    tok = lambda w, off: pl.BlockSpec((None, GLA_TS, w), lambda b, s: (b, s, off * LANES // w))
    const = lambda shape: pl.BlockSpec(shape, lambda b, s: (0, 0))
    return pl.pallas_call(
        _gla_kernel,
        grid=(B, S // GLA_TS),
        in_specs=[tok(qk_w, CB_GQ), tok(qk_w, CB_GK), tok(v_w, CB_GV), tok(v_w, CB_GO), tok(LANES, CB_MISC - CB_NQ),
                  const((GLA_GATE_RANK, qk_w)), const((1, qk_w)), const((1, DV))],
        out_specs=pl.BlockSpec((None, GLA_TS, v_w), lambda b, s: (b, s, 0)),
        out_shape=jax.ShapeDtypeStruct((B, S, GLA_WIDTH), BF16),
        scratch_shapes=[pltpu.VMEM((H, GLA_TS, DK), BF16), pltpu.VMEM((H, GLA_TS, DK), BF16),
                        pltpu.VMEM((H, GLA_TS, DK), BF16), pltpu.VMEM((H, GLA_TS, DK), F32),
                        pltpu.VMEM((H, DK, DV), F32), pltpu.VMEM((H, GLA_TS // GLA_CHUNK, DK, DV), BF16)],
        compiler_params=pltpu.CompilerParams(
            dimension_semantics=("parallel", "arbitrary"), vmem_limit_bytes=VMEM_LIMIT),
        name="gla",
    )(ua3, ua3, ua3, ua3, ub3, gate_w2, gate_b2, norm_w)


N_CMP_PAD = 128


def _gelu_tanh(x):
    return x * (0.5 * (1.0 + jnp.tanh(0.7978845608028654 * (x + 0.044715 * (x * x * x)))))


def _compress_kernel(kv_ref, pos_ref, w1_ref, w2_ref, o_ref):
    half = CMP_BLOCK // 2
    p0 = jnp.zeros((N_CMP_PAD, CMP_HIDDEN), F32)
    p1 = jnp.zeros((N_CMP_PAD, CMP_HIDDEN), F32)
    for l in range(half):
        x = kv_ref[pl.ds(l, N_CMP_PAD, stride=CMP_STRIDE), :]
        a0 = (x + pos_ref[l:l + 1, :]).astype(BF16)
        a1 = (x + pos_ref[half + l:half + l + 1, :]).astype(BF16)
        p0 = p0 + _dot(a0, w1_ref[l * NSA_HD:(l + 1) * NSA_HD, :])
        p1 = p1 + _dot(a1, w1_ref[(half + l) * NSA_HD:(half + l + 1) * NSA_HD, :])
    pre = p0 + pltpu.roll(p1, N_CMP_PAD - 1, 0)
    h = _gelu_tanh(pre).astype(BF16)
    out = _dot(h, w2_ref[...])
    row = lax.broadcasted_iota(jnp.int32, out.shape, 0)
    o_ref[...] = jnp.where(row < N_CMP_PAD - 1, out, 0.0).astype(o_ref.dtype)


def _compress(u3, pos, w1, w2):
    B, S, _ = u3.shape
    G = NSA_KV_GROUPS
    return pl.pallas_call(
        _compress_kernel,
        grid=(2, B, G),
        in_specs=[
            pl.BlockSpec((None, S, NSA_HD), lambda kv, b, g: (b, 0, CB_KC - CB_NQ + 2 * kv + g)),
            pl.BlockSpec((None, CMP_BLOCK, NSA_HD), lambda kv, b, g: (kv, 0, 0)),
            pl.BlockSpec((None, CMP_BLOCK * NSA_HD, CMP_HIDDEN), lambda kv, b, g: (kv, 0, 0)),
            pl.BlockSpec((None, CMP_HIDDEN, NSA_HD), lambda kv, b, g: (kv, 0, 0)),
        ],
        out_specs=pl.BlockSpec((None, None, None, N_CMP_PAD, NSA_HD), lambda kv, b, g: (kv, b, g, 0, 0)),
        out_shape=jax.ShapeDtypeStruct((2, B, G, N_CMP_PAD, NSA_HD), BF16),
        compiler_params=pltpu.CompilerParams(
            dimension_semantics=("parallel", "parallel", "parallel"), vmem_limit_bytes=VMEM_LIMIT),
        name="compress",
    )(u3, pos, w1, w2)


ROPE_PAIR_LANE = LANES // 2


def _head_dim_order():
    half = ROPE_DIM // 2
    return [(0, half), (ROPE_DIM, ROPE_PAIR_LANE + half), (half, ROPE_DIM), (ROPE_PAIR_LANE + half, NSA_HD)]


def _rope(x, cos, sin_signed):
    return x * cos + pltpu.roll(x, ROPE_PAIR_LANE, 1) * sin_signed


def _nsa_kernel(n_side, side_slabs, *refs):
    n_in = 12
    (q_ref, misc_ref, kc_ref, vc_ref, ks_ref, vs_ref, kw_ref, vw_ref,
     cos_ref, sin_ref, ovt_ref, blk_ref) = refs[0:n_in]
    side_in = refs[n_in:n_in + n_side]
    o_ref = refs[n_in + n_side]
    side_out = refs[n_in + 1 + n_side:n_in + 1 + 2 * n_side]
    ksa_s, vsa_s, kwr_s, vwa_s = refs[n_in + 1 + 2 * n_side:n_in + 5 + 2 * n_side]
    chain_scratch = refs[n_in + 5 + 2 * n_side:]
    S = ks_ref.shape[0]
    R, TQ, HD = NSA_HPG, NSA_TQ, NSA_HD
    ss_s, es_s = chain_scratch[0:NSA_SUB], chain_scratch[NSA_SUB:2 * NSA_SUB]
    sw_s, ew_s = chain_scratch[2 * NSA_SUB:2 * NSA_SUB + R // 2], chain_scratch[2 * NSA_SUB + R // 2:]
    g = pl.program_id(1)
    qi = pl.program_id(2)
    scale = HD ** -0.5

    step_idx = (pl.program_id(0) * pl.num_programs(1) + g) * pl.num_programs(2) + qi
    for w_in_ref, w_out_ref, n_slabs in zip(side_in, side_out, side_slabs):
        def cast(w_in_ref=w_in_ref, w_out_ref=w_out_ref):
            w_out_ref[...] = w_in_ref[...].astype(BF16)
        if n_slabs is None:
            cast()
        else:
            pl.when(step_idx < n_slabs)(cast)

    @pl.when(qi == 0)
    def _():
        def prep(rb, _):
            rows = pl.ds(pl.multiple_of(rb * 256, 256), 256)
            cos, sin = cos_ref[rows, :], sin_ref[rows, :]
            ones = jnp.ones((256, LANES), BF16)
            ksa_s[rows, 0:HD] = _rope(ks_ref[rows, :], cos, sin).astype(BF16)
            ksa_s[rows, HD:] = blk_ref[rows, :]
            kwr_s[rows, :] = _rope(kw_ref[rows, :], cos, sin).astype(BF16)
            vsa_s[rows, 0:HD] = vs_ref[rows, :].astype(BF16)
            vsa_s[rows, HD:] = ones
            vwa_s[rows, 0:HD] = vw_ref[rows, :].astype(BF16)
            vwa_s[rows, HD:] = ones
            return 0
        lax.fori_loop(0, S // 256, prep, 0)

    nb = S // SLC_BLOCK
    TK = NSA_TK
    ovt = ovt_ref[...]
    kc, vc = kc_ref[...], vc_ref[...]

    def front(v, u):
        tq0 = (v * NSA_SUB + u) * TQ
        urows = slice(u * TQ, (u + 1) * TQ)
        trows = slice(tq0, tq0 + TQ)
        cos_t, sin_t = cos_ref[trows, :], sin_ref[trows, :]
        heads = [q_ref[urows, r * HD:(r + 1) * HD] for r in range(R)]
        qc = jnp.concatenate([(h * scale).astype(BF16) for h in heads], axis=0)
        qr_heads = [_rope(h * (scale * LOG2E), cos_t, sin_t).astype(BF16) for h in heads]
        t_tok = tq0 + lax.broadcasted_iota(jnp.int32, (TQ, 1), 0)
        t_col = jnp.concatenate([t_tok] * R, axis=0)

        s_c = _nt_dot(qc, kc)
        n_idx = lax.broadcasted_iota(jnp.int32, (1, N_CMP_PAD), 1)
        p_c = _masked_softmax(s_c, (n_idx * CMP_STRIDE + (CMP_BLOCK - 1)) <= t_col)
        p_cb = p_c.astype(BF16)
        o_cmp = _dot(p_cb, vc)

        imp = _nt_dot(ovt, p_cb[0:TQ])
        for r in range(1, R):
            imp = imp + _nt_dot(ovt, p_cb[r * TQ:(r + 1) * TQ])
        j_idx = lax.broadcasted_iota(jnp.int32, (nb, TQ), 0)
        cur = (tq0 + lax.broadcasted_iota(jnp.int32, (nb, TQ), 1)) // SLC_BLOCK
        imp = jnp.where(j_idx <= cur, imp, INVALID_SCORE)
        imp = jnp.where((j_idx == 0) | (j_idx == cur) | (j_idx == cur - 1), FORCED_SCORE, imp)
        rank = jnp.zeros((nb, TQ), jnp.int32)
        for i in range(nb):
            ri = imp[i:i + 1, :]
            ahead = (ri > imp) | ((ri == imp) & (i < j_idx))
            rank = rank + jnp.where(ahead, 1, 0)
        blk_bias_t = jnp.where(rank < min(SLC_TOPK, nb), 0.0, MASKED)
        blk_bias_t = jnp.concatenate([blk_bias_t, jnp.zeros((LANES - nb, TQ), F32)], axis=0)
        blk_bias = jnp.transpose(blk_bias_t).astype(BF16)
        q_aug = jnp.concatenate([jnp.concatenate([h, blk_bias], axis=1) for h in qr_heads], axis=0)
        return dict(t_col=t_col, o_cmp=o_cmp, q_aug=q_aug, qr_heads=qr_heads)

    step = NSA_SUB * TQ
    span = WINDOW + step
    n_pair = R // 2

    def step_body(v):
        subs = [front(v, u) for u in range(NSA_SUB)]
        n_keys = (v + 1) * step
        start = max(v * step - WINDOW, 0)
        diff = (v * step + lax.broadcasted_iota(jnp.int32, (step, 1), 0)) - (
            start + lax.broadcasted_iota(jnp.int32, (1, span), 1))
        band = jnp.where((diff >= 0) & (diff < WINDOW), 0.0, MASKED)

        def win_scores(c):
            q_pair = jnp.concatenate(
                [subs[u]["qr_heads"][2 * c + rr] for rr in range(2) for u in range(NSA_SUB)], axis=0)
            s_w = _nt_dot(q_pair, kwr_s[start:start + span, :])
            for rr in range(2):
                sw_s[c][rr * step:(rr + 1) * step, :] = s_w[rr * step:(rr + 1) * step] + band

        def win_softmax(c):
            m_w = jnp.max(sw_s[c][...], axis=-1, keepdims=True)
            ew_s[c][...] = jnp.exp2(sw_s[c][...] - m_w).astype(BF16)

        def win_values(c):
            acc_w = _dot(ew_s[c][...], vwa_s[start:start + span, :])
            return acc_w[:, 0:HD] * (1.0 / acc_w[:, HD:])

        def sel_scores(u):
            q_aug, t_col = subs[u]["q_aug"], subs[u]["t_col"]
            for k0 in range(0, n_keys, TK):
                w = min(TK, n_keys - k0)
                s = _nt_dot(q_aug, ksa_s[k0:k0 + w, :])
                if k0 + w == n_keys:
                    kpos = k0 + lax.broadcasted_iota(jnp.int32, (1, w), 1)
                    s = jnp.where(kpos <= t_col, s, MASKED)
                ss_s[u][:, k0:k0 + w] = s

        def sel_softmax(u):
            m = jnp.max(ss_s[u][:, 0:n_keys], axis=-1, keepdims=True)
            es_s[u][:, 0:n_keys] = jnp.exp2(ss_s[u][:, 0:n_keys] - m).astype(BF16)

        def sel_values(u):
            acc = _dot(es_s[u][:, 0:n_keys], vsa_s[0:n_keys, :])
            return acc[:, 0:HD] * (1.0 / acc[:, HD:])

        assert NSA_SUB == 2 and n_pair == 2
        win_scores(0)
        win_scores(1)
        sel_scores(0)
        win_softmax(0)
        sel_scores(1)
        win_softmax(1)
        o_win = [win_values(0)]
        sel_softmax(0)
        o_win.append(win_values(1))
        sel_softmax(1)
        o_slc = [sel_values(0), sel_values(1)]

        misc = misc_ref[...]
        ng = 3 * R
        logits = jnp.where(g == 0, misc[:, MISC_GATE_LANE:MISC_GATE_LANE + ng],
                           misc[:, MISC_GATE_LANE + ng:MISC_GATE_LANE + 2 * ng])
        gate = jax.nn.sigmoid(logits)
        for u in range(NSA_SUB):
            urows = slice(u * TQ, (u + 1) * TQ)
            for r in range(R):
                rs = slice(r * TQ, (r + 1) * TQ)
                ws = slice((r % 2) * step + u * TQ, (r % 2) * step + (u + 1) * TQ)
                o = (gate[urows, 3 * r:3 * r + 1] * subs[u]["o_cmp"][rs]
                     + gate[urows, 3 * r + 1:3 * r + 2] * o_slc[u][rs]
                     + gate[urows, 3 * r + 2:3 * r + 3] * o_win[r // 2][ws])
                o_ref[urows, r * HD:(r + 1) * HD] = o.astype(o_ref.dtype)

    for v in range(S // step):
        pl.when(qi == v)(functools.partial(step_body, v))


BF16_ROWS = 16


def _side_cast_plan(weights, n_steps):
    plan = []
    for w in weights:
        n_rows = w.shape[0]
        if n_rows % (n_steps * BF16_ROWS) == 0:
            plan.append((n_rows // n_steps, None))
        else:
            assert n_rows % LANES == 0 and n_rows // LANES <= n_steps
            plan.append((LANES, n_rows // LANES))
    return plan


def _nsa(u3, cmp_kv, cos_f, sin_s, ovt, blk_onehot, side_weights):
    B, S, _ = u3.shape
    G, R, HD = NSA_KV_GROUPS, NSA_HPG, NSA_HD
    rows = R * NSA_TQ
    step = NSA_SUB * NSA_TQ
    assert NSA_TK % step == 0
    n_q = S // step
    plan = _side_cast_plan(side_weights, B * G * n_q)

    def side_spec(w, slab_rows, n_slabs):
        def index(b, g, i):
            s = (b * G + g) * n_q + i
            return (s if n_slabs is None else jnp.minimum(s, n_slabs - 1), 0)
        return pl.BlockSpec((slab_rows, w.shape[1]), index)

    side_specs = [side_spec(w, *p) for w, p in zip(side_weights, plan)]
    full = lambda cb: pl.BlockSpec((None, S, HD), lambda b, g, i: (b, 0, cb - CB_NQ + g))
    table = pl.BlockSpec((S, LANES), lambda b, g, i: (0, 0))
    kern = functools.partial(_nsa_kernel, len(side_weights), tuple(p[1] for p in plan))
    return pl.pallas_call(
        kern,
        grid=(B, G, n_q),
        in_specs=[
            pl.BlockSpec((None, step, R * HD), lambda b, g, i: (b, i, g)),
            pl.BlockSpec((None, step, LANES), lambda b, g, i: (b, i, CB_MISC - CB_NQ)),
            pl.BlockSpec((None, None, None, N_CMP_PAD, HD), lambda b, g, i: (0, b, g, 0, 0)),
            pl.BlockSpec((None, None, None, N_CMP_PAD, HD), lambda b, g, i: (1, b, g, 0, 0)),
            full(CB_KS), full(CB_VS), full(CB_KW), full(CB_VW),
            table, table,
            pl.BlockSpec((S // SLC_BLOCK, N_CMP_PAD), lambda b, g, i: (0, 0)),
            table,
        ] + side_specs,
        out_specs=[pl.BlockSpec((None, step, R * HD), lambda b, g, i: (b, i, g))] + side_specs,
        out_shape=[jax.ShapeDtypeStruct((B, S, NSA_WIDTH), BF16)]
        + [jax.ShapeDtypeStruct(w.shape, BF16) for w in side_weights],
        scratch_shapes=[pltpu.VMEM((S, 2 * HD), BF16), pltpu.VMEM((S, 2 * HD), BF16),
                        pltpu.VMEM((S, HD), BF16), pltpu.VMEM((S, 2 * HD), BF16)]
        + [pltpu.VMEM((rows, S), F32)] * NSA_SUB + [pltpu.VMEM((rows, S), BF16)] * NSA_SUB
        + [pltpu.VMEM((2 * step, WINDOW + step), F32)] * (R // 2)
        + [pltpu.VMEM((2 * step, WINDOW + step), BF16)] * (R // 2),
        compiler_params=pltpu.CompilerParams(
            dimension_semantics=("arbitrary", "arbitrary", "arbitrary"), vmem_limit_bytes=VMEM_LIMIT),
        name="nsa",
    )(u3, u3, cmp_kv, cmp_kv, u3, u3, u3, u3, cos_f, sin_s, ovt, blk_onehot, *side_weights)


def _out_proj_kernel(og_ref, on_ref, x_ref, w_ref, g_ref, b_ref, h_ref, hb_ref):
    half = og_ref.shape[0] // 2
    halves = [slice(0, half), slice(half, 2 * half)]
    mixes = [_dot(og_ref[rows, :], w_ref[0:GLA_WIDTH, :]) + _dot(on_ref[rows, :], w_ref[GLA_WIDTH:, :])
             for rows in halves]
    for rows, mix in zip(halves, mixes):
        h = _layer_norm(DEEPNORM_ALPHA * x_ref[rows, :] + mix, g_ref[...], b_ref[...])
        h_ref[rows, :] = h
        hb_ref[rows, :] = h.astype(BF16)


def _out_proj(o_gla, o_nsa, x2, w_bf, g, b):
    T, D = x2.shape
    tile = lambda w: pl.BlockSpec((OUT_TM, w), lambda i: (i, 0))
    const = lambda s: pl.BlockSpec(s, lambda i: (0, 0))
    return pl.pallas_call(
        _out_proj_kernel,
        grid=(T // OUT_TM,),
        in_specs=[tile(GLA_WIDTH), tile(NSA_WIDTH), tile(D), const(w_bf.shape), const((1, D)), const((1, D))],
        out_specs=[tile(D), tile(D)],
        out_shape=[jax.ShapeDtypeStruct((T, D), F32), jax.ShapeDtypeStruct((T, D), BF16)],
        compiler_params=pltpu.CompilerParams(
            dimension_semantics=("parallel",), vmem_limit_bytes=VMEM_LIMIT),
        name="out_proj",
    )(o_gla, o_nsa, x2, w_bf, g, b)


def _ffn_kernel(hb_ref, h_ref, w1_ref, w3_ref, w2_ref, g_ref, b_ref, o_ref, acc_s):
    f = pl.program_id(1)

    @pl.when(f == 0)
    def _():
        acc_s[...] = jnp.zeros(acc_s.shape, F32)

    hb = hb_ref[...]
    a = _dot(hb, w1_ref[...])
    c = _dot(hb, w3_ref[...])
    acc_s[...] += _dot((a * jax.nn.sigmoid(a) * c).astype(BF16), w2_ref[...])

    @pl.when(f == pl.num_programs(1) - 1)
    def _():
        o_ref[...] = _layer_norm(DEEPNORM_ALPHA * h_ref[...] + acc_s[...], g_ref[...], b_ref[...])


def _ffn(h_bf, h, w1, w3, w2, g, b):
    T, D = h.shape
    F = w1.shape[1]
    return pl.pallas_call(
        _ffn_kernel,
        grid=(T // FFN_TM, F // FFN_TF),
        in_specs=[
            pl.BlockSpec((FFN_TM, D), lambda i, f: (i, 0)),
            pl.BlockSpec((FFN_TM, D), lambda i, f: (i, 0)),
            pl.BlockSpec((D, FFN_TF), lambda i, f: (0, f)),
            pl.BlockSpec((D, FFN_TF), lambda i, f: (0, f)),
            pl.BlockSpec((FFN_TF, D), lambda i, f: (f, 0)),
            pl.BlockSpec((1, D), lambda i, f: (0, 0)),
            pl.BlockSpec((1, D), lambda i, f: (0, 0)),
        ],
        out_specs=pl.BlockSpec((FFN_TM, D), lambda i, f: (i, 0)),
        out_shape=jax.ShapeDtypeStruct((T, D), F32),
        scratch_shapes=[pltpu.VMEM((FFN_TM, D), F32)],
        compiler_params=pltpu.CompilerParams(
            dimension_semantics=("parallel", "arbitrary"), vmem_limit_bytes=VMEM_LIMIT),
        name="ffn",
    )(h_bf, h, w1, w3, w2, g, b)


def _rope_tables(S):
    half = ROPE_DIM // 2
    pos = np.arange(S, dtype=np.float64)
    inv = np.power(ROPE_THETA, -np.arange(0, ROPE_DIM, 2, dtype=np.float64) / ROPE_DIM)
    ang = pos[:, None] * inv[None, :]
    cos_f = np.ones((S, LANES), np.float32)
    sin_s = np.zeros((S, LANES), np.float32)
    for lane0, sign in ((0, -1.0), (ROPE_PAIR_LANE, 1.0)):
        cos_f[:, lane0:lane0 + half] = np.cos(ang)
        sin_s[:, lane0:lane0 + half] = sign * np.sin(ang)
    return jnp.asarray(cos_f), jnp.asarray(sin_s)


def _selection_constants(S):
    n_cmp = (S - CMP_BLOCK) // CMP_STRIDE + 1
    nb = S // SLC_BLOCK
    c_start = np.arange(n_cmp) * CMP_STRIDE
    b_start = np.arange(nb) * SLC_BLOCK
    overlap = ((c_start[:, None] < b_start[None, :] + SLC_BLOCK) &
               (b_start[None, :] < c_start[:, None] + CMP_BLOCK)).astype(np.float32)
    ovt = np.zeros((nb, N_CMP_PAD), np.float32)
    ovt[:, :n_cmp] = overlap.T
    blk_onehot = ((np.arange(S) // SLC_BLOCK)[:, None] == np.arange(LANES)[None, :]).astype(np.float32)
    return jnp.asarray(ovt, BF16), jnp.asarray(blk_onehot, BF16)


def kernel(x, w_in, gla_gate_w2, gla_gate_b2, gla_norm_w, cmp_k_pos, cmp_k_w1, cmp_k_w2,
           cmp_v_pos, cmp_v_w1, cmp_v_w2, w_out, ln1_g, ln1_b, ffn_w1, ffn_w3, ffn_w2, ln2_g, ln2_b):
    B, S, D = x.shape
    T = B * S
    cos_f, sin_s = _rope_tables(S)
    ovt, blk_onehot = _selection_constants(S)
    reorder_dims = lambda w: jnp.concatenate([w[:, lo:hi] for lo, hi in _head_dim_order()], axis=1)

    x2 = x.reshape(T, D)
    for l in range(DEPTH):
        wt_gla, wt_nsa = _regroup_w_in(w_in[l].T)
        ua3 = _in_proj(x2, wt_gla).reshape(B, S, -1)
        u3 = _in_proj(x2, wt_nsa).reshape(B, S, -1)
        o_gla = _gla(ua3, u3, gla_gate_w2[l], gla_gate_b2[l][None, :], gla_norm_w[l][None, :])
        cmp_kv = _compress(u3, jnp.stack([cmp_k_pos[l], cmp_v_pos[l]]),
                           jnp.stack([cmp_k_w1[l], cmp_v_w1[l]]).astype(BF16),
                           jnp.stack([reorder_dims(cmp_k_w2[l]), cmp_v_w2[l]]).astype(BF16))
        o_nsa, w_out_bf, w1_bf, w3_bf, w2_bf = _nsa(u3, cmp_kv, cos_f, sin_s, ovt, blk_onehot,
                                                    [w_out[l], ffn_w1[l], ffn_w3[l], ffn_w2[l]])
        h, h_bf = _out_proj(o_gla.reshape(T, GLA_WIDTH), o_nsa.reshape(T, NSA_WIDTH), x2,
                            w_out_bf, ln1_g[l][None, :], ln1_b[l][None, :])
        x2 = _ffn(h_bf, h, w1_bf, w3_bf, w2_bf, ln2_g[l][None, :], ln2_b[l][None, :])
    return x2.reshape(B, S, D)
```

```python
import functools

import numpy as np
import jax
import jax.numpy as jnp
from jax import lax
from jax.experimental import pallas as pl
from jax.experimental.pallas import tpu as pltpu

F32 = jnp.float32
BF16 = jnp.bfloat16

D_MODEL = 2048
DEPTH = 1
GLA_HEADS = 4
GLA_DK = 128
GLA_DV = 256
GLA_GATE_RANK = 16
GLA_GATE_NORM = 16.0
GLA_CHUNK = 64
NSA_HEADS = 8
NSA_HD = 128
NSA_KV_GROUPS = 2
NSA_HPG = NSA_HEADS // NSA_KV_GROUPS
CMP_BLOCK = 32
CMP_STRIDE = 16
CMP_HIDDEN = 2 * NSA_HD
SLC_BLOCK = 64
SLC_TOPK = 16
WINDOW = 512
ROPE_THETA = 500000.0
ROPE_DIM = NSA_HD // 4
FFN_HIDDEN = 5632
DEEPNORM_ALPHA = (2.0 * DEPTH) ** 0.25
LN_EPS = 1e-5
FORCED_SCORE = 1e4
INVALID_SCORE = -1e4

GLA_WIDTH = GLA_HEADS * GLA_DV
NSA_WIDTH = NSA_HEADS * NSA_HD

LANES = 128
VMEM_LIMIT = 56 * 1024 * 1024
CB_GQ = 0
CB_GK = 4
CB_GV = 8
CB_GO = 16
CB_NQ = 24
CB_KC = 32
CB_KS = 36
CB_VS = 38
CB_KW = 40
CB_VW = 42
CB_MISC = 44
MISC_GATE_LANE = GLA_GATE_RANK
U_WIDTH = 45 * LANES

REGROUP_TC = 256
IN_TM = 512
OUT_TM = 512
FFN_TM, FFN_TF = 512, 512
NSA_TQ = 128
NSA_SUB = 2
NSA_TK = 512
WIN_SPAN = WINDOW + NSA_TQ
MASKED = -1e30
LOG2E = 1.4426950408889634


def _nt_dot(a, b):
    return lax.dot_general(a, b, (((1,), (1,)), ((), ())), preferred_element_type=F32)


def _tn_dot(a, b):
    return lax.dot_general(a, b, (((0,), (0,)), ((), ())), preferred_element_type=F32)


def _dot(a, b):
    return jnp.dot(a, b, preferred_element_type=F32)


def _layer_norm(z, g, b):
    mu = jnp.mean(z, axis=-1, keepdims=True)
    zc = z - mu
    var = jnp.mean(zc * zc, axis=-1, keepdims=True)
    return zc * lax.rsqrt(var + LN_EPS) * g + b


def _masked_softmax(s, mask):
    sm = jnp.where(mask, s, -jnp.inf)
    m = jnp.max(sm, axis=-1, keepdims=True)
    m = jnp.where(m == -jnp.inf, 0.0, m)
    e = jnp.where(mask, jnp.exp(s - m), 0.0)
    den = jnp.sum(e, axis=-1, keepdims=True)
    return e * (1.0 / jnp.where(den > 0, den, 1.0))


def _regroup_kernel(w_ref, oa_ref, ob_ref):
    cols = w_ref.shape[1]
    o_glr = CB_NQ * LANES
    o_nq = o_glr + GLA_GATE_RANK
    o_gate = o_nq + (CB_MISC - CB_NQ) * LANES
    n_gate = NSA_HEADS * 3
    oa_ref[...] = w_ref[0:o_glr, :].astype(BF16)
    roped = (list(range(CB_NQ, CB_NQ + NSA_HEADS)) + list(range(CB_KS, CB_KS + NSA_KV_GROUPS))
             + list(range(CB_KW, CB_KW + NSA_KV_GROUPS)))
    for blk in range(CB_NQ, CB_MISC):
        dst, src = (blk - CB_NQ) * LANES, blk * LANES + GLA_GATE_RANK
        pieces = _head_dim_order() if blk in roped else [(0, NSA_HD)]
        at = 0
        for lo, hi in pieces:
            ob_ref[dst + at:dst + at + hi - lo, :] = w_ref[src + lo:src + hi, :].astype(BF16)
            at += hi - lo
    misc = jnp.concatenate([w_ref[o_glr:o_nq, :], w_ref[o_gate:o_gate + n_gate, :],
                            jnp.zeros((LANES - GLA_GATE_RANK - n_gate, cols), F32)], axis=0)
    ob_ref[(CB_MISC - CB_NQ) * LANES:, :] = misc.astype(BF16)


def _regroup_w_in(wt):
    n_in, D = wt.shape
    na, nb = CB_NQ * LANES, U_WIDTH - CB_NQ * LANES
    return pl.pallas_call(
        _regroup_kernel,
        grid=(D // REGROUP_TC,),
        in_specs=[pl.BlockSpec((n_in, REGROUP_TC), lambda i: (0, i))],
        out_specs=[pl.BlockSpec((na, REGROUP_TC), lambda i: (0, i)), pl.BlockSpec((nb, REGROUP_TC), lambda i: (0, i))],
        out_shape=[jax.ShapeDtypeStruct((na, D), BF16), jax.ShapeDtypeStruct((nb, D), BF16)],
        compiler_params=pltpu.CompilerParams(
            dimension_semantics=("parallel",), vmem_limit_bytes=VMEM_LIMIT),
        name="regroup_w_in",
    )(wt)


def _in_proj_kernel(x_ref, w_ref, o_ref):
    o_ref[...] = _nt_dot(x_ref[...].astype(BF16), w_ref[...])


def _in_proj(x2, wt_bf):
    T, D = x2.shape
    N = wt_bf.shape[0]
    return pl.pallas_call(
        _in_proj_kernel,
        grid=(T // IN_TM,),
        in_specs=[pl.BlockSpec((IN_TM, D), lambda i: (i, 0)),
                  pl.BlockSpec((N, D), lambda i: (0, 0))],
        out_specs=pl.BlockSpec((IN_TM, N), lambda i: (i, 0)),
        out_shape=jax.ShapeDtypeStruct((T, N), F32),
        compiler_params=pltpu.CompilerParams(
            dimension_semantics=("parallel",), vmem_limit_bytes=VMEM_LIMIT),
        name="in_proj",
    )(x2, wt_bf)


GLA_ROWBLK = 256
GLA_TS = 512


def _split3(a):
    hi = a.astype(BF16)
    r1 = a - hi.astype(F32)
    mid = r1.astype(BF16)
    lo = (r1 - mid.astype(F32)).astype(BF16)
    return hi, mid, lo


def _log_sigmoid(z):
    return jnp.minimum(z, 0.0) - jnp.log(1.0 + jnp.exp(-jnp.abs(z)))


def _gla_kernel(q_ref, k_ref, v_ref, go_ref, misc_ref, w2_ref, b2_ref, nw_ref, o_ref,
                qd_s, ki_s, ks_s, dec_s, state_s, sbf_s):
    TS = q_ref.shape[0]
    H, C, DK, DV = GLA_HEADS, GLA_CHUNK, GLA_DK, GLA_DV
    scale = DK ** -0.5

    @pl.when(pl.program_id(1) == 0)
    def _():
        state_s[...] = jnp.zeros(state_s.shape, F32)

    r = lax.broadcasted_iota(jnp.int32, (GLA_ROWBLK, GLA_ROWBLK), 0)
    c = lax.broadcasted_iota(jnp.int32, (GLA_ROWBLK, GLA_ROWBLK), 1)
    chunk_causal = ((r // C) == (c // C)) & (c <= r)
    cum_m = jnp.where(chunk_causal, 1.0, 0.0).astype(BF16)

    w2 = w2_ref[...].astype(BF16)
    b2 = b2_ref[...]
    for rb in range(TS // GLA_ROWBLK):
        rows = slice(rb * GLA_ROWBLK, (rb + 1) * GLA_ROWBLK)
        glr = misc_ref[rows, 0:GLA_GATE_RANK].astype(BF16)
        gk = _log_sigmoid(_dot(glr, w2) + b2) * (1.0 / GLA_GATE_NORM)
        hi, mid, lo = _split3(gk)
        bc_all = _dot(cum_m, hi) + _dot(cum_m, mid) + _dot(cum_m, lo)
        bl_all = jnp.concatenate(
            [jnp.broadcast_to(bc_all[j * C + C - 1:j * C + C, :], (C, H * DK)) for j in range(GLA_ROWBLK // C)],
            axis=0)
        for h in range(H):
            hk = slice(h * DK, (h + 1) * DK)
            bc = bc_all[:, hk]
            bl = bl_all[:, hk]
            q = q_ref[rows, hk] * scale
            k = k_ref[rows, hk]
            qd_s[h, rows, :] = (q * jnp.exp(bc)).astype(BF16)
            ki_s[h, rows, :] = (k * jnp.exp(-bc)).astype(BF16)
            ks_s[h, rows, :] = (k * jnp.exp(bl - bc)).astype(BF16)
            dec_s[h, rows, :] = jnp.exp(bl)

    n_chunk = TS // C
    nw = nw_ref[...]
    for h in range(H):
        hv = slice(h * DV, (h + 1) * DV)
        kv = [_tn_dot(ks_s[h, c * C:(c + 1) * C, :], v_ref[c * C:(c + 1) * C, hv].astype(BF16))
              for c in range(n_chunk)]
        dec_rows = jnp.concatenate([dec_s[h, c * C:c * C + 1, :] for c in range(n_chunk)]
                                   + [jnp.zeros((LANES - n_chunk, DK), F32)], axis=0)
        dec_cols = jnp.transpose(dec_rows)
        state = state_s[h]
        for c in range(n_chunk):
            sbf_s[h, c] = state.astype(BF16)
            state = state * dec_cols[:, c:c + 1] + kv[c]
        state_s[h] = state

    for h in range(H):
        hv = slice(h * DV, (h + 1) * DV)
        for rb in range(TS // GLA_ROWBLK):
            rows = slice(rb * GLA_ROWBLK, (rb + 1) * GLA_ROWBLK)
            qd = qd_s[h, rows, :]
            attn = jnp.where(chunk_causal, _nt_dot(qd, ki_s[h, rows, :]), 0.0).astype(BF16)
            o = _dot(attn, v_ref[rows, hv].astype(BF16))
            inter = [_dot(qd_s[h, c * C:(c + 1) * C, :], sbf_s[h, c])
                     for c in range(rb * GLA_ROWBLK // C, (rb + 1) * GLA_ROWBLK // C)]
            o = o + jnp.concatenate(inter, axis=0)
            ms = jnp.mean(o * o, axis=-1, keepdims=True)
            o = o * lax.rsqrt(ms + LN_EPS) * nw
            go = go_ref[rows, hv]
            o_ref[rows, hv] = (o * (go * jax.nn.sigmoid(go))).astype(o_ref.dtype)


def _gla(ua3, ub3, gate_w2, gate_b2, norm_w):
    B, S, _ = ua3.shape
    H, DK, DV = GLA_HEADS, GLA_DK, GLA_DV
    qk_w, v_w = H * DK, H * DV
    tok = lambda w, off: pl.BlockSpec((None, GLA_TS, w), lambda b, s: (b, s, off * LANES // w))
    const = lambda shape: pl.BlockSpec(shape, lambda b, s: (0, 0))
    return pl.pallas_call(
        _gla_kernel,
        grid=(B, S // GLA_TS),
        in_specs=[tok(qk_w, CB_GQ), tok(qk_w, CB_GK), tok(v_w, CB_GV), tok(v_w, CB_GO), tok(LANES, CB_MISC - CB_NQ),
                  const((GLA_GATE_RANK, qk_w)), const((1, qk_w)), const((1, DV))],
        out_specs=pl.BlockSpec((None, GLA_TS, v_w), lambda b, s: (b, s, 0)),
        out_shape=jax.ShapeDtypeStruct((B, S, GLA_WIDTH), BF16),
        scratch_shapes=[pltpu.VMEM((H, GLA_TS, DK), BF16), pltpu.VMEM((H, GLA_TS, DK), BF16),
                        pltpu.VMEM((H, GLA_TS, DK), BF16), pltpu.VMEM((H, GLA_TS, DK), F32),
                        pltpu.VMEM((H, DK, DV), F32), pltpu.VMEM((H, GLA_TS // GLA_CHUNK, DK, DV), BF16)],
        compiler_params=pltpu.CompilerParams(
            dimension_semantics=("parallel", "arbitrary"), vmem_limit_bytes=VMEM_LIMIT),
        name="gla",
    )(ua3, ua3, ua3, ua3, ub3, gate_w2, gate_b2, norm_w)


N_CMP_PAD = 128


def _gelu_tanh(x):
    return x * (0.5 * (1.0 + jnp.tanh(0.7978845608028654 * (x + 0.044715 * (x * x * x)))))


def _compress_kernel(k_ref, v_ref, pos_ref, w1_ref, w2_ref, o_ref):
    half = CMP_BLOCK // 2
    for kv, src_ref in enumerate((k_ref, v_ref)):
        p0 = jnp.zeros((N_CMP_PAD, CMP_HIDDEN), F32)
        p1 = jnp.zeros((N_CMP_PAD, CMP_HIDDEN), F32)
        for l in range(half):
            x = src_ref[pl.ds(l, N_CMP_PAD, stride=CMP_STRIDE), :]
            a0 = (x + pos_ref[kv, l:l + 1, :]).astype(BF16)
            a1 = (x + pos_ref[kv, half + l:half + l + 1, :]).astype(BF16)
            p0 = p0 + _dot(a0, w1_ref[kv, l * NSA_HD:(l + 1) * NSA_HD, :])
            p1 = p1 + _dot(a1, w1_ref[kv, (half + l) * NSA_HD:(half + l + 1) * NSA_HD, :])
        pre = p0 + pltpu.roll(p1, N_CMP_PAD - 1, 0)
        h = _gelu_tanh(pre).astype(BF16)
        out = _dot(h, w2_ref[kv])
        row = lax.broadcasted_iota(jnp.int32, out.shape, 0)
        o_ref[kv] = jnp.where(row < N_CMP_PAD - 1, out, 0.0).astype(o_ref.dtype)


def _compress(u3, pos, w1, w2):
    B, S, _ = u3.shape
    G = NSA_KV_GROUPS
    src = lambda kv: pl.BlockSpec((None, S, NSA_HD), lambda b, g: (b, 0, CB_KC - CB_NQ + 2 * kv + g))
    whole = lambda a: pl.BlockSpec(a.shape, lambda b, g: (0,) * a.ndim)
    return pl.pallas_call(
        _compress_kernel,
        grid=(B, G),
        in_specs=[src(0), src(1), whole(pos), whole(w1), whole(w2)],
        out_specs=pl.BlockSpec((2, None, None, N_CMP_PAD, NSA_HD), lambda b, g: (0, b, g, 0, 0)),
        out_shape=jax.ShapeDtypeStruct((2, B, G, N_CMP_PAD, NSA_HD), BF16),
        compiler_params=pltpu.CompilerParams(
            dimension_semantics=("parallel", "parallel"), vmem_limit_bytes=VMEM_LIMIT),
        name="compress",
    )(u3, u3, pos, w1, w2)


ROPE_PAIR_LANE = LANES // 2


def _head_dim_order():
    half = ROPE_DIM // 2
    return [(0, half), (ROPE_DIM, ROPE_PAIR_LANE + half), (half, ROPE_DIM), (ROPE_PAIR_LANE + half, NSA_HD)]


def _rope(x, cos, sin_signed):
    return x * cos + pltpu.roll(x, ROPE_PAIR_LANE, 1) * sin_signed


def _nsa_kernel(n_side, side_slabs, *refs):
    n_in = 12
    (q_ref, misc_ref, kc_ref, vc_ref, ks_ref, vs_ref, kw_ref, vw_ref,
     cos_ref, sin_ref, ovt_ref, blk_ref) = refs[0:n_in]
    side_in = refs[n_in:n_in + n_side]
    o_ref = refs[n_in + n_side]
    side_out = refs[n_in + 1 + n_side:n_in + 1 + 2 * n_side]
    ksa_s, vsa_s, kwr_s, vwa_s = refs[n_in + 1 + 2 * n_side:n_in + 5 + 2 * n_side]
    chain_scratch = refs[n_in + 5 + 2 * n_side:]
    S = ks_ref.shape[0]
    R, TQ, HD = NSA_HPG, NSA_TQ, NSA_HD
    ss_s, es_s = chain_scratch[0:NSA_SUB], chain_scratch[NSA_SUB:2 * NSA_SUB]
    sw_s, ew_s = chain_scratch[2 * NSA_SUB:2 * NSA_SUB + R // 2], chain_scratch[2 * NSA_SUB + R // 2:]
    g = pl.program_id(1)
    qi = pl.program_id(2)
    scale = HD ** -0.5

    step_idx = (pl.program_id(0) * pl.num_programs(1) + g) * pl.num_programs(2) + qi
    for w_in_ref, w_out_ref, n_slabs in zip(side_in, side_out, side_slabs):
        def cast(w_in_ref=w_in_ref, w_out_ref=w_out_ref):
            w_out_ref[...] = w_in_ref[...].astype(BF16)
        if n_slabs is None:
            cast()
        else:
            pl.when(step_idx < n_slabs)(cast)

    @pl.when(qi == 0)
    def _():
        def prep(rb, _):
            rows = pl.ds(pl.multiple_of(rb * 256, 256), 256)
            cos, sin = cos_ref[rows, :], sin_ref[rows, :]
            ones = jnp.ones((256, LANES), BF16)
            ksa_s[rows, 0:HD] = _rope(ks_ref[rows, :], cos, sin).astype(BF16)
            ksa_s[rows, HD:] = blk_ref[rows, :]
            kwr_s[rows, :] = _rope(kw_ref[rows, :], cos, sin).astype(BF16)
            vsa_s[rows, 0:HD] = vs_ref[rows, :].astype(BF16)
            vsa_s[rows, HD:] = ones
            vwa_s[rows, 0:HD] = vw_ref[rows, :].astype(BF16)
            vwa_s[rows, HD:] = ones
            return 0
        lax.fori_loop(0, S // 256, prep, 0)

    nb = S // SLC_BLOCK
    TK = NSA_TK
    ovt = ovt_ref[...]
    kc, vc = kc_ref[...], vc_ref[...]

    def front(v, u):
        tq0 = (v * NSA_SUB + u) * TQ
        urows = slice(u * TQ, (u + 1) * TQ)
        trows = slice(tq0, tq0 + TQ)
        cos_t, sin_t = cos_ref[trows, :], sin_ref[trows, :]
        heads = [q_ref[urows, r * HD:(r + 1) * HD] for r in range(R)]
        qc = jnp.concatenate([(h * scale).astype(BF16) for h in heads], axis=0)
        qr_heads = [_rope(h * (scale * LOG2E), cos_t, sin_t).astype(BF16) for h in heads]
        t_tok = tq0 + lax.broadcasted_iota(jnp.int32, (TQ, 1), 0)
        t_col = jnp.concatenate([t_tok] * R, axis=0)

        s_c = _nt_dot(qc, kc)
        n_idx = lax.broadcasted_iota(jnp.int32, (1, N_CMP_PAD), 1)
        p_c = _masked_softmax(s_c, (n_idx * CMP_STRIDE + (CMP_BLOCK - 1)) <= t_col)
        p_cb = p_c.astype(BF16)
        o_cmp = _dot(p_cb, vc)

        imp = _nt_dot(ovt, p_cb[0:TQ])
        for r in range(1, R):
            imp = imp + _nt_dot(ovt, p_cb[r * TQ:(r + 1) * TQ])
        j_idx = lax.broadcasted_iota(jnp.int32, (nb, TQ), 0)
        cur = (tq0 + lax.broadcasted_iota(jnp.int32, (nb, TQ), 1)) // SLC_BLOCK
        imp = jnp.where(j_idx <= cur, imp, INVALID_SCORE)
        imp = jnp.where((j_idx == 0) | (j_idx == cur) | (j_idx == cur - 1), FORCED_SCORE, imp)
        rank = jnp.zeros((nb, TQ), jnp.int32)
        for i in range(nb):
            ri = imp[i:i + 1, :]
            ahead = (ri > imp) | ((ri == imp) & (i < j_idx))
            rank = rank + jnp.where(ahead, 1, 0)
        blk_bias_t = jnp.where(rank < min(SLC_TOPK, nb), 0.0, MASKED)
        blk_bias_t = jnp.concatenate([blk_bias_t, jnp.zeros((LANES - nb, TQ), F32)], axis=0)
        blk_bias = jnp.transpose(blk_bias_t).astype(BF16)
        q_aug = jnp.concatenate([jnp.concatenate([h, blk_bias], axis=1) for h in qr_heads], axis=0)
        return dict(t_col=t_col, o_cmp=o_cmp, q_aug=q_aug, qr_heads=qr_heads)

    step = NSA_SUB * TQ
    n_pair = R // 2

    def step_body(v):
        subs = [front(v, u) for u in range(NSA_SUB)]
        n_keys = (v + 1) * step
        start = max(v * step - WINDOW, 0)
        span = n_keys - start
        diff = (v * step + lax.broadcasted_iota(jnp.int32, (step, 1), 0)) - (
            start + lax.broadcasted_iota(jnp.int32, (1, span), 1))
        band = jnp.where((diff >= 0) & (diff < WINDOW), 0.0, MASKED)

        def win_scores(c):
            q_pair = jnp.concatenate(
                [subs[u]["qr_heads"][2 * c + rr] for rr in range(2) for u in range(NSA_SUB)], axis=0)
            s_w = _nt_dot(q_pair, kwr_s[start:start + span, :])
            for rr in range(2):
                sw_s[c][rr * step:(rr + 1) * step, 0:span] = s_w[rr * step:(rr + 1) * step] + band

        def win_softmax(c):
            m_w = jnp.max(sw_s[c][:, 0:span], axis=-1, keepdims=True)
            ew_s[c][:, 0:span] = jnp.exp2(sw_s[c][:, 0:span] - m_w).astype(BF16)

        def win_values(c):
            acc_w = _dot(ew_s[c][:, 0:span], vwa_s[start:start + span, :])
            return acc_w[:, 0:HD] * (1.0 / acc_w[:, HD:])

        def sel_scores(u):
            q_aug, t_col = subs[u]["q_aug"], subs[u]["t_col"]
            for k0 in range(0, n_keys, TK):
                w = min(TK, n_keys - k0)
                s = _nt_dot(q_aug, ksa_s[k0:k0 + w, :])
                if k0 + w == n_keys:
                    kpos = k0 + lax.broadcasted_iota(jnp.int32, (1, w), 1)
                    s = jnp.where(kpos <= t_col, s, MASKED)
                ss_s[u][:, k0:k0 + w] = s

        def sel_softmax(u):
            m = jnp.max(ss_s[u][:, 0:n_keys], axis=-1, keepdims=True)
            es_s[u][:, 0:n_keys] = jnp.exp2(ss_s[u][:, 0:n_keys] - m).astype(BF16)

        def sel_values(u):
            acc = _dot(es_s[u][:, 0:n_keys], vsa_s[0:n_keys, :])
            return acc[:, 0:HD] * (1.0 / acc[:, HD:])

        assert NSA_SUB == 2 and n_pair == 2
        win_scores(0)
        win_scores(1)
        sel_scores(0)
        win_softmax(0)
        sel_scores(1)
        win_softmax(1)
        o_win = [win_values(0)]
        sel_softmax(0)
        o_win.append(win_values(1))
        sel_softmax(1)
        o_slc = [sel_values(0), sel_values(1)]

        misc = misc_ref[...]
        ng = 3 * R
        logits = jnp.where(g == 0, misc[:, MISC_GATE_LANE:MISC_GATE_LANE + ng],
                           misc[:, MISC_GATE_LANE + ng:MISC_GATE_LANE + 2 * ng])
        gate = jax.nn.sigmoid(logits)
        for u in range(NSA_SUB):
            urows = slice(u * TQ, (u + 1) * TQ)
            for r in range(R):
                rs = slice(r * TQ, (r + 1) * TQ)
                ws = slice((r % 2) * step + u * TQ, (r % 2) * step + (u + 1) * TQ)
                o = (gate[urows, 3 * r:3 * r + 1] * subs[u]["o_cmp"][rs]
                     + gate[urows, 3 * r + 1:3 * r + 2] * o_slc[u][rs]
                     + gate[urows, 3 * r + 2:3 * r + 3] * o_win[r // 2][ws])
                o_ref[urows, r * HD:(r + 1) * HD] = o.astype(o_ref.dtype)

    for v in range(S // step):
        pl.when(qi == v)(functools.partial(step_body, v))


BF16_ROWS = 16


def _side_cast_plan(weights, n_steps):
    plan = []
    for w in weights:
        n_rows = w.shape[0]
        if n_rows % (n_steps * BF16_ROWS) == 0:
            plan.append((n_rows // n_steps, None))
        else:
            assert n_rows % LANES == 0 and n_rows // LANES <= n_steps
            plan.append((LANES, n_rows // LANES))
    return plan


def _nsa(u3, cmp_kv, cos_f, sin_s, ovt, blk_onehot, side_weights):
    B, S, _ = u3.shape
    G, R, HD = NSA_KV_GROUPS, NSA_HPG, NSA_HD
    rows = R * NSA_TQ
    step = NSA_SUB * NSA_TQ
    assert NSA_TK % step == 0
    n_q = S // step
    plan = _side_cast_plan(side_weights, B * G * n_q)

    def side_spec(w, slab_rows, n_slabs):
        def index(b, g, i):
            s = (b * G + g) * n_q + i
            return (s if n_slabs is None else jnp.minimum(s, n_slabs - 1), 0)
        return pl.BlockSpec((slab_rows, w.shape[1]), index)

    side_specs = [side_spec(w, *p) for w, p in zip(side_weights, plan)]
    full = lambda cb: pl.BlockSpec((None, S, HD), lambda b, g, i: (b, 0, cb - CB_NQ + g))
    table = pl.BlockSpec((S, LANES), lambda b, g, i: (0, 0))
    kern = functools.partial(_nsa_kernel, len(side_weights), tuple(p[1] for p in plan))
    return pl.pallas_call(
        kern,
        grid=(B, G, n_q),
        in_specs=[
            pl.BlockSpec((None, step, R * HD), lambda b, g, i: (b, i, g)),
            pl.BlockSpec((None, step, LANES), lambda b, g, i: (b, i, CB_MISC - CB_NQ)),
            pl.BlockSpec((None, None, None, N_CMP_PAD, HD), lambda b, g, i: (0, b, g, 0, 0)),
            pl.BlockSpec((None, None, None, N_CMP_PAD, HD), lambda b, g, i: (1, b, g, 0, 0)),
            full(CB_KS), full(CB_VS), full(CB_KW), full(CB_VW),
            table, table,
            pl.BlockSpec((S // SLC_BLOCK, N_CMP_PAD), lambda b, g, i: (0, 0)),
            table,
        ] + side_specs,
        out_specs=[pl.BlockSpec((None, step, R * HD), lambda b, g, i: (b, i, g))] + side_specs,
        out_shape=[jax.ShapeDtypeStruct((B, S, NSA_WIDTH), BF16)]
        + [jax.ShapeDtypeStruct(w.shape, BF16) for w in side_weights],
        scratch_shapes=[pltpu.VMEM((S, 2 * HD), BF16), pltpu.VMEM((S, 2 * HD), BF16),
                        pltpu.VMEM((S, HD), BF16), pltpu.VMEM((S, 2 * HD), BF16)]
        + [pltpu.VMEM((rows, S), F32)] * NSA_SUB + [pltpu.VMEM((rows, S), BF16)] * NSA_SUB
        + [pltpu.VMEM((2 * step, WINDOW + step), F32)] * (R // 2)
        + [pltpu.VMEM((2 * step, WINDOW + step), BF16)] * (R // 2),
        compiler_params=pltpu.CompilerParams(
            dimension_semantics=("arbitrary", "arbitrary", "arbitrary"), vmem_limit_bytes=VMEM_LIMIT),
        name="nsa",
    )(u3, u3, cmp_kv, cmp_kv, u3, u3, u3, u3, cos_f, sin_s, ovt, blk_onehot, *side_weights)


def _out_proj_kernel(og_ref, on_ref, x_ref, w_ref, g_ref, b_ref, h_ref, hb_ref):
    half = og_ref.shape[0] // 2
    halves = [slice(0, half), slice(half, 2 * half)]
    mixes = [_dot(og_ref[rows, :], w_ref[0:GLA_WIDTH, :]) + _dot(on_ref[rows, :], w_ref[GLA_WIDTH:, :])
             for rows in halves]
    for rows, mix in zip(halves, mixes):
        h = _layer_norm(DEEPNORM_ALPHA * x_ref[rows, :] + mix, g_ref[...], b_ref[...])
        h_ref[rows, :] = h
        hb_ref[rows, :] = h.astype(BF16)


def _out_proj(o_gla, o_nsa, x2, w_bf, g, b):
    T, D = x2.shape
    tile = lambda w: pl.BlockSpec((OUT_TM, w), lambda i: (i, 0))
    const = lambda s: pl.BlockSpec(s, lambda i: (0, 0))
    return pl.pallas_call(
        _out_proj_kernel,
        grid=(T // OUT_TM,),
        in_specs=[tile(GLA_WIDTH), tile(NSA_WIDTH), tile(D), const(w_bf.shape), const((1, D)), const((1, D))],
        out_specs=[tile(D), tile(D)],
        out_shape=[jax.ShapeDtypeStruct((T, D), F32), jax.ShapeDtypeStruct((T, D), BF16)],
        compiler_params=pltpu.CompilerParams(
            dimension_semantics=("parallel",), vmem_limit_bytes=VMEM_LIMIT),
        name="out_proj",
    )(o_gla, o_nsa, x2, w_bf, g, b)


def _ffn_kernel(hb_ref, h_ref, w1_ref, w3_ref, w2_ref, g_ref, b_ref, o_ref, acc_s):
    f = pl.program_id(1)

    @pl.when(f == 0)
    def _():
        acc_s[...] = jnp.zeros(acc_s.shape, F32)

    hb = hb_ref[...]
    a = _dot(hb, w1_ref[...])
    c = _dot(hb, w3_ref[...])
    acc_s[...] += _dot((a * jax.nn.sigmoid(a) * c).astype(BF16), w2_ref[...])

    @pl.when(f == pl.num_programs(1) - 1)
    def _():
        o_ref[...] = _layer_norm(DEEPNORM_ALPHA * h_ref[...] + acc_s[...], g_ref[...], b_ref[...])


def _ffn(h_bf, h, w1, w3, w2, g, b):
    T, D = h.shape
    F = w1.shape[1]
    return pl.pallas_call(
        _ffn_kernel,
        grid=(T // FFN_TM, F // FFN_TF),
        in_specs=[
            pl.BlockSpec((FFN_TM, D), lambda i, f: (i, 0)),
            pl.BlockSpec((FFN_TM, D), lambda i, f: (i, 0)),
            pl.BlockSpec((D, FFN_TF), lambda i, f: (0, f)),
            pl.BlockSpec((D, FFN_TF), lambda i, f: (0, f)),
            pl.BlockSpec((FFN_TF, D), lambda i, f: (f, 0)),
            pl.BlockSpec((1, D), lambda i, f: (0, 0)),
            pl.BlockSpec((1, D), lambda i, f: (0, 0)),
        ],
        out_specs=pl.BlockSpec((FFN_TM, D), lambda i, f: (i, 0)),
        out_shape=jax.ShapeDtypeStruct((T, D), F32),
        scratch_shapes=[pltpu.VMEM((FFN_TM, D), F32)],
        compiler_params=pltpu.CompilerParams(
            dimension_semantics=("parallel", "arbitrary"), vmem_limit_bytes=VMEM_LIMIT),
        name="ffn",
    )(h_bf, h, w1, w3, w2, g, b)


def _rope_tables(S):
    half = ROPE_DIM // 2
    pos = np.arange(S, dtype=np.float64)
    inv = np.power(ROPE_THETA, -np.arange(0, ROPE_DIM, 2, dtype=np.float64) / ROPE_DIM)
    ang = pos[:, None] * inv[None, :]
    cos_f = np.ones((S, LANES), np.float32)
    sin_s = np.zeros((S, LANES), np.float32)
    for lane0, sign in ((0, -1.0), (ROPE_PAIR_LANE, 1.0)):
        cos_f[:, lane0:lane0 + half] = np.cos(ang)
        sin_s[:, lane0:lane0 + half] = sign * np.sin(ang)
    return jnp.asarray(cos_f), jnp.asarray(sin_s)


def _selection_constants(S):
    n_cmp = (S - CMP_BLOCK) // CMP_STRIDE + 1
    nb = S // SLC_BLOCK
    c_start = np.arange(n_cmp) * CMP_STRIDE
    b_start = np.arange(nb) * SLC_BLOCK
    overlap = ((c_start[:, None] < b_start[None, :] + SLC_BLOCK) &
               (b_start[None, :] < c_start[:, None] + CMP_BLOCK)).astype(np.float32)
    ovt = np.zeros((nb, N_CMP_PAD), np.float32)
    ovt[:, :n_cmp] = overlap.T
    blk_onehot = ((np.arange(S) // SLC_BLOCK)[:, None] == np.arange(LANES)[None, :]).astype(np.float32)
    return jnp.asarray(ovt, BF16), jnp.asarray(blk_onehot, BF16)


def kernel(x, w_in, gla_gate_w2, gla_gate_b2, gla_norm_w, cmp_k_pos, cmp_k_w1, cmp_k_w2,
           cmp_v_pos, cmp_v_w1, cmp_v_w2, w_out, ln1_g, ln1_b, ffn_w1, ffn_w3, ffn_w2, ln2_g, ln2_b):
    B, S, D = x.shape
    T = B * S
    cos_f, sin_s = _rope_tables(S)
    ovt, blk_onehot = _selection_constants(S)
    reorder_dims = lambda w: jnp.concatenate([w[:, lo:hi] for lo, hi in _head_dim_order()], axis=1)

    x2 = x.reshape(T, D)
    for l in range(DEPTH):
        wt_gla, wt_nsa = _regroup_w_in(w_in[l].T)
        ua3 = _in_proj(x2, wt_gla).reshape(B, S, -1)
        u3 = _in_proj(x2, wt_nsa).reshape(B, S, -1)
        o_gla = _gla(ua3, u3, gla_gate_w2[l], gla_gate_b2[l][None, :], gla_norm_w[l][None, :])
        cmp_kv = _compress(u3, jnp.stack([cmp_k_pos[l], cmp_v_pos[l]]),
                           jnp.stack([cmp_k_w1[l], cmp_v_w1[l]]).astype(BF16),
                           jnp.stack([reorder_dims(cmp_k_w2[l]), cmp_v_w2[l]]).astype(BF16))
        o_nsa, w_out_bf, w1_bf, w3_bf, w2_bf = _nsa(u3, cmp_kv, cos_f, sin_s, ovt, blk_onehot,
                                                    [w_out[l], ffn_w1[l], ffn_w3[l], ffn_w2[l]])
        h, h_bf = _out_proj(o_gla.reshape(T, GLA_WIDTH), o_nsa.reshape(T, NSA_WIDTH), x2,
                            w_out_bf, ln1_g[l][None, :], ln1_b[l][None, :])
        x2 = _ffn(h_bf, h, w1_bf, w3_bf, w2_bf, ln2_g[l][None, :], ln2_b[l][None, :])
    return x2.reshape(B, S, D)
```

```python
import functools

import numpy as np
import jax
import jax.numpy as jnp
from jax import lax
from jax.experimental import pallas as pl
from jax.experimental.pallas import tpu as pltpu

F32 = jnp.float32
BF16 = jnp.bfloat16

D_MODEL = 2048
DEPTH = 1
GLA_HEADS = 4
GLA_DK = 128
GLA_DV = 256
GLA_GATE_RANK = 16
GLA_GATE_NORM = 16.0
GLA_CHUNK = 64
NSA_HEADS = 8
NSA_HD = 128
NSA_KV_GROUPS = 2
NSA_HPG = NSA_HEADS // NSA_KV_GROUPS
CMP_BLOCK = 32
CMP_STRIDE = 16
CMP_HIDDEN = 2 * NSA_HD
SLC_BLOCK = 64
SLC_TOPK = 16
WINDOW = 512
ROPE_THETA = 500000.0
ROPE_DIM = NSA_HD // 4
FFN_HIDDEN = 5632
DEEPNORM_ALPHA = (2.0 * DEPTH) ** 0.25
LN_EPS = 1e-5
FORCED_SCORE = 1e4
INVALID_SCORE = -1e4

GLA_WIDTH = GLA_HEADS * GLA_DV
NSA_WIDTH = NSA_HEADS * NSA_HD

LANES = 128
VMEM_LIMIT = 56 * 1024 * 1024
CB_GQ = 0
CB_GK = 4
CB_GV = 8
CB_GO = 16
CB_NQ = 24
CB_KC = 32
CB_KS = 36
CB_VS = 38
CB_KW = 40
CB_VW = 42
CB_MISC = 44
MISC_GATE_LANE = GLA_GATE_RANK
U_WIDTH = 45 * LANES

REGROUP_TC = 256
IN_TM = 512
OUT_TM = 512
FFN_TM, FFN_TF = 512, 512
NSA_TQ = 128
NSA_SUB = 2
NSA_TK = 512
WIN_SPAN = WINDOW + NSA_TQ
MASKED = -1e30
LOG2E = 1.4426950408889634


def _nt_dot(a, b):
    return lax.dot_general(a, b, (((1,), (1,)), ((), ())), preferred_element_type=F32)


def _tn_dot(a, b):
    return lax.dot_general(a, b, (((0,), (0,)), ((), ())), preferred_element_type=F32)


def _dot(a, b):
    return jnp.dot(a, b, preferred_element_type=F32)


def _layer_norm(z, g, b):
    mu = jnp.mean(z, axis=-1, keepdims=True)
    zc = z - mu
    var = jnp.mean(zc * zc, axis=-1, keepdims=True)
    return zc * lax.rsqrt(var + LN_EPS) * g + b


def _masked_softmax(s, mask):
    sm = jnp.where(mask, s, -jnp.inf)
    m = jnp.max(sm, axis=-1, keepdims=True)
    m = jnp.where(m == -jnp.inf, 0.0, m)
    e = jnp.where(mask, jnp.exp(s - m), 0.0)
    den = jnp.sum(e, axis=-1, keepdims=True)
    return e * (1.0 / jnp.where(den > 0, den, 1.0))


def _regroup_kernel(w_ref, oa_ref, ob_ref):
    cols = w_ref.shape[1]
    o_glr = CB_NQ * LANES
    o_nq = o_glr + GLA_GATE_RANK
    o_gate = o_nq + (CB_MISC - CB_NQ) * LANES
    n_gate = NSA_HEADS * 3
    oa_ref[...] = w_ref[0:o_glr, :].astype(BF16)
    roped = (list(range(CB_NQ, CB_NQ + NSA_HEADS)) + list(range(CB_KS, CB_KS + NSA_KV_GROUPS))
             + list(range(CB_KW, CB_KW + NSA_KV_GROUPS)))
    for blk in range(CB_NQ, CB_MISC):
        dst, src = (blk - CB_NQ) * LANES, blk * LANES + GLA_GATE_RANK
        pieces = _head_dim_order() if blk in roped else [(0, NSA_HD)]
        at = 0
        for lo, hi in pieces:
            ob_ref[dst + at:dst + at + hi - lo, :] = w_ref[src + lo:src + hi, :].astype(BF16)
            at += hi - lo
    misc = jnp.concatenate([w_ref[o_glr:o_nq, :], w_ref[o_gate:o_gate + n_gate, :],
                            jnp.zeros((LANES - GLA_GATE_RANK - n_gate, cols), F32)], axis=0)
    ob_ref[(CB_MISC - CB_NQ) * LANES:, :] = misc.astype(BF16)


def _regroup_w_in(wt):
    n_in, D = wt.shape
    na, nb = CB_NQ * LANES, U_WIDTH - CB_NQ * LANES
    return pl.pallas_call(
        _regroup_kernel,
        grid=(D // REGROUP_TC,),
        in_specs=[pl.BlockSpec((n_in, REGROUP_TC), lambda i: (0, i))],
        out_specs=[pl.BlockSpec((na, REGROUP_TC), lambda i: (0, i)), pl.BlockSpec((nb, REGROUP_TC), lambda i: (0, i))],
        out_shape=[jax.ShapeDtypeStruct((na, D), BF16), jax.ShapeDtypeStruct((nb, D), BF16)],
        compiler_params=pltpu.CompilerParams(
            dimension_semantics=("parallel",), vmem_limit_bytes=VMEM_LIMIT),
        name="regroup_w_in",
    )(wt)


def _in_proj_kernel(x_ref, w_ref, o_ref):
    o_ref[...] = _nt_dot(x_ref[...].astype(BF16), w_ref[...])


def _in_proj(x2, wt_bf):
    T, D = x2.shape
    N = wt_bf.shape[0]
    return pl.pallas_call(
        _in_proj_kernel,
        grid=(T // IN_TM,),
        in_specs=[pl.BlockSpec((IN_TM, D), lambda i: (i, 0)),
                  pl.BlockSpec((N, D), lambda i: (0, 0))],
        out_specs=pl.BlockSpec((IN_TM, N), lambda i: (i, 0)),
        out_shape=jax.ShapeDtypeStruct((T, N), F32),
        compiler_params=pltpu.CompilerParams(
            dimension_semantics=("parallel",), vmem_limit_bytes=VMEM_LIMIT),
        name="in_proj",
    )(x2, wt_bf)


GLA_ROWBLK = 256
GLA_TS = 512


def _split3(a):
    hi = a.astype(BF16)
    r1 = a - hi.astype(F32)
    mid = r1.astype(BF16)
    lo = (r1 - mid.astype(F32)).astype(BF16)
    return hi, mid, lo


def _log_sigmoid(z):
    return jnp.minimum(z, 0.0) - jnp.log(1.0 + jnp.exp(-jnp.abs(z)))


def _gla_kernel(q_ref, k_ref, v_ref, go_ref, misc_ref, w2_ref, b2_ref, nw_ref, o_ref,
                qd_s, ki_s, ks_s, dec_s, state_s, sbf_s):
    TS = q_ref.shape[0]
    H, C, DK, DV = GLA_HEADS, GLA_CHUNK, GLA_DK, GLA_DV
    scale = DK ** -0.5

    @pl.when(pl.program_id(1) == 0)
    def _():
        state_s[...] = jnp.zeros(state_s.shape, F32)

    r = lax.broadcasted_iota(jnp.int32, (GLA_ROWBLK, GLA_ROWBLK), 0)
    c = lax.broadcasted_iota(jnp.int32, (GLA_ROWBLK, GLA_ROWBLK), 1)
    chunk_causal = ((r // C) == (c // C)) & (c <= r)
    cum_m = jnp.where(chunk_causal, 1.0, 0.0).astype(BF16)

    w2 = w2_ref[...].astype(BF16)
    b2 = b2_ref[...]
    for rb in range(TS // GLA_ROWBLK):
        rows = slice(rb * GLA_ROWBLK, (rb + 1) * GLA_ROWBLK)
        glr = misc_ref[rows, 0:GLA_GATE_RANK].astype(BF16)
        gk = _log_sigmoid(_dot(glr, w2) + b2) * (1.0 / GLA_GATE_NORM)
        hi, mid, lo = _split3(gk)
        bc_all = _dot(cum_m, hi) + _dot(cum_m, mid) + _dot(cum_m, lo)
        bl_all = jnp.concatenate(
            [jnp.broadcast_to(bc_all[j * C + C - 1:j * C + C, :], (C, H * DK)) for j in range(GLA_ROWBLK // C)],
            axis=0)
        for h in range(H):
            hk = slice(h * DK, (h + 1) * DK)
            bc = bc_all[:, hk]
            bl = bl_all[:, hk]
            q = q_ref[rows, hk] * scale
            k = k_ref[rows, hk]
            qd_s[h, rows, :] = (q * jnp.exp(bc)).astype(BF16)
            ki_s[h, rows, :] = (k * jnp.exp(-bc)).astype(BF16)
            ks_s[h, rows, :] = (k * jnp.exp(bl - bc)).astype(BF16)
            dec_s[h, rows, :] = jnp.exp(bl)

    n_chunk = TS // C
    nw = nw_ref[...]
    for h in range(H):
        hv = slice(h * DV, (h + 1) * DV)
        kv = [_tn_dot(ks_s[h, c * C:(c + 1) * C, :], v_ref[c * C:(c + 1) * C, hv].astype(BF16))
              for c in range(n_chunk)]
        dec_rows = jnp.concatenate([dec_s[h, c * C:c * C + 1, :] for c in range(n_chunk)]
                                   + [jnp.zeros((LANES - n_chunk, DK), F32)], axis=0)
        dec_cols = jnp.transpose(dec_rows)
        state = state_s[h]
        for c in range(n_chunk):
            sbf_s[h, c] = state.astype(BF16)
            state = state * dec_cols[:, c:c + 1] + kv[c]
        state_s[h] = state

    for h in range(H):
        hv = slice(h * DV, (h + 1) * DV)
        for rb in range(TS // GLA_ROWBLK):
            rows = slice(rb * GLA_ROWBLK, (rb + 1) * GLA_ROWBLK)
            qd = qd_s[h, rows, :]
            attn = jnp.where(chunk_causal, _nt_dot(qd, ki_s[h, rows, :]), 0.0).astype(BF16)
            o = _dot(attn, v_ref[rows, hv].astype(BF16))
            inter = [_dot(qd_s[h, c * C:(c + 1) * C, :], sbf_s[h, c])
                     for c in range(rb * GLA_ROWBLK // C, (rb + 1) * GLA_ROWBLK // C)]
            o = o + jnp.concatenate(inter, axis=0)
            ms = jnp.mean(o * o, axis=-1, keepdims=True)
            o = o * lax.rsqrt(ms + LN_EPS) * nw
            go = go_ref[rows, hv]
            o_ref[rows, hv] = (o * (go * jax.nn.sigmoid(go))).astype(o_ref.dtype)


def _gla(ua3, ub3, gate_w2, gate_b2, norm_w):
    B, S, _ = ua3.shape
    H, DK, DV = GLA_HEADS, GLA_DK, GLA_DV
    qk_w, v_w = H * DK, H * DV
    tok = lambda w, off: pl.BlockSpec((None, GLA_TS, w), lambda b, s: (b, s, off * LANES // w))
    const = lambda shape: pl.BlockSpec(shape, lambda b, s: (0, 0))
    return pl.pallas_call(
        _gla_kernel,
        grid=(B, S // GLA_TS),
        in_specs=[tok(qk_w, CB_GQ), tok(qk_w, CB_GK), tok(v_w, CB_GV), tok(v_w, CB_GO), tok(LANES, CB_MISC - CB_NQ),
                  const((GLA_GATE_RANK, qk_w)), const((1, qk_w)), const((1, DV))],
        out_specs=pl.BlockSpec((None, GLA_TS, v_w), lambda b, s: (b, s, 0)),
        out_shape=jax.ShapeDtypeStruct((B, S, GLA_WIDTH), BF16),
        scratch_shapes=[pltpu.VMEM((H, GLA_TS, DK), BF16), pltpu.VMEM((H, GLA_TS, DK), BF16),
                        pltpu.VMEM((H, GLA_TS, DK), BF16), pltpu.VMEM((H, GLA_TS, DK), F32),
                        pltpu.VMEM((H, DK, DV), F32), pltpu.VMEM((H, GLA_TS // GLA_CHUNK, DK, DV), BF16)],
        compiler_params=pltpu.CompilerParams(
            dimension_semantics=("parallel", "arbitrary"), vmem_limit_bytes=VMEM_LIMIT),
        name="gla",
    )(ua3, ua3, ua3, ua3, ub3, gate_w2, gate_b2, norm_w)


N_CMP_PAD = 128


def _gelu_tanh(x):
    return x * (0.5 * (1.0 + jnp.tanh(0.7978845608028654 * (x + 0.044715 * (x * x * x)))))


def _compress_kernel(k_ref, v_ref, kpos_ref, vpos_ref, kw1_ref, vw1_ref, kw2_ref, vw2_ref, o_ref):
    half = CMP_BLOCK // 2
    chains = ((k_ref, kpos_ref, kw1_ref, kw2_ref, _head_dim_order()),
              (v_ref, vpos_ref, vw1_ref, vw2_ref, [(0, NSA_HD)]))
    for kv, (src_ref, pos_ref, w1_ref, w2_ref, out_order) in enumerate(chains):
        p0 = jnp.zeros((N_CMP_PAD, CMP_HIDDEN), F32)
        p1 = jnp.zeros((N_CMP_PAD, CMP_HIDDEN), F32)
        for l in range(half):
            x = src_ref[pl.ds(l, N_CMP_PAD, stride=CMP_STRIDE), :]
            a0 = (x + pos_ref[l:l + 1, :]).astype(BF16)
            a1 = (x + pos_ref[half + l:half + l + 1, :]).astype(BF16)
            p0 = p0 + _dot(a0, w1_ref[l * NSA_HD:(l + 1) * NSA_HD, :].astype(BF16))
            p1 = p1 + _dot(a1, w1_ref[(half + l) * NSA_HD:(half + l + 1) * NSA_HD, :].astype(BF16))
        pre = p0 + pltpu.roll(p1, N_CMP_PAD - 1, 0)
        h = _gelu_tanh(pre).astype(BF16)
        w2 = jnp.concatenate([w2_ref[:, lo:hi] for lo, hi in out_order], axis=1).astype(BF16)
        out = _dot(h, w2)
        row = lax.broadcasted_iota(jnp.int32, out.shape, 0)
        o_ref[kv] = jnp.where(row < N_CMP_PAD - 1, out, 0.0).astype(o_ref.dtype)


def _compress(u3, k_params, v_params):
    B, S, _ = u3.shape
    G = NSA_KV_GROUPS
    src = lambda kv: pl.BlockSpec((None, S, NSA_HD), lambda b, g: (b, 0, CB_KC - CB_NQ + 2 * kv + g))
    whole = lambda a: pl.BlockSpec(a.shape, lambda b, g: (0,) * a.ndim)
    params = [k_params[0], v_params[0], k_params[1], v_params[1], k_params[2], v_params[2]]
    return pl.pallas_call(
        _compress_kernel,
        grid=(B, G),
        in_specs=[src(0), src(1)] + [whole(a) for a in params],
        out_specs=pl.BlockSpec((2, None, None, N_CMP_PAD, NSA_HD), lambda b, g: (0, b, g, 0, 0)),
        out_shape=jax.ShapeDtypeStruct((2, B, G, N_CMP_PAD, NSA_HD), BF16),
        compiler_params=pltpu.CompilerParams(
            dimension_semantics=("parallel", "parallel"), vmem_limit_bytes=VMEM_LIMIT),
        name="compress",
    )(u3, u3, *params)


ROPE_PAIR_LANE = LANES // 2


def _head_dim_order():
    half = ROPE_DIM // 2
    return [(0, half), (ROPE_DIM, ROPE_PAIR_LANE + half), (half, ROPE_DIM), (ROPE_PAIR_LANE + half, NSA_HD)]


def _rope(x, cos, sin_signed):
    return x * cos + pltpu.roll(x, ROPE_PAIR_LANE, 1) * sin_signed


def _nsa_kernel(n_side, side_slabs, *refs):
    n_in = 12
    (q_ref, misc_ref, kc_ref, vc_ref, ks_ref, vs_ref, kw_ref, vw_ref,
     cos_ref, sin_ref, ovt_ref, blk_ref) = refs[0:n_in]
    side_in = refs[n_in:n_in + n_side]
    o_ref = refs[n_in + n_side]
    side_out = refs[n_in + 1 + n_side:n_in + 1 + 2 * n_side]
    ksa_s, vsa_s, kwr_s, vwa_s = refs[n_in + 1 + 2 * n_side:n_in + 5 + 2 * n_side]
    chain_scratch = refs[n_in + 5 + 2 * n_side:]
    S = ks_ref.shape[0]
    R, TQ, HD = NSA_HPG, NSA_TQ, NSA_HD
    ss_s, es_s = chain_scratch[0:NSA_SUB], chain_scratch[NSA_SUB:2 * NSA_SUB]
    sw_s, ew_s = chain_scratch[2 * NSA_SUB:2 * NSA_SUB + R // 2], chain_scratch[2 * NSA_SUB + R // 2:]
    g = pl.program_id(1)
    qi = pl.program_id(2)
    scale = HD ** -0.5

    step_idx = (pl.program_id(0) * pl.num_programs(1) + g) * pl.num_programs(2) + qi
    for w_in_ref, w_out_ref, n_slabs in zip(side_in, side_out, side_slabs):
        def cast(w_in_ref=w_in_ref, w_out_ref=w_out_ref):
            w_out_ref[...] = w_in_ref[...].astype(BF16)
        if n_slabs is None:
            cast()
        else:
            pl.when(step_idx < n_slabs)(cast)

    @pl.when(qi == 0)
    def _():
        def prep(rb, _):
            rows = pl.ds(pl.multiple_of(rb * 256, 256), 256)
            cos, sin = cos_ref[rows, :], sin_ref[rows, :]
            ones = jnp.ones((256, LANES), BF16)
            ksa_s[rows, 0:HD] = _rope(ks_ref[rows, :], cos, sin).astype(BF16)
            ksa_s[rows, HD:] = blk_ref[rows, :]
            kwr_s[rows, :] = _rope(kw_ref[rows, :], cos, sin).astype(BF16)
            vsa_s[rows, 0:HD] = vs_ref[rows, :].astype(BF16)
            vsa_s[rows, HD:] = ones
            vwa_s[rows, 0:HD] = vw_ref[rows, :].astype(BF16)
            vwa_s[rows, HD:] = ones
            return 0
        lax.fori_loop(0, S // 256, prep, 0)

    nb = S // SLC_BLOCK
    TK = NSA_TK
    ovt = ovt_ref[...]
    kc, vc = kc_ref[...], vc_ref[...]

    def front(v, u):
        tq0 = (v * NSA_SUB + u) * TQ
        urows = slice(u * TQ, (u + 1) * TQ)
        trows = slice(tq0, tq0 + TQ)
        cos_t, sin_t = cos_ref[trows, :], sin_ref[trows, :]
        heads = [q_ref[urows, r * HD:(r + 1) * HD] for r in range(R)]
        qc = jnp.concatenate([(h * scale).astype(BF16) for h in heads], axis=0)
        qr_heads = [_rope(h * (scale * LOG2E), cos_t, sin_t).astype(BF16) for h in heads]
        t_tok = tq0 + lax.broadcasted_iota(jnp.int32, (TQ, 1), 0)
        t_col = jnp.concatenate([t_tok] * R, axis=0)

        s_c = _nt_dot(qc, kc)
        n_idx = lax.broadcasted_iota(jnp.int32, (1, N_CMP_PAD), 1)
        p_c = _masked_softmax(s_c, (n_idx * CMP_STRIDE + (CMP_BLOCK - 1)) <= t_col)
        p_cb = p_c.astype(BF16)
        o_cmp = _dot(p_cb, vc)

        imp = _nt_dot(ovt, p_cb[0:TQ])
        for r in range(1, R):
            imp = imp + _nt_dot(ovt, p_cb[r * TQ:(r + 1) * TQ])
        j_idx = lax.broadcasted_iota(jnp.int32, (nb, TQ), 0)
        cur = (tq0 + lax.broadcasted_iota(jnp.int32, (nb, TQ), 1)) // SLC_BLOCK
        imp = jnp.where(j_idx <= cur, imp, INVALID_SCORE)
        imp = jnp.where((j_idx == 0) | (j_idx == cur) | (j_idx == cur - 1), FORCED_SCORE, imp)
        rank = jnp.zeros((nb, TQ), jnp.int32)
        for i in range(nb):
            ri = imp[i:i + 1, :]
            ahead = (ri > imp) | ((ri == imp) & (i < j_idx))
            rank = rank + jnp.where(ahead, 1, 0)
        blk_bias_t = jnp.where(rank < min(SLC_TOPK, nb), 0.0, MASKED)
        blk_bias_t = jnp.concatenate([blk_bias_t, jnp.zeros((LANES - nb, TQ), F32)], axis=0)
        blk_bias = jnp.transpose(blk_bias_t).astype(BF16)
        q_aug = jnp.concatenate([jnp.concatenate([h, blk_bias], axis=1) for h in qr_heads], axis=0)
        return dict(t_col=t_col, o_cmp=o_cmp, q_aug=q_aug, qr_heads=qr_heads)

    step = NSA_SUB * TQ
    n_pair = R // 2

    def step_body(v):
        subs = [front(v, u) for u in range(NSA_SUB)]
        n_keys = (v + 1) * step
        start = max(v * step - WINDOW, 0)
        span = n_keys - start
        diff = (v * step + lax.broadcasted_iota(jnp.int32, (step, 1), 0)) - (
            start + lax.broadcasted_iota(jnp.int32, (1, span), 1))
        band = jnp.where((diff >= 0) & (diff < WINDOW), 0.0, MASKED)

        def win_scores(c):
            q_pair = jnp.concatenate(
                [subs[u]["qr_heads"][2 * c + rr] for rr in range(2) for u in range(NSA_SUB)], axis=0)
            s_w = _nt_dot(q_pair, kwr_s[start:start + span, :])
            for rr in range(2):
                sw_s[c][rr * step:(rr + 1) * step, 0:span] = s_w[rr * step:(rr + 1) * step] + band

        def win_softmax(c):
            m_w = jnp.max(sw_s[c][:, 0:span], axis=-1, keepdims=True)
            ew_s[c][:, 0:span] = jnp.exp2(sw_s[c][:, 0:span] - m_w).astype(BF16)

        def win_values(c):
            acc_w = _dot(ew_s[c][:, 0:span], vwa_s[start:start + span, :])
            return acc_w[:, 0:HD] * (1.0 / acc_w[:, HD:])

        def sel_scores(u):
            q_aug, t_col = subs[u]["q_aug"], subs[u]["t_col"]
            for k0 in range(0, n_keys, TK):
                w = min(TK, n_keys - k0)
                s = _nt_dot(q_aug, ksa_s[k0:k0 + w, :])
                if k0 + w == n_keys:
                    kpos = k0 + lax.broadcasted_iota(jnp.int32, (1, w), 1)
                    s = jnp.where(kpos <= t_col, s, MASKED)
                ss_s[u][:, k0:k0 + w] = s

        def sel_softmax(u):
            m = jnp.max(ss_s[u][:, 0:n_keys], axis=-1, keepdims=True)
            es_s[u][:, 0:n_keys] = jnp.exp2(ss_s[u][:, 0:n_keys] - m).astype(BF16)

        def sel_values(u):
            acc = _dot(es_s[u][:, 0:n_keys], vsa_s[0:n_keys, :])
            return acc[:, 0:HD] * (1.0 / acc[:, HD:])

        assert NSA_SUB == 2 and n_pair == 2
        win_scores(0)
        win_scores(1)
        sel_scores(0)
        win_softmax(0)
        sel_scores(1)
        win_softmax(1)
        o_win = [win_values(0)]
        sel_softmax(0)
        o_win.append(win_values(1))
        sel_softmax(1)
        o_slc = [sel_values(0), sel_values(1)]

        misc = misc_ref[...]
        ng = 3 * R
        logits = jnp.where(g == 0, misc[:, MISC_GATE_LANE:MISC_GATE_LANE + ng],
                           misc[:, MISC_GATE_LANE + ng:MISC_GATE_LANE + 2 * ng])
        gate = jax.nn.sigmoid(logits)
        for u in range(NSA_SUB):
            urows = slice(u * TQ, (u + 1) * TQ)
            for r in range(R):
                rs = slice(r * TQ, (r + 1) * TQ)
                ws = slice((r % 2) * step + u * TQ, (r % 2) * step + (u + 1) * TQ)
                o = (gate[urows, 3 * r:3 * r + 1] * subs[u]["o_cmp"][rs]
                     + gate[urows, 3 * r + 1:3 * r + 2] * o_slc[u][rs]
                     + gate[urows, 3 * r + 2:3 * r + 3] * o_win[r // 2][ws])
                o_ref[urows, r * HD:(r + 1) * HD] = o.astype(o_ref.dtype)

    for v in range(S // step):
        pl.when(qi == v)(functools.partial(step_body, v))


BF16_ROWS = 16


def _side_cast_plan(weights, n_steps):
    plan = []
    for w in weights:
        n_rows = w.shape[0]
        if n_rows % (n_steps * BF16_ROWS) == 0:
            plan.append((n_rows // n_steps, None))
        else:
            assert n_rows % LANES == 0 and n_rows // LANES <= n_steps
            plan.append((LANES, n_rows // LANES))
    return plan


def _nsa(u3, cmp_kv, cos_f, sin_s, ovt, blk_onehot, side_weights):
    B, S, _ = u3.shape
    G, R, HD = NSA_KV_GROUPS, NSA_HPG, NSA_HD
    rows = R * NSA_TQ
    step = NSA_SUB * NSA_TQ
    assert NSA_TK % step == 0
    n_q = S // step
    plan = _side_cast_plan(side_weights, B * G * n_q)

    def side_spec(w, slab_rows, n_slabs):
        def index(b, g, i):
            s = (b * G + g) * n_q + i
            return (s if n_slabs is None else jnp.minimum(s, n_slabs - 1), 0)
        return pl.BlockSpec((slab_rows, w.shape[1]), index)

    side_specs = [side_spec(w, *p) for w, p in zip(side_weights, plan)]
    full = lambda cb: pl.BlockSpec((None, S, HD), lambda b, g, i: (b, 0, cb - CB_NQ + g))
    table = pl.BlockSpec((S, LANES), lambda b, g, i: (0, 0))
    kern = functools.partial(_nsa_kernel, len(side_weights), tuple(p[1] for p in plan))
    return pl.pallas_call(
        kern,
        grid=(B, G, n_q),
        in_specs=[
            pl.BlockSpec((None, step, R * HD), lambda b, g, i: (b, i, g)),
            pl.BlockSpec((None, step, LANES), lambda b, g, i: (b, i, CB_MISC - CB_NQ)),
            pl.BlockSpec((None, None, None, N_CMP_PAD, HD), lambda b, g, i: (0, b, g, 0, 0)),
            pl.BlockSpec((None, None, None, N_CMP_PAD, HD), lambda b, g, i: (1, b, g, 0, 0)),
            full(CB_KS), full(CB_VS), full(CB_KW), full(CB_VW),
            table, table,
            pl.BlockSpec((S // SLC_BLOCK, N_CMP_PAD), lambda b, g, i: (0, 0)),
            table,
        ] + side_specs,
        out_specs=[pl.BlockSpec((None, step, R * HD), lambda b, g, i: (b, i, g))] + side_specs,
        out_shape=[jax.ShapeDtypeStruct((B, S, NSA_WIDTH), BF16)]
        + [jax.ShapeDtypeStruct(w.shape, BF16) for w in side_weights],
        scratch_shapes=[pltpu.VMEM((S, 2 * HD), BF16), pltpu.VMEM((S, 2 * HD), BF16),
                        pltpu.VMEM((S, HD), BF16), pltpu.VMEM((S, 2 * HD), BF16)]
        + [pltpu.VMEM((rows, S), F32)] * NSA_SUB + [pltpu.VMEM((rows, S), BF16)] * NSA_SUB
        + [pltpu.VMEM((2 * step, WINDOW + step), F32)] * (R // 2)
        + [pltpu.VMEM((2 * step, WINDOW + step), BF16)] * (R // 2),
        compiler_params=pltpu.CompilerParams(
            dimension_semantics=("arbitrary", "arbitrary", "arbitrary"), vmem_limit_bytes=VMEM_LIMIT),
        name="nsa",
    )(u3, u3, cmp_kv, cmp_kv, u3, u3, u3, u3, cos_f, sin_s, ovt, blk_onehot, *side_weights)


def _out_proj_kernel(og_ref, on_ref, x_ref, w_ref, g_ref, b_ref, h_ref, hb_ref):
    half = og_ref.shape[0] // 2
    halves = [slice(0, half), slice(half, 2 * half)]
    mixes = [_dot(og_ref[rows, :], w_ref[0:GLA_WIDTH, :]) + _dot(on_ref[rows, :], w_ref[GLA_WIDTH:, :])
             for rows in halves]
    for rows, mix in zip(halves, mixes):
        h = _layer_norm(DEEPNORM_ALPHA * x_ref[rows, :] + mix, g_ref[...], b_ref[...])
        h_ref[rows, :] = h
        hb_ref[rows, :] = h.astype(BF16)


def _out_proj(o_gla, o_nsa, x2, w_bf, g, b):
    T, D = x2.shape
    tile = lambda w: pl.BlockSpec((OUT_TM, w), lambda i: (i, 0))
    const = lambda s: pl.BlockSpec(s, lambda i: (0, 0))
    return pl.pallas_call(
        _out_proj_kernel,
        grid=(T // OUT_TM,),
        in_specs=[tile(GLA_WIDTH), tile(NSA_WIDTH), tile(D), const(w_bf.shape), const((1, D)), const((1, D))],
        out_specs=[tile(D), tile(D)],
        out_shape=[jax.ShapeDtypeStruct((T, D), F32), jax.ShapeDtypeStruct((T, D), BF16)],
        compiler_params=pltpu.CompilerParams(
            dimension_semantics=("parallel",), vmem_limit_bytes=VMEM_LIMIT),
        name="out_proj",
    )(o_gla, o_nsa, x2, w_bf, g, b)


def _ffn_kernel(hb_ref, h_ref, w1_ref, w3_ref, w2_ref, g_ref, b_ref, o_ref, acc_s):
    f = pl.program_id(1)

    @pl.when(f == 0)
    def _():
        acc_s[...] = jnp.zeros(acc_s.shape, F32)

    hb = hb_ref[...]
    a = _dot(hb, w1_ref[...])
    c = _dot(hb, w3_ref[...])
    acc_s[...] += _dot((a * jax.nn.sigmoid(a) * c).astype(BF16), w2_ref[...])

    @pl.when(f == pl.num_programs(1) - 1)
    def _():
        o_ref[...] = _layer_norm(DEEPNORM_ALPHA * h_ref[...] + acc_s[...], g_ref[...], b_ref[...])


def _ffn(h_bf, h, w1, w3, w2, g, b):
    T, D = h.shape
    F = w1.shape[1]
    return pl.pallas_call(
        _ffn_kernel,
        grid=(T // FFN_TM, F // FFN_TF),
        in_specs=[
            pl.BlockSpec((FFN_TM, D), lambda i, f: (i, 0)),
            pl.BlockSpec((FFN_TM, D), lambda i, f: (i, 0)),
            pl.BlockSpec((D, FFN_TF), lambda i, f: (0, f)),
            pl.BlockSpec((D, FFN_TF), lambda i, f: (0, f)),
            pl.BlockSpec((FFN_TF, D), lambda i, f: (f, 0)),
            pl.BlockSpec((1, D), lambda i, f: (0, 0)),
            pl.BlockSpec((1, D), lambda i, f: (0, 0)),
        ],
        out_specs=pl.BlockSpec((FFN_TM, D), lambda i, f: (i, 0)),
        out_shape=jax.ShapeDtypeStruct((T, D), F32),
        scratch_shapes=[pltpu.VMEM((FFN_TM, D), F32)],
        compiler_params=pltpu.CompilerParams(
            dimension_semantics=("parallel", "arbitrary"), vmem_limit_bytes=VMEM_LIMIT),
        name="ffn",
    )(h_bf, h, w1, w3, w2, g, b)


def _rope_tables(S):
    half = ROPE_DIM // 2
    pos = np.arange(S, dtype=np.float64)
    inv = np.power(ROPE_THETA, -np.arange(0, ROPE_DIM, 2, dtype=np.float64) / ROPE_DIM)
    ang = pos[:, None] * inv[None, :]
    cos_f = np.ones((S, LANES), np.float32)
    sin_s = np.zeros((S, LANES), np.float32)
    for lane0, sign in ((0, -1.0), (ROPE_PAIR_LANE, 1.0)):
        cos_f[:, lane0:lane0 + half] = np.cos(ang)
        sin_s[:, lane0:lane0 + half] = sign * np.sin(ang)
    return jnp.asarray(cos_f), jnp.asarray(sin_s)


def _selection_constants(S):
    n_cmp = (S - CMP_BLOCK) // CMP_STRIDE + 1
    nb = S // SLC_BLOCK
    c_start = np.arange(n_cmp) * CMP_STRIDE
    b_start = np.arange(nb) * SLC_BLOCK
    overlap = ((c_start[:, None] < b_start[None, :] + SLC_BLOCK) &
               (b_start[None, :] < c_start[:, None] + CMP_BLOCK)).astype(np.float32)
    ovt = np.zeros((nb, N_CMP_PAD), np.float32)
    ovt[:, :n_cmp] = overlap.T
    blk_onehot = ((np.arange(S) // SLC_BLOCK)[:, None] == np.arange(LANES)[None, :]).astype(np.float32)
    return jnp.asarray(ovt, BF16), jnp.asarray(blk_onehot, BF16)


def kernel(x, w_in, gla_gate_w2, gla_gate_b2, gla_norm_w, cmp_k_pos, cmp_k_w1, cmp_k_w2,
           cmp_v_pos, cmp_v_w1, cmp_v_w2, w_out, ln1_g, ln1_b, ffn_w1, ffn_w3, ffn_w2, ln2_g, ln2_b):
    B, S, D = x.shape
    T = B * S
    cos_f, sin_s = _rope_tables(S)
    ovt, blk_onehot = _selection_constants(S)

    x2 = x.reshape(T, D)
    for l in range(DEPTH):
        wt_gla, wt_nsa = _regroup_w_in(w_in[l].T)
        ua3 = _in_proj(x2, wt_gla).reshape(B, S, -1)
        u3 = _in_proj(x2, wt_nsa).reshape(B, S, -1)
        o_gla = _gla(ua3, u3, gla_gate_w2[l], gla_gate_b2[l][None, :], gla_norm_w[l][None, :])
        cmp_kv = _compress(u3, (cmp_k_pos[l], cmp_k_w1[l], cmp_k_w2[l]), (cmp_v_pos[l], cmp_v_w1[l], cmp_v_w2[l]))
        o_nsa, w_out_bf, w1_bf, w3_bf, w2_bf = _nsa(u3, cmp_kv, cos_f, sin_s, ovt, blk_onehot,
                                                    [w_out[l], ffn_w1[l], ffn_w3[l], ffn_w2[l]])
        h, h_bf = _out_proj(o_gla.reshape(T, GLA_WIDTH), o_nsa.reshape(T, NSA_WIDTH), x2,
                            w_out_bf, ln1_g[l][None, :], ln1_b[l][None, :])
        x2 = _ffn(h_bf, h, w1_bf, w3_bf, w2_bf, ln2_g[l][None, :], ln2_b[l][None, :])
    return x2.reshape(B, S, D)
```

```python
import functools

import numpy as np
import jax
import jax.numpy as jnp
from jax import lax
from jax.experimental import pallas as pl
from jax.experimental.pallas import tpu as pltpu

F32 = jnp.float32
BF16 = jnp.bfloat16

D_MODEL = 2048
DEPTH = 1
GLA_HEADS = 4
GLA_DK = 128
GLA_DV = 256
GLA_GATE_RANK = 16
GLA_GATE_NORM = 16.0
GLA_CHUNK = 64
NSA_HEADS = 8
NSA_HD = 128
NSA_KV_GROUPS = 2
NSA_HPG = NSA_HEADS // NSA_KV_GROUPS
CMP_BLOCK = 32
CMP_STRIDE = 16
CMP_HIDDEN = 2 * NSA_HD
SLC_BLOCK = 64
SLC_TOPK = 16
WINDOW = 512
ROPE_THETA = 500000.0
ROPE_DIM = NSA_HD // 4
FFN_HIDDEN = 5632
DEEPNORM_ALPHA = (2.0 * DEPTH) ** 0.25
LN_EPS = 1e-5
FORCED_SCORE = 1e4
INVALID_SCORE = -1e4

GLA_WIDTH = GLA_HEADS * GLA_DV
NSA_WIDTH = NSA_HEADS * NSA_HD

LANES = 128
VMEM_LIMIT = 56 * 1024 * 1024
CB_GQ = 0
CB_GK = 4
CB_GV = 8
CB_GO = 16
CB_NQ = 24
CB_KC = 32
CB_KS = 36
CB_VS = 38
CB_KW = 40
CB_VW = 42
CB_MISC = 44
MISC_GATE_LANE = GLA_GATE_RANK
U_WIDTH = 45 * LANES

REGROUP_TC = 256
IN_TM = 512
OUT_TM = 512
FFN_TM, FFN_TF = 512, 512
NSA_TQ = 128
NSA_SUB = 2
NSA_TK = 512
WIN_SPAN = WINDOW + NSA_TQ
MASKED = -1e30
LOG2E = 1.4426950408889634


def _nt_dot(a, b):
    return lax.dot_general(a, b, (((1,), (1,)), ((), ())), preferred_element_type=F32)


def _tn_dot(a, b):
    return lax.dot_general(a, b, (((0,), (0,)), ((), ())), preferred_element_type=F32)


def _dot(a, b):
    return jnp.dot(a, b, preferred_element_type=F32)


def _layer_norm(z, g, b):
    mu = jnp.mean(z, axis=-1, keepdims=True)
    zc = z - mu
    var = jnp.mean(zc * zc, axis=-1, keepdims=True)
    return zc * lax.rsqrt(var + LN_EPS) * g + b


def _masked_softmax(s, mask):
    sm = jnp.where(mask, s, -jnp.inf)
    m = jnp.max(sm, axis=-1, keepdims=True)
    m = jnp.where(m == -jnp.inf, 0.0, m)
    e = jnp.where(mask, jnp.exp(s - m), 0.0)
    den = jnp.sum(e, axis=-1, keepdims=True)
    return e * (1.0 / jnp.where(den > 0, den, 1.0))


def _regroup_kernel(w_ref, oa_ref, ob_ref):
    cols = w_ref.shape[1]
    o_glr = CB_NQ * LANES
    o_nq = o_glr + GLA_GATE_RANK
    o_gate = o_nq + (CB_MISC - CB_NQ) * LANES
    n_gate = NSA_HEADS * 3
    oa_ref[...] = w_ref[0:o_glr, :].astype(BF16)
    roped = (list(range(CB_NQ, CB_NQ + NSA_HEADS)) + list(range(CB_KS, CB_KS + NSA_KV_GROUPS))
             + list(range(CB_KW, CB_KW + NSA_KV_GROUPS)))
    for blk in range(CB_NQ, CB_MISC):
        dst, src = (blk - CB_NQ) * LANES, blk * LANES + GLA_GATE_RANK
        pieces = _head_dim_order() if blk in roped else [(0, NSA_HD)]
        at = 0
        for lo, hi in pieces:
            ob_ref[dst + at:dst + at + hi - lo, :] = w_ref[src + lo:src + hi, :].astype(BF16)
            at += hi - lo
    misc = jnp.concatenate([w_ref[o_glr:o_nq, :], w_ref[o_gate:o_gate + n_gate, :],
                            jnp.zeros((LANES - GLA_GATE_RANK - n_gate, cols), F32)], axis=0)
    ob_ref[(CB_MISC - CB_NQ) * LANES:, :] = misc.astype(BF16)


def _regroup_w_in(wt):
    n_in, D = wt.shape
    na, nb = CB_NQ * LANES, U_WIDTH - CB_NQ * LANES
    return pl.pallas_call(
        _regroup_kernel,
        grid=(D // REGROUP_TC,),
        in_specs=[pl.BlockSpec((n_in, REGROUP_TC), lambda i: (0, i))],
        out_specs=[pl.BlockSpec((na, REGROUP_TC), lambda i: (0, i)), pl.BlockSpec((nb, REGROUP_TC), lambda i: (0, i))],
        out_shape=[jax.ShapeDtypeStruct((na, D), BF16), jax.ShapeDtypeStruct((nb, D), BF16)],
        compiler_params=pltpu.CompilerParams(
            dimension_semantics=("parallel",), vmem_limit_bytes=VMEM_LIMIT),
        name="regroup_w_in",
    )(wt)


def _in_proj_kernel(x_ref, w_ref, o_ref):
    o_ref[...] = _nt_dot(x_ref[...].astype(BF16), w_ref[...])


def _in_proj(x2, wt_bf):
    T, D = x2.shape
    N = wt_bf.shape[0]
    return pl.pallas_call(
        _in_proj_kernel,
        grid=(T // IN_TM,),
        in_specs=[pl.BlockSpec((IN_TM, D), lambda i: (i, 0)),
                  pl.BlockSpec((N, D), lambda i: (0, 0))],
        out_specs=pl.BlockSpec((IN_TM, N), lambda i: (i, 0)),
        out_shape=jax.ShapeDtypeStruct((T, N), F32),
        compiler_params=pltpu.CompilerParams(
            dimension_semantics=("parallel",), vmem_limit_bytes=VMEM_LIMIT),
        name="in_proj",
    )(x2, wt_bf)


def _in_proj_nsa_kernel(x_ref, w_ref, cos_ref, sin_ref, blk_ref, u_ref, ksa_ref, vsa_ref, kwr_ref, vwa_ref):
    res = _nt_dot(x_ref[...].astype(BF16), w_ref[...])
    u_ref[...] = res
    cos, sin = cos_ref[...], sin_ref[...]
    ones = jnp.ones((res.shape[0], LANES), BF16)
    col = lambda cb, g: slice((cb - CB_NQ + g) * LANES, (cb - CB_NQ + g + 1) * LANES)
    for g in range(NSA_KV_GROUPS):
        lo, hi = slice(2 * g * LANES, (2 * g + 1) * LANES), slice((2 * g + 1) * LANES, (2 * g + 2) * LANES)
        ksa_ref[:, lo] = _rope(res[:, col(CB_KS, g)], cos, sin).astype(BF16)
        ksa_ref[:, hi] = blk_ref[...]
        vsa_ref[:, lo] = res[:, col(CB_VS, g)].astype(BF16)
        vsa_ref[:, hi] = ones
        kwr_ref[:, g * LANES:(g + 1) * LANES] = _rope(res[:, col(CB_KW, g)], cos, sin).astype(BF16)
        vwa_ref[:, lo] = res[:, col(CB_VW, g)].astype(BF16)
        vwa_ref[:, hi] = ones


def _in_proj_nsa(x2, wt_bf, S, cos_f, sin_s, blk_onehot):
    T, D = x2.shape
    N = wt_bf.shape[0]
    G = NSA_KV_GROUPS
    per_seq = S // IN_TM
    table = pl.BlockSpec((IN_TM, LANES), lambda i: (i % per_seq, 0))
    rows = lambda w: pl.BlockSpec((IN_TM, w), lambda i: (i, 0))
    widths = [N, 2 * G * LANES, 2 * G * LANES, G * LANES, 2 * G * LANES]
    return pl.pallas_call(
        _in_proj_nsa_kernel,
        grid=(T // IN_TM,),
        in_specs=[rows(D), pl.BlockSpec((N, D), lambda i: (0, 0)), table, table, table],
        out_specs=[rows(w) for w in widths],
        out_shape=[jax.ShapeDtypeStruct((T, N), F32)] + [jax.ShapeDtypeStruct((T, w), BF16) for w in widths[1:]],
        compiler_params=pltpu.CompilerParams(
            dimension_semantics=("parallel",), vmem_limit_bytes=VMEM_LIMIT),
        name="in_proj_nsa",
    )(x2, wt_bf, cos_f, sin_s, blk_onehot)


GLA_ROWBLK = 256
GLA_TS = 512


def _split3(a):
    hi = a.astype(BF16)
    r1 = a - hi.astype(F32)
    mid = r1.astype(BF16)
    lo = (r1 - mid.astype(F32)).astype(BF16)
    return hi, mid, lo


def _log_sigmoid(z):
    return jnp.minimum(z, 0.0) - jnp.log(1.0 + jnp.exp(-jnp.abs(z)))


def _gla_kernel(q_ref, k_ref, v_ref, go_ref, misc_ref, w2_ref, b2_ref, nw_ref, o_ref,
                qd_s, ki_s, ks_s, dec_s, state_s, sbf_s):
    TS = q_ref.shape[0]
    H, C, DK, DV = GLA_HEADS, GLA_CHUNK, GLA_DK, GLA_DV
    scale = DK ** -0.5

    @pl.when(pl.program_id(1) == 0)
    def _():
        state_s[...] = jnp.zeros(state_s.shape, F32)

    r = lax.broadcasted_iota(jnp.int32, (GLA_ROWBLK, GLA_ROWBLK), 0)
    c = lax.broadcasted_iota(jnp.int32, (GLA_ROWBLK, GLA_ROWBLK), 1)
    chunk_causal = ((r // C) == (c // C)) & (c <= r)
    cum_m = jnp.where(chunk_causal, 1.0, 0.0).astype(BF16)

    w2 = w2_ref[...].astype(BF16)
    b2 = b2_ref[...]
    for rb in range(TS // GLA_ROWBLK):
        rows = slice(rb * GLA_ROWBLK, (rb + 1) * GLA_ROWBLK)
        glr = misc_ref[rows, 0:GLA_GATE_RANK].astype(BF16)
        gk = _log_sigmoid(_dot(glr, w2) + b2) * (1.0 / GLA_GATE_NORM)
        hi, mid, lo = _split3(gk)
        bc_all = _dot(cum_m, hi) + _dot(cum_m, mid) + _dot(cum_m, lo)
        bl_all = jnp.concatenate(
            [jnp.broadcast_to(bc_all[j * C + C - 1:j * C + C, :], (C, H * DK)) for j in range(GLA_ROWBLK // C)],
            axis=0)
        for h in range(H):
            hk = slice(h * DK, (h + 1) * DK)
            bc = bc_all[:, hk]
            bl = bl_all[:, hk]
            q = q_ref[rows, hk] * scale
            k = k_ref[rows, hk]
            qd_s[h, rows, :] = (q * jnp.exp(bc)).astype(BF16)
            ki_s[h, rows, :] = (k * jnp.exp(-bc)).astype(BF16)
            ks_s[h, rows, :] = (k * jnp.exp(bl - bc)).astype(BF16)
            dec_s[h, rows, :] = jnp.exp(bl)

    n_chunk = TS // C
    nw = nw_ref[...]
    for h in range(H):
        hv = slice(h * DV, (h + 1) * DV)
        kv = [_tn_dot(ks_s[h, c * C:(c + 1) * C, :], v_ref[c * C:(c + 1) * C, hv].astype(BF16))
              for c in range(n_chunk)]
        dec_rows = jnp.concatenate([dec_s[h, c * C:c * C + 1, :] for c in range(n_chunk)]
                                   + [jnp.zeros((LANES - n_chunk, DK), F32)], axis=0)
        dec_cols = jnp.transpose(dec_rows)
        state = state_s[h]
        for c in range(n_chunk):
            sbf_s[h, c] = state.astype(BF16)
            state = state * dec_cols[:, c:c + 1] + kv[c]
        state_s[h] = state

    for h in range(H):
        hv = slice(h * DV, (h + 1) * DV)
        for rb in range(TS // GLA_ROWBLK):
            rows = slice(rb * GLA_ROWBLK, (rb + 1) * GLA_ROWBLK)
            qd = qd_s[h, rows, :]
            attn = jnp.where(chunk_causal, _nt_dot(qd, ki_s[h, rows, :]), 0.0).astype(BF16)
            o = _dot(attn, v_ref[rows, hv].astype(BF16))
            inter = [_dot(qd_s[h, c * C:(c + 1) * C, :], sbf_s[h, c])
                     for c in range(rb * GLA_ROWBLK // C, (rb + 1) * GLA_ROWBLK // C)]
            o = o + jnp.concatenate(inter, axis=0)
            ms = jnp.mean(o * o, axis=-1, keepdims=True)
            o = o * lax.rsqrt(ms + LN_EPS) * nw
            go = go_ref[rows, hv]
            o_ref[rows, hv] = (o * (go * jax.nn.sigmoid(go))).astype(o_ref.dtype)


def _gla(ua3, ub3, gate_w2, gate_b2, norm_w):
    B, S, _ = ua3.shape
    H, DK, DV = GLA_HEADS, GLA_DK, GLA_DV
    qk_w, v_w = H * DK, H * DV
    tok = lambda w, off: pl.BlockSpec((None, GLA_TS, w), lambda b, s: (b, s, off * LANES // w))
    const = lambda shape: pl.BlockSpec(shape, lambda b, s: (0, 0))
    return pl.pallas_call(
        _gla_kernel,
        grid=(B, S // GLA_TS),
        in_specs=[tok(qk_w, CB_GQ), tok(qk_w, CB_GK), tok(v_w, CB_GV), tok(v_w, CB_GO), tok(LANES, CB_MISC - CB_NQ),
                  const((GLA_GATE_RANK, qk_w)), const((1, qk_w)), const((1, DV))],
        out_specs=pl.BlockSpec((None, GLA_TS, v_w), lambda b, s: (b, s, 0)),
        out_shape=jax.ShapeDtypeStruct((B, S, GLA_WIDTH), BF16),
        scratch_shapes=[pltpu.VMEM((H, GLA_TS, DK), BF16), pltpu.VMEM((H, GLA_TS, DK), BF16),
                        pltpu.VMEM((H, GLA_TS, DK), BF16), pltpu.VMEM((H, GLA_TS, DK), F32),
                        pltpu.VMEM((H, DK, DV), F32), pltpu.VMEM((H, GLA_TS // GLA_CHUNK, DK, DV), BF16)],
        compiler_params=pltpu.CompilerParams(
            dimension_semantics=("parallel", "arbitrary"), vmem_limit_bytes=VMEM_LIMIT),
        name="gla",
    )(ua3, ua3, ua3, ua3, ub3, gate_w2, gate_b2, norm_w)


N_CMP_PAD = 128


def _gelu_tanh(x):
    return x * (0.5 * (1.0 + jnp.tanh(0.7978845608028654 * (x + 0.044715 * (x * x * x)))))


def _compress_kernel(k_ref, v_ref, kpos_ref, vpos_ref, kw1_ref, vw1_ref, kw2_ref, vw2_ref, o_ref):
    half = CMP_BLOCK // 2
    chains = ((k_ref, kpos_ref, kw1_ref, kw2_ref, _head_dim_order()),
              (v_ref, vpos_ref, vw1_ref, vw2_ref, [(0, NSA_HD)]))
    for kv, (src_ref, pos_ref, w1_ref, w2_ref, out_order) in enumerate(chains):
        p0 = jnp.zeros((N_CMP_PAD, CMP_HIDDEN), F32)
        p1 = jnp.zeros((N_CMP_PAD, CMP_HIDDEN), F32)
        for l in range(half):
            x = src_ref[pl.ds(l, N_CMP_PAD, stride=CMP_STRIDE), :]
            a0 = (x + pos_ref[l:l + 1, :]).astype(BF16)
            a1 = (x + pos_ref[half + l:half + l + 1, :]).astype(BF16)
            p0 = p0 + _dot(a0, w1_ref[l * NSA_HD:(l + 1) * NSA_HD, :].astype(BF16))
            p1 = p1 + _dot(a1, w1_ref[(half + l) * NSA_HD:(half + l + 1) * NSA_HD, :].astype(BF16))
        pre = p0 + pltpu.roll(p1, N_CMP_PAD - 1, 0)
        h = _gelu_tanh(pre).astype(BF16)
        w2 = jnp.concatenate([w2_ref[:, lo:hi] for lo, hi in out_order], axis=1).astype(BF16)
        out = _dot(h, w2)
        row = lax.broadcasted_iota(jnp.int32, out.shape, 0)
        o_ref[kv] = jnp.where(row < N_CMP_PAD - 1, out, 0.0).astype(o_ref.dtype)


def _compress(u3, k_params, v_params):
    B, S, _ = u3.shape
    G = NSA_KV_GROUPS
    src = lambda kv: pl.BlockSpec((None, S, NSA_HD), lambda b, g: (b, 0, CB_KC - CB_NQ + 2 * kv + g))
    whole = lambda a: pl.BlockSpec(a.shape, lambda b, g: (0,) * a.ndim)
    params = [k_params[0], v_params[0], k_params[1], v_params[1], k_params[2], v_params[2]]
    return pl.pallas_call(
        _compress_kernel,
        grid=(B, G),
        in_specs=[src(0), src(1)] + [whole(a) for a in params],
        out_specs=pl.BlockSpec((2, None, None, N_CMP_PAD, NSA_HD), lambda b, g: (0, b, g, 0, 0)),
        out_shape=jax.ShapeDtypeStruct((2, B, G, N_CMP_PAD, NSA_HD), BF16),
        compiler_params=pltpu.CompilerParams(
            dimension_semantics=("parallel", "parallel"), vmem_limit_bytes=VMEM_LIMIT),
        name="compress",
    )(u3, u3, *params)


ROPE_PAIR_LANE = LANES // 2


def _head_dim_order():
    half = ROPE_DIM // 2
    return [(0, half), (ROPE_DIM, ROPE_PAIR_LANE + half), (half, ROPE_DIM), (ROPE_PAIR_LANE + half, NSA_HD)]


def _rope(x, cos, sin_signed):
    return x * cos + pltpu.roll(x, ROPE_PAIR_LANE, 1) * sin_signed


def _nsa_kernel(n_side, side_slabs, *refs):
    n_in = 11
    (q_ref, misc_ref, kc_ref, vc_ref, ksa_s, vsa_s, kwr_s, vwa_s, cos_ref, sin_ref, ovt_ref) = refs[0:n_in]
    side_in = refs[n_in:n_in + n_side]
    o_ref = refs[n_in + n_side]
    side_out = refs[n_in + 1 + n_side:n_in + 1 + 2 * n_side]
    chain_scratch = refs[n_in + 1 + 2 * n_side:]
    S = ksa_s.shape[0]
    R, TQ, HD = NSA_HPG, NSA_TQ, NSA_HD
    ss_s, es_s = chain_scratch[0:NSA_SUB], chain_scratch[NSA_SUB:2 * NSA_SUB]
    sw_s, ew_s = chain_scratch[2 * NSA_SUB:2 * NSA_SUB + R // 2], chain_scratch[2 * NSA_SUB + R // 2:]
    g = pl.program_id(1)
    qi = pl.program_id(2)
    scale = HD ** -0.5

    step_idx = (pl.program_id(0) * pl.num_programs(1) + g) * pl.num_programs(2) + qi
    for w_in_ref, w_out_ref, n_slabs in zip(side_in, side_out, side_slabs):
        def cast(w_in_ref=w_in_ref, w_out_ref=w_out_ref):
            w_out_ref[...] = w_in_ref[...].astype(BF16)
        if n_slabs is None:
            cast()
        else:
            pl.when(step_idx < n_slabs)(cast)

    nb = S // SLC_BLOCK
    TK = NSA_TK
    ovt = ovt_ref[...]
    kc, vc = kc_ref[...], vc_ref[...]

    def front(v, u):
        tq0 = (v * NSA_SUB + u) * TQ
        urows = slice(u * TQ, (u + 1) * TQ)
        trows = slice(tq0, tq0 + TQ)
        cos_t, sin_t = cos_ref[trows, :], sin_ref[trows, :]
        heads = [q_ref[urows, r * HD:(r + 1) * HD] for r in range(R)]
        qc = jnp.concatenate([(h * scale).astype(BF16) for h in heads], axis=0)
        qr_heads = [_rope(h * (scale * LOG2E), cos_t, sin_t).astype(BF16) for h in heads]
        t_tok = tq0 + lax.broadcasted_iota(jnp.int32, (TQ, 1), 0)
        t_col = jnp.concatenate([t_tok] * R, axis=0)

        s_c = _nt_dot(qc, kc)
        n_idx = lax.broadcasted_iota(jnp.int32, (1, N_CMP_PAD), 1)
        p_c = _masked_softmax(s_c, (n_idx * CMP_STRIDE + (CMP_BLOCK - 1)) <= t_col)
        p_cb = p_c.astype(BF16)
        o_cmp = _dot(p_cb, vc)

        imp = _nt_dot(ovt, p_cb[0:TQ])
        for r in range(1, R):
            imp = imp + _nt_dot(ovt, p_cb[r * TQ:(r + 1) * TQ])
        j_idx = lax.broadcasted_iota(jnp.int32, (nb, TQ), 0)
        cur = (tq0 + lax.broadcasted_iota(jnp.int32, (nb, TQ), 1)) // SLC_BLOCK
        imp = jnp.where(j_idx <= cur, imp, INVALID_SCORE)
        imp = jnp.where((j_idx == 0) | (j_idx == cur) | (j_idx == cur - 1), FORCED_SCORE, imp)
        rank = jnp.zeros((nb, TQ), jnp.int32)
        for i in range(nb):
            ri = imp[i:i + 1, :]
            ahead = (ri > imp) | ((ri == imp) & (i < j_idx))
            rank = rank + jnp.where(ahead, 1, 0)
        blk_bias_t = jnp.where(rank < min(SLC_TOPK, nb), 0.0, MASKED)
        blk_bias_t = jnp.concatenate([blk_bias_t, jnp.zeros((LANES - nb, TQ), F32)], axis=0)
        blk_bias = jnp.transpose(blk_bias_t).astype(BF16)
        q_aug = jnp.concatenate([jnp.concatenate([h, blk_bias], axis=1) for h in qr_heads], axis=0)
        return dict(t_col=t_col, o_cmp=o_cmp, q_aug=q_aug, qr_heads=qr_heads)

    step = NSA_SUB * TQ
    n_pair = R // 2

    def step_body(v):
        subs = [front(v, u) for u in range(NSA_SUB)]
        n_keys = (v + 1) * step
        start = max(v * step - WINDOW, 0)
        span = n_keys - start
        diff = (v * step + lax.broadcasted_iota(jnp.int32, (step, 1), 0)) - (
            start + lax.broadcasted_iota(jnp.int32, (1, span), 1))
        band = jnp.where((diff >= 0) & (diff < WINDOW), 0.0, MASKED)

        def win_scores(c):
            q_pair = jnp.concatenate(
                [subs[u]["qr_heads"][2 * c + rr] for rr in range(2) for u in range(NSA_SUB)], axis=0)
            s_w = _nt_dot(q_pair, kwr_s[start:start + span, :])
            for rr in range(2):
                sw_s[c][rr * step:(rr + 1) * step, 0:span] = s_w[rr * step:(rr + 1) * step] + band

        def win_softmax(c):
            m_w = jnp.max(sw_s[c][:, 0:span], axis=-1, keepdims=True)
            ew_s[c][:, 0:span] = jnp.exp2(sw_s[c][:, 0:span] - m_w).astype(BF16)

        def win_values(c):
            acc_w = _dot(ew_s[c][:, 0:span], vwa_s[start:start + span, :])
            return acc_w[:, 0:HD] * (1.0 / acc_w[:, HD:])

        def sel_scores(u):
            q_aug, t_col = subs[u]["q_aug"], subs[u]["t_col"]
            for k0 in range(0, n_keys, TK):
                w = min(TK, n_keys - k0)
                s = _nt_dot(q_aug, ksa_s[k0:k0 + w, :])
                if k0 + w == n_keys:
                    kpos = k0 + lax.broadcasted_iota(jnp.int32, (1, w), 1)
                    s = jnp.where(kpos <= t_col, s, MASKED)
                ss_s[u][:, k0:k0 + w] = s

        def sel_softmax(u):
            m = jnp.max(ss_s[u][:, 0:n_keys], axis=-1, keepdims=True)
            es_s[u][:, 0:n_keys] = jnp.exp2(ss_s[u][:, 0:n_keys] - m).astype(BF16)

        def sel_values(u):
            acc = _dot(es_s[u][:, 0:n_keys], vsa_s[0:n_keys, :])
            return acc[:, 0:HD] * (1.0 / acc[:, HD:])

        assert NSA_SUB == 2 and n_pair == 2
        win_scores(0)
        win_scores(1)
        sel_scores(0)
        win_softmax(0)
        sel_scores(1)
        win_softmax(1)
        o_win = [win_values(0)]
        sel_softmax(0)
        o_win.append(win_values(1))
        sel_softmax(1)
        o_slc = [sel_values(0), sel_values(1)]

        misc = misc_ref[...]
        ng = 3 * R
        logits = jnp.where(g == 0, misc[:, MISC_GATE_LANE:MISC_GATE_LANE + ng],
                           misc[:, MISC_GATE_LANE + ng:MISC_GATE_LANE + 2 * ng])
        gate = jax.nn.sigmoid(logits)
        for u in range(NSA_SUB):
            urows = slice(u * TQ, (u + 1) * TQ)
            for r in range(R):
                rs = slice(r * TQ, (r + 1) * TQ)
                ws = slice((r % 2) * step + u * TQ, (r % 2) * step + (u + 1) * TQ)
                o = (gate[urows, 3 * r:3 * r + 1] * subs[u]["o_cmp"][rs]
                     + gate[urows, 3 * r + 1:3 * r + 2] * o_slc[u][rs]
                     + gate[urows, 3 * r + 2:3 * r + 3] * o_win[r // 2][ws])
                o_ref[urows, r * HD:(r + 1) * HD] = o.astype(o_ref.dtype)

    for v in range(S // step):
        pl.when(qi == v)(functools.partial(step_body, v))


BF16_ROWS = 16


def _side_cast_plan(weights, n_steps):
    plan = []
    for w in weights:
        n_rows = w.shape[0]
        if n_rows % (n_steps * BF16_ROWS) == 0:
            plan.append((n_rows // n_steps, None))
        else:
            assert n_rows % LANES == 0 and n_rows // LANES <= n_steps
            plan.append((LANES, n_rows // LANES))
    return plan


def _nsa(u3, cmp_kv, kv_ops, cos_f, sin_s, ovt, side_weights):
    B, S, _ = u3.shape
    G, R, HD = NSA_KV_GROUPS, NSA_HPG, NSA_HD
    rows = R * NSA_TQ
    step = NSA_SUB * NSA_TQ
    assert NSA_TK % step == 0
    n_q = S // step
    plan = _side_cast_plan(side_weights, B * G * n_q)

    def side_spec(w, slab_rows, n_slabs):
        def index(b, g, i):
            s = (b * G + g) * n_q + i
            return (s if n_slabs is None else jnp.minimum(s, n_slabs - 1), 0)
        return pl.BlockSpec((slab_rows, w.shape[1]), index)

    side_specs = [side_spec(w, *p) for w, p in zip(side_weights, plan)]
    full = lambda a: pl.BlockSpec((None, S, a.shape[2] // G), lambda b, g, i: (b, 0, g))
    table = pl.BlockSpec((S, LANES), lambda b, g, i: (0, 0))
    kern = functools.partial(_nsa_kernel, len(side_weights), tuple(p[1] for p in plan))
    return pl.pallas_call(
        kern,
        grid=(B, G, n_q),
        in_specs=[
            pl.BlockSpec((None, step, R * HD), lambda b, g, i: (b, i, g)),
            pl.BlockSpec((None, step, LANES), lambda b, g, i: (b, i, CB_MISC - CB_NQ)),
            pl.BlockSpec((None, None, None, N_CMP_PAD, HD), lambda b, g, i: (0, b, g, 0, 0)),
            pl.BlockSpec((None, None, None, N_CMP_PAD, HD), lambda b, g, i: (1, b, g, 0, 0)),
            full(kv_ops[0]), full(kv_ops[1]), full(kv_ops[2]), full(kv_ops[3]),
            table, table,
            pl.BlockSpec((S // SLC_BLOCK, N_CMP_PAD), lambda b, g, i: (0, 0)),
        ] + side_specs,
        out_specs=[pl.BlockSpec((None, step, R * HD), lambda b, g, i: (b, i, g))] + side_specs,
        out_shape=[jax.ShapeDtypeStruct((B, S, NSA_WIDTH), BF16)]
        + [jax.ShapeDtypeStruct(w.shape, BF16) for w in side_weights],
        scratch_shapes=[pltpu.VMEM((rows, S), F32)] * NSA_SUB + [pltpu.VMEM((rows, S), BF16)] * NSA_SUB
        + [pltpu.VMEM((2 * step, WINDOW + step), F32)] * (R // 2)
        + [pltpu.VMEM((2 * step, WINDOW + step), BF16)] * (R // 2),
        compiler_params=pltpu.CompilerParams(
            dimension_semantics=("arbitrary", "arbitrary", "arbitrary"), vmem_limit_bytes=VMEM_LIMIT),
        name="nsa",
    )(u3, u3, cmp_kv, cmp_kv, *kv_ops, cos_f, sin_s, ovt, *side_weights)


def _out_proj_kernel(og_ref, on_ref, x_ref, w_ref, g_ref, b_ref, h_ref, hb_ref):
    half = og_ref.shape[0] // 2
    halves = [slice(0, half), slice(half, 2 * half)]
    mixes = [_dot(og_ref[rows, :], w_ref[0:GLA_WIDTH, :]) + _dot(on_ref[rows, :], w_ref[GLA_WIDTH:, :])
             for rows in halves]
    for rows, mix in zip(halves, mixes):
        h = _layer_norm(DEEPNORM_ALPHA * x_ref[rows, :] + mix, g_ref[...], b_ref[...])
        h_ref[rows, :] = h
        hb_ref[rows, :] = h.astype(BF16)


def _out_proj(o_gla, o_nsa, x2, w_bf, g, b):
    T, D = x2.shape
    tile = lambda w: pl.BlockSpec((OUT_TM, w), lambda i: (i, 0))
    const = lambda s: pl.BlockSpec(s, lambda i: (0, 0))
    return pl.pallas_call(
        _out_proj_kernel,
        grid=(T // OUT_TM,),
        in_specs=[tile(GLA_WIDTH), tile(NSA_WIDTH), tile(D), const(w_bf.shape), const((1, D)), const((1, D))],
        out_specs=[tile(D), tile(D)],
        out_shape=[jax.ShapeDtypeStruct((T, D), F32), jax.ShapeDtypeStruct((T, D), BF16)],
        compiler_params=pltpu.CompilerParams(
            dimension_semantics=("parallel",), vmem_limit_bytes=VMEM_LIMIT),
        name="out_proj",
    )(o_gla, o_nsa, x2, w_bf, g, b)


def _ffn_kernel(hb_ref, h_ref, w1_ref, w3_ref, w2_ref, g_ref, b_ref, o_ref, acc_s):
    f = pl.program_id(1)

    @pl.when(f == 0)
    def _():
        acc_s[...] = jnp.zeros(acc_s.shape, F32)

    hb = hb_ref[...]
    a = _dot(hb, w1_ref[...])
    c = _dot(hb, w3_ref[...])
    acc_s[...] += _dot((a * jax.nn.sigmoid(a) * c).astype(BF16), w2_ref[...])

    @pl.when(f == pl.num_programs(1) - 1)
    def _():
        o_ref[...] = _layer_norm(DEEPNORM_ALPHA * h_ref[...] + acc_s[...], g_ref[...], b_ref[...])


def _ffn(h_bf, h, w1, w3, w2, g, b):
    T, D = h.shape
    F = w1.shape[1]
    return pl.pallas_call(
        _ffn_kernel,
        grid=(T // FFN_TM, F // FFN_TF),
        in_specs=[
            pl.BlockSpec((FFN_TM, D), lambda i, f: (i, 0)),
            pl.BlockSpec((FFN_TM, D), lambda i, f: (i, 0)),
            pl.BlockSpec((D, FFN_TF), lambda i, f: (0, f)),
            pl.BlockSpec((D, FFN_TF), lambda i, f: (0, f)),
            pl.BlockSpec((FFN_TF, D), lambda i, f: (f, 0)),
            pl.BlockSpec((1, D), lambda i, f: (0, 0)),
            pl.BlockSpec((1, D), lambda i, f: (0, 0)),
        ],
        out_specs=pl.BlockSpec((FFN_TM, D), lambda i, f: (i, 0)),
        out_shape=jax.ShapeDtypeStruct((T, D), F32),
        scratch_shapes=[pltpu.VMEM((FFN_TM, D), F32)],
        compiler_params=pltpu.CompilerParams(
            dimension_semantics=("parallel", "arbitrary"), vmem_limit_bytes=VMEM_LIMIT),
        name="ffn",
    )(h_bf, h, w1, w3, w2, g, b)


def _rope_tables(S):
    half = ROPE_DIM // 2
    pos = np.arange(S, dtype=np.float64)
    inv = np.power(ROPE_THETA, -np.arange(0, ROPE_DIM, 2, dtype=np.float64) / ROPE_DIM)
    ang = pos[:, None] * inv[None, :]
    cos_f = np.ones((S, LANES), np.float32)
    sin_s = np.zeros((S, LANES), np.float32)
    for lane0, sign in ((0, -1.0), (ROPE_PAIR_LANE, 1.0)):
        cos_f[:, lane0:lane0 + half] = np.cos(ang)
        sin_s[:, lane0:lane0 + half] = sign * np.sin(ang)
    return jnp.asarray(cos_f), jnp.asarray(sin_s)


def _selection_constants(S):
    n_cmp = (S - CMP_BLOCK) // CMP_STRIDE + 1
    nb = S // SLC_BLOCK
    c_start = np.arange(n_cmp) * CMP_STRIDE
    b_start = np.arange(nb) * SLC_BLOCK
    overlap = ((c_start[:, None] < b_start[None, :] + SLC_BLOCK) &
               (b_start[None, :] < c_start[:, None] + CMP_BLOCK)).astype(np.float32)
    ovt = np.zeros((nb, N_CMP_PAD), np.float32)
    ovt[:, :n_cmp] = overlap.T
    blk_onehot = ((np.arange(S) // SLC_BLOCK)[:, None] == np.arange(LANES)[None, :]).astype(np.float32)
    return jnp.asarray(ovt, BF16), jnp.asarray(blk_onehot, BF16)


def kernel(x, w_in, gla_gate_w2, gla_gate_b2, gla_norm_w, cmp_k_pos, cmp_k_w1, cmp_k_w2,
           cmp_v_pos, cmp_v_w1, cmp_v_w2, w_out, ln1_g, ln1_b, ffn_w1, ffn_w3, ffn_w2, ln2_g, ln2_b):
    B, S, D = x.shape
    T = B * S
    cos_f, sin_s = _rope_tables(S)
    ovt, blk_onehot = _selection_constants(S)

    x2 = x.reshape(T, D)
    for l in range(DEPTH):
        wt_gla, wt_nsa = _regroup_w_in(w_in[l].T)
        ua3 = _in_proj(x2, wt_gla).reshape(B, S, -1)
        u_nsa, *kv_ops = _in_proj_nsa(x2, wt_nsa, S, cos_f, sin_s, blk_onehot)
        u3 = u_nsa.reshape(B, S, -1)
        kv_ops = [a.reshape(B, S, -1) for a in kv_ops]
        o_gla = _gla(ua3, u3, gla_gate_w2[l], gla_gate_b2[l][None, :], gla_norm_w[l][None, :])
        cmp_kv = _compress(u3, (cmp_k_pos[l], cmp_k_w1[l], cmp_k_w2[l]), (cmp_v_pos[l], cmp_v_w1[l], cmp_v_w2[l]))
        o_nsa, w_out_bf, w1_bf, w3_bf, w2_bf = _nsa(u3, cmp_kv, kv_ops, cos_f, sin_s, ovt,
                                                    [w_out[l], ffn_w1[l], ffn_w3[l], ffn_w2[l]])
        h, h_bf = _out_proj(o_gla.reshape(T, GLA_WIDTH), o_nsa.reshape(T, NSA_WIDTH), x2,
                            w_out_bf, ln1_g[l][None, :], ln1_b[l][None, :])
        x2 = _ffn(h_bf, h, w1_bf, w3_bf, w2_bf, ln2_g[l][None, :], ln2_b[l][None, :])
    return x2.reshape(B, S, D)
```

```python
import functools

import numpy as np
import jax
import jax.numpy as jnp
from jax import lax
from jax.experimental import pallas as pl
from jax.experimental.pallas import tpu as pltpu

F32 = jnp.float32
BF16 = jnp.bfloat16

D_MODEL = 2048
DEPTH = 1
GLA_HEADS = 4
GLA_DK = 128
GLA_DV = 256
GLA_GATE_RANK = 16
GLA_GATE_NORM = 16.0
GLA_CHUNK = 64
NSA_HEADS = 8
NSA_HD = 128
NSA_KV_GROUPS = 2
NSA_HPG = NSA_HEADS // NSA_KV_GROUPS
CMP_BLOCK = 32
CMP_STRIDE = 16
CMP_HIDDEN = 2 * NSA_HD
SLC_BLOCK = 64
SLC_TOPK = 16
WINDOW = 512
ROPE_THETA = 500000.0
ROPE_DIM = NSA_HD // 4
FFN_HIDDEN = 5632
DEEPNORM_ALPHA = (2.0 * DEPTH) ** 0.25
LN_EPS = 1e-5
FORCED_SCORE = 1e4
INVALID_SCORE = -1e4

GLA_WIDTH = GLA_HEADS * GLA_DV
NSA_WIDTH = NSA_HEADS * NSA_HD

LANES = 128
VMEM_LIMIT = 56 * 1024 * 1024
CB_GQ = 0
CB_GK = 4
CB_GV = 8
CB_GO = 16
CB_NQ = 24
CB_KC = 32
CB_KS = 36
CB_VS = 38
CB_KW = 40
CB_VW = 42
CB_MISC = 44
MISC_GATE_LANE = GLA_GATE_RANK
U_WIDTH = 45 * LANES

REGROUP_TC = 256
IN_TM = 512
OUT_TM = 512
FFN_TM, FFN_TF = 512, 512
NSA_TQ = 128
NSA_SUB = 2
NSA_TK = 512
WIN_SPAN = WINDOW + NSA_TQ
MASKED = -1e30
LOG2E = 1.4426950408889634


def _nt_dot(a, b):
    return lax.dot_general(a, b, (((1,), (1,)), ((), ())), preferred_element_type=F32)


def _tn_dot(a, b):
    return lax.dot_general(a, b, (((0,), (0,)), ((), ())), preferred_element_type=F32)


def _dot(a, b):
    return jnp.dot(a, b, preferred_element_type=F32)


def _layer_norm(z, g, b):
    mu = jnp.mean(z, axis=-1, keepdims=True)
    zc = z - mu
    var = jnp.mean(zc * zc, axis=-1, keepdims=True)
    return zc * lax.rsqrt(var + LN_EPS) * g + b


def _masked_softmax(s, mask):
    sm = jnp.where(mask, s, -jnp.inf)
    m = jnp.max(sm, axis=-1, keepdims=True)
    m = jnp.where(m == -jnp.inf, 0.0, m)
    e = jnp.where(mask, jnp.exp(s - m), 0.0)
    den = jnp.sum(e, axis=-1, keepdims=True)
    return e * (1.0 / jnp.where(den > 0, den, 1.0))


def _regroup_kernel(w_ref, oa_ref, ob_ref):
    cols = w_ref.shape[1]
    o_glr = CB_NQ * LANES
    o_nq = o_glr + GLA_GATE_RANK
    o_gate = o_nq + (CB_MISC - CB_NQ) * LANES
    n_gate = NSA_HEADS * 3
    oa_ref[...] = w_ref[0:o_glr, :].astype(BF16)
    roped = (list(range(CB_NQ, CB_NQ + NSA_HEADS)) + list(range(CB_KS, CB_KS + NSA_KV_GROUPS))
             + list(range(CB_KW, CB_KW + NSA_KV_GROUPS)))
    for blk in range(CB_NQ, CB_MISC):
        dst, src = (blk - CB_NQ) * LANES, blk * LANES + GLA_GATE_RANK
        pieces = _head_dim_order() if blk in roped else [(0, NSA_HD)]
        at = 0
        for lo, hi in pieces:
            ob_ref[dst + at:dst + at + hi - lo, :] = w_ref[src + lo:src + hi, :].astype(BF16)
            at += hi - lo
    misc = jnp.concatenate([w_ref[o_glr:o_nq, :], w_ref[o_gate:o_gate + n_gate, :],
                            jnp.zeros((LANES - GLA_GATE_RANK - n_gate, cols), F32)], axis=0)
    ob_ref[(CB_MISC - CB_NQ) * LANES:, :] = misc.astype(BF16)


def _regroup_w_in(wt):
    n_in, D = wt.shape
    na, nb = CB_NQ * LANES, U_WIDTH - CB_NQ * LANES
    return pl.pallas_call(
        _regroup_kernel,
        grid=(D // REGROUP_TC,),
        in_specs=[pl.BlockSpec((n_in, REGROUP_TC), lambda i: (0, i))],
        out_specs=[pl.BlockSpec((na, REGROUP_TC), lambda i: (0, i)), pl.BlockSpec((nb, REGROUP_TC), lambda i: (0, i))],
        out_shape=[jax.ShapeDtypeStruct((na, D), BF16), jax.ShapeDtypeStruct((nb, D), BF16)],
        compiler_params=pltpu.CompilerParams(
            dimension_semantics=("parallel",), vmem_limit_bytes=VMEM_LIMIT),
        name="regroup_w_in",
    )(wt)


def _in_proj_kernel(x_ref, w_ref, o_ref):
    o_ref[...] = _nt_dot(x_ref[...].astype(BF16), w_ref[...])


def _in_proj(x2, wt_bf):
    T, D = x2.shape
    N = wt_bf.shape[0]
    return pl.pallas_call(
        _in_proj_kernel,
        grid=(T // IN_TM,),
        in_specs=[pl.BlockSpec((IN_TM, D), lambda i: (i, 0)),
                  pl.BlockSpec((N, D), lambda i: (0, 0))],
        out_specs=pl.BlockSpec((IN_TM, N), lambda i: (i, 0)),
        out_shape=jax.ShapeDtypeStruct((T, N), F32),
        compiler_params=pltpu.CompilerParams(
            dimension_semantics=("parallel",), vmem_limit_bytes=VMEM_LIMIT),
        name="in_proj",
    )(x2, wt_bf)


def _in_proj_nsa_kernel(x_ref, w_ref, cos_ref, sin_ref, blk_ref,
                        u_ref, qc_ref, qr_ref, ksa_ref, vsa_ref, kwr_ref, vwa_ref):
    res = _nt_dot(x_ref[...].astype(BF16), w_ref[...])
    u_ref[...] = res
    cos, sin = cos_ref[...], sin_ref[...]
    scale = NSA_HD ** -0.5
    for h in range(NSA_HEADS):
        cols = slice(h * NSA_HD, (h + 1) * NSA_HD)
        qc_ref[:, cols] = (res[:, cols] * scale).astype(BF16)
        qr_ref[:, cols] = _rope(res[:, cols] * (scale * LOG2E), cos, sin).astype(BF16)
    ones = jnp.ones((res.shape[0], LANES), BF16)
    col = lambda cb, g: slice((cb - CB_NQ + g) * LANES, (cb - CB_NQ + g + 1) * LANES)
    for g in range(NSA_KV_GROUPS):
        lo, hi = slice(2 * g * LANES, (2 * g + 1) * LANES), slice((2 * g + 1) * LANES, (2 * g + 2) * LANES)
        ksa_ref[:, lo] = _rope(res[:, col(CB_KS, g)], cos, sin).astype(BF16)
        ksa_ref[:, hi] = blk_ref[...]
        vsa_ref[:, lo] = res[:, col(CB_VS, g)].astype(BF16)
        vsa_ref[:, hi] = ones
        kwr_ref[:, g * LANES:(g + 1) * LANES] = _rope(res[:, col(CB_KW, g)], cos, sin).astype(BF16)
        vwa_ref[:, lo] = res[:, col(CB_VW, g)].astype(BF16)
        vwa_ref[:, hi] = ones


def _in_proj_nsa(x2, wt_bf, S, cos_f, sin_s, blk_onehot):
    T, D = x2.shape
    N = wt_bf.shape[0]
    G = NSA_KV_GROUPS
    per_seq = S // IN_TM
    table = pl.BlockSpec((IN_TM, LANES), lambda i: (i % per_seq, 0))
    rows = lambda w: pl.BlockSpec((IN_TM, w), lambda i: (i, 0))
    widths = [N, NSA_WIDTH, NSA_WIDTH, 2 * G * LANES, 2 * G * LANES, G * LANES, 2 * G * LANES]
    return pl.pallas_call(
        _in_proj_nsa_kernel,
        grid=(T // IN_TM,),
        in_specs=[rows(D), pl.BlockSpec((N, D), lambda i: (0, 0)), table, table, table],
        out_specs=[rows(w) for w in widths],
        out_shape=[jax.ShapeDtypeStruct((T, N), F32)] + [jax.ShapeDtypeStruct((T, w), BF16) for w in widths[1:]],
        compiler_params=pltpu.CompilerParams(
            dimension_semantics=("parallel",), vmem_limit_bytes=VMEM_LIMIT),
        name="in_proj_nsa",
    )(x2, wt_bf, cos_f, sin_s, blk_onehot)


GLA_ROWBLK = 256
GLA_TS = 512


def _split3(a):
    hi = a.astype(BF16)
    r1 = a - hi.astype(F32)
    mid = r1.astype(BF16)
    lo = (r1 - mid.astype(F32)).astype(BF16)
    return hi, mid, lo


def _log_sigmoid(z):
    return jnp.minimum(z, 0.0) - jnp.log(1.0 + jnp.exp(-jnp.abs(z)))


def _gla_kernel(q_ref, k_ref, v_ref, go_ref, misc_ref, w2_ref, b2_ref, nw_ref, o_ref,
                qd_s, ki_s, ks_s, dec_s, state_s, sbf_s):
    TS = q_ref.shape[0]
    H, C, DK, DV = GLA_HEADS, GLA_CHUNK, GLA_DK, GLA_DV
    scale = DK ** -0.5

    @pl.when(pl.program_id(1) == 0)
    def _():
        state_s[...] = jnp.zeros(state_s.shape, F32)

    r = lax.broadcasted_iota(jnp.int32, (GLA_ROWBLK, GLA_ROWBLK), 0)
    c = lax.broadcasted_iota(jnp.int32, (GLA_ROWBLK, GLA_ROWBLK), 1)
    chunk_causal = ((r // C) == (c // C)) & (c <= r)
    cum_m = jnp.where(chunk_causal, 1.0, 0.0).astype(BF16)

    w2 = w2_ref[...].astype(BF16)
    b2 = b2_ref[...]
    for rb in range(TS // GLA_ROWBLK):
        rows = slice(rb * GLA_ROWBLK, (rb + 1) * GLA_ROWBLK)
        glr = misc_ref[rows, 0:GLA_GATE_RANK].astype(BF16)
        gk = _log_sigmoid(_dot(glr, w2) + b2) * (1.0 / GLA_GATE_NORM)
        hi, mid, lo = _split3(gk)
        bc_all = _dot(cum_m, hi) + _dot(cum_m, mid) + _dot(cum_m, lo)
        bl_all = jnp.concatenate(
            [jnp.broadcast_to(bc_all[j * C + C - 1:j * C + C, :], (C, H * DK)) for j in range(GLA_ROWBLK // C)],
            axis=0)
        for h in range(H):
            hk = slice(h * DK, (h + 1) * DK)
            bc = bc_all[:, hk]
            bl = bl_all[:, hk]
            q = q_ref[rows, hk] * scale
            k = k_ref[rows, hk]
            qd_s[h, rows, :] = (q * jnp.exp(bc)).astype(BF16)
            ki_s[h, rows, :] = (k * jnp.exp(-bc)).astype(BF16)
            ks_s[h, rows, :] = (k * jnp.exp(bl - bc)).astype(BF16)
            dec_s[h, rows, :] = jnp.exp(bl)

    n_chunk = TS // C
    nw = nw_ref[...]
    for h in range(H):
        hv = slice(h * DV, (h + 1) * DV)
        kv = [_tn_dot(ks_s[h, c * C:(c + 1) * C, :], v_ref[c * C:(c + 1) * C, hv].astype(BF16))
              for c in range(n_chunk)]
        dec_rows = jnp.concatenate([dec_s[h, c * C:c * C + 1, :] for c in range(n_chunk)]
                                   + [jnp.zeros((LANES - n_chunk, DK), F32)], axis=0)
        dec_cols = jnp.transpose(dec_rows)
        state = state_s[h]
        for c in range(n_chunk):
            sbf_s[h, c] = state.astype(BF16)
            state = state * dec_cols[:, c:c + 1] + kv[c]
        state_s[h] = state

    for h in range(H):
        hv = slice(h * DV, (h + 1) * DV)
        for rb in range(TS // GLA_ROWBLK):
            rows = slice(rb * GLA_ROWBLK, (rb + 1) * GLA_ROWBLK)
            qd = qd_s[h, rows, :]
            attn = jnp.where(chunk_causal, _nt_dot(qd, ki_s[h, rows, :]), 0.0).astype(BF16)
            o = _dot(attn, v_ref[rows, hv].astype(BF16))
            inter = [_dot(qd_s[h, c * C:(c + 1) * C, :], sbf_s[h, c])
                     for c in range(rb * GLA_ROWBLK // C, (rb + 1) * GLA_ROWBLK // C)]
            o = o + jnp.concatenate(inter, axis=0)
            ms = jnp.mean(o * o, axis=-1, keepdims=True)
            o = o * lax.rsqrt(ms + LN_EPS) * nw
            go = go_ref[rows, hv]
            o_ref[rows, hv] = (o * (go * jax.nn.sigmoid(go))).astype(o_ref.dtype)


def _gla(ua3, ub3, gate_w2, gate_b2, norm_w):
    B, S, _ = ua3.shape
    H, DK, DV = GLA_HEADS, GLA_DK, GLA_DV
    qk_w, v_w = H * DK, H * DV
    tok = lambda w, off: pl.BlockSpec((None, GLA_TS, w), lambda b, s: (b, s, off * LANES // w))
    const = lambda shape: pl.BlockSpec(shape, lambda b, s: (0, 0))
    return pl.pallas_call(
        _gla_kernel,
        grid=(B, S // GLA_TS),
        in_specs=[tok(qk_w, CB_GQ), tok(qk_w, CB_GK), tok(v_w, CB_GV), tok(v_w, CB_GO), tok(LANES, CB_MISC - CB_NQ),
                  const((GLA_GATE_RANK, qk_w)), const((1, qk_w)), const((1, DV))],
        out_specs=pl.BlockSpec((None, GLA_TS, v_w), lambda b, s: (b, s, 0)),
        out_shape=jax.ShapeDtypeStruct((B, S, GLA_WIDTH), BF16),
        scratch_shapes=[pltpu.VMEM((H, GLA_TS, DK), BF16), pltpu.VMEM((H, GLA_TS, DK), BF16),
                        pltpu.VMEM((H, GLA_TS, DK), BF16), pltpu.VMEM((H, GLA_TS, DK), F32),
                        pltpu.VMEM((H, DK, DV), F32), pltpu.VMEM((H, GLA_TS // GLA_CHUNK, DK, DV), BF16)],
        compiler_params=pltpu.CompilerParams(
            dimension_semantics=("parallel", "arbitrary"), vmem_limit_bytes=VMEM_LIMIT),
        name="gla",
    )(ua3, ua3, ua3, ua3, ub3, gate_w2, gate_b2, norm_w)


N_CMP_PAD = 128


def _gelu_tanh(x):
    return x * (0.5 * (1.0 + jnp.tanh(0.7978845608028654 * (x + 0.044715 * (x * x * x)))))


def _compress_kernel(k_ref, v_ref, kpos_ref, vpos_ref, kw1_ref, vw1_ref, kw2_ref, vw2_ref, o_ref):
    half = CMP_BLOCK // 2
    chains = ((k_ref, kpos_ref, kw1_ref, kw2_ref, _head_dim_order()),
              (v_ref, vpos_ref, vw1_ref, vw2_ref, [(0, NSA_HD)]))
    for kv, (src_ref, pos_ref, w1_ref, w2_ref, out_order) in enumerate(chains):
        p0 = jnp.zeros((N_CMP_PAD, CMP_HIDDEN), F32)
        p1 = jnp.zeros((N_CMP_PAD, CMP_HIDDEN), F32)
        for l in range(half):
            x = src_ref[pl.ds(l, N_CMP_PAD, stride=CMP_STRIDE), :]
            a0 = (x + pos_ref[l:l + 1, :]).astype(BF16)
            a1 = (x + pos_ref[half + l:half + l + 1, :]).astype(BF16)
            p0 = p0 + _dot(a0, w1_ref[l * NSA_HD:(l + 1) * NSA_HD, :].astype(BF16))
            p1 = p1 + _dot(a1, w1_ref[(half + l) * NSA_HD:(half + l + 1) * NSA_HD, :].astype(BF16))
        pre = p0 + pltpu.roll(p1, N_CMP_PAD - 1, 0)
        h = _gelu_tanh(pre).astype(BF16)
        w2 = jnp.concatenate([w2_ref[:, lo:hi] for lo, hi in out_order], axis=1).astype(BF16)
        out = _dot(h, w2)
        row = lax.broadcasted_iota(jnp.int32, out.shape, 0)
        o_ref[kv] = jnp.where(row < N_CMP_PAD - 1, out, 0.0).astype(o_ref.dtype)


def _compress(u3, k_params, v_params):
    B, S, _ = u3.shape
    G = NSA_KV_GROUPS
    src = lambda kv: pl.BlockSpec((None, S, NSA_HD), lambda b, g: (b, 0, CB_KC - CB_NQ + 2 * kv + g))
    whole = lambda a: pl.BlockSpec(a.shape, lambda b, g: (0,) * a.ndim)
    params = [k_params[0], v_params[0], k_params[1], v_params[1], k_params[2], v_params[2]]
    return pl.pallas_call(
        _compress_kernel,
        grid=(B, G),
        in_specs=[src(0), src(1)] + [whole(a) for a in params],
        out_specs=pl.BlockSpec((2, None, None, N_CMP_PAD, NSA_HD), lambda b, g: (0, b, g, 0, 0)),
        out_shape=jax.ShapeDtypeStruct((2, B, G, N_CMP_PAD, NSA_HD), BF16),
        compiler_params=pltpu.CompilerParams(
            dimension_semantics=("parallel", "parallel"), vmem_limit_bytes=VMEM_LIMIT),
        name="compress",
    )(u3, u3, *params)


ROPE_PAIR_LANE = LANES // 2


def _head_dim_order():
    half = ROPE_DIM // 2
    return [(0, half), (ROPE_DIM, ROPE_PAIR_LANE + half), (half, ROPE_DIM), (ROPE_PAIR_LANE + half, NSA_HD)]


def _rope(x, cos, sin_signed):
    return x * cos + pltpu.roll(x, ROPE_PAIR_LANE, 1) * sin_signed


def _nsa_kernel(n_side, side_slabs, *refs):
    n_in = 10
    (qc_ref, qr_ref, misc_ref, kc_ref, vc_ref, ksa_s, vsa_s, kwr_s, vwa_s, ovt_ref) = refs[0:n_in]
    side_in = refs[n_in:n_in + n_side]
    o_ref = refs[n_in + n_side]
    side_out = refs[n_in + 1 + n_side:n_in + 1 + 2 * n_side]
    chain_scratch = refs[n_in + 1 + 2 * n_side:]
    S = ksa_s.shape[0]
    R, TQ, HD = NSA_HPG, NSA_TQ, NSA_HD
    ss_s, es_s = chain_scratch[0:NSA_SUB], chain_scratch[NSA_SUB:2 * NSA_SUB]
    sw_s, ew_s = chain_scratch[2 * NSA_SUB:2 * NSA_SUB + R // 2], chain_scratch[2 * NSA_SUB + R // 2:]
    g = pl.program_id(1)
    qi = pl.program_id(2)

    step_idx = (pl.program_id(0) * pl.num_programs(1) + g) * pl.num_programs(2) + qi
    for w_in_ref, w_out_ref, n_slabs in zip(side_in, side_out, side_slabs):
        def cast(w_in_ref=w_in_ref, w_out_ref=w_out_ref):
            w_out_ref[...] = w_in_ref[...].astype(BF16)
        if n_slabs is None:
            cast()
        else:
            pl.when(step_idx < n_slabs)(cast)

    nb = S // SLC_BLOCK
    TK = NSA_TK
    ovt = ovt_ref[...]
    kc, vc = kc_ref[...], vc_ref[...]

    def front(v, u):
        tq0 = (v * NSA_SUB + u) * TQ
        urows = slice(u * TQ, (u + 1) * TQ)
        qc = jnp.concatenate([qc_ref[urows, r * HD:(r + 1) * HD] for r in range(R)], axis=0)
        qr_heads = [qr_ref[urows, r * HD:(r + 1) * HD] for r in range(R)]
        t_tok = tq0 + lax.broadcasted_iota(jnp.int32, (TQ, 1), 0)
        t_col = jnp.concatenate([t_tok] * R, axis=0)

        s_c = _nt_dot(qc, kc)
        n_idx = lax.broadcasted_iota(jnp.int32, (1, N_CMP_PAD), 1)
        p_c = _masked_softmax(s_c, (n_idx * CMP_STRIDE + (CMP_BLOCK - 1)) <= t_col)
        p_cb = p_c.astype(BF16)
        o_cmp = _dot(p_cb, vc)

        imp = _nt_dot(ovt, p_cb[0:TQ])
        for r in range(1, R):
            imp = imp + _nt_dot(ovt, p_cb[r * TQ:(r + 1) * TQ])
        j_idx = lax.broadcasted_iota(jnp.int32, (nb, TQ), 0)
        cur = (tq0 + lax.broadcasted_iota(jnp.int32, (nb, TQ), 1)) // SLC_BLOCK
        imp = jnp.where(j_idx <= cur, imp, INVALID_SCORE)
        imp = jnp.where((j_idx == 0) | (j_idx == cur) | (j_idx == cur - 1), FORCED_SCORE, imp)
        rank = jnp.zeros((nb, TQ), jnp.int32)
        for i in range(nb):
            ri = imp[i:i + 1, :]
            ahead = (ri > imp) | ((ri == imp) & (i < j_idx))
            rank = rank + jnp.where(ahead, 1, 0)
        blk_bias_t = jnp.where(rank < min(SLC_TOPK, nb), 0.0, MASKED)
        blk_bias_t = jnp.concatenate([blk_bias_t, jnp.zeros((LANES - nb, TQ), F32)], axis=0)
        blk_bias = jnp.transpose(blk_bias_t).astype(BF16)
        q_aug = jnp.concatenate([jnp.concatenate([h, blk_bias], axis=1) for h in qr_heads], axis=0)
        return dict(t_col=t_col, o_cmp=o_cmp, q_aug=q_aug, qr_heads=qr_heads)

    step = NSA_SUB * TQ
    n_pair = R // 2

    def step_body(v):
        subs = [front(v, u) for u in range(NSA_SUB)]
        n_keys = (v + 1) * step
        start = max(v * step - WINDOW, 0)
        span = n_keys - start
        diff = (v * step + lax.broadcasted_iota(jnp.int32, (step, 1), 0)) - (
            start + lax.broadcasted_iota(jnp.int32, (1, span), 1))
        band = jnp.where((diff >= 0) & (diff < WINDOW), 0.0, MASKED)

        def win_scores(c):
            q_pair = jnp.concatenate(
                [subs[u]["qr_heads"][2 * c + rr] for rr in range(2) for u in range(NSA_SUB)], axis=0)
            s_w = _nt_dot(q_pair, kwr_s[start:start + span, :])
            for rr in range(2):
                sw_s[c][rr * step:(rr + 1) * step, 0:span] = s_w[rr * step:(rr + 1) * step] + band

        def win_softmax(c):
            m_w = jnp.max(sw_s[c][:, 0:span], axis=-1, keepdims=True)
            ew_s[c][:, 0:span] = jnp.exp2(sw_s[c][:, 0:span] - m_w).astype(BF16)

        def win_values(c):
            acc_w = _dot(ew_s[c][:, 0:span], vwa_s[start:start + span, :])
            return acc_w[:, 0:HD] * (1.0 / acc_w[:, HD:])

        def sel_scores(u):
            q_aug, t_col = subs[u]["q_aug"], subs[u]["t_col"]
            for k0 in range(0, n_keys, TK):
                w = min(TK, n_keys - k0)
                s = _nt_dot(q_aug, ksa_s[k0:k0 + w, :])
                if k0 + w == n_keys:
                    kpos = k0 + lax.broadcasted_iota(jnp.int32, (1, w), 1)
                    s = jnp.where(kpos <= t_col, s, MASKED)
                ss_s[u][:, k0:k0 + w] = s

        def sel_softmax(u):
            m = jnp.max(ss_s[u][:, 0:n_keys], axis=-1, keepdims=True)
            es_s[u][:, 0:n_keys] = jnp.exp2(ss_s[u][:, 0:n_keys] - m).astype(BF16)

        def sel_values(u):
            acc = _dot(es_s[u][:, 0:n_keys], vsa_s[0:n_keys, :])
            return acc[:, 0:HD] * (1.0 / acc[:, HD:])

        assert NSA_SUB == 2 and n_pair == 2
        win_scores(0)
        win_scores(1)
        sel_scores(0)
        win_softmax(0)
        sel_scores(1)
        win_softmax(1)
        o_win = [win_values(0)]
        sel_softmax(0)
        o_win.append(win_values(1))
        sel_softmax(1)
        o_slc = [sel_values(0), sel_values(1)]

        misc = misc_ref[...]
        ng = 3 * R
        logits = jnp.where(g == 0, misc[:, MISC_GATE_LANE:MISC_GATE_LANE + ng],
                           misc[:, MISC_GATE_LANE + ng:MISC_GATE_LANE + 2 * ng])
        gate = jax.nn.sigmoid(logits)
        for u in range(NSA_SUB):
            urows = slice(u * TQ, (u + 1) * TQ)
            for r in range(R):
                rs = slice(r * TQ, (r + 1) * TQ)
                ws = slice((r % 2) * step + u * TQ, (r % 2) * step + (u + 1) * TQ)
                o = (gate[urows, 3 * r:3 * r + 1] * subs[u]["o_cmp"][rs]
                     + gate[urows, 3 * r + 1:3 * r + 2] * o_slc[u][rs]
                     + gate[urows, 3 * r + 2:3 * r + 3] * o_win[r // 2][ws])
                o_ref[urows, r * HD:(r + 1) * HD] = o.astype(o_ref.dtype)

    for v in range(S // step):
        pl.when(qi == v)(functools.partial(step_body, v))


BF16_ROWS = 16


def _side_cast_plan(weights, n_steps):
    plan = []
    for w in weights:
        n_rows = w.shape[0]
        if n_rows % (n_steps * BF16_ROWS) == 0:
            plan.append((n_rows // n_steps, None))
        else:
            assert n_rows % LANES == 0 and n_rows // LANES <= n_steps
            plan.append((LANES, n_rows // LANES))
    return plan


def _nsa(u3, cmp_kv, q_ops, kv_ops, ovt, side_weights):
    B, S, _ = u3.shape
    G, R, HD = NSA_KV_GROUPS, NSA_HPG, NSA_HD
    rows = R * NSA_TQ
    step = NSA_SUB * NSA_TQ
    assert NSA_TK % step == 0
    n_q = S // step
    plan = _side_cast_plan(side_weights, B * G * n_q)

    def side_spec(w, slab_rows, n_slabs):
        def index(b, g, i):
            s = (b * G + g) * n_q + i
            return (s if n_slabs is None else jnp.minimum(s, n_slabs - 1), 0)
        return pl.BlockSpec((slab_rows, w.shape[1]), index)

    side_specs = [side_spec(w, *p) for w, p in zip(side_weights, plan)]
    full = lambda a: pl.BlockSpec((None, S, a.shape[2] // G), lambda b, g, i: (b, 0, g))
    q_spec = pl.BlockSpec((None, step, R * HD), lambda b, g, i: (b, i, g))
    kern = functools.partial(_nsa_kernel, len(side_weights), tuple(p[1] for p in plan))
    return pl.pallas_call(
        kern,
        grid=(B, G, n_q),
        in_specs=[
            q_spec, q_spec,
            pl.BlockSpec((None, step, LANES), lambda b, g, i: (b, i, CB_MISC - CB_NQ)),
            pl.BlockSpec((None, None, None, N_CMP_PAD, HD), lambda b, g, i: (0, b, g, 0, 0)),
            pl.BlockSpec((None, None, None, N_CMP_PAD, HD), lambda b, g, i: (1, b, g, 0, 0)),
            full(kv_ops[0]), full(kv_ops[1]), full(kv_ops[2]), full(kv_ops[3]),
            pl.BlockSpec((S // SLC_BLOCK, N_CMP_PAD), lambda b, g, i: (0, 0)),
        ] + side_specs,
        out_specs=[q_spec] + side_specs,
        out_shape=[jax.ShapeDtypeStruct((B, S, NSA_WIDTH), BF16)]
        + [jax.ShapeDtypeStruct(w.shape, BF16) for w in side_weights],
        scratch_shapes=[pltpu.VMEM((rows, S), F32)] * NSA_SUB + [pltpu.VMEM((rows, S), BF16)] * NSA_SUB
        + [pltpu.VMEM((2 * step, WINDOW + step), F32)] * (R // 2)
        + [pltpu.VMEM((2 * step, WINDOW + step), BF16)] * (R // 2),
        compiler_params=pltpu.CompilerParams(
            dimension_semantics=("arbitrary", "arbitrary", "arbitrary"), vmem_limit_bytes=VMEM_LIMIT),
        name="nsa",
    )(*q_ops, u3, cmp_kv, cmp_kv, *kv_ops, ovt, *side_weights)


def _out_proj_kernel(og_ref, on_ref, x_ref, w_ref, g_ref, b_ref, h_ref, hb_ref):
    half = og_ref.shape[0] // 2
    halves = [slice(0, half), slice(half, 2 * half)]
    mixes = [_dot(og_ref[rows, :], w_ref[0:GLA_WIDTH, :]) + _dot(on_ref[rows, :], w_ref[GLA_WIDTH:, :])
             for rows in halves]
    for rows, mix in zip(halves, mixes):
        h = _layer_norm(DEEPNORM_ALPHA * x_ref[rows, :] + mix, g_ref[...], b_ref[...])
        h_ref[rows, :] = h
        hb_ref[rows, :] = h.astype(BF16)


def _out_proj(o_gla, o_nsa, x2, w_bf, g, b):
    T, D = x2.shape
    tile = lambda w: pl.BlockSpec((OUT_TM, w), lambda i: (i, 0))
    const = lambda s: pl.BlockSpec(s, lambda i: (0, 0))
    return pl.pallas_call(
        _out_proj_kernel,
        grid=(T // OUT_TM,),
        in_specs=[tile(GLA_WIDTH), tile(NSA_WIDTH), tile(D), const(w_bf.shape), const((1, D)), const((1, D))],
        out_specs=[tile(D), tile(D)],
        out_shape=[jax.ShapeDtypeStruct((T, D), F32), jax.ShapeDtypeStruct((T, D), BF16)],
        compiler_params=pltpu.CompilerParams(
            dimension_semantics=("parallel",), vmem_limit_bytes=VMEM_LIMIT),
        name="out_proj",
    )(o_gla, o_nsa, x2, w_bf, g, b)


def _ffn_kernel(hb_ref, h_ref, w1_ref, w3_ref, w2_ref, g_ref, b_ref, o_ref, acc_s):
    f = pl.program_id(1)

    @pl.when(f == 0)
    def _():
        acc_s[...] = jnp.zeros(acc_s.shape, F32)

    hb = hb_ref[...]
    a = _dot(hb, w1_ref[...])
    c = _dot(hb, w3_ref[...])
    acc_s[...] += _dot((a * jax.nn.sigmoid(a) * c).astype(BF16), w2_ref[...])

    @pl.when(f == pl.num_programs(1) - 1)
    def _():
        o_ref[...] = _layer_norm(DEEPNORM_ALPHA * h_ref[...] + acc_s[...], g_ref[...], b_ref[...])


def _ffn(h_bf, h, w1, w3, w2, g, b):
    T, D = h.shape
    F = w1.shape[1]
    return pl.pallas_call(
        _ffn_kernel,
        grid=(T // FFN_TM, F // FFN_TF),
        in_specs=[
            pl.BlockSpec((FFN_TM, D), lambda i, f: (i, 0)),
            pl.BlockSpec((FFN_TM, D), lambda i, f: (i, 0)),
            pl.BlockSpec((D, FFN_TF), lambda i, f: (0, f)),
            pl.BlockSpec((D, FFN_TF), lambda i, f: (0, f)),
            pl.BlockSpec((FFN_TF, D), lambda i, f: (f, 0)),
            pl.BlockSpec((1, D), lambda i, f: (0, 0)),
            pl.BlockSpec((1, D), lambda i, f: (0, 0)),
        ],
        out_specs=pl.BlockSpec((FFN_TM, D), lambda i, f: (i, 0)),
        out_shape=jax.ShapeDtypeStruct((T, D), F32),
        scratch_shapes=[pltpu.VMEM((FFN_TM, D), F32)],
        compiler_params=pltpu.CompilerParams(
            dimension_semantics=("parallel", "arbitrary"), vmem_limit_bytes=VMEM_LIMIT),
        name="ffn",
    )(h_bf, h, w1, w3, w2, g, b)


def _rope_tables(S):
    half = ROPE_DIM // 2
    pos = np.arange(S, dtype=np.float64)
    inv = np.power(ROPE_THETA, -np.arange(0, ROPE_DIM, 2, dtype=np.float64) / ROPE_DIM)
    ang = pos[:, None] * inv[None, :]
    cos_f = np.ones((S, LANES), np.float32)
    sin_s = np.zeros((S, LANES), np.float32)
    for lane0, sign in ((0, -1.0), (ROPE_PAIR_LANE, 1.0)):
        cos_f[:, lane0:lane0 + half] = np.cos(ang)
        sin_s[:, lane0:lane0 + half] = sign * np.sin(ang)
    return jnp.asarray(cos_f), jnp.asarray(sin_s)


def _selection_constants(S):
    n_cmp = (S - CMP_BLOCK) // CMP_STRIDE + 1
    nb = S // SLC_BLOCK
    c_start = np.arange(n_cmp) * CMP_STRIDE
    b_start = np.arange(nb) * SLC_BLOCK
    overlap = ((c_start[:, None] < b_start[None, :] + SLC_BLOCK) &
               (b_start[None, :] < c_start[:, None] + CMP_BLOCK)).astype(np.float32)
    ovt = np.zeros((nb, N_CMP_PAD), np.float32)
    ovt[:, :n_cmp] = overlap.T
    blk_onehot = ((np.arange(S) // SLC_BLOCK)[:, None] == np.arange(LANES)[None, :]).astype(np.float32)
    return jnp.asarray(ovt, BF16), jnp.asarray(blk_onehot, BF16)


def kernel(x, w_in, gla_gate_w2, gla_gate_b2, gla_norm_w, cmp_k_pos, cmp_k_w1, cmp_k_w2,
           cmp_v_pos, cmp_v_w1, cmp_v_w2, w_out, ln1_g, ln1_b, ffn_w1, ffn_w3, ffn_w2, ln2_g, ln2_b):
    B, S, D = x.shape
    T = B * S
    cos_f, sin_s = _rope_tables(S)
    ovt, blk_onehot = _selection_constants(S)

    x2 = x.reshape(T, D)
    for l in range(DEPTH):
        wt_gla, wt_nsa = _regroup_w_in(w_in[l].T)
        ua3 = _in_proj(x2, wt_gla).reshape(B, S, -1)
        u_nsa, *attn_ops = _in_proj_nsa(x2, wt_nsa, S, cos_f, sin_s, blk_onehot)
        u3 = u_nsa.reshape(B, S, -1)
        attn_ops = [a.reshape(B, S, -1) for a in attn_ops]
        q_ops, kv_ops = attn_ops[0:2], attn_ops[2:]
        o_gla = _gla(ua3, u3, gla_gate_w2[l], gla_gate_b2[l][None, :], gla_norm_w[l][None, :])
        cmp_kv = _compress(u3, (cmp_k_pos[l], cmp_k_w1[l], cmp_k_w2[l]), (cmp_v_pos[l], cmp_v_w1[l], cmp_v_w2[l]))
        o_nsa, w_out_bf, w1_bf, w3_bf, w2_bf = _nsa(u3, cmp_kv, q_ops, kv_ops, ovt,
                                                    [w_out[l], ffn_w1[l], ffn_w3[l], ffn_w2[l]])
        h, h_bf = _out_proj(o_gla.reshape(T, GLA_WIDTH), o_nsa.reshape(T, NSA_WIDTH), x2,
                            w_out_bf, ln1_g[l][None, :], ln1_b[l][None, :])
        x2 = _ffn(h_bf, h, w1_bf, w3_bf, w2_bf, ln2_g[l][None, :], ln2_b[l][None, :])
    return x2.reshape(B, S, D)
```

```python
import functools

import numpy as np
import jax
import jax.numpy as jnp
from jax import lax
from jax.experimental import pallas as pl
from jax.experimental.pallas import tpu as pltpu

F32 = jnp.float32
BF16 = jnp.bfloat16

D_MODEL = 2048
DEPTH = 1
GLA_HEADS = 4
GLA_DK = 128
GLA_DV = 256
GLA_GATE_RANK = 16
GLA_GATE_NORM = 16.0
GLA_CHUNK = 64
NSA_HEADS = 8
NSA_HD = 128
NSA_KV_GROUPS = 2
NSA_HPG = NSA_HEADS // NSA_KV_GROUPS
CMP_BLOCK = 32
CMP_STRIDE = 16
CMP_HIDDEN = 2 * NSA_HD
SLC_BLOCK = 64
SLC_TOPK = 16
WINDOW = 512
ROPE_THETA = 500000.0
ROPE_DIM = NSA_HD // 4
FFN_HIDDEN = 5632
DEEPNORM_ALPHA = (2.0 * DEPTH) ** 0.25
LN_EPS = 1e-5
FORCED_SCORE = 1e4
INVALID_SCORE = -1e4

GLA_WIDTH = GLA_HEADS * GLA_DV
NSA_WIDTH = NSA_HEADS * NSA_HD

LANES = 128
VMEM_LIMIT = 56 * 1024 * 1024
CB_GQ = 0
CB_GK = 4
CB_GV = 8
CB_GO = 16
CB_NQ = 24
CB_KC = 32
CB_KS = 36
CB_VS = 38
CB_KW = 40
CB_VW = 42
CB_MISC = 44
MISC_GATE_LANE = GLA_GATE_RANK
U_WIDTH = 45 * LANES

REGROUP_TC = 256
IN_TM = 512
OUT_TM = 512
FFN_TM, FFN_TF = 512, 512
NSA_TQ = 128
NSA_SUB = 2
NSA_TK = 512
WIN_SPAN = WINDOW + NSA_TQ
MASKED = -1e30
LOG2E = 1.4426950408889634


def _nt_dot(a, b):
    return lax.dot_general(a, b, (((1,), (1,)), ((), ())), preferred_element_type=F32)


def _tn_dot(a, b):
    return lax.dot_general(a, b, (((0,), (0,)), ((), ())), preferred_element_type=F32)


def _dot(a, b):
    return jnp.dot(a, b, preferred_element_type=F32)


def _layer_norm(z, g, b):
    mu = jnp.mean(z, axis=-1, keepdims=True)
    zc = z - mu
    var = jnp.mean(zc * zc, axis=-1, keepdims=True)
    return zc * lax.rsqrt(var + LN_EPS) * g + b


def _masked_softmax(s, mask):
    sm = jnp.where(mask, s, -jnp.inf)
    m = jnp.max(sm, axis=-1, keepdims=True)
    m = jnp.where(m == -jnp.inf, 0.0, m)
    e = jnp.where(mask, jnp.exp(s - m), 0.0)
    den = jnp.sum(e, axis=-1, keepdims=True)
    return e * (1.0 / jnp.where(den > 0, den, 1.0))


def _regroup_kernel(w_ref, oa_ref, ob_ref):
    cols = w_ref.shape[1]
    o_glr = CB_NQ * LANES
    o_nq = o_glr + GLA_GATE_RANK
    o_gate = o_nq + (CB_MISC - CB_NQ) * LANES
    n_gate = NSA_HEADS * 3
    oa_ref[...] = w_ref[0:o_glr, :].astype(BF16)
    roped = (list(range(CB_NQ, CB_NQ + NSA_HEADS)) + list(range(CB_KS, CB_KS + NSA_KV_GROUPS))
             + list(range(CB_KW, CB_KW + NSA_KV_GROUPS)))
    for blk in range(CB_NQ, CB_MISC):
        dst, src = (blk - CB_NQ) * LANES, blk * LANES + GLA_GATE_RANK
        pieces = _head_dim_order() if blk in roped else [(0, NSA_HD)]
        at = 0
        for lo, hi in pieces:
            ob_ref[dst + at:dst + at + hi - lo, :] = w_ref[src + lo:src + hi, :].astype(BF16)
            at += hi - lo
    misc = jnp.concatenate([w_ref[o_glr:o_nq, :], w_ref[o_gate:o_gate + n_gate, :],
                            jnp.zeros((LANES - GLA_GATE_RANK - n_gate, cols), F32)], axis=0)
    ob_ref[(CB_MISC - CB_NQ) * LANES:, :] = misc.astype(BF16)


def _regroup_w_in(wt):
    n_in, D = wt.shape
    na, nb = CB_NQ * LANES, U_WIDTH - CB_NQ * LANES
    return pl.pallas_call(
        _regroup_kernel,
        grid=(D // REGROUP_TC,),
        in_specs=[pl.BlockSpec((n_in, REGROUP_TC), lambda i: (0, i))],
        out_specs=[pl.BlockSpec((na, REGROUP_TC), lambda i: (0, i)), pl.BlockSpec((nb, REGROUP_TC), lambda i: (0, i))],
        out_shape=[jax.ShapeDtypeStruct((na, D), BF16), jax.ShapeDtypeStruct((nb, D), BF16)],
        compiler_params=pltpu.CompilerParams(
            dimension_semantics=("parallel",), vmem_limit_bytes=VMEM_LIMIT),
        name="regroup_w_in",
    )(wt)


def _in_proj_nsa_kernel(x_ref, w_ref, cos_ref, sin_ref, blk_ref,
                        u_ref, qc_ref, qr_ref, ksa_ref, vsa_ref, kwr_ref, vwa_ref):
    res = _nt_dot(x_ref[...].astype(BF16), w_ref[...])
    u_ref[...] = res
    cos, sin = cos_ref[...], sin_ref[...]
    scale = NSA_HD ** -0.5
    for h in range(NSA_HEADS):
        cols = slice(h * NSA_HD, (h + 1) * NSA_HD)
        qc_ref[:, cols] = (res[:, cols] * scale).astype(BF16)
        qr_ref[:, cols] = _rope(res[:, cols] * (scale * LOG2E), cos, sin).astype(BF16)
    ones = jnp.ones((res.shape[0], LANES), BF16)
    col = lambda cb, g: slice((cb - CB_NQ + g) * LANES, (cb - CB_NQ + g + 1) * LANES)
    for g in range(NSA_KV_GROUPS):
        lo, hi = slice(2 * g * LANES, (2 * g + 1) * LANES), slice((2 * g + 1) * LANES, (2 * g + 2) * LANES)
        ksa_ref[:, lo] = _rope(res[:, col(CB_KS, g)], cos, sin).astype(BF16)
        ksa_ref[:, hi] = blk_ref[...]
        vsa_ref[:, lo] = res[:, col(CB_VS, g)].astype(BF16)
        vsa_ref[:, hi] = ones
        kwr_ref[:, g * LANES:(g + 1) * LANES] = _rope(res[:, col(CB_KW, g)], cos, sin).astype(BF16)
        vwa_ref[:, lo] = res[:, col(CB_VW, g)].astype(BF16)
        vwa_ref[:, hi] = ones


def _in_proj_nsa(x2, wt_bf, S, cos_f, sin_s, blk_onehot):
    T, D = x2.shape
    N = wt_bf.shape[0]
    G = NSA_KV_GROUPS
    per_seq = S // IN_TM
    table = pl.BlockSpec((IN_TM, LANES), lambda i: (i % per_seq, 0))
    rows = lambda w: pl.BlockSpec((IN_TM, w), lambda i: (i, 0))
    widths = [N, NSA_WIDTH, NSA_WIDTH, 2 * G * LANES, 2 * G * LANES, G * LANES, 2 * G * LANES]
    return pl.pallas_call(
        _in_proj_nsa_kernel,
        grid=(T // IN_TM,),
        in_specs=[rows(D), pl.BlockSpec((N, D), lambda i: (0, 0)), table, table, table],
        out_specs=[rows(w) for w in widths],
        out_shape=[jax.ShapeDtypeStruct((T, N), F32)] + [jax.ShapeDtypeStruct((T, w), BF16) for w in widths[1:]],
        compiler_params=pltpu.CompilerParams(
            dimension_semantics=("parallel",), vmem_limit_bytes=VMEM_LIMIT),
        name="in_proj_nsa",
    )(x2, wt_bf, cos_f, sin_s, blk_onehot)


GLA_ROWBLK = 256
GLA_TS = 512


def _split3(a):
    hi = a.astype(BF16)
    r1 = a - hi.astype(F32)
    mid = r1.astype(BF16)
    lo = (r1 - mid.astype(F32)).astype(BF16)
    return hi, mid, lo


def _log_sigmoid(z):
    return jnp.minimum(z, 0.0) - jnp.log(1.0 + jnp.exp(-jnp.abs(z)))


def _chunk_causal():
    r = lax.broadcasted_iota(jnp.int32, (GLA_ROWBLK, GLA_ROWBLK), 0)
    c = lax.broadcasted_iota(jnp.int32, (GLA_ROWBLK, GLA_ROWBLK), 1)
    return ((r // GLA_CHUNK) == (c // GLA_CHUNK)) & (c <= r)


def _in_proj_gla_kernel(x_ref, w_ref, misc_ref, w2_ref, b2_ref, vgo_ref, qd_ref, ki_ref, ks_ref, dec_ref):
    H, C, DK = GLA_HEADS, GLA_CHUNK, GLA_DK
    qk_w = 2 * H * DK
    scale = DK ** -0.5
    xb = x_ref[...].astype(BF16)
    res_qk = _nt_dot(xb, w_ref[0:qk_w, :])
    vgo_ref[...] = _nt_dot(xb, w_ref[qk_w:, :])

    cum_m = jnp.where(_chunk_causal(), 1.0, 0.0).astype(BF16)
    w2 = w2_ref[...].astype(BF16)
    b2 = b2_ref[...]
    for rb in range(x_ref.shape[0] // GLA_ROWBLK):
        rows = slice(rb * GLA_ROWBLK, (rb + 1) * GLA_ROWBLK)
        glr = misc_ref[rows, 0:GLA_GATE_RANK].astype(BF16)
        gk = _log_sigmoid(_dot(glr, w2) + b2) * (1.0 / GLA_GATE_NORM)
        hi, mid, lo = _split3(gk)
        bc = _dot(cum_m, hi) + _dot(cum_m, mid) + _dot(cum_m, lo)
        bl = jnp.concatenate(
            [jnp.broadcast_to(bc[j * C + C - 1:j * C + C, :], (C, H * DK)) for j in range(GLA_ROWBLK // C)],
            axis=0)
        q = res_qk[rows, 0:H * DK] * scale
        k = res_qk[rows, H * DK:]
        qd_ref[rows, :] = (q * jnp.exp(bc)).astype(BF16)
        ki_ref[rows, :] = (k * jnp.exp(-bc)).astype(BF16)
        ks_ref[rows, :] = (k * jnp.exp(bl - bc)).astype(BF16)
        dec_ref[rows, :] = jnp.exp(bl)


def _in_proj_gla(x2, wt_bf, u_nsa, gate_w2, gate_b2):
    T, D = x2.shape
    N = wt_bf.shape[0]
    qk_w = GLA_HEADS * GLA_DK
    rows = lambda w, j=0: pl.BlockSpec((IN_TM, w), lambda i: (i, j))
    whole = lambda a: pl.BlockSpec(a.shape, lambda i: (0, 0))
    return pl.pallas_call(
        _in_proj_gla_kernel,
        grid=(T // IN_TM,),
        in_specs=[rows(D), whole(wt_bf), rows(LANES, CB_MISC - CB_NQ), whole(gate_w2), whole(gate_b2)],
        out_specs=[rows(N - 2 * qk_w), rows(qk_w), rows(qk_w), rows(qk_w), rows(qk_w)],
        out_shape=[jax.ShapeDtypeStruct((T, N - 2 * qk_w), F32)]
        + [jax.ShapeDtypeStruct((T, qk_w), BF16)] * 3 + [jax.ShapeDtypeStruct((T, qk_w), F32)],
        compiler_params=pltpu.CompilerParams(
            dimension_semantics=("parallel",), vmem_limit_bytes=VMEM_LIMIT),
        name="in_proj_gla",
    )(x2, wt_bf, u_nsa, gate_w2, gate_b2)


def _gla_kernel(qd_ref, ki_ref, ks_ref, dec_ref, v_ref, go_ref, nw_ref, o_ref, state_s, sbf_s):
    TS = qd_ref.shape[0]
    H, C, DK, DV = GLA_HEADS, GLA_CHUNK, GLA_DK, GLA_DV
    chunk_causal = _chunk_causal()

    @pl.when(pl.program_id(1) == 0)
    def _():
        state_s[...] = jnp.zeros(state_s.shape, F32)

    n_chunk = TS // C
    nw = nw_ref[...]
    for h in range(H):
        hv = slice(h * DV, (h + 1) * DV)
        hk = slice(h * DK, (h + 1) * DK)
        kv = [_tn_dot(ks_ref[c * C:(c + 1) * C, hk], v_ref[c * C:(c + 1) * C, hv].astype(BF16))
              for c in range(n_chunk)]
        dec_rows = jnp.concatenate([dec_ref[c * C:c * C + 1, hk] for c in range(n_chunk)]
                                   + [jnp.zeros((LANES - n_chunk, DK), F32)], axis=0)
        dec_cols = jnp.transpose(dec_rows)
        state = state_s[h]
        for c in range(n_chunk):
            sbf_s[h, c] = state.astype(BF16)
            state = state * dec_cols[:, c:c + 1] + kv[c]
        state_s[h] = state

    for h in range(H):
        hv = slice(h * DV, (h + 1) * DV)
        hk = slice(h * DK, (h + 1) * DK)
        for rb in range(TS // GLA_ROWBLK):
            rows = slice(rb * GLA_ROWBLK, (rb + 1) * GLA_ROWBLK)
            qd = qd_ref[rows, hk]
            attn = jnp.where(chunk_causal, _nt_dot(qd, ki_ref[rows, hk]), 0.0).astype(BF16)
            o = _dot(attn, v_ref[rows, hv].astype(BF16))
            inter = [_dot(qd_ref[c * C:(c + 1) * C, hk], sbf_s[h, c])
                     for c in range(rb * GLA_ROWBLK // C, (rb + 1) * GLA_ROWBLK // C)]
            o = o + jnp.concatenate(inter, axis=0)
            ms = jnp.mean(o * o, axis=-1, keepdims=True)
            o = o * lax.rsqrt(ms + LN_EPS) * nw
            go = go_ref[rows, hv]
            o_ref[rows, hv] = (o * (go * jax.nn.sigmoid(go))).astype(o_ref.dtype)


def _gla(vgo3, qd3, ki3, ks3, dec3, norm_w):
    B, S, _ = vgo3.shape
    H, DK, DV = GLA_HEADS, GLA_DK, GLA_DV
    qk_w, v_w = H * DK, H * DV
    tok = lambda w, j=0: pl.BlockSpec((None, GLA_TS, w), lambda b, s: (b, s, j))
    return pl.pallas_call(
        _gla_kernel,
        grid=(B, S // GLA_TS),
        in_specs=[tok(qk_w), tok(qk_w), tok(qk_w), tok(qk_w), tok(v_w, 0), tok(v_w, 1),
                  pl.BlockSpec((1, DV), lambda b, s: (0, 0))],
        out_specs=tok(v_w),
        out_shape=jax.ShapeDtypeStruct((B, S, GLA_WIDTH), BF16),
        scratch_shapes=[pltpu.VMEM((H, DK, DV), F32), pltpu.VMEM((H, GLA_TS // GLA_CHUNK, DK, DV), BF16)],
        compiler_params=pltpu.CompilerParams(
            dimension_semantics=("parallel", "arbitrary"), vmem_limit_bytes=VMEM_LIMIT),
        name="gla",
    )(qd3, ki3, ks3, dec3, vgo3, vgo3, norm_w)


N_CMP_PAD = 128


def _gelu_tanh(x):
    return x * (0.5 * (1.0 + jnp.tanh(0.7978845608028654 * (x + 0.044715 * (x * x * x)))))


def _compress_kernel(k_ref, v_ref, kpos_ref, vpos_ref, kw1_ref, vw1_ref, kw2_ref, vw2_ref, o_ref):
    half = CMP_BLOCK // 2
    chains = ((k_ref, kpos_ref, kw1_ref, kw2_ref, _head_dim_order()),
              (v_ref, vpos_ref, vw1_ref, vw2_ref, [(0, NSA_HD)]))
    for kv, (src_ref, pos_ref, w1_ref, w2_ref, out_order) in enumerate(chains):
        p0 = jnp.zeros((N_CMP_PAD, CMP_HIDDEN), F32)
        p1 = jnp.zeros((N_CMP_PAD, CMP_HIDDEN), F32)
        for l in range(half):
            x = src_ref[pl.ds(l, N_CMP_PAD, stride=CMP_STRIDE), :]
            a0 = (x + pos_ref[l:l + 1, :]).astype(BF16)
            a1 = (x + pos_ref[half + l:half + l + 1, :]).astype(BF16)
            p0 = p0 + _dot(a0, w1_ref[l * NSA_HD:(l + 1) * NSA_HD, :].astype(BF16))
            p1 = p1 + _dot(a1, w1_ref[(half + l) * NSA_HD:(half + l + 1) * NSA_HD, :].astype(BF16))
        pre = p0 + pltpu.roll(p1, N_CMP_PAD - 1, 0)
        h = _gelu_tanh(pre).astype(BF16)
        w2 = jnp.concatenate([w2_ref[:, lo:hi] for lo, hi in out_order], axis=1).astype(BF16)
        out = _dot(h, w2)
        row = lax.broadcasted_iota(jnp.int32, out.shape, 0)
        o_ref[kv] = jnp.where(row < N_CMP_PAD - 1, out, 0.0).astype(o_ref.dtype)


def _compress(u3, k_params, v_params):
    B, S, _ = u3.shape
    G = NSA_KV_GROUPS
    src = lambda kv: pl.BlockSpec((None, S, NSA_HD), lambda b, g: (b, 0, CB_KC - CB_NQ + 2 * kv + g))
    whole = lambda a: pl.BlockSpec(a.shape, lambda b, g: (0,) * a.ndim)
    params = [k_params[0], v_params[0], k_params[1], v_params[1], k_params[2], v_params[2]]
    return pl.pallas_call(
        _compress_kernel,
        grid=(B, G),
        in_specs=[src(0), src(1)] + [whole(a) for a in params],
        out_specs=pl.BlockSpec((2, None, None, N_CMP_PAD, NSA_HD), lambda b, g: (0, b, g, 0, 0)),
        out_shape=jax.ShapeDtypeStruct((2, B, G, N_CMP_PAD, NSA_HD), BF16),
        compiler_params=pltpu.CompilerParams(
            dimension_semantics=("parallel", "parallel"), vmem_limit_bytes=VMEM_LIMIT),
        name="compress",
    )(u3, u3, *params)


ROPE_PAIR_LANE = LANES // 2


def _head_dim_order():
    half = ROPE_DIM // 2
    return [(0, half), (ROPE_DIM, ROPE_PAIR_LANE + half), (half, ROPE_DIM), (ROPE_PAIR_LANE + half, NSA_HD)]


def _rope(x, cos, sin_signed):
    return x * cos + pltpu.roll(x, ROPE_PAIR_LANE, 1) * sin_signed


def _nsa_kernel(n_side, side_slabs, *refs):
    n_in = 10
    (qc_ref, qr_ref, misc_ref, kc_ref, vc_ref, ksa_s, vsa_s, kwr_s, vwa_s, ovt_ref) = refs[0:n_in]
    side_in = refs[n_in:n_in + n_side]
    o_ref = refs[n_in + n_side]
    side_out = refs[n_in + 1 + n_side:n_in + 1 + 2 * n_side]
    chain_scratch = refs[n_in + 1 + 2 * n_side:]
    S = ksa_s.shape[0]
    R, TQ, HD = NSA_HPG, NSA_TQ, NSA_HD
    ss_s, es_s = chain_scratch[0:NSA_SUB], chain_scratch[NSA_SUB:2 * NSA_SUB]
    sw_s, ew_s = chain_scratch[2 * NSA_SUB:2 * NSA_SUB + R // 2], chain_scratch[2 * NSA_SUB + R // 2:]
    g = pl.program_id(1)
    qi = pl.program_id(2)

    step_idx = (pl.program_id(0) * pl.num_programs(1) + g) * pl.num_programs(2) + qi
    for w_in_ref, w_out_ref, n_slabs in zip(side_in, side_out, side_slabs):
        def cast(w_in_ref=w_in_ref, w_out_ref=w_out_ref):
            w_out_ref[...] = w_in_ref[...].astype(BF16)
        if n_slabs is None:
            cast()
        else:
            pl.when(step_idx < n_slabs)(cast)

    nb = S // SLC_BLOCK
    TK = NSA_TK
    ovt = ovt_ref[...]
    kc, vc = kc_ref[...], vc_ref[...]

    def front(v, u):
        tq0 = (v * NSA_SUB + u) * TQ
        urows = slice(u * TQ, (u + 1) * TQ)
        qc = jnp.concatenate([qc_ref[urows, r * HD:(r + 1) * HD] for r in range(R)], axis=0)
        qr_heads = [qr_ref[urows, r * HD:(r + 1) * HD] for r in range(R)]
        t_tok = tq0 + lax.broadcasted_iota(jnp.int32, (TQ, 1), 0)
        t_col = jnp.concatenate([t_tok] * R, axis=0)

        s_c = _nt_dot(qc, kc)
        n_idx = lax.broadcasted_iota(jnp.int32, (1, N_CMP_PAD), 1)
        p_c = _masked_softmax(s_c, (n_idx * CMP_STRIDE + (CMP_BLOCK - 1)) <= t_col)
        p_cb = p_c.astype(BF16)
        o_cmp = _dot(p_cb, vc)

        imp = _nt_dot(ovt, p_cb[0:TQ])
        for r in range(1, R):
            imp = imp + _nt_dot(ovt, p_cb[r * TQ:(r + 1) * TQ])
        j_idx = lax.broadcasted_iota(jnp.int32, (nb, TQ), 0)
        cur = (tq0 + lax.broadcasted_iota(jnp.int32, (nb, TQ), 1)) // SLC_BLOCK
        imp = jnp.where(j_idx <= cur, imp, INVALID_SCORE)
        imp = jnp.where((j_idx == 0) | (j_idx == cur) | (j_idx == cur - 1), FORCED_SCORE, imp)
        rank = jnp.zeros((nb, TQ), jnp.int32)
        for i in range(nb):
            ri = imp[i:i + 1, :]
            ahead = (ri > imp) | ((ri == imp) & (i < j_idx))
            rank = rank + jnp.where(ahead, 1, 0)
        blk_bias_t = jnp.where(rank < min(SLC_TOPK, nb), 0.0, MASKED)
        blk_bias_t = jnp.concatenate([blk_bias_t, jnp.zeros((LANES - nb, TQ), F32)], axis=0)
        blk_bias = jnp.transpose(blk_bias_t).astype(BF16)
        q_aug = jnp.concatenate([jnp.concatenate([h, blk_bias], axis=1) for h in qr_heads], axis=0)
        return dict(t_col=t_col, o_cmp=o_cmp, q_aug=q_aug, qr_heads=qr_heads)

    step = NSA_SUB * TQ
    n_pair = R // 2

    def step_body(v):
        subs = [front(v, u) for u in range(NSA_SUB)]
        n_keys = (v + 1) * step
        start = max(v * step - WINDOW, 0)
        span = n_keys - start
        diff = (v * step + lax.broadcasted_iota(jnp.int32, (step, 1), 0)) - (
            start + lax.broadcasted_iota(jnp.int32, (1, span), 1))
        band = jnp.where((diff >= 0) & (diff < WINDOW), 0.0, MASKED)

        def win_scores(c):
            q_pair = jnp.concatenate(
                [subs[u]["qr_heads"][2 * c + rr] for rr in range(2) for u in range(NSA_SUB)], axis=0)
            s_w = _nt_dot(q_pair, kwr_s[start:start + span, :])
            for rr in range(2):
                sw_s[c][rr * step:(rr + 1) * step, 0:span] = s_w[rr * step:(rr + 1) * step] + band

        def win_softmax(c):
            m_w = jnp.max(sw_s[c][:, 0:span], axis=-1, keepdims=True)
            ew_s[c][:, 0:span] = jnp.exp2(sw_s[c][:, 0:span] - m_w).astype(BF16)

        def win_values(c):
            acc_w = _dot(ew_s[c][:, 0:span], vwa_s[start:start + span, :])
            return acc_w[:, 0:HD] * (1.0 / acc_w[:, HD:])

        def sel_scores(u):
            q_aug, t_col = subs[u]["q_aug"], subs[u]["t_col"]
            for k0 in range(0, n_keys, TK):
                w = min(TK, n_keys - k0)
                s = _nt_dot(q_aug, ksa_s[k0:k0 + w, :])
                if k0 + w == n_keys:
                    kpos = k0 + lax.broadcasted_iota(jnp.int32, (1, w), 1)
                    s = jnp.where(kpos <= t_col, s, MASKED)
                ss_s[u][:, k0:k0 + w] = s

        def sel_softmax(u):
            m = jnp.max(ss_s[u][:, 0:n_keys], axis=-1, keepdims=True)
            es_s[u][:, 0:n_keys] = jnp.exp2(ss_s[u][:, 0:n_keys] - m).astype(BF16)

        def sel_values(u):
            acc = _dot(es_s[u][:, 0:n_keys], vsa_s[0:n_keys, :])
            return acc[:, 0:HD] * (1.0 / acc[:, HD:])

        assert NSA_SUB == 2 and n_pair == 2
        win_scores(0)
        win_scores(1)
        sel_scores(0)
        win_softmax(0)
        sel_scores(1)
        win_softmax(1)
        o_win = [win_values(0)]
        sel_softmax(0)
        o_win.append(win_values(1))
        sel_softmax(1)
        o_slc = [sel_values(0), sel_values(1)]

        misc = misc_ref[...]
        ng = 3 * R
        logits = jnp.where(g == 0, misc[:, MISC_GATE_LANE:MISC_GATE_LANE + ng],
                           misc[:, MISC_GATE_LANE + ng:MISC_GATE_LANE + 2 * ng])
        gate = jax.nn.sigmoid(logits)
        for u in range(NSA_SUB):
            urows = slice(u * TQ, (u + 1) * TQ)
            for r in range(R):
                rs = slice(r * TQ, (r + 1) * TQ)
                ws = slice((r % 2) * step + u * TQ, (r % 2) * step + (u + 1) * TQ)
                o = (gate[urows, 3 * r:3 * r + 1] * subs[u]["o_cmp"][rs]
                     + gate[urows, 3 * r + 1:3 * r + 2] * o_slc[u][rs]
                     + gate[urows, 3 * r + 2:3 * r + 3] * o_win[r // 2][ws])
                o_ref[urows, r * HD:(r + 1) * HD] = o.astype(o_ref.dtype)

    for v in range(S // step):
        pl.when(qi == v)(functools.partial(step_body, v))


BF16_ROWS = 16


def _side_cast_plan(weights, n_steps):
    plan = []
    for w in weights:
        n_rows = w.shape[0]
        if n_rows % (n_steps * BF16_ROWS) == 0:
            plan.append((n_rows // n_steps, None))
        else:
            assert n_rows % LANES == 0 and n_rows // LANES <= n_steps
            plan.append((LANES, n_rows // LANES))
    return plan


def _nsa(u3, cmp_kv, q_ops, kv_ops, ovt, side_weights):
    B, S, _ = u3.shape
    G, R, HD = NSA_KV_GROUPS, NSA_HPG, NSA_HD
    rows = R * NSA_TQ
    step = NSA_SUB * NSA_TQ
    assert NSA_TK % step == 0
    n_q = S // step
    plan = _side_cast_plan(side_weights, B * G * n_q)

    def side_spec(w, slab_rows, n_slabs):
        def index(b, g, i):
            s = (b * G + g) * n_q + i
            return (s if n_slabs is None else jnp.minimum(s, n_slabs - 1), 0)
        return pl.BlockSpec((slab_rows, w.shape[1]), index)

    side_specs = [side_spec(w, *p) for w, p in zip(side_weights, plan)]
    full = lambda a: pl.BlockSpec((None, S, a.shape[2] // G), lambda b, g, i: (b, 0, g))
    q_spec = pl.BlockSpec((None, step, R * HD), lambda b, g, i: (b, i, g))
    kern = functools.partial(_nsa_kernel, len(side_weights), tuple(p[1] for p in plan))
    return pl.pallas_call(
        kern,
        grid=(B, G, n_q),
        in_specs=[
            q_spec, q_spec,
            pl.BlockSpec((None, step, LANES), lambda b, g, i: (b, i, CB_MISC - CB_NQ)),
            pl.BlockSpec((None, None, None, N_CMP_PAD, HD), lambda b, g, i: (0, b, g, 0, 0)),
            pl.BlockSpec((None, None, None, N_CMP_PAD, HD), lambda b, g, i: (1, b, g, 0, 0)),
            full(kv_ops[0]), full(kv_ops[1]), full(kv_ops[2]), full(kv_ops[3]),
            pl.BlockSpec((S // SLC_BLOCK, N_CMP_PAD), lambda b, g, i: (0, 0)),
        ] + side_specs,
        out_specs=[q_spec] + side_specs,
        out_shape=[jax.ShapeDtypeStruct((B, S, NSA_WIDTH), BF16)]
        + [jax.ShapeDtypeStruct(w.shape, BF16) for w in side_weights],
        scratch_shapes=[pltpu.VMEM((rows, S), F32)] * NSA_SUB + [pltpu.VMEM((rows, S), BF16)] * NSA_SUB
        + [pltpu.VMEM((2 * step, WINDOW + step), F32)] * (R // 2)
        + [pltpu.VMEM((2 * step, WINDOW + step), BF16)] * (R // 2),
        compiler_params=pltpu.CompilerParams(
            dimension_semantics=("arbitrary", "arbitrary", "arbitrary"), vmem_limit_bytes=VMEM_LIMIT),
        name="nsa",
    )(*q_ops, u3, cmp_kv, cmp_kv, *kv_ops, ovt, *side_weights)


def _out_proj_kernel(og_ref, on_ref, x_ref, w_ref, g_ref, b_ref, h_ref, hb_ref):
    half = og_ref.shape[0] // 2
    halves = [slice(0, half), slice(half, 2 * half)]
    mixes = [_dot(og_ref[rows, :], w_ref[0:GLA_WIDTH, :]) + _dot(on_ref[rows, :], w_ref[GLA_WIDTH:, :])
             for rows in halves]
    for rows, mix in zip(halves, mixes):
        h = _layer_norm(DEEPNORM_ALPHA * x_ref[rows, :] + mix, g_ref[...], b_ref[...])
        h_ref[rows, :] = h
        hb_ref[rows, :] = h.astype(BF16)


def _out_proj(o_gla, o_nsa, x2, w_bf, g, b):
    T, D = x2.shape
    tile = lambda w: pl.BlockSpec((OUT_TM, w), lambda i: (i, 0))
    const = lambda s: pl.BlockSpec(s, lambda i: (0, 0))
    return pl.pallas_call(
        _out_proj_kernel,
        grid=(T // OUT_TM,),
        in_specs=[tile(GLA_WIDTH), tile(NSA_WIDTH), tile(D), const(w_bf.shape), const((1, D)), const((1, D))],
        out_specs=[tile(D), tile(D)],
        out_shape=[jax.ShapeDtypeStruct((T, D), F32), jax.ShapeDtypeStruct((T, D), BF16)],
        compiler_params=pltpu.CompilerParams(
            dimension_semantics=("parallel",), vmem_limit_bytes=VMEM_LIMIT),
        name="out_proj",
    )(o_gla, o_nsa, x2, w_bf, g, b)


def _ffn_kernel(hb_ref, h_ref, w1_ref, w3_ref, w2_ref, g_ref, b_ref, o_ref, acc_s):
    f = pl.program_id(1)

    @pl.when(f == 0)
    def _():
        acc_s[...] = jnp.zeros(acc_s.shape, F32)

    hb = hb_ref[...]
    a = _dot(hb, w1_ref[...])
    c = _dot(hb, w3_ref[...])
    acc_s[...] += _dot((a * jax.nn.sigmoid(a) * c).astype(BF16), w2_ref[...])

    @pl.when(f == pl.num_programs(1) - 1)
    def _():
        o_ref[...] = _layer_norm(DEEPNORM_ALPHA * h_ref[...] + acc_s[...], g_ref[...], b_ref[...])


def _ffn(h_bf, h, w1, w3, w2, g, b):
    T, D = h.shape
    F = w1.shape[1]
    return pl.pallas_call(
        _ffn_kernel,
        grid=(T // FFN_TM, F // FFN_TF),
        in_specs=[
            pl.BlockSpec((FFN_TM, D), lambda i, f: (i, 0)),
            pl.BlockSpec((FFN_TM, D), lambda i, f: (i, 0)),
            pl.BlockSpec((D, FFN_TF), lambda i, f: (0, f)),
            pl.BlockSpec((D, FFN_TF), lambda i, f: (0, f)),
            pl.BlockSpec((FFN_TF, D), lambda i, f: (f, 0)),
            pl.BlockSpec((1, D), lambda i, f: (0, 0)),
            pl.BlockSpec((1, D), lambda i, f: (0, 0)),
        ],
        out_specs=pl.BlockSpec((FFN_TM, D), lambda i, f: (i, 0)),
        out_shape=jax.ShapeDtypeStruct((T, D), F32),
        scratch_shapes=[pltpu.VMEM((FFN_TM, D), F32)],
        compiler_params=pltpu.CompilerParams(
            dimension_semantics=("parallel", "arbitrary"), vmem_limit_bytes=VMEM_LIMIT),
        name="ffn",
    )(h_bf, h, w1, w3, w2, g, b)


def _rope_tables(S):
    half = ROPE_DIM // 2
    pos = np.arange(S, dtype=np.float64)
    inv = np.power(ROPE_THETA, -np.arange(0, ROPE_DIM, 2, dtype=np.float64) / ROPE_DIM)
    ang = pos[:, None] * inv[None, :]
    cos_f = np.ones((S, LANES), np.float32)
    sin_s = np.zeros((S, LANES), np.float32)
    for lane0, sign in ((0, -1.0), (ROPE_PAIR_LANE, 1.0)):
        cos_f[:, lane0:lane0 + half] = np.cos(ang)
        sin_s[:, lane0:lane0 + half] = sign * np.sin(ang)
    return jnp.asarray(cos_f), jnp.asarray(sin_s)


def _selection_constants(S):
    n_cmp = (S - CMP_BLOCK) // CMP_STRIDE + 1
    nb = S // SLC_BLOCK
    c_start = np.arange(n_cmp) * CMP_STRIDE
    b_start = np.arange(nb) * SLC_BLOCK
    overlap = ((c_start[:, None] < b_start[None, :] + SLC_BLOCK) &
               (b_start[None, :] < c_start[:, None] + CMP_BLOCK)).astype(np.float32)
    ovt = np.zeros((nb, N_CMP_PAD), np.float32)
    ovt[:, :n_cmp] = overlap.T
    blk_onehot = ((np.arange(S) // SLC_BLOCK)[:, None] == np.arange(LANES)[None, :]).astype(np.float32)
    return jnp.asarray(ovt, BF16), jnp.asarray(blk_onehot, BF16)


def kernel(x, w_in, gla_gate_w2, gla_gate_b2, gla_norm_w, cmp_k_pos, cmp_k_w1, cmp_k_w2,
           cmp_v_pos, cmp_v_w1, cmp_v_w2, w_out, ln1_g, ln1_b, ffn_w1, ffn_w3, ffn_w2, ln2_g, ln2_b):
    B, S, D = x.shape
    T = B * S
    cos_f, sin_s = _rope_tables(S)
    ovt, blk_onehot = _selection_constants(S)

    x2 = x.reshape(T, D)
    for l in range(DEPTH):
        wt_gla, wt_nsa = _regroup_w_in(w_in[l].T)
        u_nsa, *attn_ops = _in_proj_nsa(x2, wt_nsa, S, cos_f, sin_s, blk_onehot)
        u3 = u_nsa.reshape(B, S, -1)
        attn_ops = [a.reshape(B, S, -1) for a in attn_ops]
        q_ops, kv_ops = attn_ops[0:2], attn_ops[2:]
        gla_ops = _in_proj_gla(x2, wt_gla, u_nsa, gla_gate_w2[l], gla_gate_b2[l][None, :])
        o_gla = _gla(*[a.reshape(B, S, -1) for a in gla_ops], gla_norm_w[l][None, :])
        cmp_kv = _compress(u3, (cmp_k_pos[l], cmp_k_w1[l], cmp_k_w2[l]), (cmp_v_pos[l], cmp_v_w1[l], cmp_v_w2[l]))
        o_nsa, w_out_bf, w1_bf, w3_bf, w2_bf = _nsa(u3, cmp_kv, q_ops, kv_ops, ovt,
                                                    [w_out[l], ffn_w1[l], ffn_w3[l], ffn_w2[l]])
        h, h_bf = _out_proj(o_gla.reshape(T, GLA_WIDTH), o_nsa.reshape(T, NSA_WIDTH), x2,
                            w_out_bf, ln1_g[l][None, :], ln1_b[l][None, :])
        x2 = _ffn(h_bf, h, w1_bf, w3_bf, w2_bf, ln2_g[l][None, :], ln2_b[l][None, :])
    return x2.reshape(B, S, D)
```

```python
import functools

import numpy as np
import jax
import jax.numpy as jnp
from jax import lax
from jax.experimental import pallas as pl
from jax.experimental.pallas import tpu as pltpu

F32 = jnp.float32
BF16 = jnp.bfloat16

D_MODEL = 2048
DEPTH = 1
GLA_HEADS = 4
GLA_DK = 128
GLA_DV = 256
GLA_GATE_RANK = 16
GLA_GATE_NORM = 16.0
GLA_CHUNK = 64
NSA_HEADS = 8
NSA_HD = 128
NSA_KV_GROUPS = 2
NSA_HPG = NSA_HEADS // NSA_KV_GROUPS
CMP_BLOCK = 32
CMP_STRIDE = 16
CMP_HIDDEN = 2 * NSA_HD
SLC_BLOCK = 64
SLC_TOPK = 16
WINDOW = 512
ROPE_THETA = 500000.0
ROPE_DIM = NSA_HD // 4
FFN_HIDDEN = 5632
DEEPNORM_ALPHA = (2.0 * DEPTH) ** 0.25
LN_EPS = 1e-5
FORCED_SCORE = 1e4
INVALID_SCORE = -1e4

GLA_WIDTH = GLA_HEADS * GLA_DV
NSA_WIDTH = NSA_HEADS * NSA_HD

LANES = 128
VMEM_LIMIT = 56 * 1024 * 1024
CB_GQ = 0
CB_GK = 4
CB_GV = 8
CB_GO = 16
CB_NQ = 24
CB_KC = 32
CB_KS = 36
CB_VS = 38
CB_KW = 40
CB_VW = 42
CB_MISC = 44
MISC_GATE_LANE = GLA_GATE_RANK
UB_MISC = 2 * NSA_KV_GROUPS
U_WIDTH = 45 * LANES

REGROUP_TC = 256
IN_TM = 512
OUT_TM = 512
FFN_TM, FFN_TF = 512, 512
NSA_TQ = 128
NSA_SUB = 2
NSA_TK = 512
WIN_SPAN = WINDOW + NSA_TQ
MASKED = -1e30
LOG2E = 1.4426950408889634


def _nt_dot(a, b):
    return lax.dot_general(a, b, (((1,), (1,)), ((), ())), preferred_element_type=F32)


def _tn_dot(a, b):
    return lax.dot_general(a, b, (((0,), (0,)), ((), ())), preferred_element_type=F32)


def _dot(a, b):
    return jnp.dot(a, b, preferred_element_type=F32)


def _layer_norm(z, g, b):
    mu = jnp.mean(z, axis=-1, keepdims=True)
    zc = z - mu
    var = jnp.mean(zc * zc, axis=-1, keepdims=True)
    return zc * lax.rsqrt(var + LN_EPS) * g + b


def _masked_softmax(s, mask):
    sm = jnp.where(mask, s, -jnp.inf)
    m = jnp.max(sm, axis=-1, keepdims=True)
    m = jnp.where(m == -jnp.inf, 0.0, m)
    e = jnp.where(mask, jnp.exp(s - m), 0.0)
    den = jnp.sum(e, axis=-1, keepdims=True)
    return e * (1.0 / jnp.where(den > 0, den, 1.0))


def _regroup_kernel(w_ref, oa_ref, ob_ref):
    cols = w_ref.shape[1]
    o_glr = CB_NQ * LANES
    o_nq = o_glr + GLA_GATE_RANK
    o_gate = o_nq + (CB_MISC - CB_NQ) * LANES
    n_gate = NSA_HEADS * 3
    oa_ref[...] = w_ref[0:o_glr, :].astype(BF16)
    roped = (list(range(CB_NQ, CB_NQ + NSA_HEADS)) + list(range(CB_KS, CB_KS + NSA_KV_GROUPS))
             + list(range(CB_KW, CB_KW + NSA_KV_GROUPS)))
    for blk in range(CB_NQ, CB_MISC):
        dst, src = (blk - CB_NQ) * LANES, blk * LANES + GLA_GATE_RANK
        pieces = _head_dim_order() if blk in roped else [(0, NSA_HD)]
        at = 0
        for lo, hi in pieces:
            ob_ref[dst + at:dst + at + hi - lo, :] = w_ref[src + lo:src + hi, :].astype(BF16)
            at += hi - lo
    misc = jnp.concatenate([w_ref[o_glr:o_nq, :], w_ref[o_gate:o_gate + n_gate, :],
                            jnp.zeros((LANES - GLA_GATE_RANK - n_gate, cols), F32)], axis=0)
    ob_ref[(CB_MISC - CB_NQ) * LANES:, :] = misc.astype(BF16)


def _regroup_w_in(wt):
    n_in, D = wt.shape
    na, nb = CB_NQ * LANES, U_WIDTH - CB_NQ * LANES
    return pl.pallas_call(
        _regroup_kernel,
        grid=(D // REGROUP_TC,),
        in_specs=[pl.BlockSpec((n_in, REGROUP_TC), lambda i: (0, i))],
        out_specs=[pl.BlockSpec((na, REGROUP_TC), lambda i: (0, i)), pl.BlockSpec((nb, REGROUP_TC), lambda i: (0, i))],
        out_shape=[jax.ShapeDtypeStruct((na, D), BF16), jax.ShapeDtypeStruct((nb, D), BF16)],
        compiler_params=pltpu.CompilerParams(
            dimension_semantics=("parallel",), vmem_limit_bytes=VMEM_LIMIT),
        name="regroup_w_in",
    )(wt)


def _in_proj_kernel(x_ref, w_ref, o_ref):
    o_ref[...] = _nt_dot(x_ref[...].astype(BF16), w_ref[...])


def _in_proj(x2, wt_bf):
    T, D = x2.shape
    N = wt_bf.shape[0]
    return pl.pallas_call(
        _in_proj_kernel,
        grid=(T // IN_TM,),
        in_specs=[pl.BlockSpec((IN_TM, D), lambda i: (i, 0)),
                  pl.BlockSpec((N, D), lambda i: (0, 0))],
        out_specs=pl.BlockSpec((IN_TM, N), lambda i: (i, 0)),
        out_shape=jax.ShapeDtypeStruct((T, N), F32),
        compiler_params=pltpu.CompilerParams(
            dimension_semantics=("parallel",), vmem_limit_bytes=VMEM_LIMIT),
        name="in_proj",
    )(x2, wt_bf)


def _in_proj_nsa_kernel(x_ref, w_ref, cos_ref, sin_ref, blk_ref,
                        u_ref, qc_ref, qr_ref, ksa_ref, vsa_ref, kwr_ref, vwa_ref):
    xb = x_ref[...].astype(BF16)
    res_q = _nt_dot(xb, w_ref[0:NSA_WIDTH, :])
    res = _nt_dot(xb, w_ref[NSA_WIDTH:, :])
    cos, sin = cos_ref[...], sin_ref[...]
    scale = NSA_HD ** -0.5
    for h in range(NSA_HEADS):
        cols = slice(h * NSA_HD, (h + 1) * NSA_HD)
        qc_ref[:, cols] = (res_q[:, cols] * scale).astype(BF16)
        qr_ref[:, cols] = _rope(res_q[:, cols] * (scale * LOG2E), cos, sin).astype(BF16)
    first = CB_NQ + NSA_HEADS
    col = lambda cb, g: slice((cb - first + g) * LANES, (cb - first + g + 1) * LANES)
    n_cmp_cols = 2 * NSA_KV_GROUPS * NSA_HD
    u_ref[:, 0:n_cmp_cols] = res[:, col(CB_KC, 0).start:col(CB_KC, 0).start + n_cmp_cols]
    u_ref[:, n_cmp_cols:] = res[:, col(CB_MISC, 0)]
    ones = jnp.ones((res.shape[0], LANES), BF16)
    for g in range(NSA_KV_GROUPS):
        lo, hi = slice(2 * g * LANES, (2 * g + 1) * LANES), slice((2 * g + 1) * LANES, (2 * g + 2) * LANES)
        ksa_ref[:, lo] = _rope(res[:, col(CB_KS, g)], cos, sin).astype(BF16)
        ksa_ref[:, hi] = blk_ref[...]
        vsa_ref[:, lo] = res[:, col(CB_VS, g)].astype(BF16)
        vsa_ref[:, hi] = ones
        kwr_ref[:, g * LANES:(g + 1) * LANES] = _rope(res[:, col(CB_KW, g)], cos, sin).astype(BF16)
        vwa_ref[:, lo] = res[:, col(CB_VW, g)].astype(BF16)
        vwa_ref[:, hi] = ones


def _in_proj_nsa(x2, wt_bf, S, cos_f, sin_s, blk_onehot):
    T, D = x2.shape
    N = wt_bf.shape[0]
    G = NSA_KV_GROUPS
    per_seq = S // IN_TM
    table = pl.BlockSpec((IN_TM, LANES), lambda i: (i % per_seq, 0))
    rows = lambda w: pl.BlockSpec((IN_TM, w), lambda i: (i, 0))
    widths = [(UB_MISC + 1) * LANES, NSA_WIDTH, NSA_WIDTH, 2 * G * LANES, 2 * G * LANES, G * LANES, 2 * G * LANES]
    return pl.pallas_call(
        _in_proj_nsa_kernel,
        grid=(T // IN_TM,),
        in_specs=[rows(D), pl.BlockSpec((N, D), lambda i: (0, 0)), table, table, table],
        out_specs=[rows(w) for w in widths],
        out_shape=[jax.ShapeDtypeStruct((T, widths[0]), F32)]
        + [jax.ShapeDtypeStruct((T, w), BF16) for w in widths[1:]],
        compiler_params=pltpu.CompilerParams(
            dimension_semantics=("parallel",), vmem_limit_bytes=VMEM_LIMIT),
        name="in_proj_nsa",
    )(x2, wt_bf, cos_f, sin_s, blk_onehot)


GLA_ROWBLK = 256
GLA_TS = 512


def _split3(a):
    hi = a.astype(BF16)
    r1 = a - hi.astype(F32)
    mid = r1.astype(BF16)
    lo = (r1 - mid.astype(F32)).astype(BF16)
    return hi, mid, lo


def _log_sigmoid(z):
    return jnp.minimum(z, 0.0) - jnp.log(1.0 + jnp.exp(-jnp.abs(z)))


def _gla_kernel(q_ref, k_ref, v_ref, go_ref, misc_ref, w2_ref, b2_ref, nw_ref, o_ref,
                qd_s, ki_s, ks_s, dec_s, state_s, sbf_s):
    TS = q_ref.shape[0]
    H, C, DK, DV = GLA_HEADS, GLA_CHUNK, GLA_DK, GLA_DV
    scale = DK ** -0.5

    @pl.when(pl.program_id(1) == 0)
    def _():
        state_s[...] = jnp.zeros(state_s.shape, F32)

    r = lax.broadcasted_iota(jnp.int32, (GLA_ROWBLK, GLA_ROWBLK), 0)
    c = lax.broadcasted_iota(jnp.int32, (GLA_ROWBLK, GLA_ROWBLK), 1)
    chunk_causal = ((r // C) == (c // C)) & (c <= r)
    cum_m = jnp.where(chunk_causal, 1.0, 0.0).astype(BF16)

    w2 = w2_ref[...].astype(BF16)
    b2 = b2_ref[...]
    for rb in range(TS // GLA_ROWBLK):
        rows = slice(rb * GLA_ROWBLK, (rb + 1) * GLA_ROWBLK)
        glr = misc_ref[rows, 0:GLA_GATE_RANK].astype(BF16)
        gk = _log_sigmoid(_dot(glr, w2) + b2) * (1.0 / GLA_GATE_NORM)
        hi, mid, lo = _split3(gk)
        bc_all = _dot(cum_m, hi) + _dot(cum_m, mid) + _dot(cum_m, lo)
        bl_all = jnp.concatenate(
            [jnp.broadcast_to(bc_all[j * C + C - 1:j * C + C, :], (C, H * DK)) for j in range(GLA_ROWBLK // C)],
            axis=0)
        for h in range(H):
            hk = slice(h * DK, (h + 1) * DK)
            bc = bc_all[:, hk]
            bl = bl_all[:, hk]
            q = q_ref[rows, hk] * scale
            k = k_ref[rows, hk]
            qd_s[h, rows, :] = (q * jnp.exp(bc)).astype(BF16)
            ki_s[h, rows, :] = (k * jnp.exp(-bc)).astype(BF16)
            ks_s[h, rows, :] = (k * jnp.exp(bl - bc)).astype(BF16)
            dec_s[h, rows, :] = jnp.exp(bl)

    n_chunk = TS // C
    nw = nw_ref[...]
    for h in range(H):
        hv = slice(h * DV, (h + 1) * DV)
        kv = [_tn_dot(ks_s[h, c * C:(c + 1) * C, :], v_ref[c * C:(c + 1) * C, hv].astype(BF16))
              for c in range(n_chunk)]
        dec_rows = jnp.concatenate([dec_s[h, c * C:c * C + 1, :] for c in range(n_chunk)]
                                   + [jnp.zeros((LANES - n_chunk, DK), F32)], axis=0)
        dec_cols = jnp.transpose(dec_rows)
        state = state_s[h]
        for c in range(n_chunk):
            sbf_s[h, c] = state.astype(BF16)
            state = state * dec_cols[:, c:c + 1] + kv[c]
        state_s[h] = state

    for h in range(H):
        hv = slice(h * DV, (h + 1) * DV)
        for rb in range(TS // GLA_ROWBLK):
            rows = slice(rb * GLA_ROWBLK, (rb + 1) * GLA_ROWBLK)
            qd = qd_s[h, rows, :]
            attn = jnp.where(chunk_causal, _nt_dot(qd, ki_s[h, rows, :]), 0.0).astype(BF16)
            o = _dot(attn, v_ref[rows, hv].astype(BF16))
            inter = [_dot(qd_s[h, c * C:(c + 1) * C, :], sbf_s[h, c])
                     for c in range(rb * GLA_ROWBLK // C, (rb + 1) * GLA_ROWBLK // C)]
            o = o + jnp.concatenate(inter, axis=0)
            ms = jnp.mean(o * o, axis=-1, keepdims=True)
            o = o * lax.rsqrt(ms + LN_EPS) * nw
            go = go_ref[rows, hv]
            o_ref[rows, hv] = (o * (go * jax.nn.sigmoid(go))).astype(o_ref.dtype)


def _gla(ua3, ub3, gate_w2, gate_b2, norm_w):
    B, S, _ = ua3.shape
    H, DK, DV = GLA_HEADS, GLA_DK, GLA_DV
    qk_w, v_w = H * DK, H * DV
    tok = lambda w, off: pl.BlockSpec((None, GLA_TS, w), lambda b, s: (b, s, off * LANES // w))
    const = lambda shape: pl.BlockSpec(shape, lambda b, s: (0, 0))
    return pl.pallas_call(
        _gla_kernel,
        grid=(B, S // GLA_TS),
        in_specs=[tok(qk_w, CB_GQ), tok(qk_w, CB_GK), tok(v_w, CB_GV), tok(v_w, CB_GO), tok(LANES, UB_MISC),
                  const((GLA_GATE_RANK, qk_w)), const((1, qk_w)), const((1, DV))],
        out_specs=pl.BlockSpec((None, GLA_TS, v_w), lambda b, s: (b, s, 0)),
        out_shape=jax.ShapeDtypeStruct((B, S, GLA_WIDTH), BF16),
        scratch_shapes=[pltpu.VMEM((H, GLA_TS, DK), BF16), pltpu.VMEM((H, GLA_TS, DK), BF16),
                        pltpu.VMEM((H, GLA_TS, DK), BF16), pltpu.VMEM((H, GLA_TS, DK), F32),
                        pltpu.VMEM((H, DK, DV), F32), pltpu.VMEM((H, GLA_TS // GLA_CHUNK, DK, DV), BF16)],
        compiler_params=pltpu.CompilerParams(
            dimension_semantics=("parallel", "arbitrary"), vmem_limit_bytes=VMEM_LIMIT),
        name="gla",
    )(ua3, ua3, ua3, ua3, ub3, gate_w2, gate_b2, norm_w)


N_CMP_PAD = 128


def _gelu_tanh(x):
    return x * (0.5 * (1.0 + jnp.tanh(0.7978845608028654 * (x + 0.044715 * (x * x * x)))))


def _compress_kernel(k_ref, v_ref, kpos_ref, vpos_ref, kw1_ref, vw1_ref, kw2_ref, vw2_ref, o_ref):
    half = CMP_BLOCK // 2
    chains = ((k_ref, kpos_ref, kw1_ref, kw2_ref, _head_dim_order()),
              (v_ref, vpos_ref, vw1_ref, vw2_ref, [(0, NSA_HD)]))
    for kv, (src_ref, pos_ref, w1_ref, w2_ref, out_order) in enumerate(chains):
        p0 = jnp.zeros((N_CMP_PAD, CMP_HIDDEN), F32)
        p1 = jnp.zeros((N_CMP_PAD, CMP_HIDDEN), F32)
        for l in range(half):
            x = src_ref[pl.ds(l, N_CMP_PAD, stride=CMP_STRIDE), :]
            a0 = (x + pos_ref[l:l + 1, :]).astype(BF16)
            a1 = (x + pos_ref[half + l:half + l + 1, :]).astype(BF16)
            p0 = p0 + _dot(a0, w1_ref[l * NSA_HD:(l + 1) * NSA_HD, :].astype(BF16))
            p1 = p1 + _dot(a1, w1_ref[(half + l) * NSA_HD:(half + l + 1) * NSA_HD, :].astype(BF16))
        pre = p0 + pltpu.roll(p1, N_CMP_PAD - 1, 0)
        h = _gelu_tanh(pre).astype(BF16)
        w2 = jnp.concatenate([w2_ref[:, lo:hi] for lo, hi in out_order], axis=1).astype(BF16)
        out = _dot(h, w2)
        row = lax.broadcasted_iota(jnp.int32, out.shape, 0)
        o_ref[kv] = jnp.where(row < N_CMP_PAD - 1, out, 0.0).astype(o_ref.dtype)


def _compress(u3, k_params, v_params):
    B, S, _ = u3.shape
    G = NSA_KV_GROUPS
    src = lambda kv: pl.BlockSpec((None, S, NSA_HD), lambda b, g: (b, 0, NSA_KV_GROUPS * kv + g))
    whole = lambda a: pl.BlockSpec(a.shape, lambda b, g: (0,) * a.ndim)
    params = [k_params[0], v_params[0], k_params[1], v_params[1], k_params[2], v_params[2]]
    return pl.pallas_call(
        _compress_kernel,
        grid=(B, G),
        in_specs=[src(0), src(1)] + [whole(a) for a in params],
        out_specs=pl.BlockSpec((2, None, None, N_CMP_PAD, NSA_HD), lambda b, g: (0, b, g, 0, 0)),
        out_shape=jax.ShapeDtypeStruct((2, B, G, N_CMP_PAD, NSA_HD), BF16),
        compiler_params=pltpu.CompilerParams(
            dimension_semantics=("parallel", "parallel"), vmem_limit_bytes=VMEM_LIMIT),
        name="compress",
    )(u3, u3, *params)


ROPE_PAIR_LANE = LANES // 2


def _head_dim_order():
    half = ROPE_DIM // 2
    return [(0, half), (ROPE_DIM, ROPE_PAIR_LANE + half), (half, ROPE_DIM), (ROPE_PAIR_LANE + half, NSA_HD)]


def _rope(x, cos, sin_signed):
    return x * cos + pltpu.roll(x, ROPE_PAIR_LANE, 1) * sin_signed


def _nsa_kernel(n_side, side_slabs, *refs):
    n_in = 10
    (qc_ref, qr_ref, misc_ref, kc_ref, vc_ref, ksa_s, vsa_s, kwr_s, vwa_s, ovt_ref) = refs[0:n_in]
    side_in = refs[n_in:n_in + n_side]
    o_ref = refs[n_in + n_side]
    side_out = refs[n_in + 1 + n_side:n_in + 1 + 2 * n_side]
    chain_scratch = refs[n_in + 1 + 2 * n_side:]
    S = ksa_s.shape[0]
    R, TQ, HD = NSA_HPG, NSA_TQ, NSA_HD
    ss_s, es_s = chain_scratch[0:NSA_SUB], chain_scratch[NSA_SUB:2 * NSA_SUB]
    sw_s, ew_s = chain_scratch[2 * NSA_SUB:2 * NSA_SUB + R // 2], chain_scratch[2 * NSA_SUB + R // 2:]
    g = pl.program_id(1)
    qi = pl.program_id(2)

    step_idx = (pl.program_id(0) * pl.num_programs(1) + g) * pl.num_programs(2) + qi
    for w_in_ref, w_out_ref, n_slabs in zip(side_in, side_out, side_slabs):
        def cast(w_in_ref=w_in_ref, w_out_ref=w_out_ref):
            w_out_ref[...] = w_in_ref[...].astype(BF16)
        if n_slabs is None:
            cast()
        else:
            pl.when(step_idx < n_slabs)(cast)

    nb = S // SLC_BLOCK
    TK = NSA_TK
    ovt = ovt_ref[...]
    kc, vc = kc_ref[...], vc_ref[...]

    def front(v, u):
        tq0 = (v * NSA_SUB + u) * TQ
        urows = slice(u * TQ, (u + 1) * TQ)
        qc = jnp.concatenate([qc_ref[urows, r * HD:(r + 1) * HD] for r in range(R)], axis=0)
        qr_heads = [qr_ref[urows, r * HD:(r + 1) * HD] for r in range(R)]
        t_tok = tq0 + lax.broadcasted_iota(jnp.int32, (TQ, 1), 0)
        t_col = jnp.concatenate([t_tok] * R, axis=0)

        s_c = _nt_dot(qc, kc)
        n_idx = lax.broadcasted_iota(jnp.int32, (1, N_CMP_PAD), 1)
        p_c = _masked_softmax(s_c, (n_idx * CMP_STRIDE + (CMP_BLOCK - 1)) <= t_col)
        p_cb = p_c.astype(BF16)
        o_cmp = _dot(p_cb, vc)

        imp = _nt_dot(ovt, p_cb[0:TQ])
        for r in range(1, R):
            imp = imp + _nt_dot(ovt, p_cb[r * TQ:(r + 1) * TQ])
        j_idx = lax.broadcasted_iota(jnp.int32, (nb, TQ), 0)
        cur = (tq0 + lax.broadcasted_iota(jnp.int32, (nb, TQ), 1)) // SLC_BLOCK
        imp = jnp.where(j_idx <= cur, imp, INVALID_SCORE)
        imp = jnp.where((j_idx == 0) | (j_idx == cur) | (j_idx == cur - 1), FORCED_SCORE, imp)
        rank = jnp.zeros((nb, TQ), jnp.int32)
        for i in range(nb):
            ri = imp[i:i + 1, :]
            ahead = (ri > imp) | ((ri == imp) & (i < j_idx))
            rank = rank + jnp.where(ahead, 1, 0)
        blk_bias_t = jnp.where(rank < min(SLC_TOPK, nb), 0.0, MASKED)
        blk_bias_t = jnp.concatenate([blk_bias_t, jnp.zeros((LANES - nb, TQ), F32)], axis=0)
        blk_bias = jnp.transpose(blk_bias_t).astype(BF16)
        q_aug = jnp.concatenate([jnp.concatenate([h, blk_bias], axis=1) for h in qr_heads], axis=0)
        return dict(t_col=t_col, o_cmp=o_cmp, q_aug=q_aug, qr_heads=qr_heads)

    step = NSA_SUB * TQ
    n_pair = R // 2

    def step_body(v):
        subs = [front(v, u) for u in range(NSA_SUB)]
        n_keys = (v + 1) * step
        start = max(v * step - WINDOW, 0)
        span = n_keys - start
        diff = (v * step + lax.broadcasted_iota(jnp.int32, (step, 1), 0)) - (
            start + lax.broadcasted_iota(jnp.int32, (1, span), 1))
        band = jnp.where((diff >= 0) & (diff < WINDOW), 0.0, MASKED)

        def win_scores(c):
            q_pair = jnp.concatenate(
                [subs[u]["qr_heads"][2 * c + rr] for rr in range(2) for u in range(NSA_SUB)], axis=0)
            s_w = _nt_dot(q_pair, kwr_s[start:start + span, :])
            for rr in range(2):
                sw_s[c][rr * step:(rr + 1) * step, 0:span] = s_w[rr * step:(rr + 1) * step] + band

        def win_softmax(c):
            m_w = jnp.max(sw_s[c][:, 0:span], axis=-1, keepdims=True)
            ew_s[c][:, 0:span] = jnp.exp2(sw_s[c][:, 0:span] - m_w).astype(BF16)

        def win_values(c):
            acc_w = _dot(ew_s[c][:, 0:span], vwa_s[start:start + span, :])
            return acc_w[:, 0:HD] * (1.0 / acc_w[:, HD:])

        def sel_scores(u):
            q_aug, t_col = subs[u]["q_aug"], subs[u]["t_col"]
            for k0 in range(0, n_keys, TK):
                w = min(TK, n_keys - k0)
                s = _nt_dot(q_aug, ksa_s[k0:k0 + w, :])
                if k0 + w == n_keys:
                    kpos = k0 + lax.broadcasted_iota(jnp.int32, (1, w), 1)
                    s = jnp.where(kpos <= t_col, s, MASKED)
                ss_s[u][:, k0:k0 + w] = s

        def sel_softmax(u):
            m = jnp.max(ss_s[u][:, 0:n_keys], axis=-1, keepdims=True)
            es_s[u][:, 0:n_keys] = jnp.exp2(ss_s[u][:, 0:n_keys] - m).astype(BF16)

        def sel_values(u):
            acc = _dot(es_s[u][:, 0:n_keys], vsa_s[0:n_keys, :])
            return acc[:, 0:HD] * (1.0 / acc[:, HD:])

        assert NSA_SUB == 2 and n_pair == 2
        win_scores(0)
        win_scores(1)
        sel_scores(0)
        win_softmax(0)
        sel_scores(1)
        win_softmax(1)
        o_win = [win_values(0)]
        sel_softmax(0)
        o_win.append(win_values(1))
        sel_softmax(1)
        o_slc = [sel_values(0), sel_values(1)]

        misc = misc_ref[...]
        ng = 3 * R
        logits = jnp.where(g == 0, misc[:, MISC_GATE_LANE:MISC_GATE_LANE + ng],
                           misc[:, MISC_GATE_LANE + ng:MISC_GATE_LANE + 2 * ng])
        gate = jax.nn.sigmoid(logits)
        for u in range(NSA_SUB):
            urows = slice(u * TQ, (u + 1) * TQ)
            for r in range(R):
                rs = slice(r * TQ, (r + 1) * TQ)
                ws = slice((r % 2) * step + u * TQ, (r % 2) * step + (u + 1) * TQ)
                o = (gate[urows, 3 * r:3 * r + 1] * subs[u]["o_cmp"][rs]
                     + gate[urows, 3 * r + 1:3 * r + 2] * o_slc[u][rs]
                     + gate[urows, 3 * r + 2:3 * r + 3] * o_win[r // 2][ws])
                o_ref[urows, r * HD:(r + 1) * HD] = o.astype(o_ref.dtype)

    for v in range(S // step):
        pl.when(qi == v)(functools.partial(step_body, v))


BF16_ROWS = 16


def _side_cast_plan(weights, n_steps):
    plan = []
    for w in weights:
        n_rows = w.shape[0]
        if n_rows % (n_steps * BF16_ROWS) == 0:
            plan.append((n_rows // n_steps, None))
        else:
            assert n_rows % LANES == 0 and n_rows // LANES <= n_steps
            plan.append((LANES, n_rows // LANES))
    return plan


def _nsa(u3, cmp_kv, q_ops, kv_ops, ovt, side_weights):
    B, S, _ = u3.shape
    G, R, HD = NSA_KV_GROUPS, NSA_HPG, NSA_HD
    rows = R * NSA_TQ
    step = NSA_SUB * NSA_TQ
    assert NSA_TK % step == 0
    n_q = S // step
    plan = _side_cast_plan(side_weights, B * G * n_q)

    def side_spec(w, slab_rows, n_slabs):
        def index(b, g, i):
            s = (b * G + g) * n_q + i
            return (s if n_slabs is None else jnp.minimum(s, n_slabs - 1), 0)
        return pl.BlockSpec((slab_rows, w.shape[1]), index)

    side_specs = [side_spec(w, *p) for w, p in zip(side_weights, plan)]
    full = lambda a: pl.BlockSpec((None, S, a.shape[2] // G), lambda b, g, i: (b, 0, g))
    q_spec = pl.BlockSpec((None, step, R * HD), lambda b, g, i: (b, i, g))
    kern = functools.partial(_nsa_kernel, len(side_weights), tuple(p[1] for p in plan))
    return pl.pallas_call(
        kern,
        grid=(B, G, n_q),
        in_specs=[
            q_spec, q_spec,
            pl.BlockSpec((None, step, LANES), lambda b, g, i: (b, i, UB_MISC)),
            pl.BlockSpec((None, None, None, N_CMP_PAD, HD), lambda b, g, i: (0, b, g, 0, 0)),
            pl.BlockSpec((None, None, None, N_CMP_PAD, HD), lambda b, g, i: (1, b, g, 0, 0)),
            full(kv_ops[0]), full(kv_ops[1]), full(kv_ops[2]), full(kv_ops[3]),
            pl.BlockSpec((S // SLC_BLOCK, N_CMP_PAD), lambda b, g, i: (0, 0)),
        ] + side_specs,
        out_specs=[q_spec] + side_specs,
        out_shape=[jax.ShapeDtypeStruct((B, S, NSA_WIDTH), BF16)]
        + [jax.ShapeDtypeStruct(w.shape, BF16) for w in side_weights],
        scratch_shapes=[pltpu.VMEM((rows, S), F32)] * NSA_SUB + [pltpu.VMEM((rows, S), BF16)] * NSA_SUB
        + [pltpu.VMEM((2 * step, WINDOW + step), F32)] * (R // 2)
        + [pltpu.VMEM((2 * step, WINDOW + step), BF16)] * (R // 2),
        compiler_params=pltpu.CompilerParams(
            dimension_semantics=("arbitrary", "arbitrary", "arbitrary"), vmem_limit_bytes=VMEM_LIMIT),
        name="nsa",
    )(*q_ops, u3, cmp_kv, cmp_kv, *kv_ops, ovt, *side_weights)


def _out_proj_kernel(og_ref, on_ref, x_ref, w_ref, g_ref, b_ref, h_ref, hb_ref):
    half = og_ref.shape[0] // 2
    halves = [slice(0, half), slice(half, 2 * half)]
    mixes = [_dot(og_ref[rows, :], w_ref[0:GLA_WIDTH, :]) + _dot(on_ref[rows, :], w_ref[GLA_WIDTH:, :])
             for rows in halves]
    for rows, mix in zip(halves, mixes):
        h = _layer_norm(DEEPNORM_ALPHA * x_ref[rows, :] + mix, g_ref[...], b_ref[...])
        h_ref[rows, :] = h
        hb_ref[rows, :] = h.astype(BF16)


def _out_proj(o_gla, o_nsa, x2, w_bf, g, b):
    T, D = x2.shape
    tile = lambda w: pl.BlockSpec((OUT_TM, w), lambda i: (i, 0))
    const = lambda s: pl.BlockSpec(s, lambda i: (0, 0))
    return pl.pallas_call(
        _out_proj_kernel,
        grid=(T // OUT_TM,),
        in_specs=[tile(GLA_WIDTH), tile(NSA_WIDTH), tile(D), const(w_bf.shape), const((1, D)), const((1, D))],
        out_specs=[tile(D), tile(D)],
        out_shape=[jax.ShapeDtypeStruct((T, D), F32), jax.ShapeDtypeStruct((T, D), BF16)],
        compiler_params=pltpu.CompilerParams(
            dimension_semantics=("parallel",), vmem_limit_bytes=VMEM_LIMIT),
        name="out_proj",
    )(o_gla, o_nsa, x2, w_bf, g, b)


def _ffn_kernel(hb_ref, h_ref, w1_ref, w3_ref, w2_ref, g_ref, b_ref, o_ref, acc_s):
    f = pl.program_id(1)

    @pl.when(f == 0)
    def _():
        acc_s[...] = jnp.zeros(acc_s.shape, F32)

    hb = hb_ref[...]
    a = _dot(hb, w1_ref[...])
    c = _dot(hb, w3_ref[...])
    acc_s[...] += _dot((a * jax.nn.sigmoid(a) * c).astype(BF16), w2_ref[...])

    @pl.when(f == pl.num_programs(1) - 1)
    def _():
        o_ref[...] = _layer_norm(DEEPNORM_ALPHA * h_ref[...] + acc_s[...], g_ref[...], b_ref[...])


def _ffn(h_bf, h, w1, w3, w2, g, b):
    T, D = h.shape
    F = w1.shape[1]
    return pl.pallas_call(
        _ffn_kernel,
        grid=(T // FFN_TM, F // FFN_TF),
        in_specs=[
            pl.BlockSpec((FFN_TM, D), lambda i, f: (i, 0)),
            pl.BlockSpec((FFN_TM, D), lambda i, f: (i, 0)),
            pl.BlockSpec((D, FFN_TF), lambda i, f: (0, f)),
            pl.BlockSpec((D, FFN_TF), lambda i, f: (0, f)),
            pl.BlockSpec((FFN_TF, D), lambda i, f: (f, 0)),
            pl.BlockSpec((1, D), lambda i, f: (0, 0)),
            pl.BlockSpec((1, D), lambda i, f: (0, 0)),
        ],
        out_specs=pl.BlockSpec((FFN_TM, D), lambda i, f: (i, 0)),
        out_shape=jax.ShapeDtypeStruct((T, D), F32),
        scratch_shapes=[pltpu.VMEM((FFN_TM, D), F32)],
        compiler_params=pltpu.CompilerParams(
            dimension_semantics=("parallel", "arbitrary"), vmem_limit_bytes=VMEM_LIMIT),
        name="ffn",
    )(h_bf, h, w1, w3, w2, g, b)


def _rope_tables(S):
    half = ROPE_DIM // 2
    pos = np.arange(S, dtype=np.float64)
    inv = np.power(ROPE_THETA, -np.arange(0, ROPE_DIM, 2, dtype=np.float64) / ROPE_DIM)
    ang = pos[:, None] * inv[None, :]
    cos_f = np.ones((S, LANES), np.float32)
    sin_s = np.zeros((S, LANES), np.float32)
    for lane0, sign in ((0, -1.0), (ROPE_PAIR_LANE, 1.0)):
        cos_f[:, lane0:lane0 + half] = np.cos(ang)
        sin_s[:, lane0:lane0 + half] = sign * np.sin(ang)
    return jnp.asarray(cos_f), jnp.asarray(sin_s)


def _selection_constants(S):
    n_cmp = (S - CMP_BLOCK) // CMP_STRIDE + 1
    nb = S // SLC_BLOCK
    c_start = np.arange(n_cmp) * CMP_STRIDE
    b_start = np.arange(nb) * SLC_BLOCK
    overlap = ((c_start[:, None] < b_start[None, :] + SLC_BLOCK) &
               (b_start[None, :] < c_start[:, None] + CMP_BLOCK)).astype(np.float32)
    ovt = np.zeros((nb, N_CMP_PAD), np.float32)
    ovt[:, :n_cmp] = overlap.T
    blk_onehot = ((np.arange(S) // SLC_BLOCK)[:, None] == np.arange(LANES)[None, :]).astype(np.float32)
    return jnp.asarray(ovt, BF16), jnp.asarray(blk_onehot, BF16)


def kernel(x, w_in, gla_gate_w2, gla_gate_b2, gla_norm_w, cmp_k_pos, cmp_k_w1, cmp_k_w2,
           cmp_v_pos, cmp_v_w1, cmp_v_w2, w_out, ln1_g, ln1_b, ffn_w1, ffn_w3, ffn_w2, ln2_g, ln2_b):
    B, S, D = x.shape
    T = B * S
    cos_f, sin_s = _rope_tables(S)
    ovt, blk_onehot = _selection_constants(S)

    x2 = x.reshape(T, D)
    for l in range(DEPTH):
        wt_gla, wt_nsa = _regroup_w_in(w_in[l].T)
        ua3 = _in_proj(x2, wt_gla).reshape(B, S, -1)
        u_nsa, *attn_ops = _in_proj_nsa(x2, wt_nsa, S, cos_f, sin_s, blk_onehot)
        u3 = u_nsa.reshape(B, S, -1)
        attn_ops = [a.reshape(B, S, -1) for a in attn_ops]
        q_ops, kv_ops = attn_ops[0:2], attn_ops[2:]
        o_gla = _gla(ua3, u3, gla_gate_w2[l], gla_gate_b2[l][None, :], gla_norm_w[l][None, :])
        cmp_kv = _compress(u3, (cmp_k_pos[l], cmp_k_w1[l], cmp_k_w2[l]), (cmp_v_pos[l], cmp_v_w1[l], cmp_v_w2[l]))
        o_nsa, w_out_bf, w1_bf, w3_bf, w2_bf = _nsa(u3, cmp_kv, q_ops, kv_ops, ovt,
                                                    [w_out[l], ffn_w1[l], ffn_w3[l], ffn_w2[l]])
        h, h_bf = _out_proj(o_gla.reshape(T, GLA_WIDTH), o_nsa.reshape(T, NSA_WIDTH), x2,
                            w_out_bf, ln1_g[l][None, :], ln1_b[l][None, :])
        x2 = _ffn(h_bf, h, w1_bf, w3_bf, w2_bf, ln2_g[l][None, :], ln2_b[l][None, :])
    return x2.reshape(B, S, D)
```

```python
import functools

import numpy as np
import jax
import jax.numpy as jnp
from jax import lax
from jax.experimental import pallas as pl
from jax.experimental.pallas import tpu as pltpu

F32 = jnp.float32
BF16 = jnp.bfloat16

D_MODEL = 2048
DEPTH = 1
GLA_HEADS = 4
GLA_DK = 128
GLA_DV = 256
GLA_GATE_RANK = 16
GLA_GATE_NORM = 16.0
GLA_CHUNK = 64
NSA_HEADS = 8
NSA_HD = 128
NSA_KV_GROUPS = 2
NSA_HPG = NSA_HEADS // NSA_KV_GROUPS
CMP_BLOCK = 32
CMP_STRIDE = 16
CMP_HIDDEN = 2 * NSA_HD
SLC_BLOCK = 64
SLC_TOPK = 16
WINDOW = 512
ROPE_THETA = 500000.0
ROPE_DIM = NSA_HD // 4
FFN_HIDDEN = 5632
DEEPNORM_ALPHA = (2.0 * DEPTH) ** 0.25
LN_EPS = 1e-5
FORCED_SCORE = 1e4
INVALID_SCORE = -1e4

GLA_WIDTH = GLA_HEADS * GLA_DV
NSA_WIDTH = NSA_HEADS * NSA_HD

LANES = 128
VMEM_LIMIT = 56 * 1024 * 1024
CB_GQ = 0
CB_GK = 4
CB_GV = 8
CB_GO = 16
CB_NQ = 24
CB_KC = 32
CB_KS = 36
CB_VS = 38
CB_KW = 40
CB_VW = 42
CB_MISC = 44
MISC_GATE_LANE = GLA_GATE_RANK
UB_MISC = 2 * NSA_KV_GROUPS
U_WIDTH = 45 * LANES

REGROUP_TC = 256
IN_TM = 512
OUT_TM = 512
FFN_TM, FFN_TF = 512, 512
NSA_TQ = 128
NSA_SUB = 2
NSA_TK = 1024
WIN_SPAN = WINDOW + NSA_TQ
MASKED = -1e30
LOG2E = 1.4426950408889634


def _nt_dot(a, b):
    return lax.dot_general(a, b, (((1,), (1,)), ((), ())), preferred_element_type=F32)


def _tn_dot(a, b):
    return lax.dot_general(a, b, (((0,), (0,)), ((), ())), preferred_element_type=F32)


def _dot(a, b):
    return jnp.dot(a, b, preferred_element_type=F32)


def _layer_norm(z, g, b):
    mu = jnp.mean(z, axis=-1, keepdims=True)
    zc = z - mu
    var = jnp.mean(zc * zc, axis=-1, keepdims=True)
    return zc * lax.rsqrt(var + LN_EPS) * g + b


def _masked_softmax(s, mask):
    sm = jnp.where(mask, s, -jnp.inf)
    m = jnp.max(sm, axis=-1, keepdims=True)
    m = jnp.where(m == -jnp.inf, 0.0, m)
    e = jnp.where(mask, jnp.exp(s - m), 0.0)
    den = jnp.sum(e, axis=-1, keepdims=True)
    return e * (1.0 / jnp.where(den > 0, den, 1.0))


def _regroup_kernel(w_ref, oa_ref, ob_ref):
    cols = w_ref.shape[1]
    o_glr = CB_NQ * LANES
    o_nq = o_glr + GLA_GATE_RANK
    o_gate = o_nq + (CB_MISC - CB_NQ) * LANES
    n_gate = NSA_HEADS * 3
    oa_ref[...] = w_ref[0:o_glr, :].astype(BF16)
    roped = (list(range(CB_NQ, CB_NQ + NSA_HEADS)) + list(range(CB_KS, CB_KS + NSA_KV_GROUPS))
             + list(range(CB_KW, CB_KW + NSA_KV_GROUPS)))
    for blk in range(CB_NQ, CB_MISC):
        dst, src = (blk - CB_NQ) * LANES, blk * LANES + GLA_GATE_RANK
        pieces = _head_dim_order() if blk in roped else [(0, NSA_HD)]
        at = 0
        for lo, hi in pieces:
            ob_ref[dst + at:dst + at + hi - lo, :] = w_ref[src + lo:src + hi, :].astype(BF16)
            at += hi - lo
    misc = jnp.concatenate([w_ref[o_glr:o_nq, :], w_ref[o_gate:o_gate + n_gate, :],
                            jnp.zeros((LANES - GLA_GATE_RANK - n_gate, cols), F32)], axis=0)
    ob_ref[(CB_MISC - CB_NQ) * LANES:, :] = misc.astype(BF16)


def _regroup_w_in(wt):
    n_in, D = wt.shape
    na, nb = CB_NQ * LANES, U_WIDTH - CB_NQ * LANES
    return pl.pallas_call(
        _regroup_kernel,
        grid=(D // REGROUP_TC,),
        in_specs=[pl.BlockSpec((n_in, REGROUP_TC), lambda i: (0, i))],
        out_specs=[pl.BlockSpec((na, REGROUP_TC), lambda i: (0, i)), pl.BlockSpec((nb, REGROUP_TC), lambda i: (0, i))],
        out_shape=[jax.ShapeDtypeStruct((na, D), BF16), jax.ShapeDtypeStruct((nb, D), BF16)],
        compiler_params=pltpu.CompilerParams(
            dimension_semantics=("parallel",), vmem_limit_bytes=VMEM_LIMIT),
        name="regroup_w_in",
    )(wt)


def _in_proj_kernel(x_ref, w_ref, o_ref):
    o_ref[...] = _nt_dot(x_ref[...].astype(BF16), w_ref[...])


def _in_proj(x2, wt_bf):
    T, D = x2.shape
    N = wt_bf.shape[0]
    return pl.pallas_call(
        _in_proj_kernel,
        grid=(T // IN_TM,),
        in_specs=[pl.BlockSpec((IN_TM, D), lambda i: (i, 0)),
                  pl.BlockSpec((N, D), lambda i: (0, 0))],
        out_specs=pl.BlockSpec((IN_TM, N), lambda i: (i, 0)),
        out_shape=jax.ShapeDtypeStruct((T, N), F32),
        compiler_params=pltpu.CompilerParams(
            dimension_semantics=("parallel",), vmem_limit_bytes=VMEM_LIMIT),
        name="in_proj",
    )(x2, wt_bf)


def _in_proj_nsa_kernel(x_ref, w_ref, cos_ref, sin_ref, blk_ref,
                        u_ref, qc_ref, qr_ref, ksa_ref, vsa_ref, kwr_ref, vwa_ref):
    xb = x_ref[...].astype(BF16)
    res_q = _nt_dot(xb, w_ref[0:NSA_WIDTH, :])
    res = _nt_dot(xb, w_ref[NSA_WIDTH:, :])
    cos, sin = cos_ref[...], sin_ref[...]
    scale = NSA_HD ** -0.5
    for h in range(NSA_HEADS):
        cols = slice(h * NSA_HD, (h + 1) * NSA_HD)
        qc_ref[:, cols] = (res_q[:, cols] * scale).astype(BF16)
        qr_ref[:, cols] = _rope(res_q[:, cols] * (scale * LOG2E), cos, sin).astype(BF16)
    first = CB_NQ + NSA_HEADS
    col = lambda cb, g: slice((cb - first + g) * LANES, (cb - first + g + 1) * LANES)
    n_cmp_cols = 2 * NSA_KV_GROUPS * NSA_HD
    u_ref[:, 0:n_cmp_cols] = res[:, col(CB_KC, 0).start:col(CB_KC, 0).start + n_cmp_cols]
    u_ref[:, n_cmp_cols:] = res[:, col(CB_MISC, 0)]
    ones = jnp.ones((res.shape[0], LANES), BF16)
    for g in range(NSA_KV_GROUPS):
        lo, hi = slice(2 * g * LANES, (2 * g + 1) * LANES), slice((2 * g + 1) * LANES, (2 * g + 2) * LANES)
        ksa_ref[:, lo] = _rope(res[:, col(CB_KS, g)], cos, sin).astype(BF16)
        ksa_ref[:, hi] = blk_ref[...]
        vsa_ref[:, lo] = res[:, col(CB_VS, g)].astype(BF16)
        vsa_ref[:, hi] = ones
        kwr_ref[:, g * LANES:(g + 1) * LANES] = _rope(res[:, col(CB_KW, g)], cos, sin).astype(BF16)
        vwa_ref[:, lo] = res[:, col(CB_VW, g)].astype(BF16)
        vwa_ref[:, hi] = ones


def _in_proj_nsa(x2, wt_bf, S, cos_f, sin_s, blk_onehot):
    T, D = x2.shape
    N = wt_bf.shape[0]
    G = NSA_KV_GROUPS
    per_seq = S // IN_TM
    table = pl.BlockSpec((IN_TM, LANES), lambda i: (i % per_seq, 0))
    rows = lambda w: pl.BlockSpec((IN_TM, w), lambda i: (i, 0))
    widths = [(UB_MISC + 1) * LANES, NSA_WIDTH, NSA_WIDTH, 2 * G * LANES, 2 * G * LANES, G * LANES, 2 * G * LANES]
    return pl.pallas_call(
        _in_proj_nsa_kernel,
        grid=(T // IN_TM,),
        in_specs=[rows(D), pl.BlockSpec((N, D), lambda i: (0, 0)), table, table, table],
        out_specs=[rows(w) for w in widths],
        out_shape=[jax.ShapeDtypeStruct((T, widths[0]), F32)]
        + [jax.ShapeDtypeStruct((T, w), BF16) for w in widths[1:]],
        compiler_params=pltpu.CompilerParams(
            dimension_semantics=("parallel",), vmem_limit_bytes=VMEM_LIMIT),
        name="in_proj_nsa",
    )(x2, wt_bf, cos_f, sin_s, blk_onehot)


GLA_ROWBLK = 256
GLA_TS = 512


def _split3(a):
    hi = a.astype(BF16)
    r1 = a - hi.astype(F32)
    mid = r1.astype(BF16)
    lo = (r1 - mid.astype(F32)).astype(BF16)
    return hi, mid, lo


def _log_sigmoid(z):
    return jnp.minimum(z, 0.0) - jnp.log(1.0 + jnp.exp(-jnp.abs(z)))


def _gla_kernel(q_ref, k_ref, v_ref, go_ref, misc_ref, w2_ref, b2_ref, nw_ref, o_ref,
                qd_s, ki_s, ks_s, dec_s, state_s, sbf_s):
    TS = q_ref.shape[0]
    H, C, DK, DV = GLA_HEADS, GLA_CHUNK, GLA_DK, GLA_DV
    scale = DK ** -0.5

    @pl.when(pl.program_id(1) == 0)
    def _():
        state_s[...] = jnp.zeros(state_s.shape, F32)

    r = lax.broadcasted_iota(jnp.int32, (GLA_ROWBLK, GLA_ROWBLK), 0)
    c = lax.broadcasted_iota(jnp.int32, (GLA_ROWBLK, GLA_ROWBLK), 1)
    chunk_causal = ((r // C) == (c // C)) & (c <= r)
    cum_m = jnp.where(chunk_causal, 1.0, 0.0).astype(BF16)

    w2 = w2_ref[...].astype(BF16)
    b2 = b2_ref[...]
    for rb in range(TS // GLA_ROWBLK):
        rows = slice(rb * GLA_ROWBLK, (rb + 1) * GLA_ROWBLK)
        glr = misc_ref[rows, 0:GLA_GATE_RANK].astype(BF16)
        gk = _log_sigmoid(_dot(glr, w2) + b2) * (1.0 / GLA_GATE_NORM)
        hi, mid, lo = _split3(gk)
        bc_all = _dot(cum_m, hi) + _dot(cum_m, mid) + _dot(cum_m, lo)
        bl_all = jnp.concatenate(
            [jnp.broadcast_to(bc_all[j * C + C - 1:j * C + C, :], (C, H * DK)) for j in range(GLA_ROWBLK // C)],
            axis=0)
        for h in range(H):
            hk = slice(h * DK, (h + 1) * DK)
            bc = bc_all[:, hk]
            bl = bl_all[:, hk]
            q = q_ref[rows, hk] * scale
            k = k_ref[rows, hk]
            qd_s[h, rows, :] = (q * jnp.exp(bc)).astype(BF16)
            ki_s[h, rows, :] = (k * jnp.exp(-bc)).astype(BF16)
            ks_s[h, rows, :] = (k * jnp.exp(bl - bc)).astype(BF16)
            dec_s[h, rows, :] = jnp.exp(bl)

    n_chunk = TS // C
    nw = nw_ref[...]
    for h in range(H):
        hv = slice(h * DV, (h + 1) * DV)
        kv = [_tn_dot(ks_s[h, c * C:(c + 1) * C, :], v_ref[c * C:(c + 1) * C, hv].astype(BF16))
              for c in range(n_chunk)]
        dec_rows = jnp.concatenate([dec_s[h, c * C:c * C + 1, :] for c in range(n_chunk)]
                                   + [jnp.zeros((LANES - n_chunk, DK), F32)], axis=0)
        dec_cols = jnp.transpose(dec_rows)
        state = state_s[h]
        for c in range(n_chunk):
            sbf_s[h, c] = state.astype(BF16)
            state = state * dec_cols[:, c:c + 1] + kv[c]
        state_s[h] = state

    for h in range(H):
        hv = slice(h * DV, (h + 1) * DV)
        for rb in range(TS // GLA_ROWBLK):
            rows = slice(rb * GLA_ROWBLK, (rb + 1) * GLA_ROWBLK)
            qd = qd_s[h, rows, :]
            attn = jnp.where(chunk_causal, _nt_dot(qd, ki_s[h, rows, :]), 0.0).astype(BF16)
            o = _dot(attn, v_ref[rows, hv].astype(BF16))
            inter = [_dot(qd_s[h, c * C:(c + 1) * C, :], sbf_s[h, c])
                     for c in range(rb * GLA_ROWBLK // C, (rb + 1) * GLA_ROWBLK // C)]
            o = o + jnp.concatenate(inter, axis=0)
            ms = jnp.mean(o * o, axis=-1, keepdims=True)
            o = o * lax.rsqrt(ms + LN_EPS) * nw
            go = go_ref[rows, hv]
            o_ref[rows, hv] = (o * (go * jax.nn.sigmoid(go))).astype(o_ref.dtype)


def _gla(ua3, ub3, gate_w2, gate_b2, norm_w):
    B, S, _ = ua3.shape
    H, DK, DV = GLA_HEADS, GLA_DK, GLA_DV
    qk_w, v_w = H * DK, H * DV
    tok = lambda w, off: pl.BlockSpec((None, GLA_TS, w), lambda b, s: (b, s, off * LANES // w))
    const = lambda shape: pl.BlockSpec(shape, lambda b, s: (0, 0))
    return pl.pallas_call(
        _gla_kernel,
        grid=(B, S // GLA_TS),
        in_specs=[tok(qk_w, CB_GQ), tok(qk_w, CB_GK), tok(v_w, CB_GV), tok(v_w, CB_GO), tok(LANES, UB_MISC),
                  const((GLA_GATE_RANK, qk_w)), const((1, qk_w)), const((1, DV))],
        out_specs=pl.BlockSpec((None, GLA_TS, v_w), lambda b, s: (b, s, 0)),
        out_shape=jax.ShapeDtypeStruct((B, S, GLA_WIDTH), BF16),
        scratch_shapes=[pltpu.VMEM((H, GLA_TS, DK), BF16), pltpu.VMEM((H, GLA_TS, DK), BF16),
                        pltpu.VMEM((H, GLA_TS, DK), BF16), pltpu.VMEM((H, GLA_TS, DK), F32),
                        pltpu.VMEM((H, DK, DV), F32), pltpu.VMEM((H, GLA_TS // GLA_CHUNK, DK, DV), BF16)],
        compiler_params=pltpu.CompilerParams(
            dimension_semantics=("parallel", "arbitrary"), vmem_limit_bytes=VMEM_LIMIT),
        name="gla",
    )(ua3, ua3, ua3, ua3, ub3, gate_w2, gate_b2, norm_w)


N_CMP_PAD = 128


def _gelu_tanh(x):
    return x * (0.5 * (1.0 + jnp.tanh(0.7978845608028654 * (x + 0.044715 * (x * x * x)))))


def _compress_kernel(k_ref, v_ref, kpos_ref, vpos_ref, kw1_ref, vw1_ref, kw2_ref, vw2_ref, o_ref):
    half = CMP_BLOCK // 2
    chains = ((k_ref, kpos_ref, kw1_ref, kw2_ref, _head_dim_order()),
              (v_ref, vpos_ref, vw1_ref, vw2_ref, [(0, NSA_HD)]))
    for kv, (src_ref, pos_ref, w1_ref, w2_ref, out_order) in enumerate(chains):
        p0 = jnp.zeros((N_CMP_PAD, CMP_HIDDEN), F32)
        p1 = jnp.zeros((N_CMP_PAD, CMP_HIDDEN), F32)
        for l in range(half):
            x = src_ref[pl.ds(l, N_CMP_PAD, stride=CMP_STRIDE), :]
            a0 = (x + pos_ref[l:l + 1, :]).astype(BF16)
            a1 = (x + pos_ref[half + l:half + l + 1, :]).astype(BF16)
            p0 = p0 + _dot(a0, w1_ref[l * NSA_HD:(l + 1) * NSA_HD, :].astype(BF16))
            p1 = p1 + _dot(a1, w1_ref[(half + l) * NSA_HD:(half + l + 1) * NSA_HD, :].astype(BF16))
        pre = p0 + pltpu.roll(p1, N_CMP_PAD - 1, 0)
        h = _gelu_tanh(pre).astype(BF16)
        w2 = jnp.concatenate([w2_ref[:, lo:hi] for lo, hi in out_order], axis=1).astype(BF16)
        out = _dot(h, w2)
        row = lax.broadcasted_iota(jnp.int32, out.shape, 0)
        o_ref[kv] = jnp.where(row < N_CMP_PAD - 1, out, 0.0).astype(o_ref.dtype)


def _compress(u3, k_params, v_params):
    B, S, _ = u3.shape
    G = NSA_KV_GROUPS
    src = lambda kv: pl.BlockSpec((None, S, NSA_HD), lambda b, g: (b, 0, NSA_KV_GROUPS * kv + g))
    whole = lambda a: pl.BlockSpec(a.shape, lambda b, g: (0,) * a.ndim)
    params = [k_params[0], v_params[0], k_params[1], v_params[1], k_params[2], v_params[2]]
    return pl.pallas_call(
        _compress_kernel,
        grid=(B, G),
        in_specs=[src(0), src(1)] + [whole(a) for a in params],
        out_specs=pl.BlockSpec((2, None, None, N_CMP_PAD, NSA_HD), lambda b, g: (0, b, g, 0, 0)),
        out_shape=jax.ShapeDtypeStruct((2, B, G, N_CMP_PAD, NSA_HD), BF16),
        compiler_params=pltpu.CompilerParams(
            dimension_semantics=("parallel", "parallel"), vmem_limit_bytes=VMEM_LIMIT),
        name="compress",
    )(u3, u3, *params)


ROPE_PAIR_LANE = LANES // 2


def _head_dim_order():
    half = ROPE_DIM // 2
    return [(0, half), (ROPE_DIM, ROPE_PAIR_LANE + half), (half, ROPE_DIM), (ROPE_PAIR_LANE + half, NSA_HD)]


def _rope(x, cos, sin_signed):
    return x * cos + pltpu.roll(x, ROPE_PAIR_LANE, 1) * sin_signed


def _nsa_kernel(n_side, side_slabs, *refs):
    n_in = 10
    (qc_ref, qr_ref, misc_ref, kc_ref, vc_ref, ksa_s, vsa_s, kwr_s, vwa_s, ovt_ref) = refs[0:n_in]
    side_in = refs[n_in:n_in + n_side]
    o_ref = refs[n_in + n_side]
    side_out = refs[n_in + 1 + n_side:n_in + 1 + 2 * n_side]
    chain_scratch = refs[n_in + 1 + 2 * n_side:]
    S = ksa_s.shape[0]
    R, TQ, HD = NSA_HPG, NSA_TQ, NSA_HD
    ss_s, es_s = chain_scratch[0:NSA_SUB], chain_scratch[NSA_SUB:2 * NSA_SUB]
    sw_s, ew_s = chain_scratch[2 * NSA_SUB:2 * NSA_SUB + R // 2], chain_scratch[2 * NSA_SUB + R // 2:]
    g = pl.program_id(1)
    qi = pl.program_id(2)

    step_idx = (pl.program_id(0) * pl.num_programs(1) + g) * pl.num_programs(2) + qi
    for w_in_ref, w_out_ref, n_slabs in zip(side_in, side_out, side_slabs):
        def cast(w_in_ref=w_in_ref, w_out_ref=w_out_ref):
            w_out_ref[...] = w_in_ref[...].astype(BF16)
        if n_slabs is None:
            cast()
        else:
            pl.when(step_idx < n_slabs)(cast)

    nb = S // SLC_BLOCK
    TK = NSA_TK
    ovt = ovt_ref[...]
    kc, vc = kc_ref[...], vc_ref[...]

    def front(v, u):
        tq0 = (v * NSA_SUB + u) * TQ
        urows = slice(u * TQ, (u + 1) * TQ)
        qc = jnp.concatenate([qc_ref[urows, r * HD:(r + 1) * HD] for r in range(R)], axis=0)
        qr_heads = [qr_ref[urows, r * HD:(r + 1) * HD] for r in range(R)]
        t_tok = tq0 + lax.broadcasted_iota(jnp.int32, (TQ, 1), 0)
        t_col = jnp.concatenate([t_tok] * R, axis=0)

        s_c = _nt_dot(qc, kc)
        n_idx = lax.broadcasted_iota(jnp.int32, (1, N_CMP_PAD), 1)
        p_c = _masked_softmax(s_c, (n_idx * CMP_STRIDE + (CMP_BLOCK - 1)) <= t_col)
        p_cb = p_c.astype(BF16)
        o_cmp = _dot(p_cb, vc)

        imp = _nt_dot(ovt, p_cb[0:TQ])
        for r in range(1, R):
            imp = imp + _nt_dot(ovt, p_cb[r * TQ:(r + 1) * TQ])
        j_idx = lax.broadcasted_iota(jnp.int32, (nb, TQ), 0)
        cur = (tq0 + lax.broadcasted_iota(jnp.int32, (nb, TQ), 1)) // SLC_BLOCK
        imp = jnp.where(j_idx <= cur, imp, INVALID_SCORE)
        imp = jnp.where((j_idx == 0) | (j_idx == cur) | (j_idx == cur - 1), FORCED_SCORE, imp)
        rank = jnp.zeros((nb, TQ), jnp.int32)
        for i in range(nb):
            ri = imp[i:i + 1, :]
            ahead = (ri > imp) | ((ri == imp) & (i < j_idx))
            rank = rank + jnp.where(ahead, 1, 0)
        blk_bias_t = jnp.where(rank < min(SLC_TOPK, nb), 0.0, MASKED)
        blk_bias_t = jnp.concatenate([blk_bias_t, jnp.zeros((LANES - nb, TQ), F32)], axis=0)
        blk_bias = jnp.transpose(blk_bias_t).astype(BF16)
        q_aug = jnp.concatenate([jnp.concatenate([h, blk_bias], axis=1) for h in qr_heads], axis=0)
        return dict(t_col=t_col, o_cmp=o_cmp, q_aug=q_aug, qr_heads=qr_heads)

    step = NSA_SUB * TQ
    n_pair = R // 2

    def step_body(v):
        subs = [front(v, u) for u in range(NSA_SUB)]
        n_keys = (v + 1) * step
        start = max(v * step - WINDOW, 0)
        span = n_keys - start
        diff = (v * step + lax.broadcasted_iota(jnp.int32, (step, 1), 0)) - (
            start + lax.broadcasted_iota(jnp.int32, (1, span), 1))
        band = jnp.where((diff >= 0) & (diff < WINDOW), 0.0, MASKED)

        def win_scores(c):
            q_pair = jnp.concatenate(
                [subs[u]["qr_heads"][2 * c + rr] for rr in range(2) for u in range(NSA_SUB)], axis=0)
            s_w = _nt_dot(q_pair, kwr_s[start:start + span, :])
            for rr in range(2):
                sw_s[c][rr * step:(rr + 1) * step, 0:span] = s_w[rr * step:(rr + 1) * step] + band

        def win_softmax(c):
            m_w = jnp.max(sw_s[c][:, 0:span], axis=-1, keepdims=True)
            ew_s[c][:, 0:span] = jnp.exp2(sw_s[c][:, 0:span] - m_w).astype(BF16)

        def win_values(c):
            acc_w = _dot(ew_s[c][:, 0:span], vwa_s[start:start + span, :])
            return acc_w[:, 0:HD] * (1.0 / acc_w[:, HD:])

        def sel_scores(u):
            q_aug, t_col = subs[u]["q_aug"], subs[u]["t_col"]
            for k0 in range(0, n_keys, TK):
                w = min(TK, n_keys - k0)
                s = _nt_dot(q_aug, ksa_s[k0:k0 + w, :])
                if k0 + w == n_keys:
                    kpos = k0 + lax.broadcasted_iota(jnp.int32, (1, w), 1)
                    s = jnp.where(kpos <= t_col, s, MASKED)
                ss_s[u][:, k0:k0 + w] = s

        def sel_softmax(u):
            m = jnp.max(ss_s[u][:, 0:n_keys], axis=-1, keepdims=True)
            es_s[u][:, 0:n_keys] = jnp.exp2(ss_s[u][:, 0:n_keys] - m).astype(BF16)

        def sel_values(u):
            acc = _dot(es_s[u][:, 0:n_keys], vsa_s[0:n_keys, :])
            return acc[:, 0:HD] * (1.0 / acc[:, HD:])

        assert NSA_SUB == 2 and n_pair == 2
        win_scores(0)
        win_scores(1)
        sel_scores(0)
        win_softmax(0)
        sel_scores(1)
        win_softmax(1)
        o_win = [win_values(0)]
        sel_softmax(0)
        o_win.append(win_values(1))
        sel_softmax(1)
        o_slc = [sel_values(0), sel_values(1)]

        misc = misc_ref[...]
        ng = 3 * R
        logits = jnp.where(g == 0, misc[:, MISC_GATE_LANE:MISC_GATE_LANE + ng],
                           misc[:, MISC_GATE_LANE + ng:MISC_GATE_LANE + 2 * ng])
        gate = jax.nn.sigmoid(logits)
        for u in range(NSA_SUB):
            urows = slice(u * TQ, (u + 1) * TQ)
            for r in range(R):
                rs = slice(r * TQ, (r + 1) * TQ)
                ws = slice((r % 2) * step + u * TQ, (r % 2) * step + (u + 1) * TQ)
                o = (gate[urows, 3 * r:3 * r + 1] * subs[u]["o_cmp"][rs]
                     + gate[urows, 3 * r + 1:3 * r + 2] * o_slc[u][rs]
                     + gate[urows, 3 * r + 2:3 * r + 3] * o_win[r // 2][ws])
                o_ref[urows, r * HD:(r + 1) * HD] = o.astype(o_ref.dtype)

    for v in range(S // step):
        pl.when(qi == v)(functools.partial(step_body, v))


BF16_ROWS = 16


def _side_cast_plan(weights, n_steps):
    plan = []
    for w in weights:
        n_rows = w.shape[0]
        if n_rows % (n_steps * BF16_ROWS) == 0:
            plan.append((n_rows // n_steps, None))
        else:
            assert n_rows % LANES == 0 and n_rows // LANES <= n_steps
            plan.append((LANES, n_rows // LANES))
    return plan


def _nsa(u3, cmp_kv, q_ops, kv_ops, ovt, side_weights):
    B, S, _ = u3.shape
    G, R, HD = NSA_KV_GROUPS, NSA_HPG, NSA_HD
    rows = R * NSA_TQ
    step = NSA_SUB * NSA_TQ
    assert NSA_TK % step == 0
    n_q = S // step
    plan = _side_cast_plan(side_weights, B * G * n_q)

    def side_spec(w, slab_rows, n_slabs):
        def index(b, g, i):
            s = (b * G + g) * n_q + i
            return (s if n_slabs is None else jnp.minimum(s, n_slabs - 1), 0)
        return pl.BlockSpec((slab_rows, w.shape[1]), index)

    side_specs = [side_spec(w, *p) for w, p in zip(side_weights, plan)]
    full = lambda a: pl.BlockSpec((None, S, a.shape[2] // G), lambda b, g, i: (b, 0, g))
    q_spec = pl.BlockSpec((None, step, R * HD), lambda b, g, i: (b, i, g))
    kern = functools.partial(_nsa_kernel, len(side_weights), tuple(p[1] for p in plan))
    return pl.pallas_call(
        kern,
        grid=(B, G, n_q),
        in_specs=[
            q_spec, q_spec,
            pl.BlockSpec((None, step, LANES), lambda b, g, i: (b, i, UB_MISC)),
            pl.BlockSpec((None, None, None, N_CMP_PAD, HD), lambda b, g, i: (0, b, g, 0, 0)),
            pl.BlockSpec((None, None, None, N_CMP_PAD, HD), lambda b, g, i: (1, b, g, 0, 0)),
            full(kv_ops[0]), full(kv_ops[1]), full(kv_ops[2]), full(kv_ops[3]),
            pl.BlockSpec((S // SLC_BLOCK, N_CMP_PAD), lambda b, g, i: (0, 0)),
        ] + side_specs,
        out_specs=[q_spec] + side_specs,
        out_shape=[jax.ShapeDtypeStruct((B, S, NSA_WIDTH), BF16)]
        + [jax.ShapeDtypeStruct(w.shape, BF16) for w in side_weights],
        scratch_shapes=[pltpu.VMEM((rows, S), F32)] * NSA_SUB + [pltpu.VMEM((rows, S), BF16)] * NSA_SUB
        + [pltpu.VMEM((2 * step, WINDOW + step), F32)] * (R // 2)
        + [pltpu.VMEM((2 * step, WINDOW + step), BF16)] * (R // 2),
        compiler_params=pltpu.CompilerParams(
            dimension_semantics=("arbitrary", "arbitrary", "arbitrary"), vmem_limit_bytes=VMEM_LIMIT),
        name="nsa",
    )(*q_ops, u3, cmp_kv, cmp_kv, *kv_ops, ovt, *side_weights)


def _out_proj_kernel(og_ref, on_ref, x_ref, w_ref, g_ref, b_ref, h_ref, hb_ref):
    half = og_ref.shape[0] // 2
    halves = [slice(0, half), slice(half, 2 * half)]
    mixes = [_dot(og_ref[rows, :], w_ref[0:GLA_WIDTH, :]) + _dot(on_ref[rows, :], w_ref[GLA_WIDTH:, :])
             for rows in halves]
    for rows, mix in zip(halves, mixes):
        h = _layer_norm(DEEPNORM_ALPHA * x_ref[rows, :] + mix, g_ref[...], b_ref[...])
        h_ref[rows, :] = h
        hb_ref[rows, :] = h.astype(BF16)


def _out_proj(o_gla, o_nsa, x2, w_bf, g, b):
    T, D = x2.shape
    tile = lambda w: pl.BlockSpec((OUT_TM, w), lambda i: (i, 0))
    const = lambda s: pl.BlockSpec(s, lambda i: (0, 0))
    return pl.pallas_call(
        _out_proj_kernel,
        grid=(T // OUT_TM,),
        in_specs=[tile(GLA_WIDTH), tile(NSA_WIDTH), tile(D), const(w_bf.shape), const((1, D)), const((1, D))],
        out_specs=[tile(D), tile(D)],
        out_shape=[jax.ShapeDtypeStruct((T, D), F32), jax.ShapeDtypeStruct((T, D), BF16)],
        compiler_params=pltpu.CompilerParams(
            dimension_semantics=("parallel",), vmem_limit_bytes=VMEM_LIMIT),
        name="out_proj",
    )(o_gla, o_nsa, x2, w_bf, g, b)


def _ffn_kernel(hb_ref, h_ref, w1_ref, w3_ref, w2_ref, g_ref, b_ref, o_ref, acc_s):
    f = pl.program_id(1)

    @pl.when(f == 0)
    def _():
        acc_s[...] = jnp.zeros(acc_s.shape, F32)

    hb = hb_ref[...]
    a = _dot(hb, w1_ref[...])
    c = _dot(hb, w3_ref[...])
    acc_s[...] += _dot((a * jax.nn.sigmoid(a) * c).astype(BF16), w2_ref[...])

    @pl.when(f == pl.num_programs(1) - 1)
    def _():
        o_ref[...] = _layer_norm(DEEPNORM_ALPHA * h_ref[...] + acc_s[...], g_ref[...], b_ref[...])


def _ffn(h_bf, h, w1, w3, w2, g, b):
    T, D = h.shape
    F = w1.shape[1]
    return pl.pallas_call(
        _ffn_kernel,
        grid=(T // FFN_TM, F // FFN_TF),
        in_specs=[
            pl.BlockSpec((FFN_TM, D), lambda i, f: (i, 0)),
            pl.BlockSpec((FFN_TM, D), lambda i, f: (i, 0)),
            pl.BlockSpec((D, FFN_TF), lambda i, f: (0, f)),
            pl.BlockSpec((D, FFN_TF), lambda i, f: (0, f)),
            pl.BlockSpec((FFN_TF, D), lambda i, f: (f, 0)),
            pl.BlockSpec((1, D), lambda i, f: (0, 0)),
            pl.BlockSpec((1, D), lambda i, f: (0, 0)),
        ],
        out_specs=pl.BlockSpec((FFN_TM, D), lambda i, f: (i, 0)),
        out_shape=jax.ShapeDtypeStruct((T, D), F32),
        scratch_shapes=[pltpu.VMEM((FFN_TM, D), F32)],
        compiler_params=pltpu.CompilerParams(
            dimension_semantics=("parallel", "arbitrary"), vmem_limit_bytes=VMEM_LIMIT),
        name="ffn",
    )(h_bf, h, w1, w3, w2, g, b)


def _rope_tables(S):
    half = ROPE_DIM // 2
    pos = np.arange(S, dtype=np.float64)
    inv = np.power(ROPE_THETA, -np.arange(0, ROPE_DIM, 2, dtype=np.float64) / ROPE_DIM)
    ang = pos[:, None] * inv[None, :]
    cos_f = np.ones((S, LANES), np.float32)
    sin_s = np.zeros((S, LANES), np.float32)
    for lane0, sign in ((0, -1.0), (ROPE_PAIR_LANE, 1.0)):
        cos_f[:, lane0:lane0 + half] = np.cos(ang)
        sin_s[:, lane0:lane0 + half] = sign * np.sin(ang)
    return jnp.asarray(cos_f), jnp.asarray(sin_s)


def _selection_constants(S):
    n_cmp = (S - CMP_BLOCK) // CMP_STRIDE + 1
    nb = S // SLC_BLOCK
    c_start = np.arange(n_cmp) * CMP_STRIDE
    b_start = np.arange(nb) * SLC_BLOCK
    overlap = ((c_start[:, None] < b_start[None, :] + SLC_BLOCK) &
               (b_start[None, :] < c_start[:, None] + CMP_BLOCK)).astype(np.float32)
    ovt = np.zeros((nb, N_CMP_PAD), np.float32)
    ovt[:, :n_cmp] = overlap.T
    blk_onehot = ((np.arange(S) // SLC_BLOCK)[:, None] == np.arange(LANES)[None, :]).astype(np.float32)
    return jnp.asarray(ovt, BF16), jnp.asarray(blk_onehot, BF16)


def kernel(x, w_in, gla_gate_w2, gla_gate_b2, gla_norm_w, cmp_k_pos, cmp_k_w1, cmp_k_w2,
           cmp_v_pos, cmp_v_w1, cmp_v_w2, w_out, ln1_g, ln1_b, ffn_w1, ffn_w3, ffn_w2, ln2_g, ln2_b):
    B, S, D = x.shape
    T = B * S
    cos_f, sin_s = _rope_tables(S)
    ovt, blk_onehot = _selection_constants(S)

    x2 = x.reshape(T, D)
    for l in range(DEPTH):
        wt_gla, wt_nsa = _regroup_w_in(w_in[l].T)
        ua3 = _in_proj(x2, wt_gla).reshape(B, S, -1)
        u_nsa, *attn_ops = _in_proj_nsa(x2, wt_nsa, S, cos_f, sin_s, blk_onehot)
        u3 = u_nsa.reshape(B, S, -1)
        attn_ops = [a.reshape(B, S, -1) for a in attn_ops]
        q_ops, kv_ops = attn_ops[0:2], attn_ops[2:]
        o_gla = _gla(ua3, u3, gla_gate_w2[l], gla_gate_b2[l][None, :], gla_norm_w[l][None, :])
        cmp_kv = _compress(u3, (cmp_k_pos[l], cmp_k_w1[l], cmp_k_w2[l]), (cmp_v_pos[l], cmp_v_w1[l], cmp_v_w2[l]))
        o_nsa, w_out_bf, w1_bf, w3_bf, w2_bf = _nsa(u3, cmp_kv, q_ops, kv_ops, ovt,
                                                    [w_out[l], ffn_w1[l], ffn_w3[l], ffn_w2[l]])
        h, h_bf = _out_proj(o_gla.reshape(T, GLA_WIDTH), o_nsa.reshape(T, NSA_WIDTH), x2,
                            w_out_bf, ln1_g[l][None, :], ln1_b[l][None, :])
        x2 = _ffn(h_bf, h, w1_bf, w3_bf, w2_bf, ln2_g[l][None, :], ln2_b[l][None, :])
    return x2.reshape(B, S, D)
```

```python
import functools

import numpy as np
import jax
import jax.numpy as jnp
from jax import lax
from jax.experimental import pallas as pl
from jax.experimental.pallas import tpu as pltpu

F32 = jnp.float32
BF16 = jnp.bfloat16

D_MODEL = 2048
DEPTH = 1
GLA_HEADS = 4
GLA_DK = 128
GLA_DV = 256
GLA_GATE_RANK = 16
GLA_GATE_NORM = 16.0
GLA_CHUNK = 64
NSA_HEADS = 8
NSA_HD = 128
NSA_KV_GROUPS = 2
NSA_HPG = NSA_HEADS // NSA_KV_GROUPS
CMP_BLOCK = 32
CMP_STRIDE = 16
CMP_HIDDEN = 2 * NSA_HD
SLC_BLOCK = 64
SLC_TOPK = 16
WINDOW = 512
ROPE_THETA = 500000.0
ROPE_DIM = NSA_HD // 4
FFN_HIDDEN = 5632
DEEPNORM_ALPHA = (2.0 * DEPTH) ** 0.25
LN_EPS = 1e-5
FORCED_SCORE = 1e4
INVALID_SCORE = -1e4

GLA_WIDTH = GLA_HEADS * GLA_DV
NSA_WIDTH = NSA_HEADS * NSA_HD

LANES = 128
VMEM_LIMIT = 56 * 1024 * 1024
CB_GQ = 0
CB_GK = 4
CB_GV = 8
CB_GO = 16
CB_NQ = 24
CB_KC = 32
CB_KS = 36
CB_VS = 38
CB_KW = 40
CB_VW = 42
CB_MISC = 44
MISC_GATE_LANE = GLA_GATE_RANK
UB_MISC = 2 * NSA_KV_GROUPS
U_WIDTH = 45 * LANES

REGROUP_TC = 256
IN_TM = 512
OUT_TM = 512
FFN_TM, FFN_TF = 1024, 512
NSA_TQ = 128
NSA_SUB = 2
NSA_TK = 512
WIN_SPAN = WINDOW + NSA_TQ
MASKED = -1e30
LOG2E = 1.4426950408889634


def _nt_dot(a, b):
    return lax.dot_general(a, b, (((1,), (1,)), ((), ())), preferred_element_type=F32)


def _tn_dot(a, b):
    return lax.dot_general(a, b, (((0,), (0,)), ((), ())), preferred_element_type=F32)


def _dot(a, b):
    return jnp.dot(a, b, preferred_element_type=F32)


def _layer_norm(z, g, b):
    mu = jnp.mean(z, axis=-1, keepdims=True)
    zc = z - mu
    var = jnp.mean(zc * zc, axis=-1, keepdims=True)
    return zc * lax.rsqrt(var + LN_EPS) * g + b


def _masked_softmax(s, mask):
    sm = jnp.where(mask, s, -jnp.inf)
    m = jnp.max(sm, axis=-1, keepdims=True)
    m = jnp.where(m == -jnp.inf, 0.0, m)
    e = jnp.where(mask, jnp.exp(s - m), 0.0)
    den = jnp.sum(e, axis=-1, keepdims=True)
    return e * (1.0 / jnp.where(den > 0, den, 1.0))


def _regroup_kernel(w_ref, oa_ref, ob_ref):
    cols = w_ref.shape[1]
    o_glr = CB_NQ * LANES
    o_nq = o_glr + GLA_GATE_RANK
    o_gate = o_nq + (CB_MISC - CB_NQ) * LANES
    n_gate = NSA_HEADS * 3
    oa_ref[...] = w_ref[0:o_glr, :].astype(BF16)
    roped = (list(range(CB_NQ, CB_NQ + NSA_HEADS)) + list(range(CB_KS, CB_KS + NSA_KV_GROUPS))
             + list(range(CB_KW, CB_KW + NSA_KV_GROUPS)))
    for blk in range(CB_NQ, CB_MISC):
        dst, src = (blk - CB_NQ) * LANES, blk * LANES + GLA_GATE_RANK
        pieces = _head_dim_order() if blk in roped else [(0, NSA_HD)]
        at = 0
        for lo, hi in pieces:
            ob_ref[dst + at:dst + at + hi - lo, :] = w_ref[src + lo:src + hi, :].astype(BF16)
            at += hi - lo
    misc = jnp.concatenate([w_ref[o_glr:o_nq, :], w_ref[o_gate:o_gate + n_gate, :],
                            jnp.zeros((LANES - GLA_GATE_RANK - n_gate, cols), F32)], axis=0)
    ob_ref[(CB_MISC - CB_NQ) * LANES:, :] = misc.astype(BF16)


def _regroup_w_in(wt):
    n_in, D = wt.shape
    na, nb = CB_NQ * LANES, U_WIDTH - CB_NQ * LANES
    return pl.pallas_call(
        _regroup_kernel,
        grid=(D // REGROUP_TC,),
        in_specs=[pl.BlockSpec((n_in, REGROUP_TC), lambda i: (0, i))],
        out_specs=[pl.BlockSpec((na, REGROUP_TC), lambda i: (0, i)), pl.BlockSpec((nb, REGROUP_TC), lambda i: (0, i))],
        out_shape=[jax.ShapeDtypeStruct((na, D), BF16), jax.ShapeDtypeStruct((nb, D), BF16)],
        compiler_params=pltpu.CompilerParams(
            dimension_semantics=("parallel",), vmem_limit_bytes=VMEM_LIMIT),
        name="regroup_w_in",
    )(wt)


def _in_proj_kernel(x_ref, w_ref, o_ref):
    o_ref[...] = _nt_dot(x_ref[...].astype(BF16), w_ref[...])


def _in_proj(x2, wt_bf):
    T, D = x2.shape
    N = wt_bf.shape[0]
    return pl.pallas_call(
        _in_proj_kernel,
        grid=(T // IN_TM,),
        in_specs=[pl.BlockSpec((IN_TM, D), lambda i: (i, 0)),
                  pl.BlockSpec((N, D), lambda i: (0, 0))],
        out_specs=pl.BlockSpec((IN_TM, N), lambda i: (i, 0)),
        out_shape=jax.ShapeDtypeStruct((T, N), F32),
        compiler_params=pltpu.CompilerParams(
            dimension_semantics=("parallel",), vmem_limit_bytes=VMEM_LIMIT),
        name="in_proj",
    )(x2, wt_bf)


def _in_proj_nsa_kernel(x_ref, w_ref, cos_ref, sin_ref, blk_ref,
                        u_ref, qc_ref, qr_ref, ksa_ref, vsa_ref, kwr_ref, vwa_ref):
    xb = x_ref[...].astype(BF16)
    res_q = _nt_dot(xb, w_ref[0:NSA_WIDTH, :])
    res = _nt_dot(xb, w_ref[NSA_WIDTH:, :])
    cos, sin = cos_ref[...], sin_ref[...]
    scale = NSA_HD ** -0.5
    for h in range(NSA_HEADS):
        cols = slice(h * NSA_HD, (h + 1) * NSA_HD)
        qc_ref[:, cols] = (res_q[:, cols] * scale).astype(BF16)
        qr_ref[:, cols] = _rope(res_q[:, cols] * (scale * LOG2E), cos, sin).astype(BF16)
    first = CB_NQ + NSA_HEADS
    col = lambda cb, g: slice((cb - first + g) * LANES, (cb - first + g + 1) * LANES)
    n_cmp_cols = 2 * NSA_KV_GROUPS * NSA_HD
    u_ref[:, 0:n_cmp_cols] = res[:, col(CB_KC, 0).start:col(CB_KC, 0).start + n_cmp_cols]
    u_ref[:, n_cmp_cols:] = res[:, col(CB_MISC, 0)]
    ones = jnp.ones((res.shape[0], LANES), BF16)
    for g in range(NSA_KV_GROUPS):
        lo, hi = slice(2 * g * LANES, (2 * g + 1) * LANES), slice((2 * g + 1) * LANES, (2 * g + 2) * LANES)
        ksa_ref[:, lo] = _rope(res[:, col(CB_KS, g)], cos, sin).astype(BF16)
        ksa_ref[:, hi] = blk_ref[...]
        vsa_ref[:, lo] = res[:, col(CB_VS, g)].astype(BF16)
        vsa_ref[:, hi] = ones
        kwr_ref[:, g * LANES:(g + 1) * LANES] = _rope(res[:, col(CB_KW, g)], cos, sin).astype(BF16)
        vwa_ref[:, lo] = res[:, col(CB_VW, g)].astype(BF16)
        vwa_ref[:, hi] = ones


def _in_proj_nsa(x2, wt_bf, S, cos_f, sin_s, blk_onehot):
    T, D = x2.shape
    N = wt_bf.shape[0]
    G = NSA_KV_GROUPS
    per_seq = S // IN_TM
    table = pl.BlockSpec((IN_TM, LANES), lambda i: (i % per_seq, 0))
    rows = lambda w: pl.BlockSpec((IN_TM, w), lambda i: (i, 0))
    widths = [(UB_MISC + 1) * LANES, NSA_WIDTH, NSA_WIDTH, 2 * G * LANES, 2 * G * LANES, G * LANES, 2 * G * LANES]
    return pl.pallas_call(
        _in_proj_nsa_kernel,
        grid=(T // IN_TM,),
        in_specs=[rows(D), pl.BlockSpec((N, D), lambda i: (0, 0)), table, table, table],
        out_specs=[rows(w) for w in widths],
        out_shape=[jax.ShapeDtypeStruct((T, widths[0]), F32)]
        + [jax.ShapeDtypeStruct((T, w), BF16) for w in widths[1:]],
        compiler_params=pltpu.CompilerParams(
            dimension_semantics=("parallel",), vmem_limit_bytes=VMEM_LIMIT),
        name="in_proj_nsa",
    )(x2, wt_bf, cos_f, sin_s, blk_onehot)


GLA_ROWBLK = 256
GLA_TS = 512


def _split3(a):
    hi = a.astype(BF16)
    r1 = a - hi.astype(F32)
    mid = r1.astype(BF16)
    lo = (r1 - mid.astype(F32)).astype(BF16)
    return hi, mid, lo


def _log_sigmoid(z):
    return jnp.minimum(z, 0.0) - jnp.log(1.0 + jnp.exp(-jnp.abs(z)))


def _gla_kernel(q_ref, k_ref, v_ref, go_ref, misc_ref, w2_ref, b2_ref, nw_ref, o_ref,
                qd_s, ki_s, ks_s, dec_s, state_s, sbf_s):
    TS = q_ref.shape[0]
    H, C, DK, DV = GLA_HEADS, GLA_CHUNK, GLA_DK, GLA_DV
    scale = DK ** -0.5

    @pl.when(pl.program_id(1) == 0)
    def _():
        state_s[...] = jnp.zeros(state_s.shape, F32)

    r = lax.broadcasted_iota(jnp.int32, (GLA_ROWBLK, GLA_ROWBLK), 0)
    c = lax.broadcasted_iota(jnp.int32, (GLA_ROWBLK, GLA_ROWBLK), 1)
    chunk_causal = ((r // C) == (c // C)) & (c <= r)
    cum_m = jnp.where(chunk_causal, 1.0, 0.0).astype(BF16)

    w2 = w2_ref[...].astype(BF16)
    b2 = b2_ref[...]
    for rb in range(TS // GLA_ROWBLK):
        rows = slice(rb * GLA_ROWBLK, (rb + 1) * GLA_ROWBLK)
        glr = misc_ref[rows, 0:GLA_GATE_RANK].astype(BF16)
        gk = _log_sigmoid(_dot(glr, w2) + b2) * (1.0 / GLA_GATE_NORM)
        hi, mid, lo = _split3(gk)
        bc_all = _dot(cum_m, hi) + _dot(cum_m, mid) + _dot(cum_m, lo)
        bl_all = jnp.concatenate(
            [jnp.broadcast_to(bc_all[j * C + C - 1:j * C + C, :], (C, H * DK)) for j in range(GLA_ROWBLK // C)],
            axis=0)
        for h in range(H):
            hk = slice(h * DK, (h + 1) * DK)
            bc = bc_all[:, hk]
            bl = bl_all[:, hk]
            q = q_ref[rows, hk] * scale
            k = k_ref[rows, hk]
            qd_s[h, rows, :] = (q * jnp.exp(bc)).astype(BF16)
            ki_s[h, rows, :] = (k * jnp.exp(-bc)).astype(BF16)
            ks_s[h, rows, :] = (k * jnp.exp(bl - bc)).astype(BF16)
            dec_s[h, rows, :] = jnp.exp(bl)

    n_chunk = TS // C
    nw = nw_ref[...]
    for h in range(H):
        hv = slice(h * DV, (h + 1) * DV)
        kv = [_tn_dot(ks_s[h, c * C:(c + 1) * C, :], v_ref[c * C:(c + 1) * C, hv].astype(BF16))
              for c in range(n_chunk)]
        dec_rows = jnp.concatenate([dec_s[h, c * C:c * C + 1, :] for c in range(n_chunk)]
                                   + [jnp.zeros((LANES - n_chunk, DK), F32)], axis=0)
        dec_cols = jnp.transpose(dec_rows)
        state = state_s[h]
        for c in range(n_chunk):
            sbf_s[h, c] = state.astype(BF16)
            state = state * dec_cols[:, c:c + 1] + kv[c]
        state_s[h] = state

    for h in range(H):
        hv = slice(h * DV, (h + 1) * DV)
        for rb in range(TS // GLA_ROWBLK):
            rows = slice(rb * GLA_ROWBLK, (rb + 1) * GLA_ROWBLK)
            qd = qd_s[h, rows, :]
            attn = jnp.where(chunk_causal, _nt_dot(qd, ki_s[h, rows, :]), 0.0).astype(BF16)
            o = _dot(attn, v_ref[rows, hv].astype(BF16))
            inter = [_dot(qd_s[h, c * C:(c + 1) * C, :], sbf_s[h, c])
                     for c in range(rb * GLA_ROWBLK // C, (rb + 1) * GLA_ROWBLK // C)]
            o = o + jnp.concatenate(inter, axis=0)
            ms = jnp.mean(o * o, axis=-1, keepdims=True)
            o = o * lax.rsqrt(ms + LN_EPS) * nw
            go = go_ref[rows, hv]
            o_ref[rows, hv] = (o * (go * jax.nn.sigmoid(go))).astype(o_ref.dtype)


def _gla(ua3, ub3, gate_w2, gate_b2, norm_w):
    B, S, _ = ua3.shape
    H, DK, DV = GLA_HEADS, GLA_DK, GLA_DV
    qk_w, v_w = H * DK, H * DV
    tok = lambda w, off: pl.BlockSpec((None, GLA_TS, w), lambda b, s: (b, s, off * LANES // w))
    const = lambda shape: pl.BlockSpec(shape, lambda b, s: (0, 0))
    return pl.pallas_call(
        _gla_kernel,
        grid=(B, S // GLA_TS),
        in_specs=[tok(qk_w, CB_GQ), tok(qk_w, CB_GK), tok(v_w, CB_GV), tok(v_w, CB_GO), tok(LANES, UB_MISC),
                  const((GLA_GATE_RANK, qk_w)), const((1, qk_w)), const((1, DV))],
        out_specs=pl.BlockSpec((None, GLA_TS, v_w), lambda b, s: (b, s, 0)),
        out_shape=jax.ShapeDtypeStruct((B, S, GLA_WIDTH), BF16),
        scratch_shapes=[pltpu.VMEM((H, GLA_TS, DK), BF16), pltpu.VMEM((H, GLA_TS, DK), BF16),
                        pltpu.VMEM((H, GLA_TS, DK), BF16), pltpu.VMEM((H, GLA_TS, DK), F32),
                        pltpu.VMEM((H, DK, DV), F32), pltpu.VMEM((H, GLA_TS // GLA_CHUNK, DK, DV), BF16)],
        compiler_params=pltpu.CompilerParams(
            dimension_semantics=("parallel", "arbitrary"), vmem_limit_bytes=VMEM_LIMIT),
        name="gla",
    )(ua3, ua3, ua3, ua3, ub3, gate_w2, gate_b2, norm_w)


N_CMP_PAD = 128


def _gelu_tanh(x):
    return x * (0.5 * (1.0 + jnp.tanh(0.7978845608028654 * (x + 0.044715 * (x * x * x)))))


def _compress_kernel(k_ref, v_ref, kpos_ref, vpos_ref, kw1_ref, vw1_ref, kw2_ref, vw2_ref, o_ref):
    half = CMP_BLOCK // 2
    chains = ((k_ref, kpos_ref, kw1_ref, kw2_ref, _head_dim_order()),
              (v_ref, vpos_ref, vw1_ref, vw2_ref, [(0, NSA_HD)]))
    for kv, (src_ref, pos_ref, w1_ref, w2_ref, out_order) in enumerate(chains):
        p0 = jnp.zeros((N_CMP_PAD, CMP_HIDDEN), F32)
        p1 = jnp.zeros((N_CMP_PAD, CMP_HIDDEN), F32)
        for l in range(half):
            x = src_ref[pl.ds(l, N_CMP_PAD, stride=CMP_STRIDE), :]
            a0 = (x + pos_ref[l:l + 1, :]).astype(BF16)
            a1 = (x + pos_ref[half + l:half + l + 1, :]).astype(BF16)
            p0 = p0 + _dot(a0, w1_ref[l * NSA_HD:(l + 1) * NSA_HD, :].astype(BF16))
            p1 = p1 + _dot(a1, w1_ref[(half + l) * NSA_HD:(half + l + 1) * NSA_HD, :].astype(BF16))
        pre = p0 + pltpu.roll(p1, N_CMP_PAD - 1, 0)
        h = _gelu_tanh(pre).astype(BF16)
        w2 = jnp.concatenate([w2_ref[:, lo:hi] for lo, hi in out_order], axis=1).astype(BF16)
        out = _dot(h, w2)
        row = lax.broadcasted_iota(jnp.int32, out.shape, 0)
        o_ref[kv] = jnp.where(row < N_CMP_PAD - 1, out, 0.0).astype(o_ref.dtype)


def _compress(u3, k_params, v_params):
    B, S, _ = u3.shape
    G = NSA_KV_GROUPS
    src = lambda kv: pl.BlockSpec((None, S, NSA_HD), lambda b, g: (b, 0, NSA_KV_GROUPS * kv + g))
    whole = lambda a: pl.BlockSpec(a.shape, lambda b, g: (0,) * a.ndim)
    params = [k_params[0], v_params[0], k_params[1], v_params[1], k_params[2], v_params[2]]
    return pl.pallas_call(
        _compress_kernel,
        grid=(B, G),
        in_specs=[src(0), src(1)] + [whole(a) for a in params],
        out_specs=pl.BlockSpec((2, None, None, N_CMP_PAD, NSA_HD), lambda b, g: (0, b, g, 0, 0)),
        out_shape=jax.ShapeDtypeStruct((2, B, G, N_CMP_PAD, NSA_HD), BF16),
        compiler_params=pltpu.CompilerParams(
            dimension_semantics=("parallel", "parallel"), vmem_limit_bytes=VMEM_LIMIT),
        name="compress",
    )(u3, u3, *params)


ROPE_PAIR_LANE = LANES // 2


def _head_dim_order():
    half = ROPE_DIM // 2
    return [(0, half), (ROPE_DIM, ROPE_PAIR_LANE + half), (half, ROPE_DIM), (ROPE_PAIR_LANE + half, NSA_HD)]


def _rope(x, cos, sin_signed):
    return x * cos + pltpu.roll(x, ROPE_PAIR_LANE, 1) * sin_signed


def _nsa_kernel(n_side, side_slabs, *refs):
    n_in = 10
    (qc_ref, qr_ref, misc_ref, kc_ref, vc_ref, ksa_s, vsa_s, kwr_s, vwa_s, ovt_ref) = refs[0:n_in]
    side_in = refs[n_in:n_in + n_side]
    o_ref = refs[n_in + n_side]
    side_out = refs[n_in + 1 + n_side:n_in + 1 + 2 * n_side]
    chain_scratch = refs[n_in + 1 + 2 * n_side:]
    S = ksa_s.shape[0]
    R, TQ, HD = NSA_HPG, NSA_TQ, NSA_HD
    ss_s, es_s = chain_scratch[0:NSA_SUB], chain_scratch[NSA_SUB:2 * NSA_SUB]
    sw_s, ew_s = chain_scratch[2 * NSA_SUB:2 * NSA_SUB + R // 2], chain_scratch[2 * NSA_SUB + R // 2:]
    g = pl.program_id(1)
    qi = pl.program_id(2)

    step_idx = (pl.program_id(0) * pl.num_programs(1) + g) * pl.num_programs(2) + qi
    for w_in_ref, w_out_ref, n_slabs in zip(side_in, side_out, side_slabs):
        def cast(w_in_ref=w_in_ref, w_out_ref=w_out_ref):
            w_out_ref[...] = w_in_ref[...].astype(BF16)
        if n_slabs is None:
            cast()
        else:
            pl.when(step_idx < n_slabs)(cast)

    nb = S // SLC_BLOCK
    TK = NSA_TK
    ovt = ovt_ref[...]
    kc, vc = kc_ref[...], vc_ref[...]

    def front(v, u):
        tq0 = (v * NSA_SUB + u) * TQ
        urows = slice(u * TQ, (u + 1) * TQ)
        qc = jnp.concatenate([qc_ref[urows, r * HD:(r + 1) * HD] for r in range(R)], axis=0)
        qr_heads = [qr_ref[urows, r * HD:(r + 1) * HD] for r in range(R)]
        t_tok = tq0 + lax.broadcasted_iota(jnp.int32, (TQ, 1), 0)
        t_col = jnp.concatenate([t_tok] * R, axis=0)

        s_c = _nt_dot(qc, kc)
        n_idx = lax.broadcasted_iota(jnp.int32, (1, N_CMP_PAD), 1)
        p_c = _masked_softmax(s_c, (n_idx * CMP_STRIDE + (CMP_BLOCK - 1)) <= t_col)
        p_cb = p_c.astype(BF16)
        o_cmp = _dot(p_cb, vc)

        imp = _nt_dot(ovt, p_cb[0:TQ])
        for r in range(1, R):
            imp = imp + _nt_dot(ovt, p_cb[r * TQ:(r + 1) * TQ])
        j_idx = lax.broadcasted_iota(jnp.int32, (nb, TQ), 0)
        cur = (tq0 + lax.broadcasted_iota(jnp.int32, (nb, TQ), 1)) // SLC_BLOCK
        imp = jnp.where(j_idx <= cur, imp, INVALID_SCORE)
        imp = jnp.where((j_idx == 0) | (j_idx == cur) | (j_idx == cur - 1), FORCED_SCORE, imp)
        rank = jnp.zeros((nb, TQ), jnp.int32)
        for i in range(nb):
            ri = imp[i:i + 1, :]
            ahead = (ri > imp) | ((ri == imp) & (i < j_idx))
            rank = rank + jnp.where(ahead, 1, 0)
        blk_bias_t = jnp.where(rank < min(SLC_TOPK, nb), 0.0, MASKED)
        blk_bias_t = jnp.concatenate([blk_bias_t, jnp.zeros((LANES - nb, TQ), F32)], axis=0)
        blk_bias = jnp.transpose(blk_bias_t).astype(BF16)
        q_aug = jnp.concatenate([jnp.concatenate([h, blk_bias], axis=1) for h in qr_heads], axis=0)
        return dict(t_col=t_col, o_cmp=o_cmp, q_aug=q_aug, qr_heads=qr_heads)

    step = NSA_SUB * TQ
    n_pair = R // 2

    def step_body(v):
        subs = [front(v, u) for u in range(NSA_SUB)]
        n_keys = (v + 1) * step
        start = max(v * step - WINDOW, 0)
        span = n_keys - start
        diff = (v * step + lax.broadcasted_iota(jnp.int32, (step, 1), 0)) - (
            start + lax.broadcasted_iota(jnp.int32, (1, span), 1))
        band = jnp.where((diff >= 0) & (diff < WINDOW), 0.0, MASKED)

        def win_scores(c):
            q_pair = jnp.concatenate(
                [subs[u]["qr_heads"][2 * c + rr] for rr in range(2) for u in range(NSA_SUB)], axis=0)
            s_w = _nt_dot(q_pair, kwr_s[start:start + span, :])
            for rr in range(2):
                sw_s[c][rr * step:(rr + 1) * step, 0:span] = s_w[rr * step:(rr + 1) * step] + band

        def win_softmax(c):
            m_w = jnp.max(sw_s[c][:, 0:span], axis=-1, keepdims=True)
            ew_s[c][:, 0:span] = jnp.exp2(sw_s[c][:, 0:span] - m_w).astype(BF16)

        def win_values(c):
            acc_w = _dot(ew_s[c][:, 0:span], vwa_s[start:start + span, :])
            return acc_w[:, 0:HD] * (1.0 / acc_w[:, HD:])

        def sel_scores(u):
            q_aug, t_col = subs[u]["q_aug"], subs[u]["t_col"]
            for k0 in range(0, n_keys, TK):
                w = min(TK, n_keys - k0)
                s = _nt_dot(q_aug, ksa_s[k0:k0 + w, :])
                if k0 + w == n_keys:
                    kpos = k0 + lax.broadcasted_iota(jnp.int32, (1, w), 1)
                    s = jnp.where(kpos <= t_col, s, MASKED)
                ss_s[u][:, k0:k0 + w] = s

        def sel_softmax(u):
            m = jnp.max(ss_s[u][:, 0:n_keys], axis=-1, keepdims=True)
            es_s[u][:, 0:n_keys] = jnp.exp2(ss_s[u][:, 0:n_keys] - m).astype(BF16)

        def sel_values(u):
            acc = _dot(es_s[u][:, 0:n_keys], vsa_s[0:n_keys, :])
            return acc[:, 0:HD] * (1.0 / acc[:, HD:])

        assert NSA_SUB == 2 and n_pair == 2
        win_scores(0)
        win_scores(1)
        sel_scores(0)
        win_softmax(0)
        sel_scores(1)
        win_softmax(1)
        o_win = [win_values(0)]
        sel_softmax(0)
        o_win.append(win_values(1))
        sel_softmax(1)
        o_slc = [sel_values(0), sel_values(1)]

        misc = misc_ref[...]
        ng = 3 * R
        logits = jnp.where(g == 0, misc[:, MISC_GATE_LANE:MISC_GATE_LANE + ng],
                           misc[:, MISC_GATE_LANE + ng:MISC_GATE_LANE + 2 * ng])
        gate = jax.nn.sigmoid(logits)
        for u in range(NSA_SUB):
            urows = slice(u * TQ, (u + 1) * TQ)
            for r in range(R):
                rs = slice(r * TQ, (r + 1) * TQ)
                ws = slice((r % 2) * step + u * TQ, (r % 2) * step + (u + 1) * TQ)
                o = (gate[urows, 3 * r:3 * r + 1] * subs[u]["o_cmp"][rs]
                     + gate[urows, 3 * r + 1:3 * r + 2] * o_slc[u][rs]
                     + gate[urows, 3 * r + 2:3 * r + 3] * o_win[r // 2][ws])
                o_ref[urows, r * HD:(r + 1) * HD] = o.astype(o_ref.dtype)

    for v in range(S // step):
        pl.when(qi == v)(functools.partial(step_body, v))


BF16_ROWS = 16


def _side_cast_plan(weights, n_steps):
    plan = []
    for w in weights:
        n_rows = w.shape[0]
        if n_rows % (n_steps * BF16_ROWS) == 0:
            plan.append((n_rows // n_steps, None))
        else:
            assert n_rows % LANES == 0 and n_rows // LANES <= n_steps
            plan.append((LANES, n_rows // LANES))
    return plan


def _nsa(u3, cmp_kv, q_ops, kv_ops, ovt, side_weights):
    B, S, _ = u3.shape
    G, R, HD = NSA_KV_GROUPS, NSA_HPG, NSA_HD
    rows = R * NSA_TQ
    step = NSA_SUB * NSA_TQ
    assert NSA_TK % step == 0
    n_q = S // step
    plan = _side_cast_plan(side_weights, B * G * n_q)

    def side_spec(w, slab_rows, n_slabs):
        def index(b, g, i):
            s = (b * G + g) * n_q + i
            return (s if n_slabs is None else jnp.minimum(s, n_slabs - 1), 0)
        return pl.BlockSpec((slab_rows, w.shape[1]), index)

    side_specs = [side_spec(w, *p) for w, p in zip(side_weights, plan)]
    full = lambda a: pl.BlockSpec((None, S, a.shape[2] // G), lambda b, g, i: (b, 0, g))
    q_spec = pl.BlockSpec((None, step, R * HD), lambda b, g, i: (b, i, g))
    kern = functools.partial(_nsa_kernel, len(side_weights), tuple(p[1] for p in plan))
    return pl.pallas_call(
        kern,
        grid=(B, G, n_q),
        in_specs=[
            q_spec, q_spec,
            pl.BlockSpec((None, step, LANES), lambda b, g, i: (b, i, UB_MISC)),
            pl.BlockSpec((None, None, None, N_CMP_PAD, HD), lambda b, g, i: (0, b, g, 0, 0)),
            pl.BlockSpec((None, None, None, N_CMP_PAD, HD), lambda b, g, i: (1, b, g, 0, 0)),
            full(kv_ops[0]), full(kv_ops[1]), full(kv_ops[2]), full(kv_ops[3]),
            pl.BlockSpec((S // SLC_BLOCK, N_CMP_PAD), lambda b, g, i: (0, 0)),
        ] + side_specs,
        out_specs=[q_spec] + side_specs,
        out_shape=[jax.ShapeDtypeStruct((B, S, NSA_WIDTH), BF16)]
        + [jax.ShapeDtypeStruct(w.shape, BF16) for w in side_weights],
        scratch_shapes=[pltpu.VMEM((rows, S), F32)] * NSA_SUB + [pltpu.VMEM((rows, S), BF16)] * NSA_SUB
        + [pltpu.VMEM((2 * step, WINDOW + step), F32)] * (R // 2)
        + [pltpu.VMEM((2 * step, WINDOW + step), BF16)] * (R // 2),
        compiler_params=pltpu.CompilerParams(
            dimension_semantics=("arbitrary", "arbitrary", "arbitrary"), vmem_limit_bytes=VMEM_LIMIT),
        name="nsa",
    )(*q_ops, u3, cmp_kv, cmp_kv, *kv_ops, ovt, *side_weights)


def _out_proj_kernel(og_ref, on_ref, x_ref, w_ref, g_ref, b_ref, h_ref, hb_ref):
    half = og_ref.shape[0] // 2
    halves = [slice(0, half), slice(half, 2 * half)]
    mixes = [_dot(og_ref[rows, :], w_ref[0:GLA_WIDTH, :]) + _dot(on_ref[rows, :], w_ref[GLA_WIDTH:, :])
             for rows in halves]
    for rows, mix in zip(halves, mixes):
        h = _layer_norm(DEEPNORM_ALPHA * x_ref[rows, :] + mix, g_ref[...], b_ref[...])
        h_ref[rows, :] = h
        hb_ref[rows, :] = h.astype(BF16)


def _out_proj(o_gla, o_nsa, x2, w_bf, g, b):
    T, D = x2.shape
    tile = lambda w: pl.BlockSpec((OUT_TM, w), lambda i: (i, 0))
    const = lambda s: pl.BlockSpec(s, lambda i: (0, 0))
    return pl.pallas_call(
        _out_proj_kernel,
        grid=(T // OUT_TM,),
        in_specs=[tile(GLA_WIDTH), tile(NSA_WIDTH), tile(D), const(w_bf.shape), const((1, D)), const((1, D))],
        out_specs=[tile(D), tile(D)],
        out_shape=[jax.ShapeDtypeStruct((T, D), F32), jax.ShapeDtypeStruct((T, D), BF16)],
        compiler_params=pltpu.CompilerParams(
            dimension_semantics=("parallel",), vmem_limit_bytes=VMEM_LIMIT),
        name="out_proj",
    )(o_gla, o_nsa, x2, w_bf, g, b)


FFN_LN_ROWS = 256


def _ffn_residual_copy(h_hbm, res_s, sem, i):
    tm = res_s.shape[0]
    return pltpu.make_async_copy(h_hbm.at[pl.ds(pl.multiple_of(i * tm, tm), tm), :], res_s, sem)


def _ffn_kernel(hb_ref, h_hbm, w1_ref, w3_ref, w2_ref, g_ref, b_ref, o_ref, res_s, sem):
    i, f = pl.program_id(0), pl.program_id(1)
    last = pl.num_programs(1) - 1

    @pl.when(f == 0)
    def _():
        _ffn_residual_copy(h_hbm, res_s, sem, i).start()
        o_ref[...] = jnp.zeros(o_ref.shape, F32)

    hb = hb_ref[...]
    a = _dot(hb, w1_ref[...])
    c = _dot(hb, w3_ref[...])
    o_ref[...] += _dot((a * jax.nn.sigmoid(a) * c).astype(BF16), w2_ref[...])

    @pl.when(f == last)
    def _():
        _ffn_residual_copy(h_hbm, res_s, sem, i).wait()

        def norm_rows(rb, _):
            rows = pl.ds(pl.multiple_of(rb * FFN_LN_ROWS, FFN_LN_ROWS), FFN_LN_ROWS)
            o_ref[rows, :] = _layer_norm(DEEPNORM_ALPHA * res_s[rows, :] + o_ref[rows, :], g_ref[...], b_ref[...])
            return 0
        lax.fori_loop(0, o_ref.shape[0] // FFN_LN_ROWS, norm_rows, 0)


def _ffn(h_bf, h, w1, w3, w2, g, b):
    T, D = h.shape
    F = w1.shape[1]
    return pl.pallas_call(
        _ffn_kernel,
        grid=(T // FFN_TM, F // FFN_TF),
        in_specs=[
            pl.BlockSpec((FFN_TM, D), lambda i, f: (i, 0)),
            pl.BlockSpec(memory_space=pl.ANY),
            pl.BlockSpec((D, FFN_TF), lambda i, f: (0, f)),
            pl.BlockSpec((D, FFN_TF), lambda i, f: (0, f)),
            pl.BlockSpec((FFN_TF, D), lambda i, f: (f, 0)),
            pl.BlockSpec((1, D), lambda i, f: (0, 0)),
            pl.BlockSpec((1, D), lambda i, f: (0, 0)),
        ],
        out_specs=pl.BlockSpec((FFN_TM, D), lambda i, f: (i, 0)),
        out_shape=jax.ShapeDtypeStruct((T, D), F32),
        scratch_shapes=[pltpu.VMEM((FFN_TM, D), F32), pltpu.SemaphoreType.DMA(())],
        compiler_params=pltpu.CompilerParams(
            dimension_semantics=("arbitrary", "arbitrary"), vmem_limit_bytes=VMEM_LIMIT),
        name="ffn",
    )(h_bf, h, w1, w3, w2, g, b)


def _rope_tables(S):
    half = ROPE_DIM // 2
    pos = np.arange(S, dtype=np.float64)
    inv = np.power(ROPE_THETA, -np.arange(0, ROPE_DIM, 2, dtype=np.float64) / ROPE_DIM)
    ang = pos[:, None] * inv[None, :]
    cos_f = np.ones((S, LANES), np.float32)
    sin_s = np.zeros((S, LANES), np.float32)
    for lane0, sign in ((0, -1.0), (ROPE_PAIR_LANE, 1.0)):
        cos_f[:, lane0:lane0 + half] = np.cos(ang)
        sin_s[:, lane0:lane0 + half] = sign * np.sin(ang)
    return jnp.asarray(cos_f), jnp.asarray(sin_s)


def _selection_constants(S):
    n_cmp = (S - CMP_BLOCK) // CMP_STRIDE + 1
    nb = S // SLC_BLOCK
    c_start = np.arange(n_cmp) * CMP_STRIDE
    b_start = np.arange(nb) * SLC_BLOCK
    overlap = ((c_start[:, None] < b_start[None, :] + SLC_BLOCK) &
               (b_start[None, :] < c_start[:, None] + CMP_BLOCK)).astype(np.float32)
    ovt = np.zeros((nb, N_CMP_PAD), np.float32)
    ovt[:, :n_cmp] = overlap.T
    blk_onehot = ((np.arange(S) // SLC_BLOCK)[:, None] == np.arange(LANES)[None, :]).astype(np.float32)
    return jnp.asarray(ovt, BF16), jnp.asarray(blk_onehot, BF16)


def kernel(x, w_in, gla_gate_w2, gla_gate_b2, gla_norm_w, cmp_k_pos, cmp_k_w1, cmp_k_w2,
           cmp_v_pos, cmp_v_w1, cmp_v_w2, w_out, ln1_g, ln1_b, ffn_w1, ffn_w3, ffn_w2, ln2_g, ln2_b):
    B, S, D = x.shape
    T = B * S
    cos_f, sin_s = _rope_tables(S)
    ovt, blk_onehot = _selection_constants(S)

    x2 = x.reshape(T, D)
    for l in range(DEPTH):
        wt_gla, wt_nsa = _regroup_w_in(w_in[l].T)
        ua3 = _in_proj(x2, wt_gla).reshape(B, S, -1)
        u_nsa, *attn_ops = _in_proj_nsa(x2, wt_nsa, S, cos_f, sin_s, blk_onehot)
        u3 = u_nsa.reshape(B, S, -1)
        attn_ops = [a.reshape(B, S, -1) for a in attn_ops]
        q_ops, kv_ops = attn_ops[0:2], attn_ops[2:]
        o_gla = _gla(ua3, u3, gla_gate_w2[l], gla_gate_b2[l][None, :], gla_norm_w[l][None, :])
        cmp_kv = _compress(u3, (cmp_k_pos[l], cmp_k_w1[l], cmp_k_w2[l]), (cmp_v_pos[l], cmp_v_w1[l], cmp_v_w2[l]))
        o_nsa, w_out_bf, w1_bf, w3_bf, w2_bf = _nsa(u3, cmp_kv, q_ops, kv_ops, ovt,
                                                    [w_out[l], ffn_w1[l], ffn_w3[l], ffn_w2[l]])
        h, h_bf = _out_proj(o_gla.reshape(T, GLA_WIDTH), o_nsa.reshape(T, NSA_WIDTH), x2,
                            w_out_bf, ln1_g[l][None, :], ln1_b[l][None, :])
        x2 = _ffn(h_bf, h, w1_bf, w3_bf, w2_bf, ln2_g[l][None, :], ln2_b[l][None, :])
    return x2.reshape(B, S, D)
```

```python
import functools

import numpy as np
import jax
import jax.numpy as jnp
from jax import lax
from jax.experimental import pallas as pl
from jax.experimental.pallas import tpu as pltpu

F32 = jnp.float32
BF16 = jnp.bfloat16

D_MODEL = 2048
DEPTH = 1
GLA_HEADS = 4
GLA_DK = 128
GLA_DV = 256
GLA_GATE_RANK = 16
GLA_GATE_NORM = 16.0
GLA_CHUNK = 64
NSA_HEADS = 8
NSA_HD = 128
NSA_KV_GROUPS = 2
NSA_HPG = NSA_HEADS // NSA_KV_GROUPS
CMP_BLOCK = 32
CMP_STRIDE = 16
CMP_HIDDEN = 2 * NSA_HD
SLC_BLOCK = 64
SLC_TOPK = 16
WINDOW = 512
ROPE_THETA = 500000.0
ROPE_DIM = NSA_HD // 4
FFN_HIDDEN = 5632
DEEPNORM_ALPHA = (2.0 * DEPTH) ** 0.25
LN_EPS = 1e-5
FORCED_SCORE = 1e4
INVALID_SCORE = -1e4

GLA_WIDTH = GLA_HEADS * GLA_DV
NSA_WIDTH = NSA_HEADS * NSA_HD

LANES = 128
VMEM_LIMIT = 56 * 1024 * 1024
CB_GQ = 0
CB_GK = 4
CB_GV = 8
CB_GO = 16
CB_NQ = 24
CB_KC = 32
CB_KS = 36
CB_VS = 38
CB_KW = 40
CB_VW = 42
CB_MISC = 44
MISC_GATE_LANE = GLA_GATE_RANK
UB_MISC = 2 * NSA_KV_GROUPS
U_WIDTH = 45 * LANES

REGROUP_TC = 256
IN_TM = 512
OUT_TM = 512
FFN_TM, FFN_TF = 1024, 512
NSA_TQ = 128
NSA_SUB = 2
NSA_TK = 512
WIN_SPAN = WINDOW + NSA_TQ
MASKED = -1e30
LOG2E = 1.4426950408889634


def _nt_dot(a, b):
    return lax.dot_general(a, b, (((1,), (1,)), ((), ())), preferred_element_type=F32)


def _tn_dot(a, b):
    return lax.dot_general(a, b, (((0,), (0,)), ((), ())), preferred_element_type=F32)


def _dot(a, b):
    return jnp.dot(a, b, preferred_element_type=F32)


def _layer_norm(z, g, b):
    mu = jnp.mean(z, axis=-1, keepdims=True)
    zc = z - mu
    var = jnp.mean(zc * zc, axis=-1, keepdims=True)
    return zc * lax.rsqrt(var + LN_EPS) * g + b


def _masked_softmax(s, mask):
    sm = jnp.where(mask, s, -jnp.inf)
    m = jnp.max(sm, axis=-1, keepdims=True)
    m = jnp.where(m == -jnp.inf, 0.0, m)
    e = jnp.where(mask, jnp.exp(s - m), 0.0)
    den = jnp.sum(e, axis=-1, keepdims=True)
    return e * (1.0 / jnp.where(den > 0, den, 1.0))


def _regroup_kernel(w_ref, oa_ref, ob_ref):
    cols = w_ref.shape[1]
    o_glr = CB_NQ * LANES
    o_nq = o_glr + GLA_GATE_RANK
    o_gate = o_nq + (CB_MISC - CB_NQ) * LANES
    n_gate = NSA_HEADS * 3
    oa_ref[...] = w_ref[0:o_glr, :].astype(BF16)
    roped = (list(range(CB_NQ, CB_NQ + NSA_HEADS)) + list(range(CB_KS, CB_KS + NSA_KV_GROUPS))
             + list(range(CB_KW, CB_KW + NSA_KV_GROUPS)))
    for blk in range(CB_NQ, CB_MISC):
        dst, src = (blk - CB_NQ) * LANES, blk * LANES + GLA_GATE_RANK
        pieces = _head_dim_order() if blk in roped else [(0, NSA_HD)]
        at = 0
        for lo, hi in pieces:
            ob_ref[dst + at:dst + at + hi - lo, :] = w_ref[src + lo:src + hi, :].astype(BF16)
            at += hi - lo
    misc = jnp.concatenate([w_ref[o_glr:o_nq, :], w_ref[o_gate:o_gate + n_gate, :],
                            jnp.zeros((LANES - GLA_GATE_RANK - n_gate, cols), F32)], axis=0)
    ob_ref[(CB_MISC - CB_NQ) * LANES:, :] = misc.astype(BF16)


def _regroup_w_in(wt):
    n_in, D = wt.shape
    na, nb = CB_NQ * LANES, U_WIDTH - CB_NQ * LANES
    return pl.pallas_call(
        _regroup_kernel,
        grid=(D // REGROUP_TC,),
        in_specs=[pl.BlockSpec((n_in, REGROUP_TC), lambda i: (0, i))],
        out_specs=[pl.BlockSpec((na, REGROUP_TC), lambda i: (0, i)), pl.BlockSpec((nb, REGROUP_TC), lambda i: (0, i))],
        out_shape=[jax.ShapeDtypeStruct((na, D), BF16), jax.ShapeDtypeStruct((nb, D), BF16)],
        compiler_params=pltpu.CompilerParams(
            dimension_semantics=("parallel",), vmem_limit_bytes=VMEM_LIMIT),
        name="regroup_w_in",
    )(wt)


def _in_proj_kernel(x_ref, w_ref, o_ref):
    o_ref[...] = _nt_dot(x_ref[...].astype(BF16), w_ref[...])


def _in_proj(x2, wt_bf):
    T, D = x2.shape
    N = wt_bf.shape[0]
    return pl.pallas_call(
        _in_proj_kernel,
        grid=(T // IN_TM,),
        in_specs=[pl.BlockSpec((IN_TM, D), lambda i: (i, 0)),
                  pl.BlockSpec((N, D), lambda i: (0, 0))],
        out_specs=pl.BlockSpec((IN_TM, N), lambda i: (i, 0)),
        out_shape=jax.ShapeDtypeStruct((T, N), F32),
        compiler_params=pltpu.CompilerParams(
            dimension_semantics=("parallel",), vmem_limit_bytes=VMEM_LIMIT),
        name="in_proj",
    )(x2, wt_bf)


def _in_proj_nsa_kernel(x_ref, w_ref, cos_ref, sin_ref, blk_ref,
                        u_ref, qc_ref, qr_ref, ksa_ref, vsa_ref, kwr_ref, vwa_ref):
    xb = x_ref[...].astype(BF16)
    res_q = _nt_dot(xb, w_ref[0:NSA_WIDTH, :])
    res = _nt_dot(xb, w_ref[NSA_WIDTH:, :])
    cos, sin = cos_ref[...], sin_ref[...]
    scale = NSA_HD ** -0.5
    for h in range(NSA_HEADS):
        cols = slice(h * NSA_HD, (h + 1) * NSA_HD)
        qc_ref[:, cols] = (res_q[:, cols] * scale).astype(BF16)
        qr_ref[:, cols] = _rope(res_q[:, cols] * (scale * LOG2E), cos, sin).astype(BF16)
    first = CB_NQ + NSA_HEADS
    col = lambda cb, g: slice((cb - first + g) * LANES, (cb - first + g + 1) * LANES)
    n_cmp_cols = 2 * NSA_KV_GROUPS * NSA_HD
    u_ref[:, 0:n_cmp_cols] = res[:, col(CB_KC, 0).start:col(CB_KC, 0).start + n_cmp_cols]
    u_ref[:, n_cmp_cols:] = res[:, col(CB_MISC, 0)]
    ones = jnp.ones((res.shape[0], LANES), BF16)
    for g in range(NSA_KV_GROUPS):
        lo, hi = slice(2 * g * LANES, (2 * g + 1) * LANES), slice((2 * g + 1) * LANES, (2 * g + 2) * LANES)
        ksa_ref[:, lo] = _rope(res[:, col(CB_KS, g)], cos, sin).astype(BF16)
        ksa_ref[:, hi] = blk_ref[...]
        vsa_ref[:, lo] = res[:, col(CB_VS, g)].astype(BF16)
        vsa_ref[:, hi] = ones
        kwr_ref[:, g * LANES:(g + 1) * LANES] = _rope(res[:, col(CB_KW, g)], cos, sin).astype(BF16)
        vwa_ref[:, lo] = res[:, col(CB_VW, g)].astype(BF16)
        vwa_ref[:, hi] = ones


def _in_proj_nsa(x2, wt_bf, S, cos_f, sin_s, blk_onehot):
    T, D = x2.shape
    N = wt_bf.shape[0]
    G = NSA_KV_GROUPS
    per_seq = S // IN_TM
    table = pl.BlockSpec((IN_TM, LANES), lambda i: (i % per_seq, 0))
    rows = lambda w: pl.BlockSpec((IN_TM, w), lambda i: (i, 0))
    widths = [(UB_MISC + 1) * LANES, NSA_WIDTH, NSA_WIDTH, 2 * G * LANES, 2 * G * LANES, G * LANES, 2 * G * LANES]
    return pl.pallas_call(
        _in_proj_nsa_kernel,
        grid=(T // IN_TM,),
        in_specs=[rows(D), pl.BlockSpec((N, D), lambda i: (0, 0)), table, table, table],
        out_specs=[rows(w) for w in widths],
        out_shape=[jax.ShapeDtypeStruct((T, widths[0]), F32)]
        + [jax.ShapeDtypeStruct((T, w), BF16) for w in widths[1:]],
        compiler_params=pltpu.CompilerParams(
            dimension_semantics=("parallel",), vmem_limit_bytes=VMEM_LIMIT),
        name="in_proj_nsa",
    )(x2, wt_bf, cos_f, sin_s, blk_onehot)


GLA_ROWBLK = 256
GLA_TS = 512


def _split3(a):
    hi = a.astype(BF16)
    r1 = a - hi.astype(F32)
    mid = r1.astype(BF16)
    lo = (r1 - mid.astype(F32)).astype(BF16)
    return hi, mid, lo


def _log_sigmoid(z):
    return jnp.minimum(z, 0.0) - jnp.log(1.0 + jnp.exp(-jnp.abs(z)))


def _gla_kernel(q_ref, k_ref, v_ref, go_ref, misc_ref, w2_ref, b2_ref, nw_ref, o_ref,
                qd_s, ki_s, ks_s, dec_s, state_s, sbf_s):
    TS = q_ref.shape[0]
    H, C, DK, DV = GLA_HEADS, GLA_CHUNK, GLA_DK, GLA_DV
    scale = DK ** -0.5

    @pl.when(pl.program_id(1) == 0)
    def _():
        state_s[...] = jnp.zeros(state_s.shape, F32)

    r = lax.broadcasted_iota(jnp.int32, (GLA_ROWBLK, GLA_ROWBLK), 0)
    c = lax.broadcasted_iota(jnp.int32, (GLA_ROWBLK, GLA_ROWBLK), 1)
    chunk_causal = ((r // C) == (c // C)) & (c <= r)
    cum_m = jnp.where(chunk_causal, 1.0, 0.0).astype(BF16)

    w2 = w2_ref[...].astype(BF16)
    b2 = b2_ref[...]
    for rb in range(TS // GLA_ROWBLK):
        rows = slice(rb * GLA_ROWBLK, (rb + 1) * GLA_ROWBLK)
        glr = misc_ref[rows, 0:GLA_GATE_RANK].astype(BF16)
        gk = _log_sigmoid(_dot(glr, w2) + b2) * (1.0 / GLA_GATE_NORM)
        hi, mid, lo = _split3(gk)
        bc_all = _dot(cum_m, hi) + _dot(cum_m, mid) + _dot(cum_m, lo)
        bl_all = jnp.concatenate(
            [jnp.broadcast_to(bc_all[j * C + C - 1:j * C + C, :], (C, H * DK)) for j in range(GLA_ROWBLK // C)],
            axis=0)
        for h in range(H):
            hk = slice(h * DK, (h + 1) * DK)
            bc = bc_all[:, hk]
            bl = bl_all[:, hk]
            q = q_ref[rows, hk] * scale
            k = k_ref[rows, hk]
            qd_s[h, rows, :] = (q * jnp.exp(bc)).astype(BF16)
            ki_s[h, rows, :] = (k * jnp.exp(-bc)).astype(BF16)
            ks_s[h, rows, :] = (k * jnp.exp(bl - bc)).astype(BF16)
            dec_s[h, rows, :] = jnp.exp(bl)

    n_chunk = TS // C
    nw = nw_ref[...]
    for h in range(H):
        hv = slice(h * DV, (h + 1) * DV)
        kv = [_tn_dot(ks_s[h, c * C:(c + 1) * C, :], v_ref[c * C:(c + 1) * C, hv].astype(BF16))
              for c in range(n_chunk)]
        dec_rows = jnp.concatenate([dec_s[h, c * C:c * C + 1, :] for c in range(n_chunk)]
                                   + [jnp.zeros((LANES - n_chunk, DK), F32)], axis=0)
        dec_cols = jnp.transpose(dec_rows)
        state = state_s[h]
        for c in range(n_chunk):
            sbf_s[h, c] = state.astype(BF16)
            state = state * dec_cols[:, c:c + 1] + kv[c]
        state_s[h] = state

    for h in range(H):
        hv = slice(h * DV, (h + 1) * DV)
        for rb in range(TS // GLA_ROWBLK):
            rows = slice(rb * GLA_ROWBLK, (rb + 1) * GLA_ROWBLK)
            qd = qd_s[h, rows, :]
            attn = jnp.where(chunk_causal, _nt_dot(qd, ki_s[h, rows, :]), 0.0).astype(BF16)
            o = _dot(attn, v_ref[rows, hv].astype(BF16))
            inter = [_dot(qd_s[h, c * C:(c + 1) * C, :], sbf_s[h, c])
                     for c in range(rb * GLA_ROWBLK // C, (rb + 1) * GLA_ROWBLK // C)]
            o = o + jnp.concatenate(inter, axis=0)
            ms = jnp.mean(o * o, axis=-1, keepdims=True)
            o = o * lax.rsqrt(ms + LN_EPS) * nw
            go = go_ref[rows, hv]
            o_ref[rows, hv] = (o * (go * jax.nn.sigmoid(go))).astype(o_ref.dtype)


def _gla(ua3, ub3, gate_w2, gate_b2, norm_w):
    B, S, _ = ua3.shape
    H, DK, DV = GLA_HEADS, GLA_DK, GLA_DV
    qk_w, v_w = H * DK, H * DV
    tok = lambda w, off: pl.BlockSpec((None, GLA_TS, w), lambda b, s: (b, s, off * LANES // w))
    const = lambda shape: pl.BlockSpec(shape, lambda b, s: (0, 0))
    return pl.pallas_call(
        _gla_kernel,
        grid=(B, S // GLA_TS),
        in_specs=[tok(qk_w, CB_GQ), tok(qk_w, CB_GK), tok(v_w, CB_GV), tok(v_w, CB_GO), tok(LANES, UB_MISC),
                  const((GLA_GATE_RANK, qk_w)), const((1, qk_w)), const((1, DV))],
        out_specs=pl.BlockSpec((None, GLA_TS, v_w), lambda b, s: (b, s, 0)),
        out_shape=jax.ShapeDtypeStruct((B, S, GLA_WIDTH), BF16),
        scratch_shapes=[pltpu.VMEM((H, GLA_TS, DK), BF16), pltpu.VMEM((H, GLA_TS, DK), BF16),
                        pltpu.VMEM((H, GLA_TS, DK), BF16), pltpu.VMEM((H, GLA_TS, DK), F32),
                        pltpu.VMEM((H, DK, DV), F32), pltpu.VMEM((H, GLA_TS // GLA_CHUNK, DK, DV), BF16)],
        compiler_params=pltpu.CompilerParams(
            dimension_semantics=("parallel", "arbitrary"), vmem_limit_bytes=VMEM_LIMIT),
        name="gla",
    )(ua3, ua3, ua3, ua3, ub3, gate_w2, gate_b2, norm_w)


N_CMP_PAD = 128


def _gelu_tanh(x):
    return x * (0.5 * (1.0 + jnp.tanh(0.7978845608028654 * (x + 0.044715 * (x * x * x)))))


def _compress_kernel(k_ref, v_ref, kpos_ref, vpos_ref, kw1_ref, vw1_ref, kw2_ref, vw2_ref, o_ref):
    half = CMP_BLOCK // 2
    chains = ((k_ref, kpos_ref, kw1_ref, kw2_ref, _head_dim_order()),
              (v_ref, vpos_ref, vw1_ref, vw2_ref, [(0, NSA_HD)]))
    for kv, (src_ref, pos_ref, w1_ref, w2_ref, out_order) in enumerate(chains):
        p0 = jnp.zeros((N_CMP_PAD, CMP_HIDDEN), F32)
        p1 = jnp.zeros((N_CMP_PAD, CMP_HIDDEN), F32)
        for l in range(half):
            x = src_ref[pl.ds(l, N_CMP_PAD, stride=CMP_STRIDE), :]
            a0 = (x + pos_ref[l:l + 1, :]).astype(BF16)
            a1 = (x + pos_ref[half + l:half + l + 1, :]).astype(BF16)
            p0 = p0 + _dot(a0, w1_ref[l * NSA_HD:(l + 1) * NSA_HD, :].astype(BF16))
            p1 = p1 + _dot(a1, w1_ref[(half + l) * NSA_HD:(half + l + 1) * NSA_HD, :].astype(BF16))
        pre = p0 + pltpu.roll(p1, N_CMP_PAD - 1, 0)
        h = _gelu_tanh(pre).astype(BF16)
        w2 = jnp.concatenate([w2_ref[:, lo:hi] for lo, hi in out_order], axis=1).astype(BF16)
        out = _dot(h, w2)
        row = lax.broadcasted_iota(jnp.int32, out.shape, 0)
        o_ref[kv] = jnp.where(row < N_CMP_PAD - 1, out, 0.0).astype(o_ref.dtype)


def _compress(u3, k_params, v_params):
    B, S, _ = u3.shape
    G = NSA_KV_GROUPS
    src = lambda kv: pl.BlockSpec((None, S, NSA_HD), lambda b, g: (b, 0, NSA_KV_GROUPS * kv + g))
    whole = lambda a: pl.BlockSpec(a.shape, lambda b, g: (0,) * a.ndim)
    params = [k_params[0], v_params[0], k_params[1], v_params[1], k_params[2], v_params[2]]
    return pl.pallas_call(
        _compress_kernel,
        grid=(B, G),
        in_specs=[src(0), src(1)] + [whole(a) for a in params],
        out_specs=pl.BlockSpec((2, None, None, N_CMP_PAD, NSA_HD), lambda b, g: (0, b, g, 0, 0)),
        out_shape=jax.ShapeDtypeStruct((2, B, G, N_CMP_PAD, NSA_HD), BF16),
        compiler_params=pltpu.CompilerParams(
            dimension_semantics=("parallel", "parallel"), vmem_limit_bytes=VMEM_LIMIT),
        name="compress",
    )(u3, u3, *params)


ROPE_PAIR_LANE = LANES // 2


def _head_dim_order():
    half = ROPE_DIM // 2
    return [(0, half), (ROPE_DIM, ROPE_PAIR_LANE + half), (half, ROPE_DIM), (ROPE_PAIR_LANE + half, NSA_HD)]


def _rope(x, cos, sin_signed):
    return x * cos + pltpu.roll(x, ROPE_PAIR_LANE, 1) * sin_signed


def _nsa_kernel(n_side, side_slabs, *refs):
    n_in = 10
    (qc_ref, qr_ref, misc_ref, kc_ref, vc_ref, ksa_s, vsa_s, kwr_s, vwa_s, ovt_ref) = refs[0:n_in]
    side_in = refs[n_in:n_in + n_side]
    o_ref = refs[n_in + n_side]
    side_out = refs[n_in + 1 + n_side:n_in + 1 + 2 * n_side]
    chain_scratch = refs[n_in + 1 + 2 * n_side:]
    S = ksa_s.shape[0]
    R, TQ, HD = NSA_HPG, NSA_TQ, NSA_HD
    ss_s, es_s = chain_scratch[0:NSA_SUB], chain_scratch[NSA_SUB:2 * NSA_SUB]
    sw_s, ew_s = chain_scratch[2 * NSA_SUB:2 * NSA_SUB + R // 2], chain_scratch[2 * NSA_SUB + R // 2:]
    g = pl.program_id(1)
    qi = pl.program_id(2)

    step_idx = (pl.program_id(0) * pl.num_programs(1) + g) * pl.num_programs(2) + qi
    for w_in_ref, w_out_ref, n_slabs in zip(side_in, side_out, side_slabs):
        def cast(w_in_ref=w_in_ref, w_out_ref=w_out_ref):
            w_out_ref[...] = w_in_ref[...].astype(BF16)
        if n_slabs is None:
            cast()
        else:
            pl.when(step_idx < n_slabs)(cast)

    nb = S // SLC_BLOCK
    TK = NSA_TK
    ovt = ovt_ref[...]
    kc, vc = kc_ref[...], vc_ref[...]

    def front(v, u):
        tq0 = (v * NSA_SUB + u) * TQ
        urows = slice(u * TQ, (u + 1) * TQ)
        qc = jnp.concatenate([qc_ref[urows, r * HD:(r + 1) * HD] for r in range(R)], axis=0)
        qr_heads = [qr_ref[urows, r * HD:(r + 1) * HD] for r in range(R)]
        t_tok = tq0 + lax.broadcasted_iota(jnp.int32, (TQ, 1), 0)
        t_col = jnp.concatenate([t_tok] * R, axis=0)

        s_c = _nt_dot(qc, kc)
        n_idx = lax.broadcasted_iota(jnp.int32, (1, N_CMP_PAD), 1)
        p_c = _masked_softmax(s_c, (n_idx * CMP_STRIDE + (CMP_BLOCK - 1)) <= t_col)
        p_cb = p_c.astype(BF16)
        o_cmp = _dot(p_cb, vc)

        imp = _nt_dot(ovt, p_cb[0:TQ])
        for r in range(1, R):
            imp = imp + _nt_dot(ovt, p_cb[r * TQ:(r + 1) * TQ])
        j_idx = lax.broadcasted_iota(jnp.int32, (nb, TQ), 0)
        cur = (tq0 + lax.broadcasted_iota(jnp.int32, (nb, TQ), 1)) // SLC_BLOCK
        imp = jnp.where(j_idx <= cur, imp, INVALID_SCORE)
        imp = jnp.where((j_idx == 0) | (j_idx == cur) | (j_idx == cur - 1), FORCED_SCORE, imp)
        rank = jnp.zeros((nb, TQ), jnp.int32)
        for i in range(nb):
            ri = imp[i:i + 1, :]
            ahead = (ri > imp) | ((ri == imp) & (i < j_idx))
            rank = rank + jnp.where(ahead, 1, 0)
        blk_bias_t = jnp.where(rank < min(SLC_TOPK, nb), 0.0, MASKED)
        blk_bias_t = jnp.concatenate([blk_bias_t, jnp.zeros((LANES - nb, TQ), F32)], axis=0)
        blk_bias = jnp.transpose(blk_bias_t).astype(BF16)
        q_aug = jnp.concatenate([jnp.concatenate([h, blk_bias], axis=1) for h in qr_heads], axis=0)
        return dict(t_col=t_col, o_cmp=o_cmp, q_aug=q_aug, qr_heads=qr_heads)

    step = NSA_SUB * TQ
    n_pair = R // 2

    def step_body(v):
        subs = [front(v, u) for u in range(NSA_SUB)]
        n_keys = (v + 1) * step
        start = max(v * step - WINDOW, 0)
        span = n_keys - start
        diff = (v * step + lax.broadcasted_iota(jnp.int32, (step, 1), 0)) - (
            start + lax.broadcasted_iota(jnp.int32, (1, span), 1))
        band = jnp.where((diff >= 0) & (diff < WINDOW), 0.0, MASKED)

        def win_scores(c):
            q_pair = jnp.concatenate(
                [subs[u]["qr_heads"][2 * c + rr] for rr in range(2) for u in range(NSA_SUB)], axis=0)
            s_w = _nt_dot(q_pair, kwr_s[start:start + span, :])
            for rr in range(2):
                sw_s[c][rr * step:(rr + 1) * step, 0:span] = s_w[rr * step:(rr + 1) * step] + band

        def win_softmax(c):
            m_w = jnp.max(sw_s[c][:, 0:span], axis=-1, keepdims=True)
            ew_s[c][:, 0:span] = jnp.exp2(sw_s[c][:, 0:span] - m_w).astype(BF16)

        def win_values(c):
            acc_w = _dot(ew_s[c][:, 0:span], vwa_s[start:start + span, :])
            return acc_w[:, 0:HD] * (1.0 / acc_w[:, HD:])

        def sel_scores(u):
            q_aug, t_col = subs[u]["q_aug"], subs[u]["t_col"]
            for k0 in range(0, n_keys, TK):
                w = min(TK, n_keys - k0)
                s = _nt_dot(q_aug, ksa_s[k0:k0 + w, :])
                if k0 + w == n_keys:
                    kpos = k0 + lax.broadcasted_iota(jnp.int32, (1, w), 1)
                    s = jnp.where(kpos <= t_col, s, MASKED)
                ss_s[u][:, k0:k0 + w] = s

        def sel_softmax(u):
            m = jnp.max(ss_s[u][:, 0:n_keys], axis=-1, keepdims=True)
            es_s[u][:, 0:n_keys] = jnp.exp2(ss_s[u][:, 0:n_keys] - m).astype(BF16)

        def sel_values(u):
            acc = _dot(es_s[u][:, 0:n_keys], vsa_s[0:n_keys, :])
            return acc[:, 0:HD] * (1.0 / acc[:, HD:])

        assert NSA_SUB == 2 and n_pair == 2
        win_scores(0)
        win_scores(1)
        sel_scores(0)
        win_softmax(0)
        sel_scores(1)
        win_softmax(1)
        o_win = [win_values(0)]
        sel_softmax(0)
        o_win.append(win_values(1))
        sel_softmax(1)
        o_slc = [sel_values(0), sel_values(1)]

        misc = misc_ref[...]
        ng = 3 * R
        logits = jnp.where(g == 0, misc[:, MISC_GATE_LANE:MISC_GATE_LANE + ng],
                           misc[:, MISC_GATE_LANE + ng:MISC_GATE_LANE + 2 * ng])
        gate = jax.nn.sigmoid(logits)
        for u in range(NSA_SUB):
            urows = slice(u * TQ, (u + 1) * TQ)
            for r in range(R):
                rs = slice(r * TQ, (r + 1) * TQ)
                ws = slice((r % 2) * step + u * TQ, (r % 2) * step + (u + 1) * TQ)
                o = (gate[urows, 3 * r:3 * r + 1] * subs[u]["o_cmp"][rs]
                     + gate[urows, 3 * r + 1:3 * r + 2] * o_slc[u][rs]
                     + gate[urows, 3 * r + 2:3 * r + 3] * o_win[r // 2][ws])
                o_ref[urows, r * HD:(r + 1) * HD] = o.astype(o_ref.dtype)

    for v in range(S // step):
        pl.when(qi == v)(functools.partial(step_body, v))


BF16_ROWS = 16


def _side_cast_plan(weights, n_steps):
    plan = []
    for w in weights:
        n_rows = w.shape[0]
        if n_rows % (n_steps * BF16_ROWS) == 0:
            plan.append((n_rows // n_steps, None))
        else:
            assert n_rows % LANES == 0 and n_rows // LANES <= n_steps
            plan.append((LANES, n_rows // LANES))
    return plan


def _nsa(u3, cmp_kv, q_ops, kv_ops, ovt, side_weights):
    B, S, _ = u3.shape
    G, R, HD = NSA_KV_GROUPS, NSA_HPG, NSA_HD
    rows = R * NSA_TQ
    step = NSA_SUB * NSA_TQ
    assert NSA_TK % step == 0
    n_q = S // step
    plan = _side_cast_plan(side_weights, B * G * n_q)

    def side_spec(w, slab_rows, n_slabs):
        def index(b, g, i):
            s = (b * G + g) * n_q + i
            return (s if n_slabs is None else jnp.minimum(s, n_slabs - 1), 0)
        return pl.BlockSpec((slab_rows, w.shape[1]), index)

    side_specs = [side_spec(w, *p) for w, p in zip(side_weights, plan)]
    full = lambda a: pl.BlockSpec((None, S, a.shape[2] // G), lambda b, g, i: (b, 0, g))
    q_spec = pl.BlockSpec((None, step, R * HD), lambda b, g, i: (b, i, g))
    kern = functools.partial(_nsa_kernel, len(side_weights), tuple(p[1] for p in plan))
    return pl.pallas_call(
        kern,
        grid=(B, G, n_q),
        in_specs=[
            q_spec, q_spec,
            pl.BlockSpec((None, step, LANES), lambda b, g, i: (b, i, UB_MISC)),
            pl.BlockSpec((None, None, None, N_CMP_PAD, HD), lambda b, g, i: (0, b, g, 0, 0)),
            pl.BlockSpec((None, None, None, N_CMP_PAD, HD), lambda b, g, i: (1, b, g, 0, 0)),
            full(kv_ops[0]), full(kv_ops[1]), full(kv_ops[2]), full(kv_ops[3]),
            pl.BlockSpec((S // SLC_BLOCK, N_CMP_PAD), lambda b, g, i: (0, 0)),
        ] + side_specs,
        out_specs=[q_spec] + side_specs,
        out_shape=[jax.ShapeDtypeStruct((B, S, NSA_WIDTH), BF16)]
        + [jax.ShapeDtypeStruct(w.shape, BF16) for w in side_weights],
        scratch_shapes=[pltpu.VMEM((rows, S), F32)] * NSA_SUB + [pltpu.VMEM((rows, S), BF16)] * NSA_SUB
        + [pltpu.VMEM((2 * step, WINDOW + step), F32)] * (R // 2)
        + [pltpu.VMEM((2 * step, WINDOW + step), BF16)] * (R // 2),
        compiler_params=pltpu.CompilerParams(
            dimension_semantics=("arbitrary", "arbitrary", "arbitrary"), vmem_limit_bytes=VMEM_LIMIT),
        name="nsa",
    )(*q_ops, u3, cmp_kv, cmp_kv, *kv_ops, ovt, *side_weights)


def _out_proj_kernel(og_ref, on_ref, x_ref, w_ref, g_ref, b_ref, h_ref, hb_ref):
    half = og_ref.shape[0] // 2
    halves = [slice(0, half), slice(half, 2 * half)]
    mixes = [_dot(og_ref[rows, :], w_ref[0:GLA_WIDTH, :]) + _dot(on_ref[rows, :], w_ref[GLA_WIDTH:, :])
             for rows in halves]
    for rows, mix in zip(halves, mixes):
        h = _layer_norm(DEEPNORM_ALPHA * x_ref[rows, :] + mix, g_ref[...], b_ref[...])
        h_ref[rows, :] = h
        hb_ref[rows, :] = h.astype(BF16)


def _out_proj(o_gla, o_nsa, x2, w_bf, g, b):
    T, D = x2.shape
    tile = lambda w: pl.BlockSpec((OUT_TM, w), lambda i: (i, 0))
    const = lambda s: pl.BlockSpec(s, lambda i: (0, 0))
    return pl.pallas_call(
        _out_proj_kernel,
        grid=(T // OUT_TM,),
        in_specs=[tile(GLA_WIDTH), tile(NSA_WIDTH), tile(D), const(w_bf.shape), const((1, D)), const((1, D))],
        out_specs=[tile(D), tile(D)],
        out_shape=[jax.ShapeDtypeStruct((T, D), F32), jax.ShapeDtypeStruct((T, D), BF16)],
        compiler_params=pltpu.CompilerParams(
            dimension_semantics=("parallel",), vmem_limit_bytes=VMEM_LIMIT),
        name="out_proj",
    )(o_gla, o_nsa, x2, w_bf, g, b)


FFN_LN_ROWS = 256


def _ffn_residual_copy(h_hbm, res_s, sem, i):
    tm = res_s.shape[0]
    return pltpu.make_async_copy(h_hbm.at[pl.ds(pl.multiple_of(i * tm, tm), tm), :], res_s, sem)


def _ffn_kernel(hb_ref, h_hbm, w1_ref, w3_ref, w2_ref, g_ref, b_ref, o_ref, res_s, sem):
    i, f = pl.program_id(0), pl.program_id(1)
    last = pl.num_programs(1) - 1

    @pl.when(f == 0)
    def _():
        _ffn_residual_copy(h_hbm, res_s, sem, i).start()
        o_ref[...] = jnp.zeros(o_ref.shape, F32)

    def accumulate(rows):
        hb = hb_ref[rows, :]
        a = _dot(hb, w1_ref[...])
        c = _dot(hb, w3_ref[...])
        o_ref[rows, :] += _dot((a * jax.nn.sigmoid(a) * c).astype(BF16), w2_ref[...])

    @pl.when(f < last)
    def _():
        accumulate(slice(None))

    @pl.when(f == last)
    def _():
        _ffn_residual_copy(h_hbm, res_s, sem, i).wait()
        half = o_ref.shape[0] // 2
        for r0 in (0, half):
            accumulate(slice(r0, r0 + half))
            for p0 in range(r0, r0 + half, FFN_LN_ROWS):
                rows = slice(p0, p0 + FFN_LN_ROWS)
                o_ref[rows, :] = _layer_norm(DEEPNORM_ALPHA * res_s[rows, :] + o_ref[rows, :],
                                             g_ref[...], b_ref[...])


def _ffn(h_bf, h, w1, w3, w2, g, b):
    T, D = h.shape
    F = w1.shape[1]
    return pl.pallas_call(
        _ffn_kernel,
        grid=(T // FFN_TM, F // FFN_TF),
        in_specs=[
            pl.BlockSpec((FFN_TM, D), lambda i, f: (i, 0)),
            pl.BlockSpec(memory_space=pl.ANY),
            pl.BlockSpec((D, FFN_TF), lambda i, f: (0, f)),
            pl.BlockSpec((D, FFN_TF), lambda i, f: (0, f)),
            pl.BlockSpec((FFN_TF, D), lambda i, f: (f, 0)),
            pl.BlockSpec((1, D), lambda i, f: (0, 0)),
            pl.BlockSpec((1, D), lambda i, f: (0, 0)),
        ],
        out_specs=pl.BlockSpec((FFN_TM, D), lambda i, f: (i, 0)),
        out_shape=jax.ShapeDtypeStruct((T, D), F32),
        scratch_shapes=[pltpu.VMEM((FFN_TM, D), F32), pltpu.SemaphoreType.DMA(())],
        compiler_params=pltpu.CompilerParams(
            dimension_semantics=("arbitrary", "arbitrary"), vmem_limit_bytes=VMEM_LIMIT),
        name="ffn",
    )(h_bf, h, w1, w3, w2, g, b)


def _rope_tables(S):
    half = ROPE_DIM // 2
    pos = np.arange(S, dtype=np.float64)
    inv = np.power(ROPE_THETA, -np.arange(0, ROPE_DIM, 2, dtype=np.float64) / ROPE_DIM)
    ang = pos[:, None] * inv[None, :]
    cos_f = np.ones((S, LANES), np.float32)
    sin_s = np.zeros((S, LANES), np.float32)
    for lane0, sign in ((0, -1.0), (ROPE_PAIR_LANE, 1.0)):
        cos_f[:, lane0:lane0 + half] = np.cos(ang)
        sin_s[:, lane0:lane0 + half] = sign * np.sin(ang)
    return jnp.asarray(cos_f), jnp.asarray(sin_s)


def _selection_constants(S):
    n_cmp = (S - CMP_BLOCK) // CMP_STRIDE + 1
    nb = S // SLC_BLOCK
    c_start = np.arange(n_cmp) * CMP_STRIDE
    b_start = np.arange(nb) * SLC_BLOCK
    overlap = ((c_start[:, None] < b_start[None, :] + SLC_BLOCK) &
               (b_start[None, :] < c_start[:, None] + CMP_BLOCK)).astype(np.float32)
    ovt = np.zeros((nb, N_CMP_PAD), np.float32)
    ovt[:, :n_cmp] = overlap.T
    blk_onehot = ((np.arange(S) // SLC_BLOCK)[:, None] == np.arange(LANES)[None, :]).astype(np.float32)
    return jnp.asarray(ovt, BF16), jnp.asarray(blk_onehot, BF16)


def kernel(x, w_in, gla_gate_w2, gla_gate_b2, gla_norm_w, cmp_k_pos, cmp_k_w1, cmp_k_w2,
           cmp_v_pos, cmp_v_w1, cmp_v_w2, w_out, ln1_g, ln1_b, ffn_w1, ffn_w3, ffn_w2, ln2_g, ln2_b):
    B, S, D = x.shape
    T = B * S
    cos_f, sin_s = _rope_tables(S)
    ovt, blk_onehot = _selection_constants(S)

    x2 = x.reshape(T, D)
    for l in range(DEPTH):
        wt_gla, wt_nsa = _regroup_w_in(w_in[l].T)
        ua3 = _in_proj(x2, wt_gla).reshape(B, S, -1)
        u_nsa, *attn_ops = _in_proj_nsa(x2, wt_nsa, S, cos_f, sin_s, blk_onehot)
        u3 = u_nsa.reshape(B, S, -1)
        attn_ops = [a.reshape(B, S, -1) for a in attn_ops]
        q_ops, kv_ops = attn_ops[0:2], attn_ops[2:]
        o_gla = _gla(ua3, u3, gla_gate_w2[l], gla_gate_b2[l][None, :], gla_norm_w[l][None, :])
        cmp_kv = _compress(u3, (cmp_k_pos[l], cmp_k_w1[l], cmp_k_w2[l]), (cmp_v_pos[l], cmp_v_w1[l], cmp_v_w2[l]))
        o_nsa, w_out_bf, w1_bf, w3_bf, w2_bf = _nsa(u3, cmp_kv, q_ops, kv_ops, ovt,
                                                    [w_out[l], ffn_w1[l], ffn_w3[l], ffn_w2[l]])
        h, h_bf = _out_proj(o_gla.reshape(T, GLA_WIDTH), o_nsa.reshape(T, NSA_WIDTH), x2,
                            w_out_bf, ln1_g[l][None, :], ln1_b[l][None, :])
        x2 = _ffn(h_bf, h, w1_bf, w3_bf, w2_bf, ln2_g[l][None, :], ln2_b[l][None, :])
    return x2.reshape(B, S, D)
```
